```python
import math
import jax, jax.numpy as jnp
from jax import lax
import numpy as np


D_MODEL = 2048
BATCH = 4
SEQ = 4096
DEPTH = 2

N_EVEN = (DEPTH + 1) // 2
N_ODD = DEPTH // 2
EPS = 1e-6
ROPE_THETA = 500000.0
MLSTM_HEADS = 4
MLSTM_DQK = 128
MLSTM_DV = 256
MLSTM_CHUNK = 128
CONV_W = 4
DIFF_HEADS = 8
DIFF_DQK = 64
DIFF_DV = 128
ATTN_Q_BLOCK = 128
MOBA_HEADS = 8
MOBA_DH = 128
MOBA_BLOCK = 256
MOBA_TOPK = 3
MOBA_Q_STEP = 16
SGU_GROUPS = 8
SGU_CH = 128
SGU_CHUNK = 128
PEER_HEADS = 8
PEER_NKEYS = 128
PEER_N = PEER_NKEYS * PEER_NKEYS
PEER_DQ = 256
PEER_TOPK = 16
PEER_TOKEN_BLOCK = 128
PEER_V_SCALE = 1.0
A_QK = MLSTM_HEADS * MLSTM_DQK
A_V = MLSTM_HEADS * MLSTM_DV
B_QK = DIFF_HEADS * 2 * DIFF_DQK
B_V = DIFF_HEADS * DIFF_DV
EVEN_IN = 2 * A_QK + 2 * A_V + 2 * MLSTM_HEADS + 2 * B_QK + B_V
EVEN_OUT = A_V + B_V
C_W = MOBA_HEADS * MOBA_DH
D_W = SGU_GROUPS * SGU_CH
ODD_IN = 3 * C_W + 2 * D_W
ODD_OUT = C_W + D_W

kernel_name = 'hybrid_mlstm_diffattn_moba_sgu_peer'

F32 = jnp.float32


def rmsnorm(x, g):
    xf = x.astype(F32)
    y = xf * lax.rsqrt(jnp.mean(xf * xf, axis=-1, keepdims=True) + EPS)
    return (y * g.astype(F32)).astype(x.dtype)


def split_heads(t, n):
    B, S, _ = t.shape
    return t.reshape(B, S, n, -1).transpose(0, 2, 1, 3)


def merge_heads(t):
    B, H, S, d = t.shape
    return t.transpose(0, 2, 1, 3).reshape(B, S, H * d)


def rope_tables(positions, dh):
    half = dh // 8
    inv = ROPE_THETA ** (-jnp.arange(half, dtype=F32) / half)
    ang = positions.astype(F32)[..., None] * inv
    return jnp.cos(ang), jnp.sin(ang)


def apply_rope(x, cos, sin):
    half = cos.shape[-1]
    cos = cos.astype(x.dtype)
    sin = sin.astype(x.dtype)
    x1 = x[..., :half]
    x2 = x[..., half:2 * half]
    return jnp.concatenate([x1 * cos - x2 * sin, x2 * cos + x1 * sin, x[..., 2 * half:]], axis=-1)


def causal_conv(x, w, b):
    C = x.shape[-1]
    y = lax.conv_general_dilated(x, w[:, None, :].astype(x.dtype), window_strides=(1,),
                                 padding=[(CONV_W - 1, 0)], dimension_numbers=('NWC', 'WIO', 'NWC'),
                                 feature_group_count=C)
    return y + b


def mlstm_chunkwise(q, k, v, i_pre, f_pre):
    B, H, S, dqk = q.shape
    dv = v.shape[-1]
    L = MLSTM_CHUNK
    nc = S // L
    q = q * (dqk ** -0.5)
    lf = jax.nn.log_sigmoid(f_pre.astype(F32))
    li = i_pre.astype(F32)

    def to_chunks(t):
        return jnp.moveaxis(t.reshape((B, H, nc, L) + t.shape[3:]), 2, 0)

    xs = tuple(to_chunks(t) for t in (q, k, v, lf, li))
    causal = jnp.tril(jnp.ones((L, L), bool))

    def step(carry, inp):
        Cst, nst, m = carry
        qb, kb, vb, lfb, lib = inp
        qb, kb, vb = qb.astype(F32), kb.astype(F32), vb.astype(F32)
        bcum = jnp.cumsum(lfb, axis=-1)
        g = bcum[..., -1]
        d_intra = jnp.where(causal, bcum[..., :, None] - bcum[..., None, :] + lib[..., None, :], -jnp.inf)
        d_inter = bcum + m[..., None]
        m_t = jnp.maximum(d_inter, jnp.max(d_intra, axis=-1))
        w_intra = jnp.exp(d_intra - m_t[..., None])
        w_inter = jnp.exp(d_inter - m_t)
        s = jnp.einsum('bhtd,bhsd->bhts', qb, kb) * w_intra
        num = jnp.einsum('bhts,bhsv->bhtv', s, vb) + w_inter[..., None] * jnp.einsum('bhtd,bhdv->bhtv', qb, Cst)
        den = jnp.sum(s, axis=-1) + w_inter * jnp.einsum('bhtd,bhd->bht', qb, nst)
        h = num / jnp.maximum(jnp.abs(den), jnp.exp(-m_t))[..., None]
        d_state = g[..., None] - bcum + lib
        m_new = jnp.maximum(g + m, jnp.max(d_state, axis=-1))
        w_s = jnp.exp(d_state - m_new[..., None])
        w_c = jnp.exp(g + m - m_new)
        C_new = w_c[..., None, None] * Cst + jnp.einsum('bhs,bhsd,bhsv->bhdv', w_s, kb, vb)
        n_new = w_c[..., None] * nst + jnp.einsum('bhs,bhsd->bhd', w_s, kb)
        return (C_new, n_new, m_new), h.astype(v.dtype)

    init = (jnp.zeros((B, H, dqk, dv), F32), jnp.zeros((B, H, dqk), F32), jnp.zeros((B, H), F32))
    _, hs = lax.scan(step, init, xs)
    return jnp.moveaxis(hs, 0, 2).reshape(B, H, S, dv)


def diff_attention(q, k, v, lam):
    B, H, S, _, dqk = q.shape
    scale = dqk ** -0.5
    nq = S // ATTN_Q_BLOCK
    qs = jnp.moveaxis(q.reshape(B, H, nq, ATTN_Q_BLOCK, 2, dqk), 2, 0)
    kpos = jnp.arange(S)

    def step(args):
        qi, j = args
        qpos = j * ATTN_Q_BLOCK + jnp.arange(ATTN_Q_BLOCK)
        s = jnp.einsum('bhqcd,bhkcd->cbhqk', qi, k).astype(F32) * scale
        s = jnp.where(kpos[None, :] <= qpos[:, None], s, -jnp.inf)
        p = jax.nn.softmax(s, axis=-1)
        a = (p[0] - lam * p[1]).astype(v.dtype)
        return jnp.einsum('bhqk,bhkv->bhqv', a, v)

    out = lax.map(step, (qs, jnp.arange(nq)))
    return jnp.moveaxis(out, 0, 2).reshape(B, H, S, v.shape[-1])


def moba_attention(q, k, v):
    B, H, S, dh = q.shape
    nb = -(-S // MOBA_BLOCK)
    ng = max(nb, MOBA_TOPK)
    sp = nb * MOBA_BLOCK
    pad = ((0, 0), (0, 0), (0, sp - S), (0, 0))
    kp = jnp.pad(k, pad)
    vp = jnp.pad(v, pad)
    kblk = kp.reshape(B, H, nb, MOBA_BLOCK, dh)
    vblk = vp.reshape(B, H, nb, MOBA_BLOCK, dh)
    kmean = jnp.mean(kblk.astype(F32), axis=3)
    kmean = jnp.pad(kmean, ((0, 0), (0, 0), (0, ng - nb), (0, 0)))
    scale = dh ** -0.5
    nstep = S // MOBA_Q_STEP
    qs = jnp.moveaxis(q.reshape(B, H, nstep, MOBA_Q_STEP, dh), 2, 0)
    bi = jnp.arange(B)[:, None, None, None]
    hi = jnp.arange(H)[None, :, None, None]
    nsel = MOBA_TOPK * MOBA_BLOCK

    def step(args):
        qi, j = args
        qpos = j * MOBA_Q_STEP + jnp.arange(MOBA_Q_STEP)
        own = (j * MOBA_Q_STEP) // MOBA_BLOCK
        gate = jnp.einsum('bhqd,bhnd->bhqn', qi.astype(F32), kmean)
        gate = jnp.where(jnp.arange(ng) < own, gate, -jnp.inf)
        _, sel = lax.top_k(gate, MOBA_TOPK)
        sel = jnp.minimum(sel, nb - 1)
        valid = jnp.arange(MOBA_TOPK) < jnp.minimum(own, MOBA_TOPK)
        k_sel = kblk[bi, hi, sel]
        v_sel = vblk[bi, hi, sel]
        s_sel = jnp.einsum('bhqd,bhqkpd->bhqkp', qi, k_sel).astype(F32) * scale
        s_sel = jnp.where(valid[:, None], s_sel, -jnp.inf).reshape(B, H, MOBA_Q_STEP, nsel)
        k_own = lax.dynamic_slice_in_dim(kp, own * MOBA_BLOCK, MOBA_BLOCK, axis=2)
        v_own = lax.dynamic_slice_in_dim(vp, own * MOBA_BLOCK, MOBA_BLOCK, axis=2)
        s_own = jnp.einsum('bhqd,bhpd->bhqp', qi, k_own).astype(F32) * scale
        kpos = own * MOBA_BLOCK + jnp.arange(MOBA_BLOCK)
        s_own = jnp.where(kpos[None, :] <= qpos[:, None], s_own, -jnp.inf)
        p = jax.nn.softmax(jnp.concatenate([s_sel, s_own], axis=-1), axis=-1).astype(v.dtype)
        p_sel = p[..., :nsel].reshape(B, H, MOBA_Q_STEP, MOBA_TOPK, MOBA_BLOCK)
        return (jnp.einsum('bhqkp,bhqkpv->bhqv', p_sel, v_sel)
                + jnp.einsum('bhqp,bhpv->bhqv', p[..., nsel:], v_own))

    out = lax.map(step, (qs, jnp.arange(nstep)))
    return jnp.moveaxis(out, 0, 2).reshape(B, H, S, dh)


def spatial_gating(u, v, ln_g, ln_b, w_s, b_s):
    B, S, _ = u.shape
    nc = S // SGU_CHUNK
    vf = v.astype(F32).reshape(B, S, SGU_GROUPS, SGU_CH)
    mu = jnp.mean(vf, axis=-1, keepdims=True)
    var = jnp.mean(jnp.square(vf - mu), axis=-1, keepdims=True)
    vn = ((vf - mu) * lax.rsqrt(var + EPS)).reshape(B, S, D_W) * ln_g.astype(F32) + ln_b.astype(F32)
    vn = vn.astype(v.dtype).reshape(B, nc, SGU_CHUNK, SGU_GROUPS, SGU_CH)
    w = jnp.where(jnp.tril(jnp.ones((SGU_CHUNK, SGU_CHUNK), bool)), w_s, 0.0)
    mixed = jnp.einsum('gpr,bcrgd->bcpgd', w, vn) + b_s.T[None, None, :, :, None]
    return u * mixed.reshape(B, S, D_W)


def even_mixer(h, positions, w_in, w_out, conv_w, conv_b, gate_b, mlstm_g, lam_vec, diff_g, lambda_init):
    B, S, _ = h.shape
    proj = h @ w_in
    sizes = [2 * A_QK, A_V, A_V, 2 * MLSTM_HEADS, B_QK, B_QK, B_V]
    cuts = [int(c) for c in np.cumsum(sizes)[:-1]]
    a_qk, a_v, a_o, a_gates, b_q, b_k, b_v = jnp.split(proj, cuts, axis=-1)
    qk = jax.nn.silu(causal_conv(a_qk, conv_w, conv_b))
    a_q = split_heads(qk[..., :A_QK], MLSTM_HEADS)
    a_k = split_heads(qk[..., A_QK:], MLSTM_HEADS)
    gates = (a_gates.reshape(B, S, 2, MLSTM_HEADS) + gate_b).transpose(2, 0, 3, 1)
    h_a = mlstm_chunkwise(a_q, a_k, split_heads(a_v, MLSTM_HEADS), gates[0], gates[1])
    h_a = merge_heads(rmsnorm(h_a, mlstm_g[:, None, :])) * jax.nn.sigmoid(a_o)
    cos, sin = rope_tables(positions, DIFF_DQK)
    cs, sn = cos[:, None, :, None], sin[:, None, :, None]
    bq = apply_rope(b_q.reshape(B, S, DIFF_HEADS, 2, DIFF_DQK).transpose(0, 2, 1, 3, 4), cs, sn)
    bk = apply_rope(b_k.reshape(B, S, DIFF_HEADS, 2, DIFF_DQK).transpose(0, 2, 1, 3, 4), cs, sn)
    lv = lam_vec.astype(F32)
    lam = jnp.exp(jnp.sum(lv[0] * lv[1])) - jnp.exp(jnp.sum(lv[2] * lv[3])) + lambda_init
    h_b = diff_attention(bq, bk, split_heads(b_v, DIFF_HEADS), lam)
    h_b = merge_heads(rmsnorm(h_b, diff_g[:, None, :]) * (1.0 - lambda_init))
    return jnp.concatenate([h_a, h_b], axis=-1) @ w_out


def odd_mixer(h, positions, w_in, w_out, ln_g, ln_b, sgu_w, sgu_b):
    proj = h @ w_in
    c_q, c_k, c_v, d_u, d_v = jnp.split(proj, [C_W, 2 * C_W, 3 * C_W, 3 * C_W + D_W], axis=-1)
    cos, sin = rope_tables(positions, MOBA_DH)
    cs, sn = cos[:, None], sin[:, None]
    cq = apply_rope(split_heads(c_q, MOBA_HEADS), cs, sn)
    ck = apply_rope(split_heads(c_k, MOBA_HEADS), cs, sn)
    h_c = merge_heads(moba_attention(cq, ck, split_heads(c_v, MOBA_HEADS)))
    h_d = spatial_gating(jax.nn.gelu(d_u), jax.nn.gelu(d_v), ln_g, ln_b, sgu_w, sgu_b)
    return jnp.concatenate([h_c, h_d], axis=-1) @ w_out


def peer(x, w_q, sub_keys, expert_u, expert_v):
    B, S, D = x.shape
    nblk = (B * S) // PEER_TOKEN_BLOCK

    def block(xb):
        q = (xb @ w_q).reshape(-1, PEER_HEADS, 2, PEER_DQ // 2)
        s = jnp.einsum('thcd,hcnd->thcn', q, sub_keys).astype(F32)
        s1, i1 = lax.top_k(s[:, :, 0], PEER_TOPK)
        s2, i2 = lax.top_k(s[:, :, 1], PEER_TOPK)
        cand = (s1[..., :, None] + s2[..., None, :]).reshape(-1, PEER_HEADS, PEER_TOPK * PEER_TOPK)
        cidx = (i1[..., :, None] * PEER_NKEYS + i2[..., None, :]).reshape(-1, PEER_HEADS, PEER_TOPK * PEER_TOPK)
        top_s, pos = lax.top_k(cand, PEER_TOPK)
        eidx = jnp.take_along_axis(cidx, pos, axis=-1)
        g = jax.nn.softmax(top_s, axis=-1)
        act = jax.nn.gelu(jnp.einsum('td,thkd->thk', xb, expert_u[eidx]).astype(F32))
        return jnp.einsum('thk,thkd->td', (g * act).astype(x.dtype), expert_v[eidx])

    out = lax.map(block, x.reshape(nblk, PEER_TOKEN_BLOCK, D))
    return out.reshape(B, S, D)


def setup_inputs(seed: int = 0) -> dict:
    key = jax.random.key(seed)
    ks = jax.random.split(key, 25)

    def nrm(k, shape, scale):
        return scale * jax.random.normal(k, shape, jnp.float32)

    def gain(k, shape):
        return 1.0 + nrm(k, shape, 0.02)

    offsets = jax.random.randint(ks[2], (BATCH, 1), 0, 1024, dtype=jnp.int32)
    positions = offsets + jnp.arange(SEQ, dtype=jnp.int32)[None, :]
    gate_base = jnp.stack([jnp.zeros((MLSTM_HEADS,), jnp.float32), jnp.linspace(3.0, 6.0, MLSTM_HEADS)])
    return {
        'x': nrm(ks[0], (BATCH, SEQ, D_MODEL), 1.0),
        'c': nrm(ks[1], (BATCH, D_MODEL), 1.0),
        'positions': positions,
        'mod_w': nrm(ks[3], (DEPTH, D_MODEL, 6 * D_MODEL), 0.5 * D_MODEL ** -0.5),
        'mod_b': nrm(ks[4], (DEPTH, 6 * D_MODEL), 0.02),
        'norm_g': gain(ks[5], (DEPTH, 2, D_MODEL)),
        'final_g': gain(ks[6], (D_MODEL,)),
        'even_w_in': nrm(ks[7], (N_EVEN, D_MODEL, EVEN_IN), D_MODEL ** -0.5),
        'even_w_out': nrm(ks[8], (N_EVEN, EVEN_OUT, D_MODEL), EVEN_OUT ** -0.5),
        'mlstm_conv_w': nrm(ks[9], (N_EVEN, CONV_W, 2 * A_QK), CONV_W ** -0.5),
        'mlstm_conv_b': nrm(ks[10], (N_EVEN, 2 * A_QK), 0.02),
        'mlstm_gate_b': gate_base[None] + nrm(ks[11], (N_EVEN, 2, MLSTM_HEADS), 0.1),
        'mlstm_head_g': gain(ks[12], (N_EVEN, MLSTM_HEADS, MLSTM_DV)),
        'diff_lambda': nrm(ks[13], (N_EVEN, 4, DIFF_DQK), 0.1),
        'diff_head_g': gain(ks[14], (N_EVEN, DIFF_HEADS, DIFF_DV)),
        'odd_w_in': nrm(ks[15], (N_ODD, D_MODEL, ODD_IN), D_MODEL ** -0.5),
        'odd_w_out': nrm(ks[16], (N_ODD, ODD_OUT, D_MODEL), ODD_OUT ** -0.5),
        'sgu_ln_g': gain(ks[17], (N_ODD, D_W)),
        'sgu_ln_b': nrm(ks[18], (N_ODD, D_W), 0.02),
        'sgu_w': nrm(ks[19], (N_ODD, SGU_GROUPS, SGU_CHUNK, SGU_CHUNK), SGU_CHUNK ** -0.5),
        'sgu_b': gain(ks[20], (N_ODD, SGU_GROUPS, SGU_CHUNK)),
        'peer_w_q': nrm(ks[21], (DEPTH, D_MODEL, PEER_HEADS * PEER_DQ), D_MODEL ** -0.5),
        'peer_sub_keys': nrm(ks[22], (DEPTH, PEER_HEADS, 2, PEER_NKEYS, PEER_DQ // 2), (PEER_DQ // 2) ** -0.5),
        'peer_u': nrm(ks[23], (DEPTH, PEER_N, D_MODEL), D_MODEL ** -0.5),
        'peer_v': nrm(ks[24], (DEPTH, PEER_N, D_MODEL), PEER_V_SCALE),
    }


def reference(x, c, positions, mod_w, mod_b, norm_g, final_g, even_w_in, even_w_out,
              mlstm_conv_w, mlstm_conv_b, mlstm_gate_b, mlstm_head_g, diff_lambda, diff_head_g,
              odd_w_in, odd_w_out, sgu_ln_g, sgu_ln_b, sgu_w, sgu_b,
              peer_w_q, peer_sub_keys, peer_u, peer_v):
    h = x
    c_act = jax.nn.silu(c)
    for layer in range(DEPTH):
        mod = c_act @ mod_w[layer] + mod_b[layer]
        sh1, sc1, g1, sh2, sc2, g2 = [m[:, None, :] for m in jnp.split(mod, 6, axis=-1)]
        hn = rmsnorm(h, norm_g[layer, 0]) * (1.0 + sc1) + sh1
        if layer % 2 == 0:
            e = layer // 2
            lambda_init = 0.8 - 0.6 * math.exp(-0.3 * layer)
            mix = even_mixer(hn, positions, even_w_in[e], even_w_out[e], mlstm_conv_w[e], mlstm_conv_b[e],
                             mlstm_gate_b[e], mlstm_head_g[e], diff_lambda[e], diff_head_g[e], lambda_init)
        else:
            o = layer // 2
            mix = odd_mixer(hn, positions, odd_w_in[o], odd_w_out[o], sgu_ln_g[o], sgu_ln_b[o],
                            sgu_w[o], sgu_b[o])
        h = h + g1 * mix
        hn = rmsnorm(h, norm_g[layer, 1]) * (1.0 + sc2) + sh2
        h = h + g2 * peer(hn, peer_w_q[layer], peer_sub_keys[layer], peer_u[layer], peer_v[layer])
    return rmsnorm(h, final_g)
```

```python
import functools
import math

import numpy as np
import jax
import jax.numpy as jnp
from jax import lax
from jax.experimental import pallas as pl
from jax.experimental.pallas import tpu as pltpu

F32 = jnp.float32
BF16 = jnp.bfloat16
I32 = jnp.int32
HIGHEST = lax.Precision.HIGHEST

EPS = 1e-6
ROPE_THETA = 500000.0
LANES = 128
SUBLANES = 8
VMEM_LIMIT_BYTES = 48 * 1024 * 1024

MLSTM_HEADS, MLSTM_DQK, MLSTM_DV, MLSTM_CHUNK, CONV_W = 4, 128, 256, 128, 4
DIFF_HEADS, DIFF_DQK, DIFF_DV = 8, 64, 128
MOBA_HEADS, MOBA_DH, MOBA_BLOCK, MOBA_TOPK = 8, 128, 256, 3
SGU_GROUPS, SGU_CH, SGU_CHUNK = 8, 128, 128
PEER_HEADS, PEER_NKEYS, PEER_DQ, PEER_TOPK = 8, 128, 256, 16
PEER_SEL = PEER_HEADS * PEER_TOPK

A_QK = MLSTM_HEADS * MLSTM_DQK
A_V = MLSTM_HEADS * MLSTM_DV
B_QK = DIFF_HEADS * 2 * DIFF_DQK
B_V = DIFF_HEADS * DIFF_DV
C_W = MOBA_HEADS * MOBA_DH
D_W = SGU_GROUPS * SGU_CH


def _cparams(*sem):
    return pltpu.CompilerParams(dimension_semantics=sem, vmem_limit_bytes=VMEM_LIMIT_BYTES)


def _dot_nt(a, b, precision=None):
    return lax.dot_general(a, b, (((1,), (1,)), ((), ())), preferred_element_type=F32, precision=precision)


def _store_row(ref, row, val, cols=slice(None)):
    base = pl.multiple_of((row // SUBLANES) * SUBLANES, SUBLANES)
    blk = ref[pl.ds(base, SUBLANES), cols]
    r = lax.broadcasted_iota(I32, blk.shape, 0)
    ref[pl.ds(base, SUBLANES), cols] = jnp.where(r == row - base, val, blk)


def _sigmoid(x):
    return 1.0 / (1.0 + jnp.exp(-x))


def _silu(x):
    return x * _sigmoid(x)


def _gelu_tanh(x):
    return 0.5 * x * (1.0 + jnp.tanh(math.sqrt(2.0 / math.pi) * (x + 0.044715 * (x * x * x))))


def _log_sigmoid(x):
    return jnp.minimum(x, 0.0) - jnp.log1p(jnp.exp(-jnp.abs(x)))


def _mod_kernel(c_ref, w_ref, b_ref, o_ref):
    ca = _silu(c_ref[...])
    o_ref[0] = jnp.dot(ca, w_ref[0], preferred_element_type=F32, precision=HIGHEST) + b_ref[0]


def _modulation(c, mod_w, mod_b):
    depth, d, n = mod_w.shape
    bsz = c.shape[0]
    rows = 8
    cp = jnp.zeros((rows, d), F32).at[:bsz].set(c)
    tn = 1024
    out = pl.pallas_call(
        _mod_kernel,
        grid=(depth, n // tn),
        in_specs=[pl.BlockSpec((rows, d), lambda l, j: (0, 0)),
                  pl.BlockSpec((1, d, tn), lambda l, j: (l, 0, j)),
                  pl.BlockSpec((1, 1, tn), lambda l, j: (l, 0, j))],
        out_specs=pl.BlockSpec((1, rows, tn), lambda l, j: (l, 0, j)),
        out_shape=jax.ShapeDtypeStruct((depth, rows, n), F32),
        compiler_params=_cparams("parallel", "parallel"),
        name="modulation",
    )(cp, mod_w, mod_b.reshape(depth, 1, n))
    return out[:, :bsz]


def _nm_matmul_kernel(h_ref, g_ref, sc_ref, sh_ref, w_ref, *rest, emit_hn):
    if emit_hn:
        o_ref, hn_ref, hn_scr = rest
    else:
        o_ref, hn_scr = rest

    @pl.when(pl.program_id(2) == 0)
    def _():
        x = h_ref[0]
        y = x * lax.rsqrt(jnp.mean(x * x, axis=-1, keepdims=True) + EPS) * g_ref[...]
        y = y * (1.0 + sc_ref[0]) + sh_ref[0]
        hn_scr[...] = y.astype(BF16)
        if emit_hn:
            hn_ref[0] = y

    o_ref[0] = jnp.dot(hn_scr[...], w_ref[...], preferred_element_type=F32).astype(o_ref.dtype)


def _nm_matmul(h, g, sc, sh, w, *, tm, tn, emit_hn=False, name="nm_matmul"):
    bsz, s, d = h.shape
    n = w.shape[1]
    tm = min(tm, s)
    out_shape = [jax.ShapeDtypeStruct((bsz, s, n), F32)]
    out_specs = [pl.BlockSpec((1, tm, tn), lambda b, i, j: (b, i, j))]
    if emit_hn:
        out_shape.append(jax.ShapeDtypeStruct((bsz, s, d), F32))
        out_specs.append(pl.BlockSpec((1, tm, d), lambda b, i, j: (b, i, 0)))
    res = pl.pallas_call(
        functools.partial(_nm_matmul_kernel, emit_hn=emit_hn),
        grid=(bsz, s // tm, n // tn),
        in_specs=[pl.BlockSpec((1, tm, d), lambda b, i, j: (b, i, 0)),
                  pl.BlockSpec((1, d), lambda b, i, j: (0, 0)),
                  pl.BlockSpec((1, 1, d), lambda b, i, j: (b, 0, 0)),
                  pl.BlockSpec((1, 1, d), lambda b, i, j: (b, 0, 0)),
                  pl.BlockSpec((d, tn), lambda b, i, j: (0, j))],
        out_specs=out_specs,
        out_shape=out_shape,
        scratch_shapes=[pltpu.VMEM((tm, d), BF16)],
        compiler_params=_cparams("parallel", "parallel", "arbitrary"),
        name=name,
    )(h, g.reshape(1, d), sc.reshape(bsz, 1, d), sh.reshape(bsz, 1, d), w)
    return res if emit_hn else res[0]


def _out_proj_kernel(xa_ref, xb_ref, wa_ref, wb_ref, h_ref, g_ref, o_ref):
    acc = jnp.dot(xa_ref[0], wa_ref[...], preferred_element_type=F32)
    acc += jnp.dot(xb_ref[0], wb_ref[...], preferred_element_type=F32)
    o_ref[0] = h_ref[0] + g_ref[0] * acc


def _out_proj(xa, xb, w, h, g1, *, tm=512, tn=1024):
    bsz, s, ka = xa.shape
    kb = xb.shape[2]
    n = w.shape[1]
    tm = min(tm, s)
    wa = w[:ka].astype(BF16)
    wb = w[ka:].astype(BF16)
    return pl.pallas_call(
        _out_proj_kernel,
        grid=(bsz, s // tm, n // tn),
        in_specs=[pl.BlockSpec((1, tm, ka), lambda b, i, j: (b, i, 0)),
                  pl.BlockSpec((1, tm, kb), lambda b, i, j: (b, i, 0)),
                  pl.BlockSpec((ka, tn), lambda b, i, j: (0, j)),
                  pl.BlockSpec((kb, tn), lambda b, i, j: (0, j)),
                  pl.BlockSpec((1, tm, tn), lambda b, i, j: (b, i, j)),
                  pl.BlockSpec((1, 1, tn), lambda b, i, j: (b, 0, j))],
        out_specs=pl.BlockSpec((1, tm, tn), lambda b, i, j: (b, i, j)),
        out_shape=jax.ShapeDtypeStruct((bsz, s, n), F32),
        compiler_params=_cparams("parallel", "parallel", "parallel"),
        name="out_proj",
    )(xa, xb, wa, wb, h, g1.reshape(bsz, 1, n))


def _mlstm_kernel(gb_ref, aqk_ref, v_ref, ao_ref, gcol_ref, grow_ref, cw_ref, cb_ref, hg_ref,
                  o_ref, xbuf, c_scr, n_scr, m_scr):
    L = MLSTM_CHUNK
    H, DQK, DV = MLSTM_HEADS, MLSTM_DQK, MLSTM_DV
    c = pl.program_id(1)

    @pl.when(c == 0)
    def _():
        xbuf[0:8, :] = jnp.zeros((8, 2 * A_QK), F32)
        c_scr[...] = jnp.zeros_like(c_scr)
        n_scr[...] = jnp.zeros_like(n_scr)
        m_scr[...] = jnp.zeros_like(m_scr)

    @pl.when(c > 0)
    def _():
        xbuf[0:8, :] = xbuf[L:L + 8, :]

    xbuf[8:L + 8, :] = aqk_ref[0]
    conv = cb_ref[...] + cw_ref[0:1, :] * xbuf[5:5 + L, :]
    for j in range(1, CONV_W):
        conv = conv + cw_ref[j:j + 1, :] * xbuf[5 + j:5 + j + L, :]
    qk = _silu(conv)

    r_i = lax.broadcasted_iota(I32, (L, L), 0)
    c_i = lax.broadcasted_iota(I32, (L, L), 1)
    tril = c_i <= r_i
    gcol = gcol_ref[0]
    grow = grow_ref[0]

    for h in range(H):
        q = qk[:, h * DQK:(h + 1) * DQK] * (DQK ** -0.5)
        k = qk[:, A_QK + h * DQK:A_QK + (h + 1) * DQK]
        v = v_ref[0, :, h * DV:(h + 1) * DV]
        i_col = gcol[:, h:h + 1] + gb_ref[h]
        i_row = grow[h:h + 1, :] + gb_ref[h]
        lf_col = _log_sigmoid(gcol[:, H + h:H + h + 1] + gb_ref[H + h])
        lf_row = _log_sigmoid(grow[H + h:H + h + 1, :] + gb_ref[H + h])
        bcum_col = jnp.sum(jnp.where(tril, lf_row, 0.0), axis=1, keepdims=True)
        bcum_row = jnp.sum(jnp.where(r_i <= c_i, lf_col, 0.0), axis=0, keepdims=True)
        g_tot = jnp.sum(lf_row, axis=1, keepdims=True)
        m_prev = m_scr[h:h + 1, 0:1]
        d_intra = jnp.where(tril, bcum_col - bcum_row + i_row, -jnp.inf)
        d_inter = bcum_col + m_prev
        m_t = jnp.maximum(d_inter, jnp.max(d_intra, axis=1, keepdims=True))
        w_intra = jnp.exp(d_intra - m_t)
        w_inter = jnp.exp(d_inter - m_t)
        qb = q.astype(BF16)
        kb = k.astype(BF16)
        vb = v.astype(BF16)
        s = _dot_nt(qb, kb) * w_intra
        c_prev = c_scr[h]
        n_prev = n_scr[h:h + 1, :]
        num = jnp.dot(s.astype(BF16), vb, preferred_element_type=F32)
        num = num + w_inter * jnp.dot(qb, c_prev.astype(BF16), preferred_element_type=F32)
        den = jnp.sum(s, axis=1, keepdims=True) + w_inter * jnp.sum(q * n_prev, axis=1, keepdims=True)
        hh = num / jnp.maximum(jnp.abs(den), jnp.exp(-m_t))
        d_state = g_tot - bcum_col + i_col
        m_new = jnp.maximum(g_tot + m_prev, jnp.max(d_state, axis=0, keepdims=True))
        w_s = jnp.exp(d_state - m_new)
        w_c = jnp.exp(g_tot + m_prev - m_new)
        kw = k * w_s
        c_scr[h] = w_c * c_prev + jnp.dot(kw.T.astype(BF16), vb, preferred_element_type=F32)
        n_scr[h:h + 1, :] = w_c * n_prev + jnp.sum(kw, axis=0, keepdims=True)
        m_scr[h:h + 1, 0:1] = m_new
        y = hh * lax.rsqrt(jnp.mean(hh * hh, axis=-1, keepdims=True) + EPS) * hg_ref[:, h * DV:(h + 1) * DV]
        y = y * _sigmoid(ao_ref[0, :, h * DV:(h + 1) * DV])
        o_ref[0, :, h * DV:(h + 1) * DV] = y.astype(o_ref.dtype)


def _mlstm(proj, gates_t, gate_b, conv_w, conv_b, head_g):
    bsz, s, _ = proj.shape
    L = MLSTM_CHUNK
    gate_blk = (2 * A_QK + 2 * A_V + 2 * B_QK + B_V) // LANES
    return pl.pallas_call(
        _mlstm_kernel,
        grid=(bsz, s // L),
        in_specs=[pl.BlockSpec(memory_space=pltpu.SMEM),
                  pl.BlockSpec((1, L, 2 * A_QK), lambda b, c: (b, c, 0)),
                  pl.BlockSpec((1, L, A_V), lambda b, c: (b, c, 1)),
                  pl.BlockSpec((1, L, A_V), lambda b, c: (b, c, 2)),
                  pl.BlockSpec((1, L, LANES), lambda b, c: (b, c, gate_blk)),
                  pl.BlockSpec((1, 2 * MLSTM_HEADS, L), lambda b, c: (b, 0, c)),
                  pl.BlockSpec((CONV_W, 2 * A_QK), lambda b, c: (0, 0)),
                  pl.BlockSpec((1, 2 * A_QK), lambda b, c: (0, 0)),
                  pl.BlockSpec((1, A_V), lambda b, c: (0, 0))],
        out_specs=pl.BlockSpec((1, L, A_V), lambda b, c: (b, c, 0)),
        out_shape=jax.ShapeDtypeStruct((bsz, s, A_V), BF16),
        scratch_shapes=[pltpu.VMEM((L + 8, 2 * A_QK), F32),
                        pltpu.VMEM((MLSTM_HEADS, MLSTM_DQK, MLSTM_DV), F32),
                        pltpu.VMEM((8, MLSTM_DQK), F32),
                        pltpu.VMEM((8, LANES), F32)],
        compiler_params=_cparams("parallel", "arbitrary"),
        name="mlstm",
    )(gate_b.reshape(-1), proj, proj, proj, proj, gates_t, conv_w, conv_b.reshape(1, -1),
      head_g.reshape(1, -1))


def _rope_consts(dh):
    half = dh // 8
    lane = np.arange(LANES) % dh
    inv_half = ROPE_THETA ** (-jnp.arange(half, dtype=F32) / half)
    inv = jnp.where(jnp.asarray(lane < 2 * half), inv_half[lane % half], 0.0)
    lo = (lane < half).astype(np.float32)
    hi = ((lane >= half) & (lane < 2 * half)).astype(np.float32)
    return (inv.astype(F32).reshape(1, LANES), jnp.asarray(lo).reshape(1, LANES),
            jnp.asarray(hi).reshape(1, LANES))


def _rope_tables(pos_ref, inv_ref, lo_ref, hi_ref):
    ang = pos_ref[0].astype(F32) * inv_ref[...]
    cs = jnp.cos(ang)
    sn = jnp.sin(ang)
    return cs, -sn * lo_ref[...], sn * hi_ref[...]


def _rope_apply(x, cs, sa, sb, half):
    return x * cs + pltpu.roll(x, LANES - half, 1) * sa + pltpu.roll(x, half, 1) * sb


def _diff_prep_kernel(pos_ref, inv_ref, lo_ref, hi_ref, q_ref, k_ref, v_ref, qo_ref, ko_ref, vo_ref):
    cs, sa, sb = _rope_tables(pos_ref, inv_ref, lo_ref, hi_ref)
    half = DIFF_DQK // 8
    scale = DIFF_DQK ** -0.5
    for h in range(DIFF_HEADS):
        sl = slice(h * LANES, (h + 1) * LANES)
        qo_ref[0, :, sl] = (_rope_apply(q_ref[0, :, sl], cs, sa, sb, half) * scale).astype(BF16)
        ko_ref[0, :, sl] = _rope_apply(k_ref[0, :, sl], cs, sa, sb, half).astype(BF16)
    vo_ref[0] = v_ref[0].astype(BF16)


def _diff_prep(proj, pos3, ts=512):
    bsz, s, _ = proj.shape
    ts = min(ts, s)
    inv, lo, hi = _rope_consts(DIFF_DQK)
    base = (2 * A_QK + 2 * A_V) // B_QK
    cst = pl.BlockSpec((1, LANES), lambda b, i: (0, 0))
    osd = jax.ShapeDtypeStruct((bsz, s, B_QK), BF16)
    return pl.pallas_call(
        _diff_prep_kernel,
        grid=(bsz, s // ts),
        in_specs=[pl.BlockSpec((1, ts, 1), lambda b, i: (b, i, 0)), cst, cst, cst,
                  pl.BlockSpec((1, ts, B_QK), lambda b, i: (b, i, base)),
                  pl.BlockSpec((1, ts, B_QK), lambda b, i: (b, i, base + 1)),
                  pl.BlockSpec((1, ts, B_V), lambda b, i: (b, i, base + 2))],
        out_specs=[pl.BlockSpec((1, ts, B_QK), lambda b, i: (b, i, 0))] * 3,
        out_shape=[osd, osd, osd],
        compiler_params=_cparams("parallel", "parallel"),
        name="diff_prep",
    )(pos3, inv, lo, hi, proj, proj, proj)


def _diff_attn_kernel(lam_ref, q_ref, k_ref, v_ref, g_ref, o_ref, qs_scr, m_scr, l_scr, acc_scr,
                      *, tq, tk, lambda_init):
    i = pl.program_id(2)
    j = pl.program_id(3)

    @pl.when(j == 0)
    def _():
        q = q_ref[0]
        lane = lax.broadcasted_iota(I32, q.shape, 1)
        zero = jnp.zeros_like(q)
        qs_scr[0:tq, :] = jnp.where(lane < DIFF_DQK, q, zero)
        qs_scr[tq:2 * tq, :] = jnp.where(lane >= DIFF_DQK, q, zero)
        m_scr[...] = jnp.full_like(m_scr, -jnp.inf)
        l_scr[...] = jnp.zeros_like(l_scr)
        acc_scr[...] = jnp.zeros_like(acc_scr)

    def step(masked):
        s = _dot_nt(qs_scr[...], k_ref[0])
        if masked:
            row = lax.broadcasted_iota(I32, (2 * tq, tk), 0)
            row = jnp.where(row >= tq, row - tq, row)
            col = lax.broadcasted_iota(I32, (2 * tq, tk), 1)
            s = jnp.where((j * tk + col) <= (i * tq + row), s, -jnp.inf)
        m_prev = m_scr[...]
        m_new = jnp.maximum(m_prev, jnp.max(s, axis=1, keepdims=True))
        alpha = jnp.exp(m_prev - m_new)
        p = jnp.exp(s - m_new)
        l_scr[...] = alpha * l_scr[...] + jnp.sum(p, axis=1, keepdims=True)
        acc_scr[...] = alpha * acc_scr[...] + jnp.dot(p.astype(BF16), v_ref[0], preferred_element_type=F32)
        m_scr[...] = m_new

    @pl.when(j < i)
    def _():
        step(False)

    @pl.when(j == i)
    def _():
        step(True)
        lv = lam_ref[...]
        lam = (jnp.exp(jnp.sum(lv[0:1] * lv[1:2], axis=1, keepdims=True))
               - jnp.exp(jnp.sum(lv[2:3] * lv[3:4], axis=1, keepdims=True)) + lambda_init)
        o0 = acc_scr[0:tq, :] / l_scr[0:tq, :]
        o1 = acc_scr[tq:2 * tq, :] / l_scr[tq:2 * tq, :]
        a = o0 - lam * o1
        y = a * lax.rsqrt(jnp.mean(a * a, axis=-1, keepdims=True) + EPS) * g_ref[0]
        o_ref[0] = (y * (1.0 - lambda_init)).astype(o_ref.dtype)


def _diff_attn(q, k, v, lam_vec, head_g, lambda_init, *, tq=512):
    bsz, s, _ = q.shape
    tq = min(tq, s)
    nq = s // tq
    return pl.pallas_call(
        functools.partial(_diff_attn_kernel, tq=tq, tk=tq, lambda_init=lambda_init),
        grid=(bsz, DIFF_HEADS, nq, nq),
        in_specs=[pl.BlockSpec((4, DIFF_DQK), lambda b, h, i, j: (0, 0)),
                  pl.BlockSpec((1, tq, LANES), lambda b, h, i, j: (b, i, h)),
                  pl.BlockSpec((1, tq, LANES), lambda b, h, i, j: (b, jnp.minimum(j, i), h)),
                  pl.BlockSpec((1, tq, LANES), lambda b, h, i, j: (b, jnp.minimum(j, i), h)),
                  pl.BlockSpec((1, 1, DIFF_DV), lambda b, h, i, j: (h, 0, 0))],
        out_specs=pl.BlockSpec((1, tq, DIFF_DV), lambda b, h, i, j: (b, i, h)),
        out_shape=jax.ShapeDtypeStruct((bsz, s, B_V), BF16),
        scratch_shapes=[pltpu.VMEM((2 * tq, LANES), BF16),
                        pltpu.VMEM((2 * tq, 1), F32),
                        pltpu.VMEM((2 * tq, 1), F32),
                        pltpu.VMEM((2 * tq, DIFF_DV), F32)],
        compiler_params=_cparams("parallel", "parallel", "parallel", "arbitrary"),
        name="diff_attn",
    )(lam_vec, q, k, v, head_g.reshape(DIFF_HEADS, 1, DIFF_DV))


def _moba_prep_kernel(pos_ref, inv_ref, lo_ref, hi_ref, q_ref, k_ref, v_ref,
                      qo_ref, ko_ref, vo_ref, sel_ref, kmt_scr, qf_scr, *, nb):
    i = pl.program_id(1)
    tb = MOBA_BLOCK
    half = MOBA_DH // 8
    scale = MOBA_DH ** -0.5

    @pl.when(i == 0)
    def _():
        kmt_scr[...] = jnp.zeros_like(kmt_scr)

    cs, sa, sb = _rope_tables(pos_ref, inv_ref, lo_ref, hi_ref)
    kmeans = []
    for h in range(MOBA_HEADS):
        sl = slice(h * LANES, (h + 1) * LANES)
        qr = _rope_apply(q_ref[0, :, sl], cs, sa, sb, half)
        kr = _rope_apply(k_ref[0, :, sl], cs, sa, sb, half)
        qf_scr[:, sl] = qr
        qo_ref[0, :, sl] = (qr * scale).astype(BF16)
        ko_ref[0, :, sl] = kr.astype(BF16)
        kmeans.append(jnp.mean(kr, axis=0, keepdims=True))
    vo_ref[0] = v_ref[0].astype(BF16)

    gate = _dot_nt(qf_scr[...], kmt_scr[...], precision=HIGHEST)
    lane = lax.broadcasted_iota(I32, (tb, LANES), 1)
    n = lane % nb
    valid = n < i
    gm = jnp.where(valid, gate, -jnp.inf)
    rank = jnp.zeros((tb, LANES), I32)
    for d in range(1, nb):
        wraps = (n + d) >= nb
        partner = jnp.where(wraps, pltpu.roll(gm, nb - d, 1), pltpu.roll(gm, LANES - d, 1))
        pn = jnp.where(wraps, n + d - nb, n + d)
        beats = (partner > gm) | ((partner == gm) & (pn < n))
        rank = rank + beats.astype(I32)
    sel_ref[0] = jnp.where(valid & (rank < MOBA_TOPK), 1.0, 0.0).astype(F32)

    for h in range(MOBA_HEADS):
        _store_row(kmt_scr, h * nb + i, kmeans[h], slice(h * LANES, (h + 1) * LANES))


def _moba_prep(proj, pos3):
    bsz, s, _ = proj.shape
    tb = MOBA_BLOCK
    nb = s // tb
    assert s % tb == 0 and MOBA_HEADS * nb <= LANES and nb >= MOBA_TOPK
    inv, lo, hi = _rope_consts(MOBA_DH)
    cst = pl.BlockSpec((1, LANES), lambda b, i: (0, 0))
    osd = jax.ShapeDtypeStruct((bsz, s, C_W), BF16)
    return pl.pallas_call(
        functools.partial(_moba_prep_kernel, nb=nb),
        grid=(bsz, nb),
        in_specs=[pl.BlockSpec((1, tb, 1), lambda b, i: (b, i, 0)), cst, cst, cst,
                  pl.BlockSpec((1, tb, C_W), lambda b, i: (b, i, 0)),
                  pl.BlockSpec((1, tb, C_W), lambda b, i: (b, i, 1)),
                  pl.BlockSpec((1, tb, C_W), lambda b, i: (b, i, 2))],
        out_specs=[pl.BlockSpec((1, tb, C_W), lambda b, i: (b, i, 0))] * 3
        + [pl.BlockSpec((1, tb, LANES), lambda b, i: (b, i, 0))],
        out_shape=[osd, osd, osd, jax.ShapeDtypeStruct((bsz, s, LANES), F32)],
        scratch_shapes=[pltpu.VMEM((LANES, C_W), F32), pltpu.VMEM((tb, C_W), F32)],
        compiler_params=_cparams("parallel", "arbitrary"),
        name="moba_prep",
    )(pos3, inv, lo, hi, proj, proj, proj)


def _moba_attn_kernel(q_ref, k_ref, v_ref, sel_ref, o_ref, m_scr, l_scr, acc_scr, *, nb):
    h = pl.program_id(1)
    i = pl.program_id(2)
    j = pl.program_id(3)
    tb = MOBA_BLOCK

    @pl.when(j == 0)
    def _():
        s = _dot_nt(q_ref[0], k_ref[0])
        row = lax.broadcasted_iota(I32, (tb, tb), 0)
        col = lax.broadcasted_iota(I32, (tb, tb), 1)
        s = jnp.where(col <= row, s, -jnp.inf)
        m = jnp.max(s, axis=1, keepdims=True)
        p = jnp.exp(s - m)
        m_scr[...] = m
        l_scr[...] = jnp.sum(p, axis=1, keepdims=True)
        acc_scr[...] = jnp.dot(p.astype(BF16), v_ref[0], preferred_element_type=F32)

    @pl.when((j > 0) & (j <= i))
    def _():
        lane = lax.broadcasted_iota(I32, (tb, LANES), 1)
        picked = jnp.max(jnp.where(lane == h * nb + (i - j), sel_ref[0], 0.0), axis=1, keepdims=True)
        s = jnp.where(picked > 0.5, _dot_nt(q_ref[0], k_ref[0]), -jnp.inf)
        m_prev = m_scr[...]
        m_new = jnp.maximum(m_prev, jnp.max(s, axis=1, keepdims=True))
        alpha = jnp.exp(m_prev - m_new)
        p = jnp.exp(s - m_new)
        l_scr[...] = alpha * l_scr[...] + jnp.sum(p, axis=1, keepdims=True)
        acc_scr[...] = alpha * acc_scr[...] + jnp.dot(p.astype(BF16), v_ref[0], preferred_element_type=F32)
        m_scr[...] = m_new

    @pl.when(j == i)
    def _():
        o_ref[0] = (acc_scr[...] / l_scr[...]).astype(o_ref.dtype)


def _moba_attn(q, k, v, sel):
    bsz, s, _ = q.shape
    tb = MOBA_BLOCK
    nb = s // tb
    kv_spec = pl.BlockSpec((1, tb, MOBA_DH), lambda b, h, i, j: (b, jnp.maximum(i - j, 0), h))
    return pl.pallas_call(
        functools.partial(_moba_attn_kernel, nb=nb),
        grid=(bsz, MOBA_HEADS, nb, nb),
        in_specs=[pl.BlockSpec((1, tb, MOBA_DH), lambda b, h, i, j: (b, i, h)), kv_spec, kv_spec,
                  pl.BlockSpec((1, tb, LANES), lambda b, h, i, j: (b, i, 0))],
        out_specs=pl.BlockSpec((1, tb, MOBA_DH), lambda b, h, i, j: (b, i, h)),
        out_shape=jax.ShapeDtypeStruct((bsz, s, C_W), BF16),
        scratch_shapes=[pltpu.VMEM((tb, 1), F32), pltpu.VMEM((tb, 1), F32), pltpu.VMEM((tb, MOBA_DH), F32)],
        compiler_params=_cparams("parallel", "parallel", "parallel", "arbitrary"),
        name="moba_attn",
    )(q, k, v, sel)


def _sgu_kernel(u_ref, v_ref, lng_ref, lnb_ref, w_ref, bt_ref, o_ref, *, ts):
    L = SGU_CHUNK
    r_i = lax.broadcasted_iota(I32, (L, L), 0)
    c_i = lax.broadcasted_iota(I32, (L, L), 1)
    tril = c_i <= r_i
    for g in range(SGU_GROUPS):
        sl = slice(g * SGU_CH, (g + 1) * SGU_CH)
        w = jnp.where(tril, w_ref[g], 0.0).astype(BF16)
        bias = bt_ref[:, g:g + 1]
        for c in range(ts // L):
            rows = slice(c * L, (c + 1) * L)
            vv = _gelu_tanh(v_ref[0, rows, sl])
            mu = jnp.mean(vv, axis=-1, keepdims=True)
            dv = vv - mu
            var = jnp.mean(dv * dv, axis=-1, keepdims=True)
            vn = dv * lax.rsqrt(var + EPS) * lng_ref[:, sl] + lnb_ref[:, sl]
            mixed = jnp.dot(w, vn.astype(BF16), preferred_element_type=F32) + bias
            o_ref[0, rows, sl] = (_gelu_tanh(u_ref[0, rows, sl]) * mixed).astype(o_ref.dtype)


def _sgu(proj, ln_g, ln_b, w_s, b_s, ts=256):
    bsz, s, _ = proj.shape
    ts = min(ts, s)
    base = 3 * C_W // D_W
    return pl.pallas_call(
        functools.partial(_sgu_kernel, ts=ts),
        grid=(bsz, s // ts),
        in_specs=[pl.BlockSpec((1, ts, D_W), lambda b, i: (b, i, base)),
                  pl.BlockSpec((1, ts, D_W), lambda b, i: (b, i, base + 1)),
                  pl.BlockSpec((1, D_W), lambda b, i: (0, 0)),
                  pl.BlockSpec((1, D_W), lambda b, i: (0, 0)),
                  pl.BlockSpec((SGU_GROUPS, SGU_CHUNK, SGU_CHUNK), lambda b, i: (0, 0, 0)),
                  pl.BlockSpec((SGU_CHUNK, SGU_GROUPS), lambda b, i: (0, 0))],
        out_specs=pl.BlockSpec((1, ts, D_W), lambda b, i: (b, i, 0)),
        out_shape=jax.ShapeDtypeStruct((bsz, s, D_W), BF16),
        compiler_params=_cparams("parallel", "parallel"),
        name="sgu",
    )(proj, proj, ln_g.reshape(1, -1), ln_b.reshape(1, -1), w_s, b_s.T)


def _topk_rows(s_scr, k, val_scr, idx_scr, payload_scr=None):
    n, t = s_scr.shape
    row = lax.broadcasted_iota(I32, (n, t), 0).astype(F32)

    def body(it, carry):
        s = s_scr[...]
        m = jnp.max(s, axis=0, keepdims=True)
        pos = jnp.min(jnp.where(s == m, row, float(n)), axis=0, keepdims=True)
        hit = row == pos
        _store_row(val_scr, it, m)
        if payload_scr is None:
            _store_row(idx_scr, it, pos)
        else:
            _store_row(idx_scr, it, jnp.max(jnp.where(hit, payload_scr[...], -1.0), axis=0, keepdims=True))
        s_scr[...] = jnp.where(hit, -jnp.inf, s)
        return carry

    val_scr[...] = jnp.zeros_like(val_scr)
    idx_scr[...] = jnp.zeros_like(idx_scr)
    lax.fori_loop(0, k, body, 0)


def _peer_topk_kernel(q_ref, keys_ref, idx_ref, g_ref, s_scr, v1, i1, v2, i2, cand, cidx, vt, it):
    K = PEER_TOPK
    half = PEER_DQ // 2
    for c, (vs, ix) in enumerate(((v1, i1), (v2, i2))):
        s_scr[...] = _dot_nt(keys_ref[0, c], q_ref[:, c * half:(c + 1) * half], precision=HIGHEST)
        _topk_rows(s_scr, K, vs, ix)
    for a in range(K):
        cand[a * K:(a + 1) * K, :] = v1[a:a + 1, :] + v2[...]
        cidx[a * K:(a + 1) * K, :] = i1[a:a + 1, :] * float(PEER_NKEYS) + i2[...]
    _topk_rows(cand, K, vt, it, payload_scr=cidx)
    ts = vt[...]
    e = jnp.exp(ts - jnp.max(ts, axis=0, keepdims=True))
    g_ref[0] = e / jnp.sum(e, axis=0, keepdims=True)
    idx_ref[0] = it[...].astype(I32)


def _peer_topk(q2, sub_keys, tt=512):
    n_tok = q2.shape[0]
    tt = min(tt, n_tok)
    K = PEER_TOPK
    half = PEER_DQ // 2
    vs = lambda r, dt: pltpu.VMEM((r, tt), dt)
    return pl.pallas_call(
        _peer_topk_kernel,
        grid=(n_tok // tt, PEER_HEADS),
        in_specs=[pl.BlockSpec((tt, PEER_DQ), lambda i, h: (i, h)),
                  pl.BlockSpec((1, 2, PEER_NKEYS, half), lambda i, h: (h, 0, 0, 0))],
        out_specs=[pl.BlockSpec((1, K, tt), lambda i, h: (h, 0, i)),
                   pl.BlockSpec((1, K, tt), lambda i, h: (h, 0, i))],
        out_shape=[jax.ShapeDtypeStruct((PEER_HEADS, K, n_tok), I32),
                   jax.ShapeDtypeStruct((PEER_HEADS, K, n_tok), F32)],
        scratch_shapes=[vs(PEER_NKEYS, F32), vs(K, F32), vs(K, F32), vs(K, F32), vs(K, F32),
                        vs(K * K, F32), vs(K * K, F32), vs(K, F32), vs(K, F32)],
        compiler_params=_cparams("parallel", "parallel"),
        name="peer_topk",
    )(q2, sub_keys)


def _peer_gather_kernel(idx_ref, gate_ref, x_ref, h_ref, g2_ref, u_hbm, v_hbm, o_ref,
                        ubuf, vbuf, sem, *, tt):
    R = PEER_SEL

    def row_copies(t, j, slot):
        e = idx_ref[t, j]
        return (pltpu.make_async_copy(u_hbm.at[pl.ds(e, 1)], ubuf.at[slot, pl.ds(j, 1)], sem.at[0, slot]),
                pltpu.make_async_copy(v_hbm.at[pl.ds(e, 1)], vbuf.at[slot, pl.ds(j, 1)], sem.at[1, slot]))

    def issue(t, slot):
        def body(j, carry):
            cu, cv = row_copies(t, j, slot)
            cu.start()
            cv.start()
            return carry
        lax.fori_loop(0, R, body, 0, unroll=8)

    def wait(slot):
        pltpu.make_async_copy(u_hbm.at[pl.ds(0, R)], ubuf.at[slot], sem.at[0, slot]).wait()
        pltpu.make_async_copy(v_hbm.at[pl.ds(0, R)], vbuf.at[slot], sem.at[1, slot]).wait()

    eye = lax.broadcasted_iota(I32, (R, R), 0) == lax.broadcasted_iota(I32, (R, R), 1)
    issue(0, 0)

    sub = lax.broadcasted_iota(I32, (SUBLANES, x_ref.shape[1]), 0)

    def group(gi, carry):
        base = pl.multiple_of(gi * SUBLANES, SUBLANES)
        xg = x_ref[pl.ds(base, SUBLANES), :]
        gg = gate_ref[pl.ds(base, SUBLANES), :]
        mixes = jnp.zeros(xg.shape, F32)
        for r in range(SUBLANES):
            t = base + r
            slot = r % 2

            @pl.when(t + 1 < tt)
            def _():
                issue(t + 1, 1 - slot)

            wait(slot)
            act = jnp.sum(ubuf[slot] * xg[r:r + 1, :], axis=1, keepdims=True)
            gcol = jnp.sum(jnp.where(eye, gg[r:r + 1, :], 0.0), axis=1, keepdims=True)
            w = gcol * _gelu_tanh(act)
            mix = jnp.sum(vbuf[slot] * w, axis=0, keepdims=True)
            mixes = jnp.where(sub == r, mix, mixes)
        o_ref[pl.ds(base, SUBLANES), :] = h_ref[pl.ds(base, SUBLANES), :] + g2_ref[0] * mixes
        return carry

    lax.fori_loop(0, tt // SUBLANES, group, 0)


def _peer_gather(idx, gates, hn2, h2, g2, u, v, s, tt=64):
    n_tok, d = hn2.shape
    tt = min(tt, s)
    assert s % tt == 0
    per_b = s // tt
    R = PEER_SEL
    return pl.pallas_call(
        functools.partial(_peer_gather_kernel, tt=tt),
        grid=(n_tok // tt,),
        in_specs=[pl.BlockSpec((tt, R), lambda i: (i, 0), memory_space=pltpu.SMEM),
                  pl.BlockSpec((tt, R), lambda i: (i, 0)),
                  pl.BlockSpec((tt, d), lambda i: (i, 0)),
                  pl.BlockSpec((tt, d), lambda i: (i, 0)),
                  pl.BlockSpec((1, 1, d), lambda i: (i // per_b, 0, 0)),
                  pl.BlockSpec(memory_space=pl.ANY),
                  pl.BlockSpec(memory_space=pl.ANY)],
        out_specs=pl.BlockSpec((tt, d), lambda i: (i, 0)),
        out_shape=jax.ShapeDtypeStruct((n_tok, d), F32),
        scratch_shapes=[pltpu.VMEM((2, R, d), F32), pltpu.VMEM((2, R, d), F32),
                        pltpu.SemaphoreType.DMA((2, 2))],
        compiler_params=_cparams("arbitrary"),
        name="peer_gather",
    )(idx, gates, hn2, h2, g2.reshape(-1, 1, d), u, v)


def _peer_layer(h, g, sc, sh, g2, w_q, sub_keys, u, v):
    bsz, s, d = h.shape
    q, hn = _nm_matmul(h, g, sc, sh, w_q.astype(BF16), tm=512, tn=1024, emit_hn=True, name="peer_query")
    n_tok = bsz * s
    idx, gates = _peer_topk(q.reshape(n_tok, -1), sub_keys)
    idx = idx.transpose(2, 0, 1).reshape(n_tok, PEER_SEL)
    gates = gates.transpose(2, 0, 1).reshape(n_tok, PEER_SEL)
    out = _peer_gather(idx, gates, hn.reshape(n_tok, d), h.reshape(n_tok, d), g2, u, v, s)
    return out.reshape(bsz, s, d)


def _rmsnorm_kernel(x_ref, g_ref, o_ref):
    x = x_ref[...]
    o_ref[...] = x * lax.rsqrt(jnp.mean(x * x, axis=-1, keepdims=True) + EPS) * g_ref[...]


def _final_norm(h, g, tm=512):
    bsz, s, d = h.shape
    n = bsz * s
    tm = min(tm, n)
    out = pl.pallas_call(
        _rmsnorm_kernel,
        grid=(n // tm,),
        in_specs=[pl.BlockSpec((tm, d), lambda i: (i, 0)), pl.BlockSpec((1, d), lambda i: (0, 0))],
        out_specs=pl.BlockSpec((tm, d), lambda i: (i, 0)),
        out_shape=jax.ShapeDtypeStruct((n, d), F32),
        compiler_params=_cparams("parallel"),
        name="final_norm",
    )(h.reshape(n, d), g.reshape(1, d))
    return out.reshape(bsz, s, d)


def _even_mixer(h, pos3, g, sc, sh, g1, layer, w_in, w_out, conv_w, conv_b, gate_b, mlstm_g, lam_vec, diff_g):
    lambda_init = 0.8 - 0.6 * math.exp(-0.3 * layer)
    d = h.shape[-1]
    cuts = np.cumsum([2 * A_QK, A_V, A_V, 2 * MLSTM_HEADS, B_QK, B_QK, B_V])
    n_gate = 2 * MLSTM_HEADS
    w_cat = jnp.concatenate([w_in[:, :cuts[2]], w_in[:, cuts[3]:], w_in[:, cuts[2]:cuts[3]],
                             jnp.zeros((d, LANES - n_gate), w_in.dtype)], axis=1).astype(BF16)
    proj = _nm_matmul(h, g, sc, sh, w_cat, tm=512, tn=896, name="even_in_proj")
    gates_t = proj[:, :, w_cat.shape[1] - LANES:w_cat.shape[1] - LANES + n_gate].transpose(0, 2, 1)
    h_a = _mlstm(proj, gates_t, gate_b, conv_w, conv_b, mlstm_g)
    bq, bk, bv = _diff_prep(proj, pos3)
    h_b = _diff_attn(bq, bk, bv, lam_vec, diff_g, lambda_init)
    return _out_proj(h_a, h_b, w_out, h, g1)


def _odd_mixer(h, pos3, g, sc, sh, g1, w_in, w_out, ln_g, ln_b, sgu_w, sgu_b):
    proj = _nm_matmul(h, g, sc, sh, w_in.astype(BF16), tm=512, tn=1024, name="odd_in_proj")
    cq, ck, cv, sel = _moba_prep(proj, pos3)
    h_c = _moba_attn(cq, ck, cv, sel)
    h_d = _sgu(proj, ln_g, ln_b, sgu_w, sgu_b)
    return _out_proj(h_c, h_d, w_out, h, g1)


def kernel(x, c, positions, mod_w, mod_b, norm_g, final_g, even_w_in, even_w_out, mlstm_conv_w, mlstm_conv_b, mlstm_gate_b, mlstm_head_g, diff_lambda, diff_head_g, odd_w_in, odd_w_out, sgu_ln_g, sgu_ln_b, sgu_w, sgu_b, peer_w_q, peer_sub_keys, peer_u, peer_v):
    depth = mod_w.shape[0]
    bsz, s, d = x.shape
    mod = _modulation(c, mod_w, mod_b)
    pos3 = positions.reshape(bsz, s, 1)
    h = x
    for layer in range(depth):
        sh1, sc1, g1, sh2, sc2, g2 = [mod[layer, :, i * d:(i + 1) * d] for i in range(6)]
        if layer % 2 == 0:
            e = layer // 2
            h = _even_mixer(h, pos3, norm_g[layer, 0], sc1, sh1, g1, layer, even_w_in[e], even_w_out[e],
                            mlstm_conv_w[e], mlstm_conv_b[e], mlstm_gate_b[e], mlstm_head_g[e],
                            diff_lambda[e], diff_head_g[e])
        else:
            o = layer // 2
            h = _odd_mixer(h, pos3, norm_g[layer, 0], sc1, sh1, g1, odd_w_in[o], odd_w_out[o],
                           sgu_ln_g[o], sgu_ln_b[o], sgu_w[o], sgu_b[o])
        h = _peer_layer(h, norm_g[layer, 1], sc2, sh2, g2, peer_w_q[layer], peer_sub_keys[layer],
                        peer_u[layer], peer_v[layer])
    return _final_norm(h, final_g)
```

```python
import functools
import math

import numpy as np
import jax
import jax.numpy as jnp
from jax import lax
from jax.experimental import pallas as pl
from jax.experimental.pallas import tpu as pltpu

F32 = jnp.float32
BF16 = jnp.bfloat16
I32 = jnp.int32
HIGHEST = lax.Precision.HIGHEST

EPS = 1e-6
ROPE_THETA = 500000.0
LANES = 128
SUBLANES = 8
VMEM_LIMIT_BYTES = 48 * 1024 * 1024

MLSTM_HEADS, MLSTM_DQK, MLSTM_DV, MLSTM_CHUNK, CONV_W = 4, 128, 256, 128, 4
DIFF_HEADS, DIFF_DQK, DIFF_DV = 8, 64, 128
MOBA_HEADS, MOBA_DH, MOBA_BLOCK, MOBA_TOPK = 8, 128, 256, 3
SGU_GROUPS, SGU_CH, SGU_CHUNK = 8, 128, 128
PEER_HEADS, PEER_NKEYS, PEER_DQ, PEER_TOPK = 8, 128, 256, 16
PEER_SEL = PEER_HEADS * PEER_TOPK

A_QK = MLSTM_HEADS * MLSTM_DQK
A_V = MLSTM_HEADS * MLSTM_DV
B_QK = DIFF_HEADS * 2 * DIFF_DQK
B_V = DIFF_HEADS * DIFF_DV
C_W = MOBA_HEADS * MOBA_DH
D_W = SGU_GROUPS * SGU_CH


def _cparams(*sem):
    return pltpu.CompilerParams(dimension_semantics=sem, vmem_limit_bytes=VMEM_LIMIT_BYTES)


def _dot_nt(a, b, precision=None):
    return lax.dot_general(a, b, (((1,), (1,)), ((), ())), preferred_element_type=F32, precision=precision)


def _store_row(ref, row, val, cols=slice(None)):
    base = pl.multiple_of((row // SUBLANES) * SUBLANES, SUBLANES)
    blk = ref[pl.ds(base, SUBLANES), cols]
    r = lax.broadcasted_iota(I32, blk.shape, 0)
    ref[pl.ds(base, SUBLANES), cols] = jnp.where(r == row - base, val, blk)


def _sigmoid(x):
    return 1.0 / (1.0 + jnp.exp(-x))


def _silu(x):
    return x * _sigmoid(x)


def _gelu_tanh(x):
    return 0.5 * x * (1.0 + jnp.tanh(math.sqrt(2.0 / math.pi) * (x + 0.044715 * (x * x * x))))


def _log_sigmoid(x):
    return jnp.minimum(x, 0.0) - jnp.log1p(jnp.exp(-jnp.abs(x)))


def _mod_kernel(c_ref, w_ref, b_ref, o_ref):
    ca = _silu(c_ref[...])
    o_ref[0] = jnp.dot(ca, w_ref[0], preferred_element_type=F32, precision=HIGHEST) + b_ref[0]


def _modulation(c, mod_w, mod_b):
    depth, d, n = mod_w.shape
    bsz = c.shape[0]
    rows = 8
    cp = jnp.zeros((rows, d), F32).at[:bsz].set(c)
    tn = 1024
    out = pl.pallas_call(
        _mod_kernel,
        grid=(depth, n // tn),
        in_specs=[pl.BlockSpec((rows, d), lambda l, j: (0, 0)),
                  pl.BlockSpec((1, d, tn), lambda l, j: (l, 0, j)),
                  pl.BlockSpec((1, 1, tn), lambda l, j: (l, 0, j))],
        out_specs=pl.BlockSpec((1, rows, tn), lambda l, j: (l, 0, j)),
        out_shape=jax.ShapeDtypeStruct((depth, rows, n), F32),
        compiler_params=_cparams("parallel", "parallel"),
        name="modulation",
    )(cp, mod_w, mod_b.reshape(depth, 1, n))
    return out[:, :bsz]


def _nm_matmul_kernel(h_ref, g_ref, sc_ref, sh_ref, w_ref, *rest, emit_hn):
    if emit_hn:
        o_ref, hn_ref, hn_scr = rest
    else:
        o_ref, hn_scr = rest

    @pl.when(pl.program_id(2) == 0)
    def _():
        x = h_ref[0]
        y = x * lax.rsqrt(jnp.mean(x * x, axis=-1, keepdims=True) + EPS) * g_ref[...]
        y = y * (1.0 + sc_ref[0]) + sh_ref[0]
        hn_scr[...] = y.astype(BF16)
        if emit_hn:
            hn_ref[0] = y

    o_ref[0] = jnp.dot(hn_scr[...], w_ref[...], preferred_element_type=F32).astype(o_ref.dtype)


def _nm_matmul(h, g, sc, sh, w, *, tm, tn, emit_hn=False, name="nm_matmul"):
    bsz, s, d = h.shape
    n = w.shape[1]
    tm = min(tm, s)
    out_shape = [jax.ShapeDtypeStruct((bsz, s, n), F32)]
    out_specs = [pl.BlockSpec((1, tm, tn), lambda b, i, j: (b, i, j))]
    if emit_hn:
        out_shape.append(jax.ShapeDtypeStruct((bsz, s, d), F32))
        out_specs.append(pl.BlockSpec((1, tm, d), lambda b, i, j: (b, i, 0)))
    res = pl.pallas_call(
        functools.partial(_nm_matmul_kernel, emit_hn=emit_hn),
        grid=(bsz, s // tm, n // tn),
        in_specs=[pl.BlockSpec((1, tm, d), lambda b, i, j: (b, i, 0)),
                  pl.BlockSpec((1, d), lambda b, i, j: (0, 0)),
                  pl.BlockSpec((1, 1, d), lambda b, i, j: (b, 0, 0)),
                  pl.BlockSpec((1, 1, d), lambda b, i, j: (b, 0, 0)),
                  pl.BlockSpec((d, tn), lambda b, i, j: (0, j))],
        out_specs=out_specs,
        out_shape=out_shape,
        scratch_shapes=[pltpu.VMEM((tm, d), BF16)],
        compiler_params=_cparams("parallel", "parallel", "arbitrary"),
        name=name,
    )(h, g.reshape(1, d), sc.reshape(bsz, 1, d), sh.reshape(bsz, 1, d), w)
    return res if emit_hn else res[0]


def _out_proj_kernel(xa_ref, xb_ref, wa_ref, wb_ref, h_ref, g_ref, o_ref):
    acc = jnp.dot(xa_ref[0], wa_ref[...], preferred_element_type=F32)
    acc += jnp.dot(xb_ref[0], wb_ref[...], preferred_element_type=F32)
    o_ref[0] = h_ref[0] + g_ref[0] * acc


def _out_proj(xa, xb, w, h, g1, *, tm=512, tn=1024):
    bsz, s, ka = xa.shape
    kb = xb.shape[2]
    n = w.shape[1]
    tm = min(tm, s)
    wa = w[:ka].astype(BF16)
    wb = w[ka:].astype(BF16)
    return pl.pallas_call(
        _out_proj_kernel,
        grid=(bsz, s // tm, n // tn),
        in_specs=[pl.BlockSpec((1, tm, ka), lambda b, i, j: (b, i, 0)),
                  pl.BlockSpec((1, tm, kb), lambda b, i, j: (b, i, 0)),
                  pl.BlockSpec((ka, tn), lambda b, i, j: (0, j)),
                  pl.BlockSpec((kb, tn), lambda b, i, j: (0, j)),
                  pl.BlockSpec((1, tm, tn), lambda b, i, j: (b, i, j)),
                  pl.BlockSpec((1, 1, tn), lambda b, i, j: (b, 0, j))],
        out_specs=pl.BlockSpec((1, tm, tn), lambda b, i, j: (b, i, j)),
        out_shape=jax.ShapeDtypeStruct((bsz, s, n), F32),
        compiler_params=_cparams("parallel", "parallel", "parallel"),
        name="out_proj",
    )(xa, xb, wa, wb, h, g1.reshape(bsz, 1, n))


def _mlstm_kernel(gb_ref, aqk_ref, v_ref, ao_ref, gcol_ref, grow_ref, cw_ref, cb_ref, hg_ref,
                  o_ref, xbuf, c_scr, n_scr, m_scr):
    L = MLSTM_CHUNK
    H, DQK, DV = MLSTM_HEADS, MLSTM_DQK, MLSTM_DV
    c = pl.program_id(1)

    @pl.when(c == 0)
    def _():
        xbuf[0:8, :] = jnp.zeros((8, 2 * A_QK), F32)
        c_scr[...] = jnp.zeros_like(c_scr)
        n_scr[...] = jnp.zeros_like(n_scr)
        m_scr[...] = jnp.zeros_like(m_scr)

    @pl.when(c > 0)
    def _():
        xbuf[0:8, :] = xbuf[L:L + 8, :]

    xbuf[8:L + 8, :] = aqk_ref[0]
    conv = cb_ref[...] + cw_ref[0:1, :] * xbuf[5:5 + L, :]
    for j in range(1, CONV_W):
        conv = conv + cw_ref[j:j + 1, :] * xbuf[5 + j:5 + j + L, :]
    qk = _silu(conv)

    r_i = lax.broadcasted_iota(I32, (L, L), 0)
    c_i = lax.broadcasted_iota(I32, (L, L), 1)
    tril = c_i <= r_i
    gcol = gcol_ref[0]
    grow = grow_ref[0]

    for h in range(H):
        q = qk[:, h * DQK:(h + 1) * DQK] * (DQK ** -0.5)
        k = qk[:, A_QK + h * DQK:A_QK + (h + 1) * DQK]
        v = v_ref[0, :, h * DV:(h + 1) * DV]
        i_col = gcol[:, h:h + 1] + gb_ref[h]
        i_row = grow[h:h + 1, :] + gb_ref[h]
        lf_col = _log_sigmoid(gcol[:, H + h:H + h + 1] + gb_ref[H + h])
        lf_row = _log_sigmoid(grow[H + h:H + h + 1, :] + gb_ref[H + h])
        bcum_col = jnp.sum(jnp.where(tril, lf_row, 0.0), axis=1, keepdims=True)
        bcum_row = jnp.sum(jnp.where(r_i <= c_i, lf_col, 0.0), axis=0, keepdims=True)
        g_tot = jnp.sum(lf_row, axis=1, keepdims=True)
        m_prev = m_scr[h:h + 1, 0:1]
        d_intra = jnp.where(tril, bcum_col - bcum_row + i_row, -jnp.inf)
        d_inter = bcum_col + m_prev
        m_t = jnp.maximum(d_inter, jnp.max(d_intra, axis=1, keepdims=True))
        w_intra = jnp.exp(d_intra - m_t)
        w_inter = jnp.exp(d_inter - m_t)
        qb = q.astype(BF16)
        kb = k.astype(BF16)
        vb = v.astype(BF16)
        s = _dot_nt(qb, kb) * w_intra
        c_prev = c_scr[h]
        n_prev = n_scr[h:h + 1, :]
        num = jnp.dot(s.astype(BF16), vb, preferred_element_type=F32)
        num = num + w_inter * jnp.dot(qb, c_prev.astype(BF16), preferred_element_type=F32)
        den = jnp.sum(s, axis=1, keepdims=True) + w_inter * jnp.sum(q * n_prev, axis=1, keepdims=True)
        hh = num / jnp.maximum(jnp.abs(den), jnp.exp(-m_t))
        d_state = g_tot - bcum_col + i_col
        m_new = jnp.maximum(g_tot + m_prev, jnp.max(d_state, axis=0, keepdims=True))
        w_s = jnp.exp(d_state - m_new)
        w_c = jnp.exp(g_tot + m_prev - m_new)
        kw = k * w_s
        c_scr[h] = w_c * c_prev + jnp.dot(kw.T.astype(BF16), vb, preferred_element_type=F32)
        n_scr[h:h + 1, :] = w_c * n_prev + jnp.sum(kw, axis=0, keepdims=True)
        m_scr[h:h + 1, 0:1] = m_new
        y = hh * lax.rsqrt(jnp.mean(hh * hh, axis=-1, keepdims=True) + EPS) * hg_ref[:, h * DV:(h + 1) * DV]
        y = y * _sigmoid(ao_ref[0, :, h * DV:(h + 1) * DV])
        o_ref[0, :, h * DV:(h + 1) * DV] = y.astype(o_ref.dtype)


def _mlstm(proj, gates_t, gate_b, conv_w, conv_b, head_g):
    bsz, s, _ = proj.shape
    L = MLSTM_CHUNK
    gate_blk = (2 * A_QK + 2 * A_V + 2 * B_QK + B_V) // LANES
    return pl.pallas_call(
        _mlstm_kernel,
        grid=(bsz, s // L),
        in_specs=[pl.BlockSpec(memory_space=pltpu.SMEM),
                  pl.BlockSpec((1, L, 2 * A_QK), lambda b, c: (b, c, 0)),
                  pl.BlockSpec((1, L, A_V), lambda b, c: (b, c, 1)),
                  pl.BlockSpec((1, L, A_V), lambda b, c: (b, c, 2)),
                  pl.BlockSpec((1, L, LANES), lambda b, c: (b, c, gate_blk)),
                  pl.BlockSpec((1, 2 * MLSTM_HEADS, L), lambda b, c: (b, 0, c)),
                  pl.BlockSpec((CONV_W, 2 * A_QK), lambda b, c: (0, 0)),
                  pl.BlockSpec((1, 2 * A_QK), lambda b, c: (0, 0)),
                  pl.BlockSpec((1, A_V), lambda b, c: (0, 0))],
        out_specs=pl.BlockSpec((1, L, A_V), lambda b, c: (b, c, 0)),
        out_shape=jax.ShapeDtypeStruct((bsz, s, A_V), BF16),
        scratch_shapes=[pltpu.VMEM((L + 8, 2 * A_QK), F32),
                        pltpu.VMEM((MLSTM_HEADS, MLSTM_DQK, MLSTM_DV), F32),
                        pltpu.VMEM((8, MLSTM_DQK), F32),
                        pltpu.VMEM((8, LANES), F32)],
        compiler_params=_cparams("parallel", "arbitrary"),
        name="mlstm",
    )(gate_b.reshape(-1), proj, proj, proj, proj, gates_t, conv_w, conv_b.reshape(1, -1),
      head_g.reshape(1, -1))


def _rope_consts(dh):
    half = dh // 8
    lane = np.arange(LANES) % dh
    inv_half = ROPE_THETA ** (-jnp.arange(half, dtype=F32) / half)
    inv = jnp.where(jnp.asarray(lane < 2 * half), inv_half[lane % half], 0.0)
    lo = (lane < half).astype(np.float32)
    hi = ((lane >= half) & (lane < 2 * half)).astype(np.float32)
    return (inv.astype(F32).reshape(1, LANES), jnp.asarray(lo).reshape(1, LANES),
            jnp.asarray(hi).reshape(1, LANES))


def _rope_tables(pos_ref, inv_ref, lo_ref, hi_ref):
    ang = pos_ref[0].astype(F32) * inv_ref[...]
    cs = jnp.cos(ang)
    sn = jnp.sin(ang)
    return cs, -sn * lo_ref[...], sn * hi_ref[...]


def _rope_apply(x, cs, sa, sb, half):
    return x * cs + pltpu.roll(x, LANES - half, 1) * sa + pltpu.roll(x, half, 1) * sb


def _diff_prep_kernel(pos_ref, inv_ref, lo_ref, hi_ref, q_ref, k_ref, v_ref, qo_ref, ko_ref, vo_ref):
    cs, sa, sb = _rope_tables(pos_ref, inv_ref, lo_ref, hi_ref)
    half = DIFF_DQK // 8
    scale = DIFF_DQK ** -0.5
    for h in range(DIFF_HEADS):
        sl = slice(h * LANES, (h + 1) * LANES)
        qo_ref[0, :, sl] = (_rope_apply(q_ref[0, :, sl], cs, sa, sb, half) * scale).astype(BF16)
        ko_ref[0, :, sl] = _rope_apply(k_ref[0, :, sl], cs, sa, sb, half).astype(BF16)
    vo_ref[0] = v_ref[0].astype(BF16)


def _diff_prep(proj, pos3, ts=512):
    bsz, s, _ = proj.shape
    ts = min(ts, s)
    inv, lo, hi = _rope_consts(DIFF_DQK)
    base = (2 * A_QK + 2 * A_V) // B_QK
    cst = pl.BlockSpec((1, LANES), lambda b, i: (0, 0))
    osd = jax.ShapeDtypeStruct((bsz, s, B_QK), BF16)
    return pl.pallas_call(
        _diff_prep_kernel,
        grid=(bsz, s // ts),
        in_specs=[pl.BlockSpec((1, ts, 1), lambda b, i: (b, i, 0)), cst, cst, cst,
                  pl.BlockSpec((1, ts, B_QK), lambda b, i: (b, i, base)),
                  pl.BlockSpec((1, ts, B_QK), lambda b, i: (b, i, base + 1)),
                  pl.BlockSpec((1, ts, B_V), lambda b, i: (b, i, base + 2))],
        out_specs=[pl.BlockSpec((1, ts, B_QK), lambda b, i: (b, i, 0))] * 3,
        out_shape=[osd, osd, osd],
        compiler_params=_cparams("parallel", "parallel"),
        name="diff_prep",
    )(pos3, inv, lo, hi, proj, proj, proj)


def _diff_attn_kernel(lam_ref, q_ref, k_ref, v_ref, g_ref, o_ref, qs_scr, m_scr, l_scr, acc_scr,
                      *, tq, tk, lambda_init):
    i = pl.program_id(2)
    j = pl.program_id(3)

    @pl.when(j == 0)
    def _():
        q = q_ref[0]
        lane = lax.broadcasted_iota(I32, q.shape, 1)
        zero = jnp.zeros_like(q)
        qs_scr[0:tq, :] = jnp.where(lane < DIFF_DQK, q, zero)
        qs_scr[tq:2 * tq, :] = jnp.where(lane >= DIFF_DQK, q, zero)
        m_scr[...] = jnp.full_like(m_scr, -jnp.inf)
        l_scr[...] = jnp.zeros_like(l_scr)
        acc_scr[...] = jnp.zeros_like(acc_scr)

    def step(masked):
        s = _dot_nt(qs_scr[...], k_ref[0])
        if masked:
            row = lax.broadcasted_iota(I32, (2 * tq, tk), 0)
            row = jnp.where(row >= tq, row - tq, row)
            col = lax.broadcasted_iota(I32, (2 * tq, tk), 1)
            s = jnp.where((j * tk + col) <= (i * tq + row), s, -jnp.inf)
        m_prev = m_scr[...]
        m_new = jnp.maximum(m_prev, jnp.max(s, axis=1, keepdims=True))
        alpha = jnp.exp(m_prev - m_new)
        p = jnp.exp(s - m_new)
        l_scr[...] = alpha * l_scr[...] + jnp.sum(p, axis=1, keepdims=True)
        acc_scr[...] = alpha * acc_scr[...] + jnp.dot(p.astype(BF16), v_ref[0], preferred_element_type=F32)
        m_scr[...] = m_new

    @pl.when(j < i)
    def _():
        step(False)

    @pl.when(j == i)
    def _():
        step(True)
        lv = lam_ref[...]
        lam = (jnp.exp(jnp.sum(lv[0:1] * lv[1:2], axis=1, keepdims=True))
               - jnp.exp(jnp.sum(lv[2:3] * lv[3:4], axis=1, keepdims=True)) + lambda_init)
        o0 = acc_scr[0:tq, :] / l_scr[0:tq, :]
        o1 = acc_scr[tq:2 * tq, :] / l_scr[tq:2 * tq, :]
        a = o0 - lam * o1
        y = a * lax.rsqrt(jnp.mean(a * a, axis=-1, keepdims=True) + EPS) * g_ref[0]
        o_ref[0] = (y * (1.0 - lambda_init)).astype(o_ref.dtype)


def _diff_attn(q, k, v, lam_vec, head_g, lambda_init, *, tq=512):
    bsz, s, _ = q.shape
    tq = min(tq, s)
    nq = s // tq
    return pl.pallas_call(
        functools.partial(_diff_attn_kernel, tq=tq, tk=tq, lambda_init=lambda_init),
        grid=(bsz, DIFF_HEADS, nq, nq),
        in_specs=[pl.BlockSpec((4, DIFF_DQK), lambda b, h, i, j: (0, 0)),
                  pl.BlockSpec((1, tq, LANES), lambda b, h, i, j: (b, i, h)),
                  pl.BlockSpec((1, tq, LANES), lambda b, h, i, j: (b, jnp.minimum(j, i), h)),
                  pl.BlockSpec((1, tq, LANES), lambda b, h, i, j: (b, jnp.minimum(j, i), h)),
                  pl.BlockSpec((1, 1, DIFF_DV), lambda b, h, i, j: (h, 0, 0))],
        out_specs=pl.BlockSpec((1, tq, DIFF_DV), lambda b, h, i, j: (b, i, h)),
        out_shape=jax.ShapeDtypeStruct((bsz, s, B_V), BF16),
        scratch_shapes=[pltpu.VMEM((2 * tq, LANES), BF16),
                        pltpu.VMEM((2 * tq, 1), F32),
                        pltpu.VMEM((2 * tq, 1), F32),
                        pltpu.VMEM((2 * tq, DIFF_DV), F32)],
        compiler_params=_cparams("parallel", "parallel", "parallel", "arbitrary"),
        name="diff_attn",
    )(lam_vec, q, k, v, head_g.reshape(DIFF_HEADS, 1, DIFF_DV))


def _moba_prep_kernel(pos_ref, inv_ref, lo_ref, hi_ref, q_ref, k_ref, v_ref,
                      qo_ref, ko_ref, vo_ref, sel_ref, kmt_scr, qf_scr, *, nb):
    i = pl.program_id(1)
    tb = MOBA_BLOCK
    half = MOBA_DH // 8
    scale = MOBA_DH ** -0.5

    @pl.when(i == 0)
    def _():
        kmt_scr[...] = jnp.zeros_like(kmt_scr)

    cs, sa, sb = _rope_tables(pos_ref, inv_ref, lo_ref, hi_ref)
    kmeans = []
    for h in range(MOBA_HEADS):
        sl = slice(h * LANES, (h + 1) * LANES)
        qr = _rope_apply(q_ref[0, :, sl], cs, sa, sb, half)
        kr = _rope_apply(k_ref[0, :, sl], cs, sa, sb, half)
        qf_scr[:, sl] = qr
        qo_ref[0, :, sl] = (qr * scale).astype(BF16)
        ko_ref[0, :, sl] = kr.astype(BF16)
        kmeans.append(jnp.mean(kr, axis=0, keepdims=True))
    vo_ref[0] = v_ref[0].astype(BF16)

    gate = _dot_nt(qf_scr[...], kmt_scr[...], precision=HIGHEST)
    lane = lax.broadcasted_iota(I32, (tb, LANES), 1)
    n = lane % nb
    valid = n < i
    gm = jnp.where(valid, gate, -jnp.inf)
    rank = jnp.zeros((tb, LANES), I32)
    for d in range(1, nb):
        wraps = (n + d) >= nb
        partner = jnp.where(wraps, pltpu.roll(gm, nb - d, 1), pltpu.roll(gm, LANES - d, 1))
        pn = jnp.where(wraps, n + d - nb, n + d)
        beats = (partner > gm) | ((partner == gm) & (pn < n))
        rank = rank + beats.astype(I32)
    sel_ref[0] = jnp.where(valid & (rank < MOBA_TOPK), 1.0, 0.0).astype(F32)

    for h in range(MOBA_HEADS):
        _store_row(kmt_scr, h * nb + i, kmeans[h], slice(h * LANES, (h + 1) * LANES))


def _moba_prep(proj, pos3):
    bsz, s, _ = proj.shape
    tb = MOBA_BLOCK
    nb = s // tb
    assert s % tb == 0 and MOBA_HEADS * nb <= LANES and nb >= MOBA_TOPK
    inv, lo, hi = _rope_consts(MOBA_DH)
    cst = pl.BlockSpec((1, LANES), lambda b, i: (0, 0))
    osd = jax.ShapeDtypeStruct((bsz, s, C_W), BF16)
    return pl.pallas_call(
        functools.partial(_moba_prep_kernel, nb=nb),
        grid=(bsz, nb),
        in_specs=[pl.BlockSpec((1, tb, 1), lambda b, i: (b, i, 0)), cst, cst, cst,
                  pl.BlockSpec((1, tb, C_W), lambda b, i: (b, i, 0)),
                  pl.BlockSpec((1, tb, C_W), lambda b, i: (b, i, 1)),
                  pl.BlockSpec((1, tb, C_W), lambda b, i: (b, i, 2))],
        out_specs=[pl.BlockSpec((1, tb, C_W), lambda b, i: (b, i, 0))] * 3
        + [pl.BlockSpec((1, tb, LANES), lambda b, i: (b, i, 0))],
        out_shape=[osd, osd, osd, jax.ShapeDtypeStruct((bsz, s, LANES), F32)],
        scratch_shapes=[pltpu.VMEM((LANES, C_W), F32), pltpu.VMEM((tb, C_W), F32)],
        compiler_params=_cparams("parallel", "arbitrary"),
        name="moba_prep",
    )(pos3, inv, lo, hi, proj, proj, proj)


def _moba_attn_kernel(q_ref, k_ref, v_ref, sel_ref, o_ref, m_scr, l_scr, acc_scr, *, nb):
    h = pl.program_id(1)
    i = pl.program_id(2)
    j = pl.program_id(3)
    tb = MOBA_BLOCK

    @pl.when(j == 0)
    def _():
        s = _dot_nt(q_ref[0], k_ref[0])
        row = lax.broadcasted_iota(I32, (tb, tb), 0)
        col = lax.broadcasted_iota(I32, (tb, tb), 1)
        s = jnp.where(col <= row, s, -jnp.inf)
        m = jnp.max(s, axis=1, keepdims=True)
        p = jnp.exp(s - m)
        m_scr[...] = m
        l_scr[...] = jnp.sum(p, axis=1, keepdims=True)
        acc_scr[...] = jnp.dot(p.astype(BF16), v_ref[0], preferred_element_type=F32)

    @pl.when((j > 0) & (j <= i))
    def _():
        lane = lax.broadcasted_iota(I32, (tb, LANES), 1)
        picked = jnp.max(jnp.where(lane == h * nb + (i - j), sel_ref[0], 0.0), axis=1, keepdims=True)
        s = jnp.where(picked > 0.5, _dot_nt(q_ref[0], k_ref[0]), -jnp.inf)
        m_prev = m_scr[...]
        m_new = jnp.maximum(m_prev, jnp.max(s, axis=1, keepdims=True))
        alpha = jnp.exp(m_prev - m_new)
        p = jnp.exp(s - m_new)
        l_scr[...] = alpha * l_scr[...] + jnp.sum(p, axis=1, keepdims=True)
        acc_scr[...] = alpha * acc_scr[...] + jnp.dot(p.astype(BF16), v_ref[0], preferred_element_type=F32)
        m_scr[...] = m_new

    @pl.when(j == i)
    def _():
        o_ref[0] = (acc_scr[...] / l_scr[...]).astype(o_ref.dtype)


def _moba_attn(q, k, v, sel):
    bsz, s, _ = q.shape
    tb = MOBA_BLOCK
    nb = s // tb
    kv_spec = pl.BlockSpec((1, tb, MOBA_DH), lambda b, h, i, j: (b, jnp.maximum(i - j, 0), h))
    return pl.pallas_call(
        functools.partial(_moba_attn_kernel, nb=nb),
        grid=(bsz, MOBA_HEADS, nb, nb),
        in_specs=[pl.BlockSpec((1, tb, MOBA_DH), lambda b, h, i, j: (b, i, h)), kv_spec, kv_spec,
                  pl.BlockSpec((1, tb, LANES), lambda b, h, i, j: (b, i, 0))],
        out_specs=pl.BlockSpec((1, tb, MOBA_DH), lambda b, h, i, j: (b, i, h)),
        out_shape=jax.ShapeDtypeStruct((bsz, s, C_W), BF16),
        scratch_shapes=[pltpu.VMEM((tb, 1), F32), pltpu.VMEM((tb, 1), F32), pltpu.VMEM((tb, MOBA_DH), F32)],
        compiler_params=_cparams("parallel", "parallel", "parallel", "arbitrary"),
        name="moba_attn",
    )(q, k, v, sel)


def _sgu_kernel(u_ref, v_ref, lng_ref, lnb_ref, w_ref, bt_ref, o_ref, *, ts):
    L = SGU_CHUNK
    r_i = lax.broadcasted_iota(I32, (L, L), 0)
    c_i = lax.broadcasted_iota(I32, (L, L), 1)
    tril = c_i <= r_i
    for g in range(SGU_GROUPS):
        sl = slice(g * SGU_CH, (g + 1) * SGU_CH)
        w = jnp.where(tril, w_ref[g], 0.0).astype(BF16)
        bias = bt_ref[:, g:g + 1]
        for c in range(ts // L):
            rows = slice(c * L, (c + 1) * L)
            vv = _gelu_tanh(v_ref[0, rows, sl])
            mu = jnp.mean(vv, axis=-1, keepdims=True)
            dv = vv - mu
            var = jnp.mean(dv * dv, axis=-1, keepdims=True)
            vn = dv * lax.rsqrt(var + EPS) * lng_ref[:, sl] + lnb_ref[:, sl]
            mixed = jnp.dot(w, vn.astype(BF16), preferred_element_type=F32) + bias
            o_ref[0, rows, sl] = (_gelu_tanh(u_ref[0, rows, sl]) * mixed).astype(o_ref.dtype)


def _sgu(proj, ln_g, ln_b, w_s, b_s, ts=256):
    bsz, s, _ = proj.shape
    ts = min(ts, s)
    base = 3 * C_W // D_W
    return pl.pallas_call(
        functools.partial(_sgu_kernel, ts=ts),
        grid=(bsz, s // ts),
        in_specs=[pl.BlockSpec((1, ts, D_W), lambda b, i: (b, i, base)),
                  pl.BlockSpec((1, ts, D_W), lambda b, i: (b, i, base + 1)),
                  pl.BlockSpec((1, D_W), lambda b, i: (0, 0)),
                  pl.BlockSpec((1, D_W), lambda b, i: (0, 0)),
                  pl.BlockSpec((SGU_GROUPS, SGU_CHUNK, SGU_CHUNK), lambda b, i: (0, 0, 0)),
                  pl.BlockSpec((SGU_CHUNK, SGU_GROUPS), lambda b, i: (0, 0))],
        out_specs=pl.BlockSpec((1, ts, D_W), lambda b, i: (b, i, 0)),
        out_shape=jax.ShapeDtypeStruct((bsz, s, D_W), BF16),
        compiler_params=_cparams("parallel", "parallel"),
        name="sgu",
    )(proj, proj, ln_g.reshape(1, -1), ln_b.reshape(1, -1), w_s, b_s.T)


def _topk_rows(s_scr, k, val_scr, idx_scr, payload_scr=None):
    n, t = s_scr.shape
    row = lax.broadcasted_iota(I32, (n, t), 0).astype(F32)

    def body(it, carry):
        s = s_scr[...]
        m = jnp.max(s, axis=0, keepdims=True)
        pos = jnp.min(jnp.where(s == m, row, float(n)), axis=0, keepdims=True)
        hit = row == pos
        _store_row(val_scr, it, m)
        if payload_scr is None:
            _store_row(idx_scr, it, pos)
        else:
            _store_row(idx_scr, it, jnp.max(jnp.where(hit, payload_scr[...], -1.0), axis=0, keepdims=True))
        s_scr[...] = jnp.where(hit, -jnp.inf, s)
        return carry

    val_scr[...] = jnp.zeros_like(val_scr)
    idx_scr[...] = jnp.zeros_like(idx_scr)
    lax.fori_loop(0, k, body, 0)


def _peer_topk_kernel(q_ref, keys_ref, idx_ref, g_ref, s_scr, v1, i1, v2, i2, cand, cidx, vt, it):
    K = PEER_TOPK
    half = PEER_DQ // 2
    for c, (vs, ix) in enumerate(((v1, i1), (v2, i2))):
        s_scr[...] = _dot_nt(keys_ref[0, c], q_ref[:, c * half:(c + 1) * half], precision=HIGHEST)
        _topk_rows(s_scr, K, vs, ix)
    for a in range(K):
        cand[a * K:(a + 1) * K, :] = v1[a:a + 1, :] + v2[...]
        cidx[a * K:(a + 1) * K, :] = i1[a:a + 1, :] * float(PEER_NKEYS) + i2[...]
    _topk_rows(cand, K, vt, it, payload_scr=cidx)
    ts = vt[...]
    e = jnp.exp(ts - jnp.max(ts, axis=0, keepdims=True))
    g_ref[0] = e / jnp.sum(e, axis=0, keepdims=True)
    idx_ref[0] = it[...].astype(I32)


def _peer_topk(q2, sub_keys, tt=512):
    n_tok = q2.shape[0]
    tt = min(tt, n_tok)
    K = PEER_TOPK
    half = PEER_DQ // 2
    vs = lambda r, dt: pltpu.VMEM((r, tt), dt)
    return pl.pallas_call(
        _peer_topk_kernel,
        grid=(n_tok // tt, PEER_HEADS),
        in_specs=[pl.BlockSpec((tt, PEER_DQ), lambda i, h: (i, h)),
                  pl.BlockSpec((1, 2, PEER_NKEYS, half), lambda i, h: (h, 0, 0, 0))],
        out_specs=[pl.BlockSpec((1, K, tt), lambda i, h: (h, 0, i)),
                   pl.BlockSpec((1, K, tt), lambda i, h: (h, 0, i))],
        out_shape=[jax.ShapeDtypeStruct((PEER_HEADS, K, n_tok), I32),
                   jax.ShapeDtypeStruct((PEER_HEADS, K, n_tok), F32)],
        scratch_shapes=[vs(PEER_NKEYS, F32), vs(K, F32), vs(K, F32), vs(K, F32), vs(K, F32),
                        vs(K * K, F32), vs(K * K, F32), vs(K, F32), vs(K, F32)],
        compiler_params=_cparams("parallel", "parallel"),
        name="peer_topk",
    )(q2, sub_keys)


def _pack_tables(u, v):
    ub = lax.bitcast_convert_type(u.astype(BF16), jnp.uint16).astype(jnp.uint32)
    vb = lax.bitcast_convert_type(v.astype(BF16), jnp.uint16).astype(jnp.uint32)
    packed = lax.bitcast_convert_type((ub << 16) | vb, I32)
    return packed.reshape(packed.shape[0], 1, packed.shape[1])


def _peer_gather_kernel(idx_ref, gate_ref, x_ref, h_ref, g2_ref, tab_hbm, o_ref, buf0, buf1, stage, sem, *, tt):
    R = PEER_SEL
    bufs = (buf0, buf1)
    d = x_ref.shape[1]
    nch = d // LANES
    ngrp = R // SUBLANES
    per_grp = R // ngrp

    def row_copy(t, j, slot):
        return pltpu.make_async_copy(tab_hbm.at[idx_ref[t, j]], bufs[slot].at[pl.ds(j, 1)], sem.at[slot])

    def wait(slot):
        pltpu.make_async_copy(bufs[1 - slot], bufs[slot], sem.at[slot]).wait()

    def tile(slot, g, c):
        return bufs[slot][g * SUBLANES:(g + 1) * SUBLANES, c * LANES:(c + 1) * LANES]

    lane = lax.broadcasted_iota(I32, (SUBLANES, LANES), 1)
    sub = lax.broadcasted_iota(I32, (SUBLANES, LANES), 0)
    hi_mask = jnp.int32(-65536)

    for j in range(R):
        row_copy(0, j, 0).start()

    def group(gi, carry):
        base = pl.multiple_of(gi * SUBLANES, SUBLANES)
        xg = x_ref[pl.ds(base, SUBLANES), :]
        gg = gate_ref[pl.ds(base, SUBLANES), :]
        for r in range(SUBLANES):
            t = base + r
            slot = r % 2
            nxt = jnp.minimum(t + 1, tt - 1)
            wait(slot)
            xb = [jnp.broadcast_to(xg[r:r + 1, c * LANES:(c + 1) * LANES], (SUBLANES, LANES)) for c in range(nch)]
            grow = jnp.broadcast_to(gg[r:r + 1, :], (SUBLANES, LANES))
            wb = []
            for g in range(ngrp):
                for j in range(g * per_grp, (g + 1) * per_grp):
                    row_copy(nxt, j, 1 - slot).start()
                acc = lax.bitcast_convert_type(tile(slot, g, 0) & hi_mask, F32) * xb[0]
                for c in range(1, nch):
                    acc = acc + lax.bitcast_convert_type(tile(slot, g, c) & hi_mask, F32) * xb[c]
                act = jnp.sum(acc, axis=1, keepdims=True)
                gcol = jnp.sum(jnp.where(lane == g * SUBLANES + sub, grow, 0.0), axis=1, keepdims=True)
                wb.append(jnp.broadcast_to(gcol * _gelu_tanh(act), (SUBLANES, LANES)))
            for c in range(nch):
                acc = lax.bitcast_convert_type(tile(slot, 0, c) << 16, F32) * wb[0]
                for g in range(1, ngrp):
                    acc = acc + lax.bitcast_convert_type(tile(slot, g, c) << 16, F32) * wb[g]
                stage[r:r + 1, c * LANES:(c + 1) * LANES] = jnp.sum(acc, axis=0, keepdims=True)
        rows = pl.ds(base, SUBLANES)
        o_ref[rows, :] = h_ref[rows, :] + g2_ref[0] * stage[...]
        return carry

    lax.fori_loop(0, tt // SUBLANES, group, 0)
    wait(tt % 2)


def _peer_gather(idx, gates, hn2, h2, g2, table, s, tt=128):
    n_tok, d = hn2.shape
    tt = min(tt, s)
    assert s % tt == 0 and tt % (2 * SUBLANES) == 0 and d % (SUBLANES * LANES) == 0 and PEER_SEL % SUBLANES == 0
    assert PEER_SEL <= LANES
    per_b = s // tt
    R = PEER_SEL
    return pl.pallas_call(
        functools.partial(_peer_gather_kernel, tt=tt),
        grid=(n_tok // tt,),
        in_specs=[pl.BlockSpec((tt, R), lambda i: (i, 0), memory_space=pltpu.SMEM),
                  pl.BlockSpec((tt, R), lambda i: (i, 0)),
                  pl.BlockSpec((tt, d), lambda i: (i, 0)),
                  pl.BlockSpec((tt, d), lambda i: (i, 0)),
                  pl.BlockSpec((1, 1, d), lambda i: (i // per_b, 0, 0)),
                  pl.BlockSpec(memory_space=pl.ANY)],
        out_specs=pl.BlockSpec((tt, d), lambda i: (i, 0)),
        out_shape=jax.ShapeDtypeStruct((n_tok, d), F32),
        scratch_shapes=[pltpu.VMEM((R, d), I32), pltpu.VMEM((R, d), I32),
                        pltpu.VMEM((SUBLANES, d), F32), pltpu.SemaphoreType.DMA((2,))],
        compiler_params=_cparams("arbitrary"),
        name="peer_gather",
    )(idx, gates, hn2, h2, g2.reshape(-1, 1, d), table)


def _peer_layer(h, g, sc, sh, g2, w_q, sub_keys, u, v):
    bsz, s, d = h.shape
    q, hn = _nm_matmul(h, g, sc, sh, w_q.astype(BF16), tm=512, tn=1024, emit_hn=True, name="peer_query")
    n_tok = bsz * s
    idx, gates = _peer_topk(q.reshape(n_tok, -1), sub_keys)
    idx = idx.transpose(2, 0, 1).reshape(n_tok, PEER_SEL)
    gates = gates.transpose(2, 0, 1).reshape(n_tok, PEER_SEL)
    out = _peer_gather(idx, gates, hn.reshape(n_tok, d), h.reshape(n_tok, d), g2, _pack_tables(u, v), s)
    return out.reshape(bsz, s, d)


def _rmsnorm_kernel(x_ref, g_ref, o_ref):
    x = x_ref[...]
    o_ref[...] = x * lax.rsqrt(jnp.mean(x * x, axis=-1, keepdims=True) + EPS) * g_ref[...]


def _final_norm(h, g, tm=512):
    bsz, s, d = h.shape
    n = bsz * s
    tm = min(tm, n)
    out = pl.pallas_call(
        _rmsnorm_kernel,
        grid=(n // tm,),
        in_specs=[pl.BlockSpec((tm, d), lambda i: (i, 0)), pl.BlockSpec((1, d), lambda i: (0, 0))],
        out_specs=pl.BlockSpec((tm, d), lambda i: (i, 0)),
        out_shape=jax.ShapeDtypeStruct((n, d), F32),
        compiler_params=_cparams("parallel"),
        name="final_norm",
    )(h.reshape(n, d), g.reshape(1, d))
    return out.reshape(bsz, s, d)


def _even_mixer(h, pos3, g, sc, sh, g1, layer, w_in, w_out, conv_w, conv_b, gate_b, mlstm_g, lam_vec, diff_g):
    lambda_init = 0.8 - 0.6 * math.exp(-0.3 * layer)
    d = h.shape[-1]
    cuts = np.cumsum([2 * A_QK, A_V, A_V, 2 * MLSTM_HEADS, B_QK, B_QK, B_V])
    n_gate = 2 * MLSTM_HEADS
    w_cat = jnp.concatenate([w_in[:, :cuts[2]], w_in[:, cuts[3]:], w_in[:, cuts[2]:cuts[3]],
                             jnp.zeros((d, LANES - n_gate), w_in.dtype)], axis=1).astype(BF16)
    proj = _nm_matmul(h, g, sc, sh, w_cat, tm=512, tn=896, name="even_in_proj")
    gates_t = proj[:, :, w_cat.shape[1] - LANES:w_cat.shape[1] - LANES + n_gate].transpose(0, 2, 1)
    h_a = _mlstm(proj, gates_t, gate_b, conv_w, conv_b, mlstm_g)
    bq, bk, bv = _diff_prep(proj, pos3)
    h_b = _diff_attn(bq, bk, bv, lam_vec, diff_g, lambda_init)
    return _out_proj(h_a, h_b, w_out, h, g1)


def _odd_mixer(h, pos3, g, sc, sh, g1, w_in, w_out, ln_g, ln_b, sgu_w, sgu_b):
    proj = _nm_matmul(h, g, sc, sh, w_in.astype(BF16), tm=512, tn=1024, name="odd_in_proj")
    cq, ck, cv, sel = _moba_prep(proj, pos3)
    h_c = _moba_attn(cq, ck, cv, sel)
    h_d = _sgu(proj, ln_g, ln_b, sgu_w, sgu_b)
    return _out_proj(h_c, h_d, w_out, h, g1)


def kernel(x, c, positions, mod_w, mod_b, norm_g, final_g, even_w_in, even_w_out, mlstm_conv_w, mlstm_conv_b, mlstm_gate_b, mlstm_head_g, diff_lambda, diff_head_g, odd_w_in, odd_w_out, sgu_ln_g, sgu_ln_b, sgu_w, sgu_b, peer_w_q, peer_sub_keys, peer_u, peer_v):
    depth = mod_w.shape[0]
    bsz, s, d = x.shape
    mod = _modulation(c, mod_w, mod_b)
    pos3 = positions.reshape(bsz, s, 1)
    h = x
    for layer in range(depth):
        sh1, sc1, g1, sh2, sc2, g2 = [mod[layer, :, i * d:(i + 1) * d] for i in range(6)]
        if layer % 2 == 0:
            e = layer // 2
            h = _even_mixer(h, pos3, norm_g[layer, 0], sc1, sh1, g1, layer, even_w_in[e], even_w_out[e],
                            mlstm_conv_w[e], mlstm_conv_b[e], mlstm_gate_b[e], mlstm_head_g[e],
                            diff_lambda[e], diff_head_g[e])
        else:
            o = layer // 2
            h = _odd_mixer(h, pos3, norm_g[layer, 0], sc1, sh1, g1, odd_w_in[o], odd_w_out[o],
                           sgu_ln_g[o], sgu_ln_b[o], sgu_w[o], sgu_b[o])
        h = _peer_layer(h, norm_g[layer, 1], sc2, sh2, g2, peer_w_q[layer], peer_sub_keys[layer],
                        peer_u[layer], peer_v[layer])
    return _final_norm(h, final_g)
```

```python
import functools
import math

import numpy as np
import jax
import jax.numpy as jnp
from jax import lax
from jax.experimental import pallas as pl
from jax.experimental.pallas import tpu as pltpu

F32 = jnp.float32
BF16 = jnp.bfloat16
I32 = jnp.int32
HIGHEST = lax.Precision.HIGHEST

EPS = 1e-6
ROPE_THETA = 500000.0
LANES = 128
SUBLANES = 8
VMEM_LIMIT_BYTES = 48 * 1024 * 1024

MLSTM_HEADS, MLSTM_DQK, MLSTM_DV, MLSTM_CHUNK, CONV_W = 4, 128, 256, 128, 4
DIFF_HEADS, DIFF_DQK, DIFF_DV = 8, 64, 128
MOBA_HEADS, MOBA_DH, MOBA_BLOCK, MOBA_TOPK = 8, 128, 256, 3
SGU_GROUPS, SGU_CH, SGU_CHUNK = 8, 128, 128
PEER_HEADS, PEER_NKEYS, PEER_DQ, PEER_TOPK = 8, 128, 256, 16
PEER_SEL = PEER_HEADS * PEER_TOPK

A_QK = MLSTM_HEADS * MLSTM_DQK
A_V = MLSTM_HEADS * MLSTM_DV
B_QK = DIFF_HEADS * 2 * DIFF_DQK
B_V = DIFF_HEADS * DIFF_DV
C_W = MOBA_HEADS * MOBA_DH
D_W = SGU_GROUPS * SGU_CH


def _cparams(*sem):
    return pltpu.CompilerParams(dimension_semantics=sem, vmem_limit_bytes=VMEM_LIMIT_BYTES)


def _dot_nt(a, b, precision=None):
    return lax.dot_general(a, b, (((1,), (1,)), ((), ())), preferred_element_type=F32, precision=precision)


def _store_row(ref, row, val, cols=slice(None)):
    base = pl.multiple_of((row // SUBLANES) * SUBLANES, SUBLANES)
    blk = ref[pl.ds(base, SUBLANES), cols]
    r = lax.broadcasted_iota(I32, blk.shape, 0)
    ref[pl.ds(base, SUBLANES), cols] = jnp.where(r == row - base, val, blk)


def _sigmoid(x):
    return 1.0 / (1.0 + jnp.exp(-x))


def _silu(x):
    return x * _sigmoid(x)


def _gelu_tanh(x):
    return 0.5 * x * (1.0 + jnp.tanh(math.sqrt(2.0 / math.pi) * (x + 0.044715 * (x * x * x))))


def _log_sigmoid(x):
    return jnp.minimum(x, 0.0) - jnp.log1p(jnp.exp(-jnp.abs(x)))


def _mod_kernel(c_ref, w_ref, b_ref, o_ref):
    ca = _silu(c_ref[...])
    o_ref[0] = jnp.dot(ca, w_ref[0], preferred_element_type=F32, precision=HIGHEST) + b_ref[0]


def _modulation(c, mod_w, mod_b):
    depth, d, n = mod_w.shape
    bsz = c.shape[0]
    rows = 8
    cp = jnp.zeros((rows, d), F32).at[:bsz].set(c)
    tn = 1024
    out = pl.pallas_call(
        _mod_kernel,
        grid=(depth, n // tn),
        in_specs=[pl.BlockSpec((rows, d), lambda l, j: (0, 0)),
                  pl.BlockSpec((1, d, tn), lambda l, j: (l, 0, j)),
                  pl.BlockSpec((1, 1, tn), lambda l, j: (l, 0, j))],
        out_specs=pl.BlockSpec((1, rows, tn), lambda l, j: (l, 0, j)),
        out_shape=jax.ShapeDtypeStruct((depth, rows, n), F32),
        compiler_params=_cparams("parallel", "parallel"),
        name="modulation",
    )(cp, mod_w, mod_b.reshape(depth, 1, n))
    return out[:, :bsz]


def _nm_matmul_kernel(h_ref, g_ref, sc_ref, sh_ref, w_ref, *rest, emit_hn):
    if emit_hn:
        o_ref, hn_ref, hn_scr = rest
    else:
        o_ref, hn_scr = rest

    @pl.when(pl.program_id(2) == 0)
    def _():
        x = h_ref[0]
        y = x * lax.rsqrt(jnp.mean(x * x, axis=-1, keepdims=True) + EPS) * g_ref[...]
        y = y * (1.0 + sc_ref[0]) + sh_ref[0]
        hn_scr[...] = y.astype(BF16)
        if emit_hn:
            hn_ref[0] = y

    o_ref[0] = jnp.dot(hn_scr[...], w_ref[...], preferred_element_type=F32).astype(o_ref.dtype)


def _nm_matmul(h, g, sc, sh, w, *, tm, tn, emit_hn=False, name="nm_matmul"):
    bsz, s, d = h.shape
    n = w.shape[1]
    tm = min(tm, s)
    out_shape = [jax.ShapeDtypeStruct((bsz, s, n), F32)]
    out_specs = [pl.BlockSpec((1, tm, tn), lambda b, i, j: (b, i, j))]
    if emit_hn:
        out_shape.append(jax.ShapeDtypeStruct((bsz, s, d), F32))
        out_specs.append(pl.BlockSpec((1, tm, d), lambda b, i, j: (b, i, 0)))
    res = pl.pallas_call(
        functools.partial(_nm_matmul_kernel, emit_hn=emit_hn),
        grid=(bsz, s // tm, n // tn),
        in_specs=[pl.BlockSpec((1, tm, d), lambda b, i, j: (b, i, 0)),
                  pl.BlockSpec((1, d), lambda b, i, j: (0, 0)),
                  pl.BlockSpec((1, 1, d), lambda b, i, j: (b, 0, 0)),
                  pl.BlockSpec((1, 1, d), lambda b, i, j: (b, 0, 0)),
                  pl.BlockSpec((d, tn), lambda b, i, j: (0, j))],
        out_specs=out_specs,
        out_shape=out_shape,
        scratch_shapes=[pltpu.VMEM((tm, d), BF16)],
        compiler_params=_cparams("parallel", "parallel", "arbitrary"),
        name=name,
    )(h, g.reshape(1, d), sc.reshape(bsz, 1, d), sh.reshape(bsz, 1, d), w)
    return res if emit_hn else res[0]


def _out_proj_kernel(xa_ref, xb_ref, wa_ref, wb_ref, h_ref, g_ref, o_ref):
    acc = jnp.dot(xa_ref[0], wa_ref[...], preferred_element_type=F32)
    acc += jnp.dot(xb_ref[0], wb_ref[...], preferred_element_type=F32)
    o_ref[0] = h_ref[0] + g_ref[0] * acc


def _out_proj(xa, xb, w, h, g1, *, tm=512, tn=1024):
    bsz, s, ka = xa.shape
    kb = xb.shape[2]
    n = w.shape[1]
    tm = min(tm, s)
    wa = w[:ka].astype(BF16)
    wb = w[ka:].astype(BF16)
    return pl.pallas_call(
        _out_proj_kernel,
        grid=(bsz, s // tm, n // tn),
        in_specs=[pl.BlockSpec((1, tm, ka), lambda b, i, j: (b, i, 0)),
                  pl.BlockSpec((1, tm, kb), lambda b, i, j: (b, i, 0)),
                  pl.BlockSpec((ka, tn), lambda b, i, j: (0, j)),
                  pl.BlockSpec((kb, tn), lambda b, i, j: (0, j)),
                  pl.BlockSpec((1, tm, tn), lambda b, i, j: (b, i, j)),
                  pl.BlockSpec((1, 1, tn), lambda b, i, j: (b, 0, j))],
        out_specs=pl.BlockSpec((1, tm, tn), lambda b, i, j: (b, i, j)),
        out_shape=jax.ShapeDtypeStruct((bsz, s, n), F32),
        compiler_params=_cparams("parallel", "parallel", "parallel"),
        name="out_proj",
    )(xa, xb, wa, wb, h, g1.reshape(bsz, 1, n))


def _mlstm_kernel(gb_ref, aqk_ref, v_ref, ao_ref, gcol_ref, grow_ref, cw_ref, cb_ref, hg_ref,
                  o_ref, xbuf, c_scr, n_scr, m_scr):
    L = MLSTM_CHUNK
    H, DQK, DV = MLSTM_HEADS, MLSTM_DQK, MLSTM_DV
    c = pl.program_id(1)

    @pl.when(c == 0)
    def _():
        xbuf[0:8, :] = jnp.zeros((8, 2 * A_QK), F32)
        c_scr[...] = jnp.zeros_like(c_scr)
        n_scr[...] = jnp.zeros_like(n_scr)
        m_scr[...] = jnp.zeros_like(m_scr)

    @pl.when(c > 0)
    def _():
        xbuf[0:8, :] = xbuf[L:L + 8, :]

    xbuf[8:L + 8, :] = aqk_ref[0]
    conv = cb_ref[...] + cw_ref[0:1, :] * xbuf[5:5 + L, :]
    for j in range(1, CONV_W):
        conv = conv + cw_ref[j:j + 1, :] * xbuf[5 + j:5 + j + L, :]
    qk = _silu(conv)

    r_i = lax.broadcasted_iota(I32, (L, L), 0)
    c_i = lax.broadcasted_iota(I32, (L, L), 1)
    tril = c_i <= r_i
    gcol = gcol_ref[0]
    grow = grow_ref[0]

    for h in range(H):
        q = qk[:, h * DQK:(h + 1) * DQK] * (DQK ** -0.5)
        k = qk[:, A_QK + h * DQK:A_QK + (h + 1) * DQK]
        v = v_ref[0, :, h * DV:(h + 1) * DV]
        i_col = gcol[:, h:h + 1] + gb_ref[h]
        i_row = grow[h:h + 1, :] + gb_ref[h]
        lf_col = _log_sigmoid(gcol[:, H + h:H + h + 1] + gb_ref[H + h])
        lf_row = _log_sigmoid(grow[H + h:H + h + 1, :] + gb_ref[H + h])
        bcum_col = jnp.sum(jnp.where(tril, lf_row, 0.0), axis=1, keepdims=True)
        bcum_row = jnp.sum(jnp.where(r_i <= c_i, lf_col, 0.0), axis=0, keepdims=True)
        g_tot = jnp.sum(lf_row, axis=1, keepdims=True)
        m_prev = m_scr[h:h + 1, 0:1]
        d_intra = jnp.where(tril, bcum_col - bcum_row + i_row, -jnp.inf)
        d_inter = bcum_col + m_prev
        m_t = jnp.maximum(d_inter, jnp.max(d_intra, axis=1, keepdims=True))
        w_intra = jnp.exp(d_intra - m_t)
        w_inter = jnp.exp(d_inter - m_t)
        qb = q.astype(BF16)
        kb = k.astype(BF16)
        vb = v.astype(BF16)
        s = _dot_nt(qb, kb) * w_intra
        c_prev = c_scr[h]
        n_prev = n_scr[h:h + 1, :]
        num = jnp.dot(s.astype(BF16), vb, preferred_element_type=F32)
        num = num + w_inter * jnp.dot(qb, c_prev.astype(BF16), preferred_element_type=F32)
        den = jnp.sum(s, axis=1, keepdims=True) + w_inter * jnp.sum(q * n_prev, axis=1, keepdims=True)
        hh = num / jnp.maximum(jnp.abs(den), jnp.exp(-m_t))
        d_state = g_tot - bcum_col + i_col
        m_new = jnp.maximum(g_tot + m_prev, jnp.max(d_state, axis=0, keepdims=True))
        w_s = jnp.exp(d_state - m_new)
        w_c = jnp.exp(g_tot + m_prev - m_new)
        kw = k * w_s
        c_scr[h] = w_c * c_prev + jnp.dot(kw.T.astype(BF16), vb, preferred_element_type=F32)
        n_scr[h:h + 1, :] = w_c * n_prev + jnp.sum(kw, axis=0, keepdims=True)
        m_scr[h:h + 1, 0:1] = m_new
        y = hh * lax.rsqrt(jnp.mean(hh * hh, axis=-1, keepdims=True) + EPS) * hg_ref[:, h * DV:(h + 1) * DV]
        y = y * _sigmoid(ao_ref[0, :, h * DV:(h + 1) * DV])
        o_ref[0, :, h * DV:(h + 1) * DV] = y.astype(o_ref.dtype)


def _mlstm(proj, gates_t, gate_b, conv_w, conv_b, head_g):
    bsz, s, _ = proj.shape
    L = MLSTM_CHUNK
    gate_blk = (2 * A_QK + 2 * A_V + 2 * B_QK + B_V) // LANES
    return pl.pallas_call(
        _mlstm_kernel,
        grid=(bsz, s // L),
        in_specs=[pl.BlockSpec(memory_space=pltpu.SMEM),
                  pl.BlockSpec((1, L, 2 * A_QK), lambda b, c: (b, c, 0)),
                  pl.BlockSpec((1, L, A_V), lambda b, c: (b, c, 1)),
                  pl.BlockSpec((1, L, A_V), lambda b, c: (b, c, 2)),
                  pl.BlockSpec((1, L, LANES), lambda b, c: (b, c, gate_blk)),
                  pl.BlockSpec((1, 2 * MLSTM_HEADS, L), lambda b, c: (b, 0, c)),
                  pl.BlockSpec((CONV_W, 2 * A_QK), lambda b, c: (0, 0)),
                  pl.BlockSpec((1, 2 * A_QK), lambda b, c: (0, 0)),
                  pl.BlockSpec((1, A_V), lambda b, c: (0, 0))],
        out_specs=pl.BlockSpec((1, L, A_V), lambda b, c: (b, c, 0)),
        out_shape=jax.ShapeDtypeStruct((bsz, s, A_V), BF16),
        scratch_shapes=[pltpu.VMEM((L + 8, 2 * A_QK), F32),
                        pltpu.VMEM((MLSTM_HEADS, MLSTM_DQK, MLSTM_DV), F32),
                        pltpu.VMEM((8, MLSTM_DQK), F32),
                        pltpu.VMEM((8, LANES), F32)],
        compiler_params=_cparams("parallel", "arbitrary"),
        name="mlstm",
    )(gate_b.reshape(-1), proj, proj, proj, proj, gates_t, conv_w, conv_b.reshape(1, -1),
      head_g.reshape(1, -1))


def _rope_consts(dh):
    half = dh // 8
    lane = np.arange(LANES) % dh
    inv_half = ROPE_THETA ** (-jnp.arange(half, dtype=F32) / half)
    inv = jnp.where(jnp.asarray(lane < 2 * half), inv_half[lane % half], 0.0)
    lo = (lane < half).astype(np.float32)
    hi = ((lane >= half) & (lane < 2 * half)).astype(np.float32)
    return (inv.astype(F32).reshape(1, LANES), jnp.asarray(lo).reshape(1, LANES),
            jnp.asarray(hi).reshape(1, LANES))


def _rope_tables(pos_ref, inv_ref, lo_ref, hi_ref):
    ang = pos_ref[0].astype(F32) * inv_ref[...]
    cs = jnp.cos(ang)
    sn = jnp.sin(ang)
    return cs, -sn * lo_ref[...], sn * hi_ref[...]


def _rope_apply(x, cs, sa, sb, half):
    return x * cs + pltpu.roll(x, LANES - half, 1) * sa + pltpu.roll(x, half, 1) * sb


def _diff_prep_kernel(pos_ref, inv_ref, lo_ref, hi_ref, q_ref, k_ref, v_ref, qo_ref, ko_ref, vo_ref):
    cs, sa, sb = _rope_tables(pos_ref, inv_ref, lo_ref, hi_ref)
    half = DIFF_DQK // 8
    scale = DIFF_DQK ** -0.5
    for h in range(DIFF_HEADS):
        sl = slice(h * LANES, (h + 1) * LANES)
        qo_ref[0, :, sl] = (_rope_apply(q_ref[0, :, sl], cs, sa, sb, half) * scale).astype(BF16)
        ko_ref[0, :, sl] = _rope_apply(k_ref[0, :, sl], cs, sa, sb, half).astype(BF16)
    vo_ref[0] = v_ref[0].astype(BF16)


def _diff_prep(proj, pos3, ts=512):
    bsz, s, _ = proj.shape
    ts = min(ts, s)
    inv, lo, hi = _rope_consts(DIFF_DQK)
    base = (2 * A_QK + 2 * A_V) // B_QK
    cst = pl.BlockSpec((1, LANES), lambda b, i: (0, 0))
    osd = jax.ShapeDtypeStruct((bsz, s, B_QK), BF16)
    return pl.pallas_call(
        _diff_prep_kernel,
        grid=(bsz, s // ts),
        in_specs=[pl.BlockSpec((1, ts, 1), lambda b, i: (b, i, 0)), cst, cst, cst,
                  pl.BlockSpec((1, ts, B_QK), lambda b, i: (b, i, base)),
                  pl.BlockSpec((1, ts, B_QK), lambda b, i: (b, i, base + 1)),
                  pl.BlockSpec((1, ts, B_V), lambda b, i: (b, i, base + 2))],
        out_specs=[pl.BlockSpec((1, ts, B_QK), lambda b, i: (b, i, 0))] * 3,
        out_shape=[osd, osd, osd],
        compiler_params=_cparams("parallel", "parallel"),
        name="diff_prep",
    )(pos3, inv, lo, hi, proj, proj, proj)


def _diff_attn_kernel(lam_ref, q_ref, k_ref, v_ref, g_ref, o_ref, qs_scr, m_scr, l_scr, acc_scr,
                      *, tq, tk, lambda_init):
    i = pl.program_id(2)
    j = pl.program_id(3)

    @pl.when(j == 0)
    def _():
        q = q_ref[0]
        lane = lax.broadcasted_iota(I32, q.shape, 1)
        zero = jnp.zeros_like(q)
        qs_scr[0:tq, :] = jnp.where(lane < DIFF_DQK, q, zero)
        qs_scr[tq:2 * tq, :] = jnp.where(lane >= DIFF_DQK, q, zero)
        m_scr[...] = jnp.full_like(m_scr, -jnp.inf)
        l_scr[...] = jnp.zeros_like(l_scr)
        acc_scr[...] = jnp.zeros_like(acc_scr)

    def step(masked):
        s = _dot_nt(qs_scr[...], k_ref[0])
        if masked:
            row = lax.broadcasted_iota(I32, (2 * tq, tk), 0)
            row = jnp.where(row >= tq, row - tq, row)
            col = lax.broadcasted_iota(I32, (2 * tq, tk), 1)
            s = jnp.where((j * tk + col) <= (i * tq + row), s, -jnp.inf)
        m_prev = m_scr[...]
        m_new = jnp.maximum(m_prev, jnp.max(s, axis=1, keepdims=True))
        alpha = jnp.exp(m_prev - m_new)
        p = jnp.exp(s - m_new)
        l_scr[...] = alpha * l_scr[...] + jnp.sum(p, axis=1, keepdims=True)
        acc_scr[...] = alpha * acc_scr[...] + jnp.dot(p.astype(BF16), v_ref[0], preferred_element_type=F32)
        m_scr[...] = m_new

    @pl.when(j < i)
    def _():
        step(False)

    @pl.when(j == i)
    def _():
        step(True)
        lv = lam_ref[...]
        lam = (jnp.exp(jnp.sum(lv[0:1] * lv[1:2], axis=1, keepdims=True))
               - jnp.exp(jnp.sum(lv[2:3] * lv[3:4], axis=1, keepdims=True)) + lambda_init)
        o0 = acc_scr[0:tq, :] / l_scr[0:tq, :]
        o1 = acc_scr[tq:2 * tq, :] / l_scr[tq:2 * tq, :]
        a = o0 - lam * o1
        y = a * lax.rsqrt(jnp.mean(a * a, axis=-1, keepdims=True) + EPS) * g_ref[0]
        o_ref[0] = (y * (1.0 - lambda_init)).astype(o_ref.dtype)


def _diff_attn(q, k, v, lam_vec, head_g, lambda_init, *, tq=512):
    bsz, s, _ = q.shape
    tq = min(tq, s)
    nq = s // tq
    return pl.pallas_call(
        functools.partial(_diff_attn_kernel, tq=tq, tk=tq, lambda_init=lambda_init),
        grid=(bsz, DIFF_HEADS, nq, nq),
        in_specs=[pl.BlockSpec((4, DIFF_DQK), lambda b, h, i, j: (0, 0)),
                  pl.BlockSpec((1, tq, LANES), lambda b, h, i, j: (b, i, h)),
                  pl.BlockSpec((1, tq, LANES), lambda b, h, i, j: (b, jnp.minimum(j, i), h)),
                  pl.BlockSpec((1, tq, LANES), lambda b, h, i, j: (b, jnp.minimum(j, i), h)),
                  pl.BlockSpec((1, 1, DIFF_DV), lambda b, h, i, j: (h, 0, 0))],
        out_specs=pl.BlockSpec((1, tq, DIFF_DV), lambda b, h, i, j: (b, i, h)),
        out_shape=jax.ShapeDtypeStruct((bsz, s, B_V), BF16),
        scratch_shapes=[pltpu.VMEM((2 * tq, LANES), BF16),
                        pltpu.VMEM((2 * tq, 1), F32),
                        pltpu.VMEM((2 * tq, 1), F32),
                        pltpu.VMEM((2 * tq, DIFF_DV), F32)],
        compiler_params=_cparams("parallel", "parallel", "parallel", "arbitrary"),
        name="diff_attn",
    )(lam_vec, q, k, v, head_g.reshape(DIFF_HEADS, 1, DIFF_DV))


def _moba_prep_kernel(pos_ref, inv_ref, lo_ref, hi_ref, q_ref, k_ref, v_ref,
                      qo_ref, ko_ref, vo_ref, sel_ref, kmt_scr, qf_scr, *, nb):
    i = pl.program_id(1)
    tb = MOBA_BLOCK
    half = MOBA_DH // 8
    scale = MOBA_DH ** -0.5

    @pl.when(i == 0)
    def _():
        kmt_scr[...] = jnp.zeros_like(kmt_scr)

    cs, sa, sb = _rope_tables(pos_ref, inv_ref, lo_ref, hi_ref)
    kmeans = []
    for h in range(MOBA_HEADS):
        sl = slice(h * LANES, (h + 1) * LANES)
        qr = _rope_apply(q_ref[0, :, sl], cs, sa, sb, half)
        kr = _rope_apply(k_ref[0, :, sl], cs, sa, sb, half)
        qf_scr[:, sl] = qr
        qo_ref[0, :, sl] = (qr * scale).astype(BF16)
        ko_ref[0, :, sl] = kr.astype(BF16)
        kmeans.append(jnp.mean(kr, axis=0, keepdims=True))
    vo_ref[0] = v_ref[0].astype(BF16)

    gate = _dot_nt(qf_scr[...], kmt_scr[...], precision=HIGHEST)
    lane = lax.broadcasted_iota(I32, (tb, LANES), 1)
    n = lane % nb
    valid = n < i
    gm = jnp.where(valid, gate, -jnp.inf)
    rank = jnp.zeros((tb, LANES), I32)
    for d in range(1, nb):
        wraps = (n + d) >= nb
        partner = jnp.where(wraps, pltpu.roll(gm, nb - d, 1), pltpu.roll(gm, LANES - d, 1))
        pn = jnp.where(wraps, n + d - nb, n + d)
        beats = (partner > gm) | ((partner == gm) & (pn < n))
        rank = rank + beats.astype(I32)
    sel_ref[0] = jnp.where(valid & (rank < MOBA_TOPK), 1.0, 0.0).astype(F32)

    for h in range(MOBA_HEADS):
        _store_row(kmt_scr, h * nb + i, kmeans[h], slice(h * LANES, (h + 1) * LANES))


def _moba_prep(proj, pos3):
    bsz, s, _ = proj.shape
    tb = MOBA_BLOCK
    nb = s // tb
    assert s % tb == 0 and MOBA_HEADS * nb <= LANES and nb >= MOBA_TOPK
    inv, lo, hi = _rope_consts(MOBA_DH)
    cst = pl.BlockSpec((1, LANES), lambda b, i: (0, 0))
    osd = jax.ShapeDtypeStruct((bsz, s, C_W), BF16)
    return pl.pallas_call(
        functools.partial(_moba_prep_kernel, nb=nb),
        grid=(bsz, nb),
        in_specs=[pl.BlockSpec((1, tb, 1), lambda b, i: (b, i, 0)), cst, cst, cst,
                  pl.BlockSpec((1, tb, C_W), lambda b, i: (b, i, 0)),
                  pl.BlockSpec((1, tb, C_W), lambda b, i: (b, i, 1)),
                  pl.BlockSpec((1, tb, C_W), lambda b, i: (b, i, 2))],
        out_specs=[pl.BlockSpec((1, tb, C_W), lambda b, i: (b, i, 0))] * 3
        + [pl.BlockSpec((1, tb, LANES), lambda b, i: (b, i, 0))],
        out_shape=[osd, osd, osd, jax.ShapeDtypeStruct((bsz, s, LANES), F32)],
        scratch_shapes=[pltpu.VMEM((LANES, C_W), F32), pltpu.VMEM((tb, C_W), F32)],
        compiler_params=_cparams("parallel", "arbitrary"),
        name="moba_prep",
    )(pos3, inv, lo, hi, proj, proj, proj)


def _moba_attn_kernel(q_ref, k_ref, v_ref, sel_ref, o_ref, m_scr, l_scr, acc_scr, *, nb):
    h = pl.program_id(1)
    i = pl.program_id(2)
    j = pl.program_id(3)
    tb = MOBA_BLOCK

    @pl.when(j == 0)
    def _():
        s = _dot_nt(q_ref[0], k_ref[0])
        row = lax.broadcasted_iota(I32, (tb, tb), 0)
        col = lax.broadcasted_iota(I32, (tb, tb), 1)
        s = jnp.where(col <= row, s, -jnp.inf)
        m = jnp.max(s, axis=1, keepdims=True)
        p = jnp.exp(s - m)
        m_scr[...] = m
        l_scr[...] = jnp.sum(p, axis=1, keepdims=True)
        acc_scr[...] = jnp.dot(p.astype(BF16), v_ref[0], preferred_element_type=F32)

    @pl.when((j > 0) & (j <= i))
    def _():
        lane = lax.broadcasted_iota(I32, (tb, LANES), 1)
        picked = jnp.max(jnp.where(lane == h * nb + (i - j), sel_ref[0], 0.0), axis=1, keepdims=True)
        s = jnp.where(picked > 0.5, _dot_nt(q_ref[0], k_ref[0]), -jnp.inf)
        m_prev = m_scr[...]
        m_new = jnp.maximum(m_prev, jnp.max(s, axis=1, keepdims=True))
        alpha = jnp.exp(m_prev - m_new)
        p = jnp.exp(s - m_new)
        l_scr[...] = alpha * l_scr[...] + jnp.sum(p, axis=1, keepdims=True)
        acc_scr[...] = alpha * acc_scr[...] + jnp.dot(p.astype(BF16), v_ref[0], preferred_element_type=F32)
        m_scr[...] = m_new

    @pl.when(j == i)
    def _():
        o_ref[0] = (acc_scr[...] / l_scr[...]).astype(o_ref.dtype)


def _moba_attn(q, k, v, sel):
    bsz, s, _ = q.shape
    tb = MOBA_BLOCK
    nb = s // tb
    kv_spec = pl.BlockSpec((1, tb, MOBA_DH), lambda b, h, i, j: (b, jnp.maximum(i - j, 0), h))
    return pl.pallas_call(
        functools.partial(_moba_attn_kernel, nb=nb),
        grid=(bsz, MOBA_HEADS, nb, nb),
        in_specs=[pl.BlockSpec((1, tb, MOBA_DH), lambda b, h, i, j: (b, i, h)), kv_spec, kv_spec,
                  pl.BlockSpec((1, tb, LANES), lambda b, h, i, j: (b, i, 0))],
        out_specs=pl.BlockSpec((1, tb, MOBA_DH), lambda b, h, i, j: (b, i, h)),
        out_shape=jax.ShapeDtypeStruct((bsz, s, C_W), BF16),
        scratch_shapes=[pltpu.VMEM((tb, 1), F32), pltpu.VMEM((tb, 1), F32), pltpu.VMEM((tb, MOBA_DH), F32)],
        compiler_params=_cparams("parallel", "parallel", "parallel", "arbitrary"),
        name="moba_attn",
    )(q, k, v, sel)


def _sgu_kernel(u_ref, v_ref, lng_ref, lnb_ref, w_ref, bt_ref, o_ref, *, ts):
    L = SGU_CHUNK
    r_i = lax.broadcasted_iota(I32, (L, L), 0)
    c_i = lax.broadcasted_iota(I32, (L, L), 1)
    tril = c_i <= r_i
    for g in range(SGU_GROUPS):
        sl = slice(g * SGU_CH, (g + 1) * SGU_CH)
        w = jnp.where(tril, w_ref[g], 0.0).astype(BF16)
        bias = bt_ref[:, g:g + 1]
        for c in range(ts // L):
            rows = slice(c * L, (c + 1) * L)
            vv = _gelu_tanh(v_ref[0, rows, sl])
            mu = jnp.mean(vv, axis=-1, keepdims=True)
            dv = vv - mu
            var = jnp.mean(dv * dv, axis=-1, keepdims=True)
            vn = dv * lax.rsqrt(var + EPS) * lng_ref[:, sl] + lnb_ref[:, sl]
            mixed = jnp.dot(w, vn.astype(BF16), preferred_element_type=F32) + bias
            o_ref[0, rows, sl] = (_gelu_tanh(u_ref[0, rows, sl]) * mixed).astype(o_ref.dtype)


def _sgu(proj, ln_g, ln_b, w_s, b_s, ts=256):
    bsz, s, _ = proj.shape
    ts = min(ts, s)
    base = 3 * C_W // D_W
    return pl.pallas_call(
        functools.partial(_sgu_kernel, ts=ts),
        grid=(bsz, s // ts),
        in_specs=[pl.BlockSpec((1, ts, D_W), lambda b, i: (b, i, base)),
                  pl.BlockSpec((1, ts, D_W), lambda b, i: (b, i, base + 1)),
                  pl.BlockSpec((1, D_W), lambda b, i: (0, 0)),
                  pl.BlockSpec((1, D_W), lambda b, i: (0, 0)),
                  pl.BlockSpec((SGU_GROUPS, SGU_CHUNK, SGU_CHUNK), lambda b, i: (0, 0, 0)),
                  pl.BlockSpec((SGU_CHUNK, SGU_GROUPS), lambda b, i: (0, 0))],
        out_specs=pl.BlockSpec((1, ts, D_W), lambda b, i: (b, i, 0)),
        out_shape=jax.ShapeDtypeStruct((bsz, s, D_W), BF16),
        compiler_params=_cparams("parallel", "parallel"),
        name="sgu",
    )(proj, proj, ln_g.reshape(1, -1), ln_b.reshape(1, -1), w_s, b_s.T)


def _topk_rows(s_scr, k, val_scr, idx_scr, payload_scr=None):
    n, t = s_scr.shape
    row = lax.broadcasted_iota(I32, (n, t), 0).astype(F32)

    def body(it, carry):
        s = s_scr[...]
        m = jnp.max(s, axis=0, keepdims=True)
        pos = jnp.min(jnp.where(s == m, row, float(n)), axis=0, keepdims=True)
        hit = row == pos
        _store_row(val_scr, it, m)
        if payload_scr is None:
            _store_row(idx_scr, it, pos)
        else:
            _store_row(idx_scr, it, jnp.max(jnp.where(hit, payload_scr[...], -1.0), axis=0, keepdims=True))
        s_scr[...] = jnp.where(hit, -jnp.inf, s)
        return carry

    val_scr[...] = jnp.zeros_like(val_scr)
    idx_scr[...] = jnp.zeros_like(idx_scr)
    lax.fori_loop(0, k, body, 0)


def _peer_topk_kernel(q_ref, keys_ref, idx_ref, g_ref, s_scr, v1, i1, v2, i2, cand, cidx, vt, it):
    K = PEER_TOPK
    half = PEER_DQ // 2
    for c, (vs, ix) in enumerate(((v1, i1), (v2, i2))):
        s_scr[...] = _dot_nt(keys_ref[0, c], q_ref[:, c * half:(c + 1) * half], precision=HIGHEST)
        _topk_rows(s_scr, K, vs, ix)
    for a in range(K):
        cand[a * K:(a + 1) * K, :] = v1[a:a + 1, :] + v2[...]
        cidx[a * K:(a + 1) * K, :] = i1[a:a + 1, :] * float(PEER_NKEYS) + i2[...]
    _topk_rows(cand, K, vt, it, payload_scr=cidx)
    ts = vt[...]
    e = jnp.exp(ts - jnp.max(ts, axis=0, keepdims=True))
    g_ref[0] = e / jnp.sum(e, axis=0, keepdims=True)
    idx_ref[0] = it[...].astype(I32)


def _peer_topk(q2, sub_keys, tt=512):
    n_tok = q2.shape[0]
    tt = min(tt, n_tok)
    K = PEER_TOPK
    half = PEER_DQ // 2
    vs = lambda r, dt: pltpu.VMEM((r, tt), dt)
    return pl.pallas_call(
        _peer_topk_kernel,
        grid=(n_tok // tt, PEER_HEADS),
        in_specs=[pl.BlockSpec((tt, PEER_DQ), lambda i, h: (i, h)),
                  pl.BlockSpec((1, 2, PEER_NKEYS, half), lambda i, h: (h, 0, 0, 0))],
        out_specs=[pl.BlockSpec((1, K, tt), lambda i, h: (h, 0, i)),
                   pl.BlockSpec((1, K, tt), lambda i, h: (h, 0, i))],
        out_shape=[jax.ShapeDtypeStruct((PEER_HEADS, K, n_tok), I32),
                   jax.ShapeDtypeStruct((PEER_HEADS, K, n_tok), F32)],
        scratch_shapes=[vs(PEER_NKEYS, F32), vs(K, F32), vs(K, F32), vs(K, F32), vs(K, F32),
                        vs(K * K, F32), vs(K * K, F32), vs(K, F32), vs(K, F32)],
        compiler_params=_cparams("parallel", "parallel"),
        name="peer_topk",
    )(q2, sub_keys)


def _pack_tables(u, v):
    ub = lax.bitcast_convert_type(u.astype(BF16), jnp.uint16).astype(jnp.uint32)
    vb = lax.bitcast_convert_type(v.astype(BF16), jnp.uint16).astype(jnp.uint32)
    packed = lax.bitcast_convert_type((ub << 16) | vb, I32)
    return packed.reshape(packed.shape[0], 1, packed.shape[1])


PEER_SLOTS = SUBLANES
PEER_AHEAD = 6


def _peer_gather_kernel(idx_ref, gate_ref, x_ref, h_ref, g2_ref, tab_hbm, o_ref, *rest, tt):
    R = PEER_SEL
    bufs, (stage, sem) = rest[:PEER_SLOTS], rest[PEER_SLOTS:]
    step = pl.program_id(0)
    d = x_ref.shape[1]
    nch = d // LANES
    ngrp = R // SUBLANES
    per_grp = R // ngrp

    def row_copy(t, j, slot):
        return pltpu.make_async_copy(tab_hbm.at[idx_ref[t, j]], bufs[slot].at[pl.ds(j, 1)], sem.at[slot])

    def wait(slot):
        pltpu.make_async_copy(bufs[(slot + 1) % PEER_SLOTS], bufs[slot], sem.at[slot]).wait()

    def tile(slot, g, c):
        return bufs[slot][g * SUBLANES:(g + 1) * SUBLANES, c * LANES:(c + 1) * LANES]

    lane = lax.broadcasted_iota(I32, (SUBLANES, LANES), 1)
    sub = lax.broadcasted_iota(I32, (SUBLANES, LANES), 0)
    hi_mask = jnp.int32(-65536)

    @pl.when(step == 0)
    def _():
        for t0 in range(PEER_AHEAD):
            def prime(j, carry, t0=t0):
                row_copy(t0, j, t0).start()
                return carry
            lax.fori_loop(0, R, prime, 0)

    def group(gi, carry):
        base = pl.multiple_of(gi * SUBLANES, SUBLANES)
        xg = x_ref[pl.ds(base, SUBLANES), :]
        gg = gate_ref[pl.ds(base, SUBLANES), :]
        for r in range(SUBLANES):
            slot = r
            nxt = base + r + PEER_AHEAD
            nslot = (r + PEER_AHEAD) % PEER_SLOTS
            wait(slot)
            xb = [jnp.broadcast_to(xg[r:r + 1, c * LANES:(c + 1) * LANES], (SUBLANES, LANES)) for c in range(nch)]
            grow = jnp.broadcast_to(gg[r:r + 1, :], (SUBLANES, LANES))
            wb = []
            for g in range(ngrp):
                for j in range(g * per_grp, (g + 1) * per_grp):
                    row_copy(nxt, j, nslot).start()
                acc = lax.bitcast_convert_type(tile(slot, g, 0) & hi_mask, F32) * xb[0]
                for c in range(1, nch):
                    acc = acc + lax.bitcast_convert_type(tile(slot, g, c) & hi_mask, F32) * xb[c]
                act = jnp.sum(acc, axis=1, keepdims=True)
                gcol = jnp.sum(jnp.where(lane == g * SUBLANES + sub, grow, 0.0), axis=1, keepdims=True)
                wb.append(jnp.broadcast_to(gcol * _gelu_tanh(act), (SUBLANES, LANES)))
            for c in range(nch):
                acc = lax.bitcast_convert_type(tile(slot, 0, c) << 16, F32) * wb[0]
                for g in range(1, ngrp):
                    acc = acc + lax.bitcast_convert_type(tile(slot, g, c) << 16, F32) * wb[g]
                stage[r:r + 1, c * LANES:(c + 1) * LANES] = jnp.sum(acc, axis=0, keepdims=True)
        rows = pl.ds(base, SUBLANES)
        o_ref[rows, :] = h_ref[rows, :] + g2_ref[0] * stage[...]
        return carry

    lax.fori_loop(0, tt // SUBLANES, group, 0)

    @pl.when(step == pl.num_programs(0) - 1)
    def _():
        for k in range(PEER_AHEAD):
            wait((tt + k) % PEER_SLOTS)


def _peer_gather(idx, gates, hn2, h2, g2, table, s, tt=128):
    n_tok, d = hn2.shape
    tt = min(tt, s)
    assert s % tt == 0 and tt % PEER_SLOTS == 0 and d % (SUBLANES * LANES) == 0 and PEER_SEL % SUBLANES == 0
    assert PEER_SEL <= LANES and PEER_AHEAD < PEER_SLOTS - 1
    per_b = s // tt
    R = PEER_SEL
    nsteps = n_tok // tt
    idx3 = idx.reshape(nsteps, tt, R)
    head = jnp.concatenate([idx3[1:, :SUBLANES], jnp.zeros((1, SUBLANES, R), I32)], axis=0)
    idx = jnp.concatenate([idx3, head], axis=1).reshape(nsteps * (tt + SUBLANES), R)
    return pl.pallas_call(
        functools.partial(_peer_gather_kernel, tt=tt),
        grid=(nsteps,),
        in_specs=[pl.BlockSpec((tt + SUBLANES, R), lambda i: (i, 0), memory_space=pltpu.SMEM),
                  pl.BlockSpec((tt, R), lambda i: (i, 0)),
                  pl.BlockSpec((tt, d), lambda i: (i, 0)),
                  pl.BlockSpec((tt, d), lambda i: (i, 0)),
                  pl.BlockSpec((1, 1, d), lambda i: (i // per_b, 0, 0)),
                  pl.BlockSpec(memory_space=pl.ANY)],
        out_specs=pl.BlockSpec((tt, d), lambda i: (i, 0)),
        out_shape=jax.ShapeDtypeStruct((n_tok, d), F32),
        scratch_shapes=[pltpu.VMEM((R, d), I32)] * PEER_SLOTS
        + [pltpu.VMEM((SUBLANES, d), F32), pltpu.SemaphoreType.DMA((PEER_SLOTS,))],
        compiler_params=_cparams("arbitrary"),
        name="peer_gather",
    )(idx, gates, hn2, h2, g2.reshape(-1, 1, d), table)


def _peer_layer(h, g, sc, sh, g2, w_q, sub_keys, u, v):
    bsz, s, d = h.shape
    q, hn = _nm_matmul(h, g, sc, sh, w_q.astype(BF16), tm=512, tn=1024, emit_hn=True, name="peer_query")
    n_tok = bsz * s
    idx, gates = _peer_topk(q.reshape(n_tok, -1), sub_keys)
    idx = idx.transpose(2, 0, 1).reshape(n_tok, PEER_SEL)
    gates = gates.transpose(2, 0, 1).reshape(n_tok, PEER_SEL)
    out = _peer_gather(idx, gates, hn.reshape(n_tok, d), h.reshape(n_tok, d), g2, _pack_tables(u, v), s)
    return out.reshape(bsz, s, d)


def _rmsnorm_kernel(x_ref, g_ref, o_ref):
    x = x_ref[...]
    o_ref[...] = x * lax.rsqrt(jnp.mean(x * x, axis=-1, keepdims=True) + EPS) * g_ref[...]


def _final_norm(h, g, tm=512):
    bsz, s, d = h.shape
    n = bsz * s
    tm = min(tm, n)
    out = pl.pallas_call(
        _rmsnorm_kernel,
        grid=(n // tm,),
        in_specs=[pl.BlockSpec((tm, d), lambda i: (i, 0)), pl.BlockSpec((1, d), lambda i: (0, 0))],
        out_specs=pl.BlockSpec((tm, d), lambda i: (i, 0)),
        out_shape=jax.ShapeDtypeStruct((n, d), F32),
        compiler_params=_cparams("parallel"),
        name="final_norm",
    )(h.reshape(n, d), g.reshape(1, d))
    return out.reshape(bsz, s, d)


def _even_mixer(h, pos3, g, sc, sh, g1, layer, w_in, w_out, conv_w, conv_b, gate_b, mlstm_g, lam_vec, diff_g):
    lambda_init = 0.8 - 0.6 * math.exp(-0.3 * layer)
    d = h.shape[-1]
    cuts = np.cumsum([2 * A_QK, A_V, A_V, 2 * MLSTM_HEADS, B_QK, B_QK, B_V])
    n_gate = 2 * MLSTM_HEADS
    w_cat = jnp.concatenate([w_in[:, :cuts[2]], w_in[:, cuts[3]:], w_in[:, cuts[2]:cuts[3]],
                             jnp.zeros((d, LANES - n_gate), w_in.dtype)], axis=1).astype(BF16)
    proj = _nm_matmul(h, g, sc, sh, w_cat, tm=512, tn=896, name="even_in_proj")
    gates_t = proj[:, :, w_cat.shape[1] - LANES:w_cat.shape[1] - LANES + n_gate].transpose(0, 2, 1)
    h_a = _mlstm(proj, gates_t, gate_b, conv_w, conv_b, mlstm_g)
    bq, bk, bv = _diff_prep(proj, pos3)
    h_b = _diff_attn(bq, bk, bv, lam_vec, diff_g, lambda_init)
    return _out_proj(h_a, h_b, w_out, h, g1)


def _odd_mixer(h, pos3, g, sc, sh, g1, w_in, w_out, ln_g, ln_b, sgu_w, sgu_b):
    proj = _nm_matmul(h, g, sc, sh, w_in.astype(BF16), tm=512, tn=1024, name="odd_in_proj")
    cq, ck, cv, sel = _moba_prep(proj, pos3)
    h_c = _moba_attn(cq, ck, cv, sel)
    h_d = _sgu(proj, ln_g, ln_b, sgu_w, sgu_b)
    return _out_proj(h_c, h_d, w_out, h, g1)


def kernel(x, c, positions, mod_w, mod_b, norm_g, final_g, even_w_in, even_w_out, mlstm_conv_w, mlstm_conv_b, mlstm_gate_b, mlstm_head_g, diff_lambda, diff_head_g, odd_w_in, odd_w_out, sgu_ln_g, sgu_ln_b, sgu_w, sgu_b, peer_w_q, peer_sub_keys, peer_u, peer_v):
    depth = mod_w.shape[0]
    bsz, s, d = x.shape
    mod = _modulation(c, mod_w, mod_b)
    pos3 = positions.reshape(bsz, s, 1)
    h = x
    for layer in range(depth):
        sh1, sc1, g1, sh2, sc2, g2 = [mod[layer, :, i * d:(i + 1) * d] for i in range(6)]
        if layer % 2 == 0:
            e = layer // 2
            h = _even_mixer(h, pos3, norm_g[layer, 0], sc1, sh1, g1, layer, even_w_in[e], even_w_out[e],
                            mlstm_conv_w[e], mlstm_conv_b[e], mlstm_gate_b[e], mlstm_head_g[e],
                            diff_lambda[e], diff_head_g[e])
        else:
            o = layer // 2
            h = _odd_mixer(h, pos3, norm_g[layer, 0], sc1, sh1, g1, odd_w_in[o], odd_w_out[o],
                           sgu_ln_g[o], sgu_ln_b[o], sgu_w[o], sgu_b[o])
        h = _peer_layer(h, norm_g[layer, 1], sc2, sh2, g2, peer_w_q[layer], peer_sub_keys[layer],
                        peer_u[layer], peer_v[layer])
    return _final_norm(h, final_g)
```

```python
import functools
import math

import numpy as np
import jax
import jax.numpy as jnp
from jax import lax
from jax.experimental import pallas as pl
from jax.experimental.pallas import tpu as pltpu

F32 = jnp.float32
BF16 = jnp.bfloat16
I32 = jnp.int32
HIGHEST = lax.Precision.HIGHEST

EPS = 1e-6
ROPE_THETA = 500000.0
LANES = 128
SUBLANES = 8
VMEM_LIMIT_BYTES = 48 * 1024 * 1024

MLSTM_HEADS, MLSTM_DQK, MLSTM_DV, MLSTM_CHUNK, CONV_W = 4, 128, 256, 128, 4
DIFF_HEADS, DIFF_DQK, DIFF_DV = 8, 64, 128
MOBA_HEADS, MOBA_DH, MOBA_BLOCK, MOBA_TOPK = 8, 128, 256, 3
SGU_GROUPS, SGU_CH, SGU_CHUNK = 8, 128, 128
PEER_HEADS, PEER_NKEYS, PEER_DQ, PEER_TOPK = 8, 128, 256, 16
PEER_SEL = PEER_HEADS * PEER_TOPK

A_QK = MLSTM_HEADS * MLSTM_DQK
A_V = MLSTM_HEADS * MLSTM_DV
B_QK = DIFF_HEADS * 2 * DIFF_DQK
B_V = DIFF_HEADS * DIFF_DV
C_W = MOBA_HEADS * MOBA_DH
D_W = SGU_GROUPS * SGU_CH


def _cparams(*sem):
    return pltpu.CompilerParams(dimension_semantics=sem, vmem_limit_bytes=VMEM_LIMIT_BYTES)


def _dot_nt(a, b, precision=None):
    return lax.dot_general(a, b, (((1,), (1,)), ((), ())), preferred_element_type=F32, precision=precision)


def _store_row(ref, row, val, cols=slice(None)):
    base = pl.multiple_of((row // SUBLANES) * SUBLANES, SUBLANES)
    blk = ref[pl.ds(base, SUBLANES), cols]
    r = lax.broadcasted_iota(I32, blk.shape, 0)
    ref[pl.ds(base, SUBLANES), cols] = jnp.where(r == row - base, val, blk)


def _sigmoid(x):
    return 1.0 / (1.0 + jnp.exp(-x))


def _silu(x):
    return x * _sigmoid(x)


def _gelu_tanh(x):
    return 0.5 * x * (1.0 + jnp.tanh(math.sqrt(2.0 / math.pi) * (x + 0.044715 * (x * x * x))))


def _log_sigmoid(x):
    return jnp.minimum(x, 0.0) - jnp.log1p(jnp.exp(-jnp.abs(x)))


def _mod_kernel(c_ref, w_ref, b_ref, o_ref):
    ca = _silu(c_ref[...])
    o_ref[0] = jnp.dot(ca, w_ref[0], preferred_element_type=F32, precision=HIGHEST) + b_ref[0]


def _modulation(c, mod_w, mod_b):
    depth, d, n = mod_w.shape
    bsz = c.shape[0]
    rows = 8
    cp = jnp.zeros((rows, d), F32).at[:bsz].set(c)
    tn = 1024
    out = pl.pallas_call(
        _mod_kernel,
        grid=(depth, n // tn),
        in_specs=[pl.BlockSpec((rows, d), lambda l, j: (0, 0)),
                  pl.BlockSpec((1, d, tn), lambda l, j: (l, 0, j)),
                  pl.BlockSpec((1, 1, tn), lambda l, j: (l, 0, j))],
        out_specs=pl.BlockSpec((1, rows, tn), lambda l, j: (l, 0, j)),
        out_shape=jax.ShapeDtypeStruct((depth, rows, n), F32),
        compiler_params=_cparams("parallel", "parallel"),
        name="modulation",
    )(cp, mod_w, mod_b.reshape(depth, 1, n))
    return out[:, :bsz]


def _nm_matmul_kernel(h_ref, g_ref, sc_ref, sh_ref, w_ref, *rest, emit_hn):
    if emit_hn:
        o_ref, hn_ref, hn_scr = rest
    else:
        o_ref, hn_scr = rest

    @pl.when(pl.program_id(2) == 0)
    def _():
        x = h_ref[0]
        y = x * lax.rsqrt(jnp.mean(x * x, axis=-1, keepdims=True) + EPS) * g_ref[...]
        y = y * (1.0 + sc_ref[0]) + sh_ref[0]
        hn_scr[...] = y.astype(BF16)
        if emit_hn:
            hn_ref[0] = y

    o_ref[0] = jnp.dot(hn_scr[...], w_ref[...], preferred_element_type=F32).astype(o_ref.dtype)


def _nm_matmul(h, g, sc, sh, w, *, tm, tn, emit_hn=False, name="nm_matmul"):
    bsz, s, d = h.shape
    n = w.shape[1]
    tm = min(tm, s)
    out_shape = [jax.ShapeDtypeStruct((bsz, s, n), F32)]
    out_specs = [pl.BlockSpec((1, tm, tn), lambda b, i, j: (b, i, j))]
    if emit_hn:
        out_shape.append(jax.ShapeDtypeStruct((bsz, s, d), F32))
        out_specs.append(pl.BlockSpec((1, tm, d), lambda b, i, j: (b, i, 0)))
    res = pl.pallas_call(
        functools.partial(_nm_matmul_kernel, emit_hn=emit_hn),
        grid=(bsz, s // tm, n // tn),
        in_specs=[pl.BlockSpec((1, tm, d), lambda b, i, j: (b, i, 0)),
                  pl.BlockSpec((1, d), lambda b, i, j: (0, 0)),
                  pl.BlockSpec((1, 1, d), lambda b, i, j: (b, 0, 0)),
                  pl.BlockSpec((1, 1, d), lambda b, i, j: (b, 0, 0)),
                  pl.BlockSpec((d, tn), lambda b, i, j: (0, j))],
        out_specs=out_specs,
        out_shape=out_shape,
        scratch_shapes=[pltpu.VMEM((tm, d), BF16)],
        compiler_params=_cparams("parallel", "parallel", "arbitrary"),
        name=name,
    )(h, g.reshape(1, d), sc.reshape(bsz, 1, d), sh.reshape(bsz, 1, d), w)
    return res if emit_hn else res[0]


def _out_proj_kernel(xa_ref, xb_ref, wa_ref, wb_ref, h_ref, g_ref, o_ref):
    acc = jnp.dot(xa_ref[0], wa_ref[...], preferred_element_type=F32)
    acc += jnp.dot(xb_ref[0], wb_ref[...], preferred_element_type=F32)
    o_ref[0] = h_ref[0] + g_ref[0] * acc


def _out_proj(xa, xb, w, h, g1, *, tm=512, tn=1024):
    bsz, s, ka = xa.shape
    kb = xb.shape[2]
    n = w.shape[1]
    tm = min(tm, s)
    wa = w[:ka].astype(BF16)
    wb = w[ka:].astype(BF16)
    return pl.pallas_call(
        _out_proj_kernel,
        grid=(bsz, s // tm, n // tn),
        in_specs=[pl.BlockSpec((1, tm, ka), lambda b, i, j: (b, i, 0)),
                  pl.BlockSpec((1, tm, kb), lambda b, i, j: (b, i, 0)),
                  pl.BlockSpec((ka, tn), lambda b, i, j: (0, j)),
                  pl.BlockSpec((kb, tn), lambda b, i, j: (0, j)),
                  pl.BlockSpec((1, tm, tn), lambda b, i, j: (b, i, j)),
                  pl.BlockSpec((1, 1, tn), lambda b, i, j: (b, 0, j))],
        out_specs=pl.BlockSpec((1, tm, tn), lambda b, i, j: (b, i, j)),
        out_shape=jax.ShapeDtypeStruct((bsz, s, n), F32),
        compiler_params=_cparams("parallel", "parallel", "parallel"),
        name="out_proj",
    )(xa, xb, wa, wb, h, g1.reshape(bsz, 1, n))


def _mlstm_kernel(gb_ref, aqk_ref, v_ref, ao_ref, gcol_ref, grow_ref, cw_ref, cb_ref, hg_ref,
                  o_ref, xbuf, c_scr, n_scr, m_scr):
    L = MLSTM_CHUNK
    H, DQK, DV = MLSTM_HEADS, MLSTM_DQK, MLSTM_DV
    c = pl.program_id(1)

    @pl.when(c == 0)
    def _():
        xbuf[0:8, :] = jnp.zeros((8, 2 * A_QK), F32)
        c_scr[...] = jnp.zeros_like(c_scr)
        n_scr[...] = jnp.zeros_like(n_scr)
        m_scr[...] = jnp.zeros_like(m_scr)

    @pl.when(c > 0)
    def _():
        xbuf[0:8, :] = xbuf[L:L + 8, :]

    xbuf[8:L + 8, :] = aqk_ref[0]
    conv = cb_ref[...] + cw_ref[0:1, :] * xbuf[5:5 + L, :]
    for j in range(1, CONV_W):
        conv = conv + cw_ref[j:j + 1, :] * xbuf[5 + j:5 + j + L, :]
    qk = _silu(conv)

    r_i = lax.broadcasted_iota(I32, (L, L), 0)
    c_i = lax.broadcasted_iota(I32, (L, L), 1)
    tril = c_i <= r_i
    gcol = gcol_ref[0]
    grow = grow_ref[0]

    for h in range(H):
        q = qk[:, h * DQK:(h + 1) * DQK] * (DQK ** -0.5)
        k = qk[:, A_QK + h * DQK:A_QK + (h + 1) * DQK]
        v = v_ref[0, :, h * DV:(h + 1) * DV]
        i_col = gcol[:, h:h + 1] + gb_ref[h]
        i_row = grow[h:h + 1, :] + gb_ref[h]
        lf_col = _log_sigmoid(gcol[:, H + h:H + h + 1] + gb_ref[H + h])
        lf_row = _log_sigmoid(grow[H + h:H + h + 1, :] + gb_ref[H + h])
        bcum_col = jnp.sum(jnp.where(tril, lf_row, 0.0), axis=1, keepdims=True)
        bcum_row = jnp.sum(jnp.where(r_i <= c_i, lf_col, 0.0), axis=0, keepdims=True)
        g_tot = jnp.sum(lf_row, axis=1, keepdims=True)
        m_prev = m_scr[h:h + 1, 0:1]
        d_intra = jnp.where(tril, bcum_col - bcum_row + i_row, -jnp.inf)
        d_inter = bcum_col + m_prev
        m_t = jnp.maximum(d_inter, jnp.max(d_intra, axis=1, keepdims=True))
        w_intra = jnp.exp(d_intra - m_t)
        w_inter = jnp.exp(d_inter - m_t)
        qb = q.astype(BF16)
        kb = k.astype(BF16)
        vb = v.astype(BF16)
        s = _dot_nt(qb, kb) * w_intra
        c_prev = c_scr[h]
        n_prev = n_scr[h:h + 1, :]
        num = jnp.dot(s.astype(BF16), vb, preferred_element_type=F32)
        num = num + w_inter * jnp.dot(qb, c_prev.astype(BF16), preferred_element_type=F32)
        den = jnp.sum(s, axis=1, keepdims=True) + w_inter * jnp.sum(q * n_prev, axis=1, keepdims=True)
        hh = num / jnp.maximum(jnp.abs(den), jnp.exp(-m_t))
        d_state = g_tot - bcum_col + i_col
        m_new = jnp.maximum(g_tot + m_prev, jnp.max(d_state, axis=0, keepdims=True))
        w_s = jnp.exp(d_state - m_new)
        w_c = jnp.exp(g_tot + m_prev - m_new)
        kw = k * w_s
        c_scr[h] = w_c * c_prev + jnp.dot(kw.T.astype(BF16), vb, preferred_element_type=F32)
        n_scr[h:h + 1, :] = w_c * n_prev + jnp.sum(kw, axis=0, keepdims=True)
        m_scr[h:h + 1, 0:1] = m_new
        y = hh * lax.rsqrt(jnp.mean(hh * hh, axis=-1, keepdims=True) + EPS) * hg_ref[:, h * DV:(h + 1) * DV]
        y = y * _sigmoid(ao_ref[0, :, h * DV:(h + 1) * DV])
        o_ref[0, :, h * DV:(h + 1) * DV] = y.astype(o_ref.dtype)


def _mlstm(proj, gates_t, gate_b, conv_w, conv_b, head_g):
    bsz, s, _ = proj.shape
    L = MLSTM_CHUNK
    gate_blk = (2 * A_QK + 2 * A_V + 2 * B_QK + B_V) // LANES
    return pl.pallas_call(
        _mlstm_kernel,
        grid=(bsz, s // L),
        in_specs=[pl.BlockSpec(memory_space=pltpu.SMEM),
                  pl.BlockSpec((1, L, 2 * A_QK), lambda b, c: (b, c, 0)),
                  pl.BlockSpec((1, L, A_V), lambda b, c: (b, c, 1)),
                  pl.BlockSpec((1, L, A_V), lambda b, c: (b, c, 2)),
                  pl.BlockSpec((1, L, LANES), lambda b, c: (b, c, gate_blk)),
                  pl.BlockSpec((1, 2 * MLSTM_HEADS, L), lambda b, c: (b, 0, c)),
                  pl.BlockSpec((CONV_W, 2 * A_QK), lambda b, c: (0, 0)),
                  pl.BlockSpec((1, 2 * A_QK), lambda b, c: (0, 0)),
                  pl.BlockSpec((1, A_V), lambda b, c: (0, 0))],
        out_specs=pl.BlockSpec((1, L, A_V), lambda b, c: (b, c, 0)),
        out_shape=jax.ShapeDtypeStruct((bsz, s, A_V), BF16),
        scratch_shapes=[pltpu.VMEM((L + 8, 2 * A_QK), F32),
                        pltpu.VMEM((MLSTM_HEADS, MLSTM_DQK, MLSTM_DV), F32),
                        pltpu.VMEM((8, MLSTM_DQK), F32),
                        pltpu.VMEM((8, LANES), F32)],
        compiler_params=_cparams("parallel", "arbitrary"),
        name="mlstm",
    )(gate_b.reshape(-1), proj, proj, proj, proj, gates_t, conv_w, conv_b.reshape(1, -1),
      head_g.reshape(1, -1))


def _rope_consts(dh):
    half = dh // 8
    lane = np.arange(LANES) % dh
    inv_half = ROPE_THETA ** (-jnp.arange(half, dtype=F32) / half)
    inv = jnp.where(jnp.asarray(lane < 2 * half), inv_half[lane % half], 0.0)
    lo = (lane < half).astype(np.float32)
    hi = ((lane >= half) & (lane < 2 * half)).astype(np.float32)
    return (inv.astype(F32).reshape(1, LANES), jnp.asarray(lo).reshape(1, LANES),
            jnp.asarray(hi).reshape(1, LANES))


def _rope_tables(pos_ref, inv_ref, lo_ref, hi_ref):
    ang = pos_ref[0].astype(F32) * inv_ref[...]
    cs = jnp.cos(ang)
    sn = jnp.sin(ang)
    return cs, -sn * lo_ref[...], sn * hi_ref[...]


def _rope_apply(x, cs, sa, sb, half):
    return x * cs + pltpu.roll(x, LANES - half, 1) * sa + pltpu.roll(x, half, 1) * sb


def _diff_prep_kernel(pos_ref, inv_ref, lo_ref, hi_ref, q_ref, k_ref, v_ref, qo_ref, ko_ref, vo_ref):
    cs, sa, sb = _rope_tables(pos_ref, inv_ref, lo_ref, hi_ref)
    half = DIFF_DQK // 8
    scale = DIFF_DQK ** -0.5
    for h in range(DIFF_HEADS):
        sl = slice(h * LANES, (h + 1) * LANES)
        qo_ref[0, :, sl] = (_rope_apply(q_ref[0, :, sl], cs, sa, sb, half) * scale).astype(BF16)
        ko_ref[0, :, sl] = _rope_apply(k_ref[0, :, sl], cs, sa, sb, half).astype(BF16)
    vo_ref[0] = v_ref[0].astype(BF16)


def _diff_prep(proj, pos3, ts=512):
    bsz, s, _ = proj.shape
    ts = min(ts, s)
    inv, lo, hi = _rope_consts(DIFF_DQK)
    base = (2 * A_QK + 2 * A_V) // B_QK
    cst = pl.BlockSpec((1, LANES), lambda b, i: (0, 0))
    osd = jax.ShapeDtypeStruct((bsz, s, B_QK), BF16)
    return pl.pallas_call(
        _diff_prep_kernel,
        grid=(bsz, s // ts),
        in_specs=[pl.BlockSpec((1, ts, 1), lambda b, i: (b, i, 0)), cst, cst, cst,
                  pl.BlockSpec((1, ts, B_QK), lambda b, i: (b, i, base)),
                  pl.BlockSpec((1, ts, B_QK), lambda b, i: (b, i, base + 1)),
                  pl.BlockSpec((1, ts, B_V), lambda b, i: (b, i, base + 2))],
        out_specs=[pl.BlockSpec((1, ts, B_QK), lambda b, i: (b, i, 0))] * 3,
        out_shape=[osd, osd, osd],
        compiler_params=_cparams("parallel", "parallel"),
        name="diff_prep",
    )(pos3, inv, lo, hi, proj, proj, proj)


def _diff_attn_kernel(lam_ref, q_ref, k_ref, v_ref, g_ref, o_ref, qs_scr, m_scr, l_scr, acc_scr,
                      *, tq, tk, lambda_init):
    i = pl.program_id(2)
    j = pl.program_id(3)

    @pl.when(j == 0)
    def _():
        q = q_ref[0]
        lane = lax.broadcasted_iota(I32, q.shape, 1)
        zero = jnp.zeros_like(q)
        qs_scr[0:tq, :] = jnp.where(lane < DIFF_DQK, q, zero)
        qs_scr[tq:2 * tq, :] = jnp.where(lane >= DIFF_DQK, q, zero)
        m_scr[...] = jnp.full_like(m_scr, -jnp.inf)
        l_scr[...] = jnp.zeros_like(l_scr)
        acc_scr[...] = jnp.zeros_like(acc_scr)

    def step(masked):
        s = _dot_nt(qs_scr[...], k_ref[0])
        if masked:
            row = lax.broadcasted_iota(I32, (2 * tq, tk), 0)
            row = jnp.where(row >= tq, row - tq, row)
            col = lax.broadcasted_iota(I32, (2 * tq, tk), 1)
            s = jnp.where((j * tk + col) <= (i * tq + row), s, -jnp.inf)
        m_prev = m_scr[...]
        m_new = jnp.maximum(m_prev, jnp.max(s, axis=1, keepdims=True))
        alpha = jnp.exp(m_prev - m_new)
        p = jnp.exp(s - m_new)
        l_scr[...] = alpha * l_scr[...] + jnp.sum(p, axis=1, keepdims=True)
        acc_scr[...] = alpha * acc_scr[...] + jnp.dot(p.astype(BF16), v_ref[0], preferred_element_type=F32)
        m_scr[...] = m_new

    @pl.when(j < i)
    def _():
        step(False)

    @pl.when(j == i)
    def _():
        step(True)
        lv = lam_ref[...]
        lam = (jnp.exp(jnp.sum(lv[0:1] * lv[1:2], axis=1, keepdims=True))
               - jnp.exp(jnp.sum(lv[2:3] * lv[3:4], axis=1, keepdims=True)) + lambda_init)
        o0 = acc_scr[0:tq, :] / l_scr[0:tq, :]
        o1 = acc_scr[tq:2 * tq, :] / l_scr[tq:2 * tq, :]
        a = o0 - lam * o1
        y = a * lax.rsqrt(jnp.mean(a * a, axis=-1, keepdims=True) + EPS) * g_ref[0]
        o_ref[0] = (y * (1.0 - lambda_init)).astype(o_ref.dtype)


def _diff_attn(q, k, v, lam_vec, head_g, lambda_init, *, tq=512):
    bsz, s, _ = q.shape
    tq = min(tq, s)
    nq = s // tq
    return pl.pallas_call(
        functools.partial(_diff_attn_kernel, tq=tq, tk=tq, lambda_init=lambda_init),
        grid=(bsz, DIFF_HEADS, nq, nq),
        in_specs=[pl.BlockSpec((4, DIFF_DQK), lambda b, h, i, j: (0, 0)),
                  pl.BlockSpec((1, tq, LANES), lambda b, h, i, j: (b, i, h)),
                  pl.BlockSpec((1, tq, LANES), lambda b, h, i, j: (b, jnp.minimum(j, i), h)),
                  pl.BlockSpec((1, tq, LANES), lambda b, h, i, j: (b, jnp.minimum(j, i), h)),
                  pl.BlockSpec((1, 1, DIFF_DV), lambda b, h, i, j: (h, 0, 0))],
        out_specs=pl.BlockSpec((1, tq, DIFF_DV), lambda b, h, i, j: (b, i, h)),
        out_shape=jax.ShapeDtypeStruct((bsz, s, B_V), BF16),
        scratch_shapes=[pltpu.VMEM((2 * tq, LANES), BF16),
                        pltpu.VMEM((2 * tq, 1), F32),
                        pltpu.VMEM((2 * tq, 1), F32),
                        pltpu.VMEM((2 * tq, DIFF_DV), F32)],
        compiler_params=_cparams("parallel", "parallel", "parallel", "arbitrary"),
        name="diff_attn",
    )(lam_vec, q, k, v, head_g.reshape(DIFF_HEADS, 1, DIFF_DV))


def _moba_prep_kernel(pos_ref, inv_ref, lo_ref, hi_ref, q_ref, k_ref, v_ref,
                      qo_ref, ko_ref, vo_ref, sel_ref, kmt_scr, qf_scr, *, nb):
    i = pl.program_id(1)
    tb = MOBA_BLOCK
    half = MOBA_DH // 8
    scale = MOBA_DH ** -0.5

    @pl.when(i == 0)
    def _():
        kmt_scr[...] = jnp.zeros_like(kmt_scr)

    cs, sa, sb = _rope_tables(pos_ref, inv_ref, lo_ref, hi_ref)
    kmeans = []
    for h in range(MOBA_HEADS):
        sl = slice(h * LANES, (h + 1) * LANES)
        qr = _rope_apply(q_ref[0, :, sl], cs, sa, sb, half)
        kr = _rope_apply(k_ref[0, :, sl], cs, sa, sb, half)
        qf_scr[:, sl] = qr
        qo_ref[0, :, sl] = (qr * scale).astype(BF16)
        ko_ref[0, :, sl] = kr.astype(BF16)
        kmeans.append(jnp.mean(kr, axis=0, keepdims=True))
    vo_ref[0] = v_ref[0].astype(BF16)

    gate = _dot_nt(qf_scr[...], kmt_scr[...], precision=HIGHEST)
    lane = lax.broadcasted_iota(I32, (tb, LANES), 1)
    n = lane % nb
    valid = n < i
    gm = jnp.where(valid, gate, -jnp.inf)
    rank = jnp.zeros((tb, LANES), I32)
    for d in range(1, nb):
        wraps = (n + d) >= nb
        partner = jnp.where(wraps, pltpu.roll(gm, nb - d, 1), pltpu.roll(gm, LANES - d, 1))
        pn = jnp.where(wraps, n + d - nb, n + d)
        beats = (partner > gm) | ((partner == gm) & (pn < n))
        rank = rank + beats.astype(I32)
    sel_ref[0] = jnp.where(valid & (rank < MOBA_TOPK), 1.0, 0.0).astype(F32)

    for h in range(MOBA_HEADS):
        _store_row(kmt_scr, h * nb + i, kmeans[h], slice(h * LANES, (h + 1) * LANES))


def _moba_prep(proj, pos3):
    bsz, s, _ = proj.shape
    tb = MOBA_BLOCK
    nb = s // tb
    assert s % tb == 0 and MOBA_HEADS * nb <= LANES and nb >= MOBA_TOPK
    inv, lo, hi = _rope_consts(MOBA_DH)
    cst = pl.BlockSpec((1, LANES), lambda b, i: (0, 0))
    osd = jax.ShapeDtypeStruct((bsz, s, C_W), BF16)
    return pl.pallas_call(
        functools.partial(_moba_prep_kernel, nb=nb),
        grid=(bsz, nb),
        in_specs=[pl.BlockSpec((1, tb, 1), lambda b, i: (b, i, 0)), cst, cst, cst,
                  pl.BlockSpec((1, tb, C_W), lambda b, i: (b, i, 0)),
                  pl.BlockSpec((1, tb, C_W), lambda b, i: (b, i, 1)),
                  pl.BlockSpec((1, tb, C_W), lambda b, i: (b, i, 2))],
        out_specs=[pl.BlockSpec((1, tb, C_W), lambda b, i: (b, i, 0))] * 3
        + [pl.BlockSpec((1, tb, LANES), lambda b, i: (b, i, 0))],
        out_shape=[osd, osd, osd, jax.ShapeDtypeStruct((bsz, s, LANES), F32)],
        scratch_shapes=[pltpu.VMEM((LANES, C_W), F32), pltpu.VMEM((tb, C_W), F32)],
        compiler_params=_cparams("parallel", "arbitrary"),
        name="moba_prep",
    )(pos3, inv, lo, hi, proj, proj, proj)


def _moba_attn_kernel(q_ref, k_ref, v_ref, sel_ref, o_ref, m_scr, l_scr, acc_scr, *, nb):
    h = pl.program_id(1)
    i = pl.program_id(2)
    j = pl.program_id(3)
    tb = MOBA_BLOCK
    tq = 2 * tb

    def picked(blk):
        lane = lax.broadcasted_iota(I32, (tq, LANES), 1)
        return jnp.max(jnp.where(lane == h * nb + blk, sel_ref[0], 0.0), axis=1, keepdims=True) > 0.5

    @pl.when(j == 0)
    def _():
        s = _dot_nt(q_ref[0], k_ref[0])
        row = lax.broadcasted_iota(I32, (tq, tq), 0)
        col = lax.broadcasted_iota(I32, (tq, tq), 1)
        past_block = (row >= tb) & (col < tb)
        ok = (col <= row) & (jnp.logical_not(past_block) | picked(2 * i))
        s = jnp.where(ok, s, -jnp.inf)
        m = jnp.max(s, axis=1, keepdims=True)
        p = jnp.exp(s - m)
        m_scr[...] = m
        l_scr[...] = jnp.sum(p, axis=1, keepdims=True)
        acc_scr[...] = jnp.dot(p.astype(BF16), v_ref[0], preferred_element_type=F32)

    @pl.when((j > 0) & (j <= i))
    def _():
        col = lax.broadcasted_iota(I32, (tq, tq), 1)
        kb = 2 * (i - j)
        ok = ((col < tb) & picked(kb)) | ((col >= tb) & picked(kb + 1))
        s = jnp.where(ok, _dot_nt(q_ref[0], k_ref[0]), -jnp.inf)
        m_prev = m_scr[...]
        m_new = jnp.maximum(m_prev, jnp.max(s, axis=1, keepdims=True))
        alpha = jnp.exp(m_prev - m_new)
        p = jnp.exp(s - m_new)
        l_scr[...] = alpha * l_scr[...] + jnp.sum(p, axis=1, keepdims=True)
        acc_scr[...] = alpha * acc_scr[...] + jnp.dot(p.astype(BF16), v_ref[0], preferred_element_type=F32)
        m_scr[...] = m_new

    @pl.when(j == i)
    def _():
        o_ref[0] = (acc_scr[...] / l_scr[...]).astype(o_ref.dtype)


def _moba_attn(q, k, v, sel):
    bsz, s, _ = q.shape
    tq = 2 * MOBA_BLOCK
    nb = s // MOBA_BLOCK
    assert s % tq == 0
    nq = s // tq
    kv_spec = pl.BlockSpec((1, tq, MOBA_DH), lambda b, h, i, j: (b, jnp.maximum(i - j, 0), h))
    return pl.pallas_call(
        functools.partial(_moba_attn_kernel, nb=nb),
        grid=(bsz, MOBA_HEADS, nq, nq),
        in_specs=[pl.BlockSpec((1, tq, MOBA_DH), lambda b, h, i, j: (b, i, h)), kv_spec, kv_spec,
                  pl.BlockSpec((1, tq, LANES), lambda b, h, i, j: (b, i, 0))],
        out_specs=pl.BlockSpec((1, tq, MOBA_DH), lambda b, h, i, j: (b, i, h)),
        out_shape=jax.ShapeDtypeStruct((bsz, s, C_W), BF16),
        scratch_shapes=[pltpu.VMEM((tq, 1), F32), pltpu.VMEM((tq, 1), F32), pltpu.VMEM((tq, MOBA_DH), F32)],
        compiler_params=_cparams("parallel", "parallel", "parallel", "arbitrary"),
        name="moba_attn",
    )(q, k, v, sel)


def _sgu_kernel(u_ref, v_ref, lng_ref, lnb_ref, w_ref, bt_ref, o_ref, *, ts):
    L = SGU_CHUNK
    r_i = lax.broadcasted_iota(I32, (L, L), 0)
    c_i = lax.broadcasted_iota(I32, (L, L), 1)
    tril = c_i <= r_i
    for g in range(SGU_GROUPS):
        sl = slice(g * SGU_CH, (g + 1) * SGU_CH)
        w = jnp.where(tril, w_ref[g], 0.0).astype(BF16)
        bias = bt_ref[:, g:g + 1]
        for c in range(ts // L):
            rows = slice(c * L, (c + 1) * L)
            vv = _gelu_tanh(v_ref[0, rows, sl])
            mu = jnp.mean(vv, axis=-1, keepdims=True)
            dv = vv - mu
            var = jnp.mean(dv * dv, axis=-1, keepdims=True)
            vn = dv * lax.rsqrt(var + EPS) * lng_ref[:, sl] + lnb_ref[:, sl]
            mixed = jnp.dot(w, vn.astype(BF16), preferred_element_type=F32) + bias
            o_ref[0, rows, sl] = (_gelu_tanh(u_ref[0, rows, sl]) * mixed).astype(o_ref.dtype)


def _sgu(proj, ln_g, ln_b, w_s, b_s, ts=256):
    bsz, s, _ = proj.shape
    ts = min(ts, s)
    base = 3 * C_W // D_W
    return pl.pallas_call(
        functools.partial(_sgu_kernel, ts=ts),
        grid=(bsz, s // ts),
        in_specs=[pl.BlockSpec((1, ts, D_W), lambda b, i: (b, i, base)),
                  pl.BlockSpec((1, ts, D_W), lambda b, i: (b, i, base + 1)),
                  pl.BlockSpec((1, D_W), lambda b, i: (0, 0)),
                  pl.BlockSpec((1, D_W), lambda b, i: (0, 0)),
                  pl.BlockSpec((SGU_GROUPS, SGU_CHUNK, SGU_CHUNK), lambda b, i: (0, 0, 0)),
                  pl.BlockSpec((SGU_CHUNK, SGU_GROUPS), lambda b, i: (0, 0))],
        out_specs=pl.BlockSpec((1, ts, D_W), lambda b, i: (b, i, 0)),
        out_shape=jax.ShapeDtypeStruct((bsz, s, D_W), BF16),
        compiler_params=_cparams("parallel", "parallel"),
        name="sgu",
    )(proj, proj, ln_g.reshape(1, -1), ln_b.reshape(1, -1), w_s, b_s.T)


def _topk_rows(s_scr, k, val_scr, idx_scr, payload_scr=None):
    n, t = s_scr.shape
    row = lax.broadcasted_iota(I32, (n, t), 0).astype(F32)

    def body(it, carry):
        s = s_scr[...]
        m = jnp.max(s, axis=0, keepdims=True)
        pos = jnp.min(jnp.where(s == m, row, float(n)), axis=0, keepdims=True)
        hit = row == pos
        _store_row(val_scr, it, m)
        if payload_scr is None:
            _store_row(idx_scr, it, pos)
        else:
            _store_row(idx_scr, it, jnp.max(jnp.where(hit, payload_scr[...], -1.0), axis=0, keepdims=True))
        s_scr[...] = jnp.where(hit, -jnp.inf, s)
        return carry

    val_scr[...] = jnp.zeros_like(val_scr)
    idx_scr[...] = jnp.zeros_like(idx_scr)
    lax.fori_loop(0, k, body, 0)


def _peer_topk_kernel(q_ref, keys_ref, idx_ref, g_ref, s_scr, v1, i1, v2, i2, cand, cidx, vt, it):
    K = PEER_TOPK
    half = PEER_DQ // 2
    for c, (vs, ix) in enumerate(((v1, i1), (v2, i2))):
        s_scr[...] = _dot_nt(keys_ref[0, c], q_ref[:, c * half:(c + 1) * half], precision=HIGHEST)
        _topk_rows(s_scr, K, vs, ix)
    cand[...] = jnp.full(cand.shape, -jnp.inf, F32)
    cidx[...] = jnp.zeros(cidx.shape, F32)
    off = 0
    for a in range(K):
        nb = K // (a + 1)
        cand[off:off + nb, :] = v1[a:a + 1, :] + v2[0:nb, :]
        cidx[off:off + nb, :] = i1[a:a + 1, :] * float(PEER_NKEYS) + i2[0:nb, :]
        off += nb
    _topk_rows(cand, K, vt, it, payload_scr=cidx)
    ts = vt[...]
    e = jnp.exp(ts - jnp.max(ts, axis=0, keepdims=True))
    g_ref[0] = e / jnp.sum(e, axis=0, keepdims=True)
    idx_ref[0] = it[...].astype(I32)


def _peer_topk(q2, sub_keys, tt=512):
    n_tok = q2.shape[0]
    tt = min(tt, n_tok)
    K = PEER_TOPK
    half = PEER_DQ // 2
    vs = lambda r, dt: pltpu.VMEM((r, tt), dt)
    n_cand = sum(K // (a + 1) for a in range(K))
    n_cand = -(-n_cand // SUBLANES) * SUBLANES
    return pl.pallas_call(
        _peer_topk_kernel,
        grid=(n_tok // tt, PEER_HEADS),
        in_specs=[pl.BlockSpec((tt, PEER_DQ), lambda i, h: (i, h)),
                  pl.BlockSpec((1, 2, PEER_NKEYS, half), lambda i, h: (h, 0, 0, 0))],
        out_specs=[pl.BlockSpec((1, K, tt), lambda i, h: (h, 0, i)),
                   pl.BlockSpec((1, K, tt), lambda i, h: (h, 0, i))],
        out_shape=[jax.ShapeDtypeStruct((PEER_HEADS, K, n_tok), I32),
                   jax.ShapeDtypeStruct((PEER_HEADS, K, n_tok), F32)],
        scratch_shapes=[vs(PEER_NKEYS, F32), vs(K, F32), vs(K, F32), vs(K, F32), vs(K, F32),
                        vs(n_cand, F32), vs(n_cand, F32), vs(K, F32), vs(K, F32)],
        compiler_params=_cparams("parallel", "parallel"),
        name="peer_topk",
    )(q2, sub_keys)


def _pack_tables(u, v):
    ub = lax.bitcast_convert_type(u.astype(BF16), jnp.uint16).astype(jnp.uint32)
    vb = lax.bitcast_convert_type(v.astype(BF16), jnp.uint16).astype(jnp.uint32)
    packed = lax.bitcast_convert_type((ub << 16) | vb, I32)
    return packed.reshape(packed.shape[0], 1, packed.shape[1])


PEER_SLOTS = SUBLANES
PEER_AHEAD = 6
PEER_DMA_QUEUES = 2


def _peer_gather_kernel(idx_ref, gate_ref, x_ref, h_ref, g2_ref, tab_hbm, o_ref, *rest, tt):
    R = PEER_SEL
    bufs, (stage, sem) = rest[:PEER_SLOTS], rest[PEER_SLOTS:]
    step = pl.program_id(0)
    d = x_ref.shape[1]
    nch = d // LANES
    ngrp = R // SUBLANES
    per_grp = R // ngrp

    def row_copy(t, j, slot):
        return pltpu.make_async_copy(tab_hbm.at[idx_ref[t, j]], bufs[slot].at[pl.ds(j, 1)], sem.at[slot])

    def wait(slot):
        pltpu.make_async_copy(bufs[(slot + 1) % PEER_SLOTS], bufs[slot], sem.at[slot]).wait()

    def tile(slot, g, c):
        return bufs[slot][g * SUBLANES:(g + 1) * SUBLANES, c * LANES:(c + 1) * LANES]

    lane = lax.broadcasted_iota(I32, (SUBLANES, LANES), 1)
    sub = lax.broadcasted_iota(I32, (SUBLANES, LANES), 0)
    hi_mask = jnp.int32(-65536)

    @pl.when(step == 0)
    def _():
        for t0 in range(PEER_AHEAD):
            def prime(j, carry, t0=t0):
                row_copy(t0, j, t0).start()
                return carry
            lax.fori_loop(0, R, prime, 0)

    def group(gi, carry):
        base = pl.multiple_of(gi * SUBLANES, SUBLANES)
        xg = x_ref[pl.ds(base, SUBLANES), :]
        gg = gate_ref[pl.ds(base, SUBLANES), :]
        for r in range(SUBLANES):
            slot = r
            nxt = base + r + PEER_AHEAD
            nslot = (r + PEER_AHEAD) % PEER_SLOTS
            wait(slot)
            xb = [jnp.broadcast_to(xg[r:r + 1, c * LANES:(c + 1) * LANES], (SUBLANES, LANES)) for c in range(nch)]
            grow = jnp.broadcast_to(gg[r:r + 1, :], (SUBLANES, LANES))
            wb = []
            for g in range(ngrp):
                for j in range(g * per_grp, (g + 1) * per_grp):
                    row_copy(nxt, j, nslot).start(priority=j % PEER_DMA_QUEUES)
                acc = lax.bitcast_convert_type(tile(slot, g, 0) & hi_mask, F32) * xb[0]
                for c in range(1, nch):
                    acc = acc + lax.bitcast_convert_type(tile(slot, g, c) & hi_mask, F32) * xb[c]
                act = jnp.sum(acc, axis=1, keepdims=True)
                gcol = jnp.sum(jnp.where(lane == g * SUBLANES + sub, grow, 0.0), axis=1, keepdims=True)
                wb.append(jnp.broadcast_to(gcol * _gelu_tanh(act), (SUBLANES, LANES)))
            for c in range(nch):
                acc = lax.bitcast_convert_type(tile(slot, 0, c) << 16, F32) * wb[0]
                for g in range(1, ngrp):
                    acc = acc + lax.bitcast_convert_type(tile(slot, g, c) << 16, F32) * wb[g]
                stage[r:r + 1, c * LANES:(c + 1) * LANES] = jnp.sum(acc, axis=0, keepdims=True)
        rows = pl.ds(base, SUBLANES)
        o_ref[rows, :] = h_ref[rows, :] + g2_ref[0] * stage[...]
        return carry

    lax.fori_loop(0, tt // SUBLANES, group, 0)

    @pl.when(step == pl.num_programs(0) - 1)
    def _():
        for k in range(PEER_AHEAD):
            wait((tt + k) % PEER_SLOTS)


def _peer_gather(idx, gates, hn2, h2, g2, table, s, tt=128):
    n_tok, d = hn2.shape
    tt = min(tt, s)
    assert s % tt == 0 and tt % PEER_SLOTS == 0 and d % (SUBLANES * LANES) == 0 and PEER_SEL % SUBLANES == 0
    assert PEER_SEL <= LANES and PEER_AHEAD < PEER_SLOTS - 1
    per_b = s // tt
    R = PEER_SEL
    nsteps = n_tok // tt
    idx3 = idx.reshape(nsteps, tt, R)
    head = jnp.concatenate([idx3[1:, :SUBLANES], jnp.zeros((1, SUBLANES, R), I32)], axis=0)
    idx = jnp.concatenate([idx3, head], axis=1).reshape(nsteps * (tt + SUBLANES), R)
    return pl.pallas_call(
        functools.partial(_peer_gather_kernel, tt=tt),
        grid=(nsteps,),
        in_specs=[pl.BlockSpec((tt + SUBLANES, R), lambda i: (i, 0), memory_space=pltpu.SMEM),
                  pl.BlockSpec((tt, R), lambda i: (i, 0)),
                  pl.BlockSpec((tt, d), lambda i: (i, 0)),
                  pl.BlockSpec((tt, d), lambda i: (i, 0)),
                  pl.BlockSpec((1, 1, d), lambda i: (i // per_b, 0, 0)),
                  pl.BlockSpec(memory_space=pl.ANY)],
        out_specs=pl.BlockSpec((tt, d), lambda i: (i, 0)),
        out_shape=jax.ShapeDtypeStruct((n_tok, d), F32),
        scratch_shapes=[pltpu.VMEM((R, d), I32)] * PEER_SLOTS
        + [pltpu.VMEM((SUBLANES, d), F32), pltpu.SemaphoreType.DMA((PEER_SLOTS,))],
        compiler_params=_cparams("arbitrary"),
        name="peer_gather",
    )(idx, gates, hn2, h2, g2.reshape(-1, 1, d), table)


def _peer_layer(h, g, sc, sh, g2, w_q, sub_keys, u, v):
    bsz, s, d = h.shape
    q, hn = _nm_matmul(h, g, sc, sh, w_q.astype(BF16), tm=512, tn=1024, emit_hn=True, name="peer_query")
    n_tok = bsz * s
    idx, gates = _peer_topk(q.reshape(n_tok, -1), sub_keys)
    idx = idx.transpose(2, 0, 1).reshape(n_tok, PEER_SEL)
    gates = gates.transpose(2, 0, 1).reshape(n_tok, PEER_SEL)
    out = _peer_gather(idx, gates, hn.reshape(n_tok, d), h.reshape(n_tok, d), g2, _pack_tables(u, v), s)
    return out.reshape(bsz, s, d)


def _rmsnorm_kernel(x_ref, g_ref, o_ref):
    x = x_ref[...]
    o_ref[...] = x * lax.rsqrt(jnp.mean(x * x, axis=-1, keepdims=True) + EPS) * g_ref[...]


def _final_norm(h, g, tm=512):
    bsz, s, d = h.shape
    n = bsz * s
    tm = min(tm, n)
    out = pl.pallas_call(
        _rmsnorm_kernel,
        grid=(n // tm,),
        in_specs=[pl.BlockSpec((tm, d), lambda i: (i, 0)), pl.BlockSpec((1, d), lambda i: (0, 0))],
        out_specs=pl.BlockSpec((tm, d), lambda i: (i, 0)),
        out_shape=jax.ShapeDtypeStruct((n, d), F32),
        compiler_params=_cparams("parallel"),
        name="final_norm",
    )(h.reshape(n, d), g.reshape(1, d))
    return out.reshape(bsz, s, d)


def _even_mixer(h, pos3, g, sc, sh, g1, layer, w_in, w_out, conv_w, conv_b, gate_b, mlstm_g, lam_vec, diff_g):
    lambda_init = 0.8 - 0.6 * math.exp(-0.3 * layer)
    d = h.shape[-1]
    cuts = np.cumsum([2 * A_QK, A_V, A_V, 2 * MLSTM_HEADS, B_QK, B_QK, B_V])
    n_gate = 2 * MLSTM_HEADS
    w_cat = jnp.concatenate([w_in[:, :cuts[2]], w_in[:, cuts[3]:], w_in[:, cuts[2]:cuts[3]],
                             jnp.zeros((d, LANES - n_gate), w_in.dtype)], axis=1).astype(BF16)
    proj = _nm_matmul(h, g, sc, sh, w_cat, tm=512, tn=896, name="even_in_proj")
    gates_t = proj[:, :, w_cat.shape[1] - LANES:w_cat.shape[1] - LANES + n_gate].transpose(0, 2, 1)
    h_a = _mlstm(proj, gates_t, gate_b, conv_w, conv_b, mlstm_g)
    bq, bk, bv = _diff_prep(proj, pos3)
    h_b = _diff_attn(bq, bk, bv, lam_vec, diff_g, lambda_init)
    return _out_proj(h_a, h_b, w_out, h, g1)


def _odd_mixer(h, pos3, g, sc, sh, g1, w_in, w_out, ln_g, ln_b, sgu_w, sgu_b):
    proj = _nm_matmul(h, g, sc, sh, w_in.astype(BF16), tm=512, tn=1024, name="odd_in_proj")
    cq, ck, cv, sel = _moba_prep(proj, pos3)
    h_c = _moba_attn(cq, ck, cv, sel)
    h_d = _sgu(proj, ln_g, ln_b, sgu_w, sgu_b)
    return _out_proj(h_c, h_d, w_out, h, g1)


def kernel(x, c, positions, mod_w, mod_b, norm_g, final_g, even_w_in, even_w_out, mlstm_conv_w, mlstm_conv_b, mlstm_gate_b, mlstm_head_g, diff_lambda, diff_head_g, odd_w_in, odd_w_out, sgu_ln_g, sgu_ln_b, sgu_w, sgu_b, peer_w_q, peer_sub_keys, peer_u, peer_v):
    depth = mod_w.shape[0]
    bsz, s, d = x.shape
    mod = _modulation(c, mod_w, mod_b)
    pos3 = positions.reshape(bsz, s, 1)
    h = x
    for layer in range(depth):
        sh1, sc1, g1, sh2, sc2, g2 = [mod[layer, :, i * d:(i + 1) * d] for i in range(6)]
        if layer % 2 == 0:
            e = layer // 2
            h = _even_mixer(h, pos3, norm_g[layer, 0], sc1, sh1, g1, layer, even_w_in[e], even_w_out[e],
                            mlstm_conv_w[e], mlstm_conv_b[e], mlstm_gate_b[e], mlstm_head_g[e],
                            diff_lambda[e], diff_head_g[e])
        else:
            o = layer // 2
            h = _odd_mixer(h, pos3, norm_g[layer, 0], sc1, sh1, g1, odd_w_in[o], odd_w_out[o],
                           sgu_ln_g[o], sgu_ln_b[o], sgu_w[o], sgu_b[o])
        h = _peer_layer(h, norm_g[layer, 1], sc2, sh2, g2, peer_w_q[layer], peer_sub_keys[layer],
                        peer_u[layer], peer_v[layer])
    return _final_norm(h, final_g)
```

```python
import functools
import math

import numpy as np
import jax
import jax.numpy as jnp
from jax import lax
from jax.experimental import pallas as pl
from jax.experimental.pallas import tpu as pltpu

F32 = jnp.float32
BF16 = jnp.bfloat16
I32 = jnp.int32
HIGHEST = lax.Precision.HIGHEST

EPS = 1e-6
ROPE_THETA = 500000.0
LANES = 128
SUBLANES = 8
VMEM_LIMIT_BYTES = 48 * 1024 * 1024

MLSTM_HEADS, MLSTM_DQK, MLSTM_DV, MLSTM_CHUNK, CONV_W = 4, 128, 256, 128, 4
DIFF_HEADS, DIFF_DQK, DIFF_DV = 8, 64, 128
MOBA_HEADS, MOBA_DH, MOBA_BLOCK, MOBA_TOPK = 8, 128, 256, 3
SGU_GROUPS, SGU_CH, SGU_CHUNK = 8, 128, 128
PEER_HEADS, PEER_NKEYS, PEER_DQ, PEER_TOPK = 8, 128, 256, 16
PEER_SEL = PEER_HEADS * PEER_TOPK

A_QK = MLSTM_HEADS * MLSTM_DQK
A_V = MLSTM_HEADS * MLSTM_DV
B_QK = DIFF_HEADS * 2 * DIFF_DQK
B_V = DIFF_HEADS * DIFF_DV
C_W = MOBA_HEADS * MOBA_DH
D_W = SGU_GROUPS * SGU_CH


def _cparams(*sem):
    return pltpu.CompilerParams(dimension_semantics=sem, vmem_limit_bytes=VMEM_LIMIT_BYTES)


def _dot_nt(a, b, precision=None):
    return lax.dot_general(a, b, (((1,), (1,)), ((), ())), preferred_element_type=F32, precision=precision)


def _store_row(ref, row, val, cols=slice(None)):
    base = pl.multiple_of((row // SUBLANES) * SUBLANES, SUBLANES)
    blk = ref[pl.ds(base, SUBLANES), cols]
    r = lax.broadcasted_iota(I32, blk.shape, 0)
    ref[pl.ds(base, SUBLANES), cols] = jnp.where(r == row - base, val, blk)


ATTN_ROWS = 128


def _flash_tiles(n_rows, score_fn, m_ref, l_ref, acc_ref, first=False):
    tiles = [slice(r0, r0 + ATTN_ROWS) for r0 in range(0, n_rows, ATTN_ROWS)]
    prev = None if first else [(m_ref[r, :], l_ref[r, :], acc_ref[r, :]) for r in tiles]
    scored = [score_fn(rows) for rows in tiles]
    soft = []
    for t, (s, v) in enumerate(scored):
        m_new = jnp.max(s, axis=1, keepdims=True)
        if not first:
            m_new = jnp.maximum(prev[t][0], m_new)
        p = jnp.exp(s - m_new)
        soft.append((m_new, jnp.sum(p, axis=1, keepdims=True), p.astype(BF16)))
    new = []
    for t, (m_new, l_new, p) in enumerate(soft):
        acc_new = jnp.dot(p, scored[t][1], preferred_element_type=F32)
        if not first:
            m_prev, l_prev, acc_prev = prev[t]
            alpha = jnp.exp(m_prev - m_new)
            l_new = alpha * l_prev + l_new
            acc_new = alpha * acc_prev + acc_new
        new.append((m_new, l_new, acc_new))
    for rows, (m_new, l_new, acc_new) in zip(tiles, new):
        m_ref[rows, :] = m_new
        l_ref[rows, :] = l_new
        acc_ref[rows, :] = acc_new


def _sigmoid(x):
    return 1.0 / (1.0 + jnp.exp(-x))


def _silu(x):
    return x * _sigmoid(x)


def _gelu_tanh(x):
    return 0.5 * x * (1.0 + jnp.tanh(math.sqrt(2.0 / math.pi) * (x + 0.044715 * (x * x * x))))


def _log_sigmoid(x):
    return jnp.minimum(x, 0.0) - jnp.log1p(jnp.exp(-jnp.abs(x)))


def _mod_kernel(c_ref, w_ref, b_ref, o_ref):
    ca = _silu(c_ref[...])
    o_ref[0] = jnp.dot(ca, w_ref[0], preferred_element_type=F32, precision=HIGHEST) + b_ref[0]


def _modulation(c, mod_w, mod_b):
    depth, d, n = mod_w.shape
    bsz = c.shape[0]
    rows = 8
    cp = jnp.zeros((rows, d), F32).at[:bsz].set(c)
    tn = 1024
    out = pl.pallas_call(
        _mod_kernel,
        grid=(depth, n // tn),
        in_specs=[pl.BlockSpec((rows, d), lambda l, j: (0, 0)),
                  pl.BlockSpec((1, d, tn), lambda l, j: (l, 0, j)),
                  pl.BlockSpec((1, 1, tn), lambda l, j: (l, 0, j))],
        out_specs=pl.BlockSpec((1, rows, tn), lambda l, j: (l, 0, j)),
        out_shape=jax.ShapeDtypeStruct((depth, rows, n), F32),
        compiler_params=_cparams("parallel", "parallel"),
        name="modulation",
    )(cp, mod_w, mod_b.reshape(depth, 1, n))
    return out[:, :bsz]


def _nm_matmul_kernel(h_ref, g_ref, sc_ref, sh_ref, w_ref, *rest, emit_hn):
    if emit_hn:
        o_ref, hn_ref, hn_scr = rest
    else:
        o_ref, hn_scr = rest

    @pl.when(pl.program_id(2) == 0)
    def _():
        x = h_ref[0]
        y = x * lax.rsqrt(jnp.mean(x * x, axis=-1, keepdims=True) + EPS) * g_ref[...]
        y = y * (1.0 + sc_ref[0]) + sh_ref[0]
        hn_scr[...] = y.astype(BF16)
        if emit_hn:
            hn_ref[0] = y

    o_ref[0] = jnp.dot(hn_scr[...], w_ref[...], preferred_element_type=F32).astype(o_ref.dtype)


def _nm_matmul(h, g, sc, sh, w, *, tm, tn, emit_hn=False, name="nm_matmul"):
    bsz, s, d = h.shape
    n = w.shape[1]
    tm = min(tm, s)
    out_shape = [jax.ShapeDtypeStruct((bsz, s, n), F32)]
    out_specs = [pl.BlockSpec((1, tm, tn), lambda b, i, j: (b, i, j))]
    if emit_hn:
        out_shape.append(jax.ShapeDtypeStruct((bsz, s, d), F32))
        out_specs.append(pl.BlockSpec((1, tm, d), lambda b, i, j: (b, i, 0)))
    res = pl.pallas_call(
        functools.partial(_nm_matmul_kernel, emit_hn=emit_hn),
        grid=(bsz, s // tm, n // tn),
        in_specs=[pl.BlockSpec((1, tm, d), lambda b, i, j: (b, i, 0)),
                  pl.BlockSpec((1, d), lambda b, i, j: (0, 0)),
                  pl.BlockSpec((1, 1, d), lambda b, i, j: (b, 0, 0)),
                  pl.BlockSpec((1, 1, d), lambda b, i, j: (b, 0, 0)),
                  pl.BlockSpec((d, tn), lambda b, i, j: (0, j))],
        out_specs=out_specs,
        out_shape=out_shape,
        scratch_shapes=[pltpu.VMEM((tm, d), BF16)],
        compiler_params=_cparams("parallel", "parallel", "arbitrary"),
        name=name,
    )(h, g.reshape(1, d), sc.reshape(bsz, 1, d), sh.reshape(bsz, 1, d), w)
    return res if emit_hn else res[0]


def _out_proj_kernel(xa_ref, xb_ref, wa_ref, wb_ref, h_ref, g_ref, o_ref):
    acc = jnp.dot(xa_ref[0], wa_ref[...], preferred_element_type=F32)
    acc += jnp.dot(xb_ref[0], wb_ref[...], preferred_element_type=F32)
    o_ref[0] = h_ref[0] + g_ref[0] * acc


def _out_proj(xa, xb, w, h, g1, *, tm=512, tn=1024):
    bsz, s, ka = xa.shape
    kb = xb.shape[2]
    n = w.shape[1]
    tm = min(tm, s)
    wa = w[:ka].astype(BF16)
    wb = w[ka:].astype(BF16)
    return pl.pallas_call(
        _out_proj_kernel,
        grid=(bsz, s // tm, n // tn),
        in_specs=[pl.BlockSpec((1, tm, ka), lambda b, i, j: (b, i, 0)),
                  pl.BlockSpec((1, tm, kb), lambda b, i, j: (b, i, 0)),
                  pl.BlockSpec((ka, tn), lambda b, i, j: (0, j)),
                  pl.BlockSpec((kb, tn), lambda b, i, j: (0, j)),
                  pl.BlockSpec((1, tm, tn), lambda b, i, j: (b, i, j)),
                  pl.BlockSpec((1, 1, tn), lambda b, i, j: (b, 0, j))],
        out_specs=pl.BlockSpec((1, tm, tn), lambda b, i, j: (b, i, j)),
        out_shape=jax.ShapeDtypeStruct((bsz, s, n), F32),
        compiler_params=_cparams("parallel", "parallel", "parallel"),
        name="out_proj",
    )(xa, xb, wa, wb, h, g1.reshape(bsz, 1, n))


def _mlstm_kernel(gb_ref, aqk_ref, v_ref, ao_ref, gcol_ref, grow_ref, cw_ref, cb_ref, hg_ref,
                  o_ref, xbuf, c_scr, n_scr, m_scr):
    L = MLSTM_CHUNK
    H, DQK, DV = MLSTM_HEADS, MLSTM_DQK, MLSTM_DV
    c = pl.program_id(1)

    @pl.when(c == 0)
    def _():
        xbuf[0:8, :] = jnp.zeros((8, 2 * A_QK), F32)
        c_scr[...] = jnp.zeros_like(c_scr)
        n_scr[...] = jnp.zeros_like(n_scr)
        m_scr[...] = jnp.zeros_like(m_scr)

    @pl.when(c > 0)
    def _():
        xbuf[0:8, :] = xbuf[L:L + 8, :]

    xbuf[8:L + 8, :] = aqk_ref[0]
    conv = cb_ref[...] + cw_ref[0:1, :] * xbuf[5:5 + L, :]
    for j in range(1, CONV_W):
        conv = conv + cw_ref[j:j + 1, :] * xbuf[5 + j:5 + j + L, :]
    qk = _silu(conv)

    r_i = lax.broadcasted_iota(I32, (L, L), 0)
    c_i = lax.broadcasted_iota(I32, (L, L), 1)
    tril = c_i <= r_i
    gcol = gcol_ref[0]
    grow = grow_ref[0]

    for h in range(H):
        q = qk[:, h * DQK:(h + 1) * DQK] * (DQK ** -0.5)
        k = qk[:, A_QK + h * DQK:A_QK + (h + 1) * DQK]
        v = v_ref[0, :, h * DV:(h + 1) * DV]
        i_col = gcol[:, h:h + 1] + gb_ref[h]
        i_row = grow[h:h + 1, :] + gb_ref[h]
        lf_col = _log_sigmoid(gcol[:, H + h:H + h + 1] + gb_ref[H + h])
        lf_row = _log_sigmoid(grow[H + h:H + h + 1, :] + gb_ref[H + h])
        bcum_col = jnp.sum(jnp.where(tril, lf_row, 0.0), axis=1, keepdims=True)
        bcum_row = jnp.sum(jnp.where(r_i <= c_i, lf_col, 0.0), axis=0, keepdims=True)
        g_tot = jnp.sum(lf_row, axis=1, keepdims=True)
        m_prev = m_scr[h:h + 1, 0:1]
        d_intra = jnp.where(tril, bcum_col - bcum_row + i_row, -jnp.inf)
        d_inter = bcum_col + m_prev
        m_t = jnp.maximum(d_inter, jnp.max(d_intra, axis=1, keepdims=True))
        w_intra = jnp.exp(d_intra - m_t)
        w_inter = jnp.exp(d_inter - m_t)
        qb = q.astype(BF16)
        kb = k.astype(BF16)
        vb = v.astype(BF16)
        s = _dot_nt(qb, kb) * w_intra
        c_prev = c_scr[h]
        n_prev = n_scr[h:h + 1, :]
        num = jnp.dot(s.astype(BF16), vb, preferred_element_type=F32)
        num = num + w_inter * jnp.dot(qb, c_prev.astype(BF16), preferred_element_type=F32)
        den = jnp.sum(s, axis=1, keepdims=True) + w_inter * jnp.sum(q * n_prev, axis=1, keepdims=True)
        hh = num / jnp.maximum(jnp.abs(den), jnp.exp(-m_t))
        d_state = g_tot - bcum_col + i_col
        m_new = jnp.maximum(g_tot + m_prev, jnp.max(d_state, axis=0, keepdims=True))
        w_s = jnp.exp(d_state - m_new)
        w_c = jnp.exp(g_tot + m_prev - m_new)
        kw = k * w_s
        c_scr[h] = w_c * c_prev + jnp.dot(kw.T.astype(BF16), vb, preferred_element_type=F32)
        n_scr[h:h + 1, :] = w_c * n_prev + jnp.sum(kw, axis=0, keepdims=True)
        m_scr[h:h + 1, 0:1] = m_new
        y = hh * lax.rsqrt(jnp.mean(hh * hh, axis=-1, keepdims=True) + EPS) * hg_ref[:, h * DV:(h + 1) * DV]
        y = y * _sigmoid(ao_ref[0, :, h * DV:(h + 1) * DV])
        o_ref[0, :, h * DV:(h + 1) * DV] = y.astype(o_ref.dtype)


def _mlstm(proj, gates_t, gate_b, conv_w, conv_b, head_g):
    bsz, s, _ = proj.shape
    L = MLSTM_CHUNK
    gate_blk = (2 * A_QK + 2 * A_V + 2 * B_QK + B_V) // LANES
    return pl.pallas_call(
        _mlstm_kernel,
        grid=(bsz, s // L),
        in_specs=[pl.BlockSpec(memory_space=pltpu.SMEM),
                  pl.BlockSpec((1, L, 2 * A_QK), lambda b, c: (b, c, 0)),
                  pl.BlockSpec((1, L, A_V), lambda b, c: (b, c, 1)),
                  pl.BlockSpec((1, L, A_V), lambda b, c: (b, c, 2)),
                  pl.BlockSpec((1, L, LANES), lambda b, c: (b, c, gate_blk)),
                  pl.BlockSpec((1, 2 * MLSTM_HEADS, L), lambda b, c: (b, 0, c)),
                  pl.BlockSpec((CONV_W, 2 * A_QK), lambda b, c: (0, 0)),
                  pl.BlockSpec((1, 2 * A_QK), lambda b, c: (0, 0)),
                  pl.BlockSpec((1, A_V), lambda b, c: (0, 0))],
        out_specs=pl.BlockSpec((1, L, A_V), lambda b, c: (b, c, 0)),
        out_shape=jax.ShapeDtypeStruct((bsz, s, A_V), BF16),
        scratch_shapes=[pltpu.VMEM((L + 8, 2 * A_QK), F32),
                        pltpu.VMEM((MLSTM_HEADS, MLSTM_DQK, MLSTM_DV), F32),
                        pltpu.VMEM((8, MLSTM_DQK), F32),
                        pltpu.VMEM((8, LANES), F32)],
        compiler_params=_cparams("parallel", "arbitrary"),
        name="mlstm",
    )(gate_b.reshape(-1), proj, proj, proj, proj, gates_t, conv_w, conv_b.reshape(1, -1),
      head_g.reshape(1, -1))


def _rope_consts(dh):
    half = dh // 8
    lane = np.arange(LANES) % dh
    inv_half = ROPE_THETA ** (-jnp.arange(half, dtype=F32) / half)
    inv = jnp.where(jnp.asarray(lane < 2 * half), inv_half[lane % half], 0.0)
    lo = (lane < half).astype(np.float32)
    hi = ((lane >= half) & (lane < 2 * half)).astype(np.float32)
    return (inv.astype(F32).reshape(1, LANES), jnp.asarray(lo).reshape(1, LANES),
            jnp.asarray(hi).reshape(1, LANES))


def _rope_tables(pos_ref, inv_ref, lo_ref, hi_ref):
    ang = pos_ref[0].astype(F32) * inv_ref[...]
    cs = jnp.cos(ang)
    sn = jnp.sin(ang)
    return cs, -sn * lo_ref[...], sn * hi_ref[...]


def _rope_apply(x, cs, sa, sb, half):
    return x * cs + pltpu.roll(x, LANES - half, 1) * sa + pltpu.roll(x, half, 1) * sb


def _diff_prep_kernel(pos_ref, inv_ref, lo_ref, hi_ref, q_ref, k_ref, v_ref, qo_ref, ko_ref, vo_ref):
    cs, sa, sb = _rope_tables(pos_ref, inv_ref, lo_ref, hi_ref)
    half = DIFF_DQK // 8
    scale = DIFF_DQK ** -0.5
    for h in range(DIFF_HEADS):
        sl = slice(h * LANES, (h + 1) * LANES)
        qo_ref[0, :, sl] = (_rope_apply(q_ref[0, :, sl], cs, sa, sb, half) * scale).astype(BF16)
        ko_ref[0, :, sl] = _rope_apply(k_ref[0, :, sl], cs, sa, sb, half).astype(BF16)
    vo_ref[0] = v_ref[0].astype(BF16)


def _diff_prep(proj, pos3, ts=512):
    bsz, s, _ = proj.shape
    ts = min(ts, s)
    inv, lo, hi = _rope_consts(DIFF_DQK)
    base = (2 * A_QK + 2 * A_V) // B_QK
    cst = pl.BlockSpec((1, LANES), lambda b, i: (0, 0))
    osd = jax.ShapeDtypeStruct((bsz, s, B_QK), BF16)
    return pl.pallas_call(
        _diff_prep_kernel,
        grid=(bsz, s // ts),
        in_specs=[pl.BlockSpec((1, ts, 1), lambda b, i: (b, i, 0)), cst, cst, cst,
                  pl.BlockSpec((1, ts, B_QK), lambda b, i: (b, i, base)),
                  pl.BlockSpec((1, ts, B_QK), lambda b, i: (b, i, base + 1)),
                  pl.BlockSpec((1, ts, B_V), lambda b, i: (b, i, base + 2))],
        out_specs=[pl.BlockSpec((1, ts, B_QK), lambda b, i: (b, i, 0))] * 3,
        out_shape=[osd, osd, osd],
        compiler_params=_cparams("parallel", "parallel"),
        name="diff_prep",
    )(pos3, inv, lo, hi, proj, proj, proj)


def _diff_attn_kernel(lam_ref, q_ref, k_ref, v_ref, g_ref, o_ref, qs_scr, m_scr, l_scr, acc_scr,
                      *, tq, tk, lambda_init):
    i = pl.program_id(2)
    j = pl.program_id(3)

    @pl.when(j == 0)
    def _():
        q = q_ref[0]
        lane = lax.broadcasted_iota(I32, q.shape, 1)
        zero = jnp.zeros_like(q)
        qs_scr[0:tq, :] = jnp.where(lane < DIFF_DQK, q, zero)
        qs_scr[tq:2 * tq, :] = jnp.where(lane >= DIFF_DQK, q, zero)
        m_scr[...] = jnp.full_like(m_scr, -jnp.inf)
        l_scr[...] = jnp.zeros_like(l_scr)
        acc_scr[...] = jnp.zeros_like(acc_scr)

    def step(masked):
        def scores(rows):
            s = _dot_nt(qs_scr[rows, :], k_ref[0])
            if masked:
                row = lax.broadcasted_iota(I32, (ATTN_ROWS, tk), 0) + (rows.start % tq)
                col = lax.broadcasted_iota(I32, (ATTN_ROWS, tk), 1)
                s = jnp.where(col <= row, s, -jnp.inf)
            return s, v_ref[0]
        _flash_tiles(2 * tq, scores, m_scr, l_scr, acc_scr)

    @pl.when(j < i)
    def _():
        step(False)

    @pl.when(j == i)
    def _():
        step(True)
        lv = lam_ref[...]
        lam = (jnp.exp(jnp.sum(lv[0:1] * lv[1:2], axis=1, keepdims=True))
               - jnp.exp(jnp.sum(lv[2:3] * lv[3:4], axis=1, keepdims=True)) + lambda_init)
        o0 = acc_scr[0:tq, :] / l_scr[0:tq, :]
        o1 = acc_scr[tq:2 * tq, :] / l_scr[tq:2 * tq, :]
        a = o0 - lam * o1
        y = a * lax.rsqrt(jnp.mean(a * a, axis=-1, keepdims=True) + EPS) * g_ref[0]
        o_ref[0] = (y * (1.0 - lambda_init)).astype(o_ref.dtype)


def _diff_attn(q, k, v, lam_vec, head_g, lambda_init, *, tq=512):
    bsz, s, _ = q.shape
    tq = min(tq, s)
    nq = s // tq
    return pl.pallas_call(
        functools.partial(_diff_attn_kernel, tq=tq, tk=tq, lambda_init=lambda_init),
        grid=(bsz, DIFF_HEADS, nq, nq),
        in_specs=[pl.BlockSpec((4, DIFF_DQK), lambda b, h, i, j: (0, 0)),
                  pl.BlockSpec((1, tq, LANES), lambda b, h, i, j: (b, i, h)),
                  pl.BlockSpec((1, tq, LANES), lambda b, h, i, j: (b, jnp.minimum(j, i), h)),
                  pl.BlockSpec((1, tq, LANES), lambda b, h, i, j: (b, jnp.minimum(j, i), h)),
                  pl.BlockSpec((1, 1, DIFF_DV), lambda b, h, i, j: (h, 0, 0))],
        out_specs=pl.BlockSpec((1, tq, DIFF_DV), lambda b, h, i, j: (b, i, h)),
        out_shape=jax.ShapeDtypeStruct((bsz, s, B_V), BF16),
        scratch_shapes=[pltpu.VMEM((2 * tq, LANES), BF16),
                        pltpu.VMEM((2 * tq, 1), F32),
                        pltpu.VMEM((2 * tq, 1), F32),
                        pltpu.VMEM((2 * tq, DIFF_DV), F32)],
        compiler_params=_cparams("parallel", "parallel", "parallel", "arbitrary"),
        name="diff_attn",
    )(lam_vec, q, k, v, head_g.reshape(DIFF_HEADS, 1, DIFF_DV))


def _moba_prep_kernel(pos_ref, inv_ref, lo_ref, hi_ref, q_ref, k_ref, v_ref,
                      qo_ref, ko_ref, vo_ref, sel_ref, kmt_scr, qf_scr, *, nb):
    i = pl.program_id(1)
    tb = MOBA_BLOCK
    half = MOBA_DH // 8
    scale = MOBA_DH ** -0.5

    @pl.when(i == 0)
    def _():
        kmt_scr[...] = jnp.zeros_like(kmt_scr)

    cs, sa, sb = _rope_tables(pos_ref, inv_ref, lo_ref, hi_ref)
    kmeans = []
    for h in range(MOBA_HEADS):
        sl = slice(h * LANES, (h + 1) * LANES)
        qr = _rope_apply(q_ref[0, :, sl], cs, sa, sb, half)
        kr = _rope_apply(k_ref[0, :, sl], cs, sa, sb, half)
        qf_scr[:, sl] = qr
        qo_ref[0, :, sl] = (qr * scale).astype(BF16)
        ko_ref[0, :, sl] = kr.astype(BF16)
        kmeans.append(jnp.mean(kr, axis=0, keepdims=True))
    vo_ref[0] = v_ref[0].astype(BF16)

    gate = _dot_nt(qf_scr[...], kmt_scr[...], precision=HIGHEST)
    lane = lax.broadcasted_iota(I32, (tb, LANES), 1)
    n = lane % nb
    valid = n < i
    gm = jnp.where(valid, gate, -jnp.inf)
    rank = jnp.zeros((tb, LANES), I32)
    for d in range(1, nb):
        wraps = (n + d) >= nb
        partner = jnp.where(wraps, pltpu.roll(gm, nb - d, 1), pltpu.roll(gm, LANES - d, 1))
        pn = jnp.where(wraps, n + d - nb, n + d)
        beats = (partner > gm) | ((partner == gm) & (pn < n))
        rank = rank + beats.astype(I32)
    sel_ref[0] = jnp.where(valid & (rank < MOBA_TOPK), 1.0, 0.0).astype(F32)

    for h in range(MOBA_HEADS):
        _store_row(kmt_scr, h * nb + i, kmeans[h], slice(h * LANES, (h + 1) * LANES))


def _moba_prep(proj, pos3):
    bsz, s, _ = proj.shape
    tb = MOBA_BLOCK
    nb = s // tb
    assert s % tb == 0 and MOBA_HEADS * nb <= LANES and nb >= MOBA_TOPK
    inv, lo, hi = _rope_consts(MOBA_DH)
    cst = pl.BlockSpec((1, LANES), lambda b, i: (0, 0))
    osd = jax.ShapeDtypeStruct((bsz, s, C_W), BF16)
    return pl.pallas_call(
        functools.partial(_moba_prep_kernel, nb=nb),
        grid=(bsz, nb),
        in_specs=[pl.BlockSpec((1, tb, 1), lambda b, i: (b, i, 0)), cst, cst, cst,
                  pl.BlockSpec((1, tb, C_W), lambda b, i: (b, i, 0)),
                  pl.BlockSpec((1, tb, C_W), lambda b, i: (b, i, 1)),
                  pl.BlockSpec((1, tb, C_W), lambda b, i: (b, i, 2))],
        out_specs=[pl.BlockSpec((1, tb, C_W), lambda b, i: (b, i, 0))] * 3
        + [pl.BlockSpec((1, tb, LANES), lambda b, i: (b, i, 0))],
        out_shape=[osd, osd, osd, jax.ShapeDtypeStruct((bsz, s, LANES), F32)],
        scratch_shapes=[pltpu.VMEM((LANES, C_W), F32), pltpu.VMEM((tb, C_W), F32)],
        compiler_params=_cparams("parallel", "arbitrary"),
        name="moba_prep",
    )(pos3, inv, lo, hi, proj, proj, proj)


def _moba_attn_kernel(q_ref, k_ref, v_ref, sel_ref, o_ref, m_scr, l_scr, acc_scr, *, nb):
    h = pl.program_id(1)
    i = pl.program_id(2)
    j = pl.program_id(3)
    tb = MOBA_BLOCK
    tq = 2 * tb

    def pick_bias(rows, blk):
        lane = lax.broadcasted_iota(I32, (ATTN_ROWS, LANES), 1)
        hit = jnp.max(jnp.where(lane == h * nb + blk, sel_ref[0, rows, :], 0.0), axis=1, keepdims=True)
        return jnp.where(hit > 0.5, 0.0, -jnp.inf)

    @pl.when(j == 0)
    def _():
        def scores(rows):
            r0 = rows.start
            if r0 < tb:
                row = lax.broadcasted_iota(I32, (ATTN_ROWS, tb), 0) + r0
                col = lax.broadcasted_iota(I32, (ATTN_ROWS, tb), 1)
                s = jnp.where(col <= row, _dot_nt(q_ref[0, rows, :], k_ref[0, 0:tb, :]), -jnp.inf)
                return s, v_ref[0, 0:tb, :]
            row = lax.broadcasted_iota(I32, (ATTN_ROWS, tq), 0) + r0
            col = lax.broadcasted_iota(I32, (ATTN_ROWS, tq), 1)
            s = _dot_nt(q_ref[0, rows, :], k_ref[0])
            s = jnp.where(col < tb, s + pick_bias(rows, 2 * i), jnp.where(col <= row, s, -jnp.inf))
            return s, v_ref[0]
        _flash_tiles(tq, scores, m_scr, l_scr, acc_scr, first=True)

    @pl.when((j > 0) & (j <= i))
    def _():
        kb = 2 * (i - j)
        col = lax.broadcasted_iota(I32, (ATTN_ROWS, tq), 1)

        def scores(rows):
            bias = jnp.where(col < tb, pick_bias(rows, kb), pick_bias(rows, kb + 1))
            return _dot_nt(q_ref[0, rows, :], k_ref[0]) + bias, v_ref[0]
        _flash_tiles(tq, scores, m_scr, l_scr, acc_scr)

    @pl.when(j == i)
    def _():
        o_ref[0] = (acc_scr[...] / l_scr[...]).astype(o_ref.dtype)


def _moba_attn(q, k, v, sel):
    bsz, s, _ = q.shape
    tq = 2 * MOBA_BLOCK
    nb = s // MOBA_BLOCK
    assert s % tq == 0
    nq = s // tq
    kv_spec = pl.BlockSpec((1, tq, MOBA_DH), lambda b, h, i, j: (b, jnp.maximum(i - j, 0), h))
    return pl.pallas_call(
        functools.partial(_moba_attn_kernel, nb=nb),
        grid=(bsz, MOBA_HEADS, nq, nq),
        in_specs=[pl.BlockSpec((1, tq, MOBA_DH), lambda b, h, i, j: (b, i, h)), kv_spec, kv_spec,
                  pl.BlockSpec((1, tq, LANES), lambda b, h, i, j: (b, i, 0))],
        out_specs=pl.BlockSpec((1, tq, MOBA_DH), lambda b, h, i, j: (b, i, h)),
        out_shape=jax.ShapeDtypeStruct((bsz, s, C_W), BF16),
        scratch_shapes=[pltpu.VMEM((tq, 1), F32), pltpu.VMEM((tq, 1), F32), pltpu.VMEM((tq, MOBA_DH), F32)],
        compiler_params=_cparams("parallel", "parallel", "parallel", "arbitrary"),
        name="moba_attn",
    )(q, k, v, sel)


def _sgu_kernel(u_ref, v_ref, lng_ref, lnb_ref, w_ref, bt_ref, o_ref, *, ts):
    L = SGU_CHUNK
    r_i = lax.broadcasted_iota(I32, (L, L), 0)
    c_i = lax.broadcasted_iota(I32, (L, L), 1)
    tril = c_i <= r_i
    for g in range(SGU_GROUPS):
        sl = slice(g * SGU_CH, (g + 1) * SGU_CH)
        w = jnp.where(tril, w_ref[g], 0.0).astype(BF16)
        bias = bt_ref[:, g:g + 1]
        for c in range(ts // L):
            rows = slice(c * L, (c + 1) * L)
            vv = _gelu_tanh(v_ref[0, rows, sl])
            mu = jnp.mean(vv, axis=-1, keepdims=True)
            dv = vv - mu
            var = jnp.mean(dv * dv, axis=-1, keepdims=True)
            vn = dv * lax.rsqrt(var + EPS) * lng_ref[:, sl] + lnb_ref[:, sl]
            mixed = jnp.dot(w, vn.astype(BF16), preferred_element_type=F32) + bias
            o_ref[0, rows, sl] = (_gelu_tanh(u_ref[0, rows, sl]) * mixed).astype(o_ref.dtype)


def _sgu(proj, ln_g, ln_b, w_s, b_s, ts=256):
    bsz, s, _ = proj.shape
    ts = min(ts, s)
    base = 3 * C_W // D_W
    return pl.pallas_call(
        functools.partial(_sgu_kernel, ts=ts),
        grid=(bsz, s // ts),
        in_specs=[pl.BlockSpec((1, ts, D_W), lambda b, i: (b, i, base)),
                  pl.BlockSpec((1, ts, D_W), lambda b, i: (b, i, base + 1)),
                  pl.BlockSpec((1, D_W), lambda b, i: (0, 0)),
                  pl.BlockSpec((1, D_W), lambda b, i: (0, 0)),
                  pl.BlockSpec((SGU_GROUPS, SGU_CHUNK, SGU_CHUNK), lambda b, i: (0, 0, 0)),
                  pl.BlockSpec((SGU_CHUNK, SGU_GROUPS), lambda b, i: (0, 0))],
        out_specs=pl.BlockSpec((1, ts, D_W), lambda b, i: (b, i, 0)),
        out_shape=jax.ShapeDtypeStruct((bsz, s, D_W), BF16),
        compiler_params=_cparams("parallel", "parallel"),
        name="sgu",
    )(proj, proj, ln_g.reshape(1, -1), ln_b.reshape(1, -1), w_s, b_s.T)


def _topk_rows(s_scr, k, val_scr, idx_scr, payload_scr=None):
    n, t = s_scr.shape
    row = lax.broadcasted_iota(I32, (n, t), 0).astype(F32)

    def body(it, carry):
        s = s_scr[...]
        m = jnp.max(s, axis=0, keepdims=True)
        pos = jnp.min(jnp.where(s == m, row, float(n)), axis=0, keepdims=True)
        hit = row == pos
        _store_row(val_scr, it, m)
        if payload_scr is None:
            _store_row(idx_scr, it, pos)
        else:
            _store_row(idx_scr, it, jnp.max(jnp.where(hit, payload_scr[...], -1.0), axis=0, keepdims=True))
        s_scr[...] = jnp.where(hit, -jnp.inf, s)
        return carry

    val_scr[...] = jnp.zeros_like(val_scr)
    idx_scr[...] = jnp.zeros_like(idx_scr)
    lax.fori_loop(0, k, body, 0)


def _peer_topk_kernel(q_ref, keys_ref, idx_ref, g_ref, s_scr, v1, i1, v2, i2, cand, cidx, vt, it):
    K = PEER_TOPK
    half = PEER_DQ // 2
    for c, (vs, ix) in enumerate(((v1, i1), (v2, i2))):
        s_scr[...] = _dot_nt(keys_ref[0, c], q_ref[:, c * half:(c + 1) * half], precision=HIGHEST)
        _topk_rows(s_scr, K, vs, ix)
    cand[...] = jnp.full(cand.shape, -jnp.inf, F32)
    cidx[...] = jnp.zeros(cidx.shape, F32)
    off = 0
    for a in range(K):
        nb = K // (a + 1)
        cand[off:off + nb, :] = v1[a:a + 1, :] + v2[0:nb, :]
        cidx[off:off + nb, :] = i1[a:a + 1, :] * float(PEER_NKEYS) + i2[0:nb, :]
        off += nb
    _topk_rows(cand, K, vt, it, payload_scr=cidx)
    ts = vt[...]
    e = jnp.exp(ts - jnp.max(ts, axis=0, keepdims=True))
    g_ref[0] = e / jnp.sum(e, axis=0, keepdims=True)
    idx_ref[0] = it[...].astype(I32)


def _peer_topk(q2, sub_keys, tt=512):
    n_tok = q2.shape[0]
    tt = min(tt, n_tok)
    K = PEER_TOPK
    half = PEER_DQ // 2
    vs = lambda r, dt: pltpu.VMEM((r, tt), dt)
    n_cand = sum(K // (a + 1) for a in range(K))
    n_cand = -(-n_cand // SUBLANES) * SUBLANES
    return pl.pallas_call(
        _peer_topk_kernel,
        grid=(n_tok // tt, PEER_HEADS),
        in_specs=[pl.BlockSpec((tt, PEER_DQ), lambda i, h: (i, h)),
                  pl.BlockSpec((1, 2, PEER_NKEYS, half), lambda i, h: (h, 0, 0, 0))],
        out_specs=[pl.BlockSpec((1, K, tt), lambda i, h: (h, 0, i)),
                   pl.BlockSpec((1, K, tt), lambda i, h: (h, 0, i))],
        out_shape=[jax.ShapeDtypeStruct((PEER_HEADS, K, n_tok), I32),
                   jax.ShapeDtypeStruct((PEER_HEADS, K, n_tok), F32)],
        scratch_shapes=[vs(PEER_NKEYS, F32), vs(K, F32), vs(K, F32), vs(K, F32), vs(K, F32),
                        vs(n_cand, F32), vs(n_cand, F32), vs(K, F32), vs(K, F32)],
        compiler_params=_cparams("parallel", "parallel"),
        name="peer_topk",
    )(q2, sub_keys)


def _pack_kernel(u_ref, v_ref, o_ref):
    ub = lax.bitcast_convert_type(u_ref[...].astype(BF16).astype(F32), I32)
    vb = lax.bitcast_convert_type(v_ref[...].astype(BF16).astype(F32), I32)
    o_ref[:, 0, :] = (ub & jnp.int32(-65536)) | lax.shift_right_logical(vb, jnp.int32(16))


def _pack_tables(u, v, tm=128):
    n, d = u.shape
    spec = pl.BlockSpec((tm, d), lambda i: (i, 0))
    return pl.pallas_call(
        _pack_kernel,
        grid=(n // tm,),
        in_specs=[spec, spec],
        out_specs=pl.BlockSpec((tm, 1, d), lambda i: (i, 0, 0)),
        out_shape=jax.ShapeDtypeStruct((n, 1, d), I32),
        compiler_params=_cparams("parallel"),
        name="peer_pack",
    )(u, v)


PEER_SLOTS = SUBLANES
PEER_AHEAD = 6
PEER_DMA_QUEUES = 2


def _peer_gather_kernel(idx_ref, gate_ref, x_ref, h_ref, g2_ref, tab_hbm, o_ref, *rest, tt):
    R = PEER_SEL
    bufs, (stage, sem) = rest[:PEER_SLOTS], rest[PEER_SLOTS:]
    step = pl.program_id(0)
    d = x_ref.shape[1]
    nch = d // LANES
    ngrp = R // SUBLANES
    per_grp = R // ngrp

    def row_copy(t, j, slot):
        return pltpu.make_async_copy(tab_hbm.at[idx_ref[t, j]], bufs[slot].at[pl.ds(j, 1)], sem.at[slot])

    def wait(slot):
        pltpu.make_async_copy(bufs[(slot + 1) % PEER_SLOTS], bufs[slot], sem.at[slot]).wait()

    def tile(slot, g, c):
        return bufs[slot][g * SUBLANES:(g + 1) * SUBLANES, c * LANES:(c + 1) * LANES]

    lane = lax.broadcasted_iota(I32, (SUBLANES, LANES), 1)
    sub = lax.broadcasted_iota(I32, (SUBLANES, LANES), 0)
    hi_mask = jnp.int32(-65536)

    @pl.when(step == 0)
    def _():
        for t0 in range(PEER_AHEAD):
            def prime(j, carry, t0=t0):
                row_copy(t0, j, t0).start()
                return carry
            lax.fori_loop(0, R, prime, 0)

    def group(gi, carry):
        base = pl.multiple_of(gi * SUBLANES, SUBLANES)
        xg = x_ref[pl.ds(base, SUBLANES), :]
        gg = gate_ref[pl.ds(base, SUBLANES), :]
        for r in range(SUBLANES):
            slot = r
            nxt = base + r + PEER_AHEAD
            nslot = (r + PEER_AHEAD) % PEER_SLOTS
            wait(slot)
            xb = [jnp.broadcast_to(xg[r:r + 1, c * LANES:(c + 1) * LANES], (SUBLANES, LANES)) for c in range(nch)]
            grow = jnp.broadcast_to(gg[r:r + 1, :], (SUBLANES, LANES))
            wb = []
            for g in range(ngrp):
                for j in range(g * per_grp, (g + 1) * per_grp):
                    row_copy(nxt, j, nslot).start(priority=j % PEER_DMA_QUEUES)
                acc = lax.bitcast_convert_type(tile(slot, g, 0) & hi_mask, F32) * xb[0]
                for c in range(1, nch):
                    acc = acc + lax.bitcast_convert_type(tile(slot, g, c) & hi_mask, F32) * xb[c]
                act = jnp.sum(acc, axis=1, keepdims=True)
                gcol = jnp.sum(jnp.where(lane == g * SUBLANES + sub, grow, 0.0), axis=1, keepdims=True)
                wb.append(jnp.broadcast_to(gcol * _gelu_tanh(act), (SUBLANES, LANES)))
            for c in range(nch):
                acc = lax.bitcast_convert_type(tile(slot, 0, c) << 16, F32) * wb[0]
                for g in range(1, ngrp):
                    acc = acc + lax.bitcast_convert_type(tile(slot, g, c) << 16, F32) * wb[g]
                stage[r:r + 1, c * LANES:(c + 1) * LANES] = jnp.sum(acc, axis=0, keepdims=True)
        rows = pl.ds(base, SUBLANES)
        o_ref[rows, :] = h_ref[rows, :] + g2_ref[0] * stage[...]
        return carry

    lax.fori_loop(0, tt // SUBLANES, group, 0)

    @pl.when(step == pl.num_programs(0) - 1)
    def _():
        for k in range(PEER_AHEAD):
            wait((tt + k) % PEER_SLOTS)


def _peer_gather(idx, gates, hn2, h2, g2, table, s, tt=128):
    n_tok, d = hn2.shape
    tt = min(tt, s)
    assert s % tt == 0 and tt % PEER_SLOTS == 0 and d % (SUBLANES * LANES) == 0 and PEER_SEL % SUBLANES == 0
    assert PEER_SEL <= LANES and PEER_AHEAD < PEER_SLOTS - 1
    per_b = s // tt
    R = PEER_SEL
    nsteps = n_tok // tt
    idx3 = idx.reshape(nsteps, tt, R)
    head = jnp.concatenate([idx3[1:, :SUBLANES], jnp.zeros((1, SUBLANES, R), I32)], axis=0)
    idx = jnp.concatenate([idx3, head], axis=1).reshape(nsteps * (tt + SUBLANES), R)
    return pl.pallas_call(
        functools.partial(_peer_gather_kernel, tt=tt),
        grid=(nsteps,),
        in_specs=[pl.BlockSpec((tt + SUBLANES, R), lambda i: (i, 0), memory_space=pltpu.SMEM),
                  pl.BlockSpec((tt, R), lambda i: (i, 0)),
                  pl.BlockSpec((tt, d), lambda i: (i, 0)),
                  pl.BlockSpec((tt, d), lambda i: (i, 0)),
                  pl.BlockSpec((1, 1, d), lambda i: (i // per_b, 0, 0)),
                  pl.BlockSpec(memory_space=pl.ANY)],
        out_specs=pl.BlockSpec((tt, d), lambda i: (i, 0)),
        out_shape=jax.ShapeDtypeStruct((n_tok, d), F32),
        scratch_shapes=[pltpu.VMEM((R, d), I32)] * PEER_SLOTS
        + [pltpu.VMEM((SUBLANES, d), F32), pltpu.SemaphoreType.DMA((PEER_SLOTS,))],
        compiler_params=_cparams("arbitrary"),
        name="peer_gather",
    )(idx, gates, hn2, h2, g2.reshape(-1, 1, d), table)


def _peer_layer(h, g, sc, sh, g2, w_q, sub_keys, u, v):
    bsz, s, d = h.shape
    q, hn = _nm_matmul(h, g, sc, sh, w_q.astype(BF16), tm=512, tn=1024, emit_hn=True, name="peer_query")
    n_tok = bsz * s
    idx, gates = _peer_topk(q.reshape(n_tok, -1), sub_keys)
    idx = idx.transpose(2, 0, 1).reshape(n_tok, PEER_SEL)
    gates = gates.transpose(2, 0, 1).reshape(n_tok, PEER_SEL)
    out = _peer_gather(idx, gates, hn.reshape(n_tok, d), h.reshape(n_tok, d), g2, _pack_tables(u, v), s)
    return out.reshape(bsz, s, d)


def _rmsnorm_kernel(x_ref, g_ref, o_ref):
    x = x_ref[...]
    o_ref[...] = x * lax.rsqrt(jnp.mean(x * x, axis=-1, keepdims=True) + EPS) * g_ref[...]


def _final_norm(h, g, tm=512):
    bsz, s, d = h.shape
    n = bsz * s
    tm = min(tm, n)
    out = pl.pallas_call(
        _rmsnorm_kernel,
        grid=(n // tm,),
        in_specs=[pl.BlockSpec((tm, d), lambda i: (i, 0)), pl.BlockSpec((1, d), lambda i: (0, 0))],
        out_specs=pl.BlockSpec((tm, d), lambda i: (i, 0)),
        out_shape=jax.ShapeDtypeStruct((n, d), F32),
        compiler_params=_cparams("parallel"),
        name="final_norm",
    )(h.reshape(n, d), g.reshape(1, d))
    return out.reshape(bsz, s, d)


def _even_mixer(h, pos3, g, sc, sh, g1, layer, w_in, w_out, conv_w, conv_b, gate_b, mlstm_g, lam_vec, diff_g):
    lambda_init = 0.8 - 0.6 * math.exp(-0.3 * layer)
    d = h.shape[-1]
    cuts = np.cumsum([2 * A_QK, A_V, A_V, 2 * MLSTM_HEADS, B_QK, B_QK, B_V])
    n_gate = 2 * MLSTM_HEADS
    w_cat = jnp.concatenate([w_in[:, :cuts[2]], w_in[:, cuts[3]:], w_in[:, cuts[2]:cuts[3]],
                             jnp.zeros((d, LANES - n_gate), w_in.dtype)], axis=1).astype(BF16)
    proj = _nm_matmul(h, g, sc, sh, w_cat, tm=512, tn=896, name="even_in_proj")
    gates_t = proj[:, :, w_cat.shape[1] - LANES:w_cat.shape[1] - LANES + n_gate].transpose(0, 2, 1)
    h_a = _mlstm(proj, gates_t, gate_b, conv_w, conv_b, mlstm_g)
    bq, bk, bv = _diff_prep(proj, pos3)
    h_b = _diff_attn(bq, bk, bv, lam_vec, diff_g, lambda_init)
    return _out_proj(h_a, h_b, w_out, h, g1)


def _odd_mixer(h, pos3, g, sc, sh, g1, w_in, w_out, ln_g, ln_b, sgu_w, sgu_b):
    proj = _nm_matmul(h, g, sc, sh, w_in.astype(BF16), tm=512, tn=1024, name="odd_in_proj")
    cq, ck, cv, sel = _moba_prep(proj, pos3)
    h_c = _moba_attn(cq, ck, cv, sel)
    h_d = _sgu(proj, ln_g, ln_b, sgu_w, sgu_b)
    return _out_proj(h_c, h_d, w_out, h, g1)


def kernel(x, c, positions, mod_w, mod_b, norm_g, final_g, even_w_in, even_w_out, mlstm_conv_w, mlstm_conv_b, mlstm_gate_b, mlstm_head_g, diff_lambda, diff_head_g, odd_w_in, odd_w_out, sgu_ln_g, sgu_ln_b, sgu_w, sgu_b, peer_w_q, peer_sub_keys, peer_u, peer_v):
    depth = mod_w.shape[0]
    bsz, s, d = x.shape
    mod = _modulation(c, mod_w, mod_b)
    pos3 = positions.reshape(bsz, s, 1)
    h = x
    for layer in range(depth):
        sh1, sc1, g1, sh2, sc2, g2 = [mod[layer, :, i * d:(i + 1) * d] for i in range(6)]
        if layer % 2 == 0:
            e = layer // 2
            h = _even_mixer(h, pos3, norm_g[layer, 0], sc1, sh1, g1, layer, even_w_in[e], even_w_out[e],
                            mlstm_conv_w[e], mlstm_conv_b[e], mlstm_gate_b[e], mlstm_head_g[e],
                            diff_lambda[e], diff_head_g[e])
        else:
            o = layer // 2
            h = _odd_mixer(h, pos3, norm_g[layer, 0], sc1, sh1, g1, odd_w_in[o], odd_w_out[o],
                           sgu_ln_g[o], sgu_ln_b[o], sgu_w[o], sgu_b[o])
        h = _peer_layer(h, norm_g[layer, 1], sc2, sh2, g2, peer_w_q[layer], peer_sub_keys[layer],
                        peer_u[layer], peer_v[layer])
    return _final_norm(h, final_g)
```

```python
import functools
import math

import numpy as np
import jax
import jax.numpy as jnp
from jax import lax
from jax.experimental import pallas as pl
from jax.experimental.pallas import tpu as pltpu

F32 = jnp.float32
BF16 = jnp.bfloat16
I32 = jnp.int32
HIGHEST = lax.Precision.HIGHEST

EPS = 1e-6
ROPE_THETA = 500000.0
LANES = 128
SUBLANES = 8
VMEM_LIMIT_BYTES = 48 * 1024 * 1024

MLSTM_HEADS, MLSTM_DQK, MLSTM_DV, MLSTM_CHUNK, CONV_W = 4, 128, 256, 128, 4
DIFF_HEADS, DIFF_DQK, DIFF_DV = 8, 64, 128
MOBA_HEADS, MOBA_DH, MOBA_BLOCK, MOBA_TOPK = 8, 128, 256, 3
SGU_GROUPS, SGU_CH, SGU_CHUNK = 8, 128, 128
PEER_HEADS, PEER_NKEYS, PEER_DQ, PEER_TOPK = 8, 128, 256, 16
PEER_SEL = PEER_HEADS * PEER_TOPK

A_QK = MLSTM_HEADS * MLSTM_DQK
A_V = MLSTM_HEADS * MLSTM_DV
B_QK = DIFF_HEADS * 2 * DIFF_DQK
B_V = DIFF_HEADS * DIFF_DV
C_W = MOBA_HEADS * MOBA_DH
D_W = SGU_GROUPS * SGU_CH


def _cparams(*sem):
    return pltpu.CompilerParams(dimension_semantics=sem, vmem_limit_bytes=VMEM_LIMIT_BYTES)


def _dot_nt(a, b, precision=None):
    return lax.dot_general(a, b, (((1,), (1,)), ((), ())), preferred_element_type=F32, precision=precision)


def _store_row(ref, row, val, cols=slice(None)):
    base = pl.multiple_of((row // SUBLANES) * SUBLANES, SUBLANES)
    blk = ref[pl.ds(base, SUBLANES), cols]
    r = lax.broadcasted_iota(I32, blk.shape, 0)
    ref[pl.ds(base, SUBLANES), cols] = jnp.where(r == row - base, val, blk)


ATTN_ROWS = 128


def _flash_tiles(n_rows, score_fn, m_ref, l_ref, acc_ref, first=False):
    tiles = [slice(r0, r0 + ATTN_ROWS) for r0 in range(0, n_rows, ATTN_ROWS)]
    prev = None if first else [(m_ref[r, :], l_ref[r, :], acc_ref[r, :]) for r in tiles]
    scored = [score_fn(rows) for rows in tiles]
    soft = []
    for t, (s, v) in enumerate(scored):
        m_new = jnp.max(s, axis=1, keepdims=True)
        if not first:
            m_new = jnp.maximum(prev[t][0], m_new)
        p = jnp.exp(s - m_new)
        soft.append((m_new, jnp.sum(p, axis=1, keepdims=True), p.astype(BF16)))
    new = []
    for t, (m_new, l_new, p) in enumerate(soft):
        acc_new = jnp.dot(p, scored[t][1], preferred_element_type=F32)
        if not first:
            m_prev, l_prev, acc_prev = prev[t]
            alpha = jnp.exp(m_prev - m_new)
            l_new = alpha * l_prev + l_new
            acc_new = alpha * acc_prev + acc_new
        new.append((m_new, l_new, acc_new))
    for rows, (m_new, l_new, acc_new) in zip(tiles, new):
        m_ref[rows, :] = m_new
        l_ref[rows, :] = l_new
        acc_ref[rows, :] = acc_new


ATTN_COLS = 128


def _flash_tiles_t(n_cols, score_fn, m_ref, l_ref, acc_ref):
    tiles = [slice(c0, c0 + ATTN_COLS) for c0 in range(0, n_cols, ATTN_COLS)]
    prev = [(m_ref[:, c], l_ref[:, c], acc_ref[:, c]) for c in tiles]
    scored = [score_fn(cols) for cols in tiles]
    soft = []
    for t, (s, vt) in enumerate(scored):
        m_new = jnp.maximum(prev[t][0], jnp.max(s, axis=0, keepdims=True))
        p = jnp.exp(s - m_new)
        soft.append((m_new, jnp.sum(p, axis=0, keepdims=True), p.astype(BF16)))
    new = []
    for t, (m_new, l_new, p) in enumerate(soft):
        m_prev, l_prev, acc_prev = prev[t]
        alpha = jnp.exp(m_prev - m_new)
        pv = jnp.dot(scored[t][1], p, preferred_element_type=F32)
        new.append((m_new, alpha * l_prev + l_new, alpha * acc_prev + pv))
    for cols, (m_new, l_new, acc_new) in zip(tiles, new):
        m_ref[:, cols] = m_new
        l_ref[:, cols] = l_new
        acc_ref[:, cols] = acc_new


def _sigmoid(x):
    return 1.0 / (1.0 + jnp.exp(-x))


def _silu(x):
    return x * _sigmoid(x)


def _gelu_tanh(x):
    return 0.5 * x * (1.0 + jnp.tanh(math.sqrt(2.0 / math.pi) * (x + 0.044715 * (x * x * x))))


def _log_sigmoid(x):
    return jnp.minimum(x, 0.0) - jnp.log1p(jnp.exp(-jnp.abs(x)))


def _mod_kernel(c_ref, w_ref, b_ref, o_ref):
    ca = _silu(c_ref[...])
    o_ref[0] = jnp.dot(ca, w_ref[0], preferred_element_type=F32, precision=HIGHEST) + b_ref[0]


def _modulation(c, mod_w, mod_b):
    depth, d, n = mod_w.shape
    bsz = c.shape[0]
    rows = 8
    cp = jnp.zeros((rows, d), F32).at[:bsz].set(c)
    tn = 1024
    out = pl.pallas_call(
        _mod_kernel,
        grid=(depth, n // tn),
        in_specs=[pl.BlockSpec((rows, d), lambda l, j: (0, 0)),
                  pl.BlockSpec((1, d, tn), lambda l, j: (l, 0, j)),
                  pl.BlockSpec((1, 1, tn), lambda l, j: (l, 0, j))],
        out_specs=pl.BlockSpec((1, rows, tn), lambda l, j: (l, 0, j)),
        out_shape=jax.ShapeDtypeStruct((depth, rows, n), F32),
        compiler_params=_cparams("parallel", "parallel"),
        name="modulation",
    )(cp, mod_w, mod_b.reshape(depth, 1, n))
    return out[:, :bsz]


def _nm_matmul_kernel(h_ref, g_ref, sc_ref, sh_ref, w_ref, *rest, emit_hn):
    if emit_hn:
        o_ref, hn_ref, hn_scr = rest
    else:
        o_ref, hn_scr = rest

    @pl.when(pl.program_id(2) == 0)
    def _():
        x = h_ref[0]
        y = x * lax.rsqrt(jnp.mean(x * x, axis=-1, keepdims=True) + EPS) * g_ref[...]
        y = y * (1.0 + sc_ref[0]) + sh_ref[0]
        hn_scr[...] = y.astype(BF16)
        if emit_hn:
            hn_ref[0] = y

    o_ref[0] = jnp.dot(hn_scr[...], w_ref[...], preferred_element_type=F32).astype(o_ref.dtype)


def _nm_matmul(h, g, sc, sh, w, *, tm, tn, emit_hn=False, name="nm_matmul"):
    bsz, s, d = h.shape
    n = w.shape[1]
    tm = min(tm, s)
    out_shape = [jax.ShapeDtypeStruct((bsz, s, n), F32)]
    out_specs = [pl.BlockSpec((1, tm, tn), lambda b, i, j: (b, i, j))]
    if emit_hn:
        out_shape.append(jax.ShapeDtypeStruct((bsz, s, d), F32))
        out_specs.append(pl.BlockSpec((1, tm, d), lambda b, i, j: (b, i, 0)))
    res = pl.pallas_call(
        functools.partial(_nm_matmul_kernel, emit_hn=emit_hn),
        grid=(bsz, s // tm, n // tn),
        in_specs=[pl.BlockSpec((1, tm, d), lambda b, i, j: (b, i, 0)),
                  pl.BlockSpec((1, d), lambda b, i, j: (0, 0)),
                  pl.BlockSpec((1, 1, d), lambda b, i, j: (b, 0, 0)),
                  pl.BlockSpec((1, 1, d), lambda b, i, j: (b, 0, 0)),
                  pl.BlockSpec((d, tn), lambda b, i, j: (0, j))],
        out_specs=out_specs,
        out_shape=out_shape,
        scratch_shapes=[pltpu.VMEM((tm, d), BF16)],
        compiler_params=_cparams("parallel", "parallel", "arbitrary"),
        name=name,
    )(h, g.reshape(1, d), sc.reshape(bsz, 1, d), sh.reshape(bsz, 1, d), w)
    return res if emit_hn else res[0]


def _out_proj_kernel(xa_ref, xb_ref, wa_ref, wb_ref, h_ref, g_ref, o_ref):
    acc = jnp.dot(xa_ref[0], wa_ref[...], preferred_element_type=F32)
    acc += jnp.dot(xb_ref[0], wb_ref[...], preferred_element_type=F32)
    o_ref[0] = h_ref[0] + g_ref[0] * acc


def _out_proj(xa, xb, w, h, g1, *, tm=512, tn=1024):
    bsz, s, ka = xa.shape
    kb = xb.shape[2]
    n = w.shape[1]
    tm = min(tm, s)
    wa = w[:ka].astype(BF16)
    wb = w[ka:].astype(BF16)
    return pl.pallas_call(
        _out_proj_kernel,
        grid=(bsz, s // tm, n // tn),
        in_specs=[pl.BlockSpec((1, tm, ka), lambda b, i, j: (b, i, 0)),
                  pl.BlockSpec((1, tm, kb), lambda b, i, j: (b, i, 0)),
                  pl.BlockSpec((ka, tn), lambda b, i, j: (0, j)),
                  pl.BlockSpec((kb, tn), lambda b, i, j: (0, j)),
                  pl.BlockSpec((1, tm, tn), lambda b, i, j: (b, i, j)),
                  pl.BlockSpec((1, 1, tn), lambda b, i, j: (b, 0, j))],
        out_specs=pl.BlockSpec((1, tm, tn), lambda b, i, j: (b, i, j)),
        out_shape=jax.ShapeDtypeStruct((bsz, s, n), F32),
        compiler_params=_cparams("parallel", "parallel", "parallel"),
        name="out_proj",
    )(xa, xb, wa, wb, h, g1.reshape(bsz, 1, n))


def _mlstm_kernel(gb_ref, aqk_ref, v_ref, ao_ref, gcol_ref, cw_ref, cb_ref, hg_ref,
                  o_ref, xbuf, c_scr, n_scr, m_scr):
    L = MLSTM_CHUNK
    H, DQK, DV = MLSTM_HEADS, MLSTM_DQK, MLSTM_DV
    c = pl.program_id(1)

    @pl.when(c == 0)
    def _():
        xbuf[0:8, :] = jnp.zeros((8, 2 * A_QK), F32)
        c_scr[...] = jnp.zeros_like(c_scr)
        n_scr[...] = jnp.zeros_like(n_scr)
        m_scr[...] = jnp.zeros_like(m_scr)

    @pl.when(c > 0)
    def _():
        xbuf[0:8, :] = xbuf[L:L + 8, :]

    xbuf[8:L + 8, :] = aqk_ref[0]
    conv = cb_ref[...] + cw_ref[0:1, :] * xbuf[5:5 + L, :]
    for j in range(1, CONV_W):
        conv = conv + cw_ref[j:j + 1, :] * xbuf[5 + j:5 + j + L, :]
    qk = _silu(conv)

    r_i = lax.broadcasted_iota(I32, (L, L), 0)
    c_i = lax.broadcasted_iota(I32, (L, L), 1)
    tril = c_i <= r_i
    gcol = gcol_ref[0]
    grow = gcol.T

    for h in range(H):
        q = qk[:, h * DQK:(h + 1) * DQK] * (DQK ** -0.5)
        k = qk[:, A_QK + h * DQK:A_QK + (h + 1) * DQK]
        v = v_ref[0, :, h * DV:(h + 1) * DV]
        i_col = gcol[:, h:h + 1] + gb_ref[h]
        i_row = grow[h:h + 1, :] + gb_ref[h]
        lf_col = _log_sigmoid(gcol[:, H + h:H + h + 1] + gb_ref[H + h])
        lf_row = _log_sigmoid(grow[H + h:H + h + 1, :] + gb_ref[H + h])
        bcum_col = jnp.sum(jnp.where(tril, lf_row, 0.0), axis=1, keepdims=True)
        bcum_row = jnp.sum(jnp.where(r_i <= c_i, lf_col, 0.0), axis=0, keepdims=True)
        g_tot = jnp.sum(lf_row, axis=1, keepdims=True)
        m_prev = m_scr[h:h + 1, 0:1]
        d_intra = jnp.where(tril, bcum_col - bcum_row + i_row, -jnp.inf)
        d_inter = bcum_col + m_prev
        m_t = jnp.maximum(d_inter, jnp.max(d_intra, axis=1, keepdims=True))
        w_intra = jnp.exp(d_intra - m_t)
        w_inter = jnp.exp(d_inter - m_t)
        qb = q.astype(BF16)
        kb = k.astype(BF16)
        vb = v.astype(BF16)
        s = _dot_nt(qb, kb) * w_intra
        c_prev = c_scr[h]
        n_prev = n_scr[h:h + 1, :]
        num = jnp.dot(s.astype(BF16), vb, preferred_element_type=F32)
        num = num + w_inter * jnp.dot(qb, c_prev.astype(BF16), preferred_element_type=F32)
        den = jnp.sum(s, axis=1, keepdims=True) + w_inter * jnp.sum(q * n_prev, axis=1, keepdims=True)
        hh = num / jnp.maximum(jnp.abs(den), jnp.exp(-m_t))
        d_state = g_tot - bcum_col + i_col
        m_new = jnp.maximum(g_tot + m_prev, jnp.max(d_state, axis=0, keepdims=True))
        w_s = jnp.exp(d_state - m_new)
        w_c = jnp.exp(g_tot + m_prev - m_new)
        kw = k * w_s
        c_scr[h] = w_c * c_prev + jnp.dot(kw.T.astype(BF16), vb, preferred_element_type=F32)
        n_scr[h:h + 1, :] = w_c * n_prev + jnp.sum(kw, axis=0, keepdims=True)
        m_scr[h:h + 1, 0:1] = m_new
        y = hh * lax.rsqrt(jnp.mean(hh * hh, axis=-1, keepdims=True) + EPS) * hg_ref[:, h * DV:(h + 1) * DV]
        y = y * _sigmoid(ao_ref[0, :, h * DV:(h + 1) * DV])
        o_ref[0, :, h * DV:(h + 1) * DV] = y.astype(o_ref.dtype)


def _mlstm(proj, gate_b, conv_w, conv_b, head_g):
    bsz, s, _ = proj.shape
    L = MLSTM_CHUNK
    gate_blk = (2 * A_QK + 2 * A_V + 2 * B_QK + B_V) // LANES
    return pl.pallas_call(
        _mlstm_kernel,
        grid=(bsz, s // L),
        in_specs=[pl.BlockSpec(memory_space=pltpu.SMEM),
                  pl.BlockSpec((1, L, 2 * A_QK), lambda b, c: (b, c, 0)),
                  pl.BlockSpec((1, L, A_V), lambda b, c: (b, c, 1)),
                  pl.BlockSpec((1, L, A_V), lambda b, c: (b, c, 2)),
                  pl.BlockSpec((1, L, LANES), lambda b, c: (b, c, gate_blk)),
                  pl.BlockSpec((CONV_W, 2 * A_QK), lambda b, c: (0, 0)),
                  pl.BlockSpec((1, 2 * A_QK), lambda b, c: (0, 0)),
                  pl.BlockSpec((1, A_V), lambda b, c: (0, 0))],
        out_specs=pl.BlockSpec((1, L, A_V), lambda b, c: (b, c, 0)),
        out_shape=jax.ShapeDtypeStruct((bsz, s, A_V), BF16),
        scratch_shapes=[pltpu.VMEM((L + 8, 2 * A_QK), F32),
                        pltpu.VMEM((MLSTM_HEADS, MLSTM_DQK, MLSTM_DV), F32),
                        pltpu.VMEM((8, MLSTM_DQK), F32),
                        pltpu.VMEM((8, LANES), F32)],
        compiler_params=_cparams("parallel", "arbitrary"),
        name="mlstm",
    )(gate_b.reshape(-1), proj, proj, proj, proj, conv_w, conv_b.reshape(1, -1), head_g.reshape(1, -1))


def _rope_consts(dh):
    half = dh // 8
    lane = np.arange(LANES) % dh
    inv_half = ROPE_THETA ** (-jnp.arange(half, dtype=F32) / half)
    inv = jnp.where(jnp.asarray(lane < 2 * half), inv_half[lane % half], 0.0)
    lo = (lane < half).astype(np.float32)
    hi = ((lane >= half) & (lane < 2 * half)).astype(np.float32)
    return (inv.astype(F32).reshape(1, LANES), jnp.asarray(lo).reshape(1, LANES),
            jnp.asarray(hi).reshape(1, LANES))


def _rope_tables(pos_ref, inv_ref, lo_ref, hi_ref):
    ang = pos_ref[0].astype(F32) * inv_ref[...]
    cs = jnp.cos(ang)
    sn = jnp.sin(ang)
    return cs, -sn * lo_ref[...], sn * hi_ref[...]


def _rope_apply(x, cs, sa, sb, half):
    return x * cs + pltpu.roll(x, LANES - half, 1) * sa + pltpu.roll(x, half, 1) * sb


def _diff_prep_kernel(pos_ref, inv_ref, lo_ref, hi_ref, q_ref, k_ref, v_ref, qo_ref, ko_ref, vo_ref):
    cs, sa, sb = _rope_tables(pos_ref, inv_ref, lo_ref, hi_ref)
    half = DIFF_DQK // 8
    scale = DIFF_DQK ** -0.5
    for h in range(DIFF_HEADS):
        sl = slice(h * LANES, (h + 1) * LANES)
        qo_ref[0, sl, :] = (_rope_apply(q_ref[0, :, sl], cs, sa, sb, half) * scale).T.astype(BF16)
        ko_ref[0, :, sl] = _rope_apply(k_ref[0, :, sl], cs, sa, sb, half).astype(BF16)
        vo_ref[0, sl, :] = v_ref[0, :, sl].T.astype(BF16)


def _diff_prep(proj, pos3, ts=512):
    bsz, s, _ = proj.shape
    ts = min(ts, s)
    inv, lo, hi = _rope_consts(DIFF_DQK)
    base = (2 * A_QK + 2 * A_V) // B_QK
    cst = pl.BlockSpec((1, LANES), lambda b, i: (0, 0))
    osd = jax.ShapeDtypeStruct((bsz, s, B_QK), BF16)
    osd_t = jax.ShapeDtypeStruct((bsz, B_QK, s), BF16)
    spec_t = pl.BlockSpec((1, B_QK, ts), lambda b, i: (b, 0, i))
    return pl.pallas_call(
        _diff_prep_kernel,
        grid=(bsz, s // ts),
        in_specs=[pl.BlockSpec((1, ts, 1), lambda b, i: (b, i, 0)), cst, cst, cst,
                  pl.BlockSpec((1, ts, B_QK), lambda b, i: (b, i, base)),
                  pl.BlockSpec((1, ts, B_QK), lambda b, i: (b, i, base + 1)),
                  pl.BlockSpec((1, ts, B_V), lambda b, i: (b, i, base + 2))],
        out_specs=[spec_t, pl.BlockSpec((1, ts, B_QK), lambda b, i: (b, i, 0)), spec_t],
        out_shape=[osd_t, osd, osd_t],
        compiler_params=_cparams("parallel", "parallel"),
        name="diff_prep",
    )(pos3, inv, lo, hi, proj, proj, proj)


def _diff_attn_kernel(lam_ref, q_ref, k_ref, v_ref, g_ref, o_ref, qs_scr, m_scr, l_scr, acc_scr,
                      *, tq, tk, lambda_init):
    i = pl.program_id(2)
    j = pl.program_id(3)

    @pl.when(j == 0)
    def _():
        q = q_ref[0]
        feat = lax.broadcasted_iota(I32, q.shape, 0)
        zero = jnp.zeros_like(q)
        qs_scr[:, 0:tq] = jnp.where(feat < DIFF_DQK, q, zero)
        qs_scr[:, tq:2 * tq] = jnp.where(feat >= DIFF_DQK, q, zero)
        m_scr[...] = jnp.full_like(m_scr, -jnp.inf)
        l_scr[...] = jnp.zeros_like(l_scr)
        acc_scr[...] = jnp.zeros_like(acc_scr)

    def step(masked):
        def scores(cols):
            s = jnp.dot(k_ref[0], qs_scr[:, cols], preferred_element_type=F32)
            if masked:
                key = lax.broadcasted_iota(I32, (tk, ATTN_COLS), 0)
                qry = lax.broadcasted_iota(I32, (tk, ATTN_COLS), 1) + (cols.start % tq)
                s = jnp.where(key <= qry, s, -jnp.inf)
            return s, v_ref[0]
        _flash_tiles_t(2 * tq, scores, m_scr, l_scr, acc_scr)

    @pl.when(j < i)
    def _():
        step(False)

    @pl.when(j == i)
    def _():
        step(True)
        lv = lam_ref[...]
        lam = (jnp.exp(jnp.sum(lv[0:1] * lv[1:2], axis=1, keepdims=True))
               - jnp.exp(jnp.sum(lv[2:3] * lv[3:4], axis=1, keepdims=True)) + lambda_init)
        o0 = acc_scr[:, 0:tq] / l_scr[:, 0:tq]
        o1 = acc_scr[:, tq:2 * tq] / l_scr[:, tq:2 * tq]
        a = o0 - lam * o1
        y = a * lax.rsqrt(jnp.mean(a * a, axis=0, keepdims=True) + EPS) * g_ref[0]
        o_ref[0] = (y * (1.0 - lambda_init)).T.astype(o_ref.dtype)


def _diff_attn(qt, k, vt, lam_vec, head_g, lambda_init, *, tq=512):
    bsz, s, _ = k.shape
    tq = min(tq, s)
    nq = s // tq
    return pl.pallas_call(
        functools.partial(_diff_attn_kernel, tq=tq, tk=tq, lambda_init=lambda_init),
        grid=(bsz, DIFF_HEADS, nq, nq),
        in_specs=[pl.BlockSpec((4, DIFF_DQK), lambda b, h, i, j: (0, 0)),
                  pl.BlockSpec((1, LANES, tq), lambda b, h, i, j: (b, h, i)),
                  pl.BlockSpec((1, tq, LANES), lambda b, h, i, j: (b, jnp.minimum(j, i), h)),
                  pl.BlockSpec((1, DIFF_DV, tq), lambda b, h, i, j: (b, h, jnp.minimum(j, i))),
                  pl.BlockSpec((1, DIFF_DV, 1), lambda b, h, i, j: (h, 0, 0))],
        out_specs=pl.BlockSpec((1, tq, DIFF_DV), lambda b, h, i, j: (b, i, h)),
        out_shape=jax.ShapeDtypeStruct((bsz, s, B_V), BF16),
        scratch_shapes=[pltpu.VMEM((LANES, 2 * tq), BF16),
                        pltpu.VMEM((1, 2 * tq), F32),
                        pltpu.VMEM((1, 2 * tq), F32),
                        pltpu.VMEM((DIFF_DV, 2 * tq), F32)],
        compiler_params=_cparams("parallel", "parallel", "parallel", "arbitrary"),
        name="diff_attn",
    )(lam_vec, qt, k, vt, head_g.reshape(DIFF_HEADS, DIFF_DV, 1))


def _moba_prep_kernel(pos_ref, inv_ref, lo_ref, hi_ref, q_ref, k_ref, v_ref,
                      qo_ref, ko_ref, vo_ref, sel_ref, kmt_scr, qf_scr, *, nb):
    i = pl.program_id(1)
    tb = MOBA_BLOCK
    half = MOBA_DH // 8
    scale = MOBA_DH ** -0.5

    @pl.when(i == 0)
    def _():
        kmt_scr[...] = jnp.zeros_like(kmt_scr)

    cs, sa, sb = _rope_tables(pos_ref, inv_ref, lo_ref, hi_ref)
    kmeans = []
    for h in range(MOBA_HEADS):
        sl = slice(h * LANES, (h + 1) * LANES)
        qr = _rope_apply(q_ref[0, :, sl], cs, sa, sb, half)
        kr = _rope_apply(k_ref[0, :, sl], cs, sa, sb, half)
        qf_scr[:, sl] = qr
        qo_ref[0, sl, :] = (qr * scale).T.astype(BF16)
        ko_ref[0, :, sl] = kr.astype(BF16)
        vo_ref[0, sl, :] = v_ref[0, :, sl].T.astype(BF16)
        kmeans.append(jnp.mean(kr, axis=0, keepdims=True))

    gate = _dot_nt(qf_scr[...], kmt_scr[...], precision=HIGHEST)
    lane = lax.broadcasted_iota(I32, (tb, LANES), 1)
    n = lane % nb
    valid = n < i
    gm = jnp.where(valid, gate, -jnp.inf)
    rank = jnp.zeros((tb, LANES), I32)
    for d in range(1, nb):
        wraps = (n + d) >= nb
        partner = jnp.where(wraps, pltpu.roll(gm, nb - d, 1), pltpu.roll(gm, LANES - d, 1))
        pn = jnp.where(wraps, n + d - nb, n + d)
        beats = (partner > gm) | ((partner == gm) & (pn < n))
        rank = rank + beats.astype(I32)
    sel_ref[0] = jnp.where(valid & (rank < MOBA_TOPK), 1.0, 0.0).astype(F32).T

    for h in range(MOBA_HEADS):
        _store_row(kmt_scr, h * nb + i, kmeans[h], slice(h * LANES, (h + 1) * LANES))


def _moba_prep(proj, pos3):
    bsz, s, _ = proj.shape
    tb = MOBA_BLOCK
    nb = s // tb
    assert s % tb == 0 and MOBA_HEADS * nb <= LANES and nb >= MOBA_TOPK
    inv, lo, hi = _rope_consts(MOBA_DH)
    cst = pl.BlockSpec((1, LANES), lambda b, i: (0, 0))
    osd = jax.ShapeDtypeStruct((bsz, s, C_W), BF16)
    osd_t = jax.ShapeDtypeStruct((bsz, C_W, s), BF16)
    spec_t = pl.BlockSpec((1, C_W, tb), lambda b, i: (b, 0, i))
    return pl.pallas_call(
        functools.partial(_moba_prep_kernel, nb=nb),
        grid=(bsz, nb),
        in_specs=[pl.BlockSpec((1, tb, 1), lambda b, i: (b, i, 0)), cst, cst, cst,
                  pl.BlockSpec((1, tb, C_W), lambda b, i: (b, i, 0)),
                  pl.BlockSpec((1, tb, C_W), lambda b, i: (b, i, 1)),
                  pl.BlockSpec((1, tb, C_W), lambda b, i: (b, i, 2))],
        out_specs=[spec_t, pl.BlockSpec((1, tb, C_W), lambda b, i: (b, i, 0)), spec_t,
                   pl.BlockSpec((1, LANES, tb), lambda b, i: (b, 0, i))],
        out_shape=[osd_t, osd, osd_t, jax.ShapeDtypeStruct((bsz, LANES, s), F32)],
        scratch_shapes=[pltpu.VMEM((LANES, C_W), F32), pltpu.VMEM((tb, C_W), F32)],
        compiler_params=_cparams("parallel", "arbitrary"),
        name="moba_prep",
    )(pos3, inv, lo, hi, proj, proj, proj)


def _moba_attn_kernel(q_ref, k_ref, v_ref, sel_ref, o_ref, m_scr, l_scr, acc_scr, *, nb):
    h = pl.program_id(1)
    i = pl.program_id(2)
    j = pl.program_id(3)
    tb = MOBA_BLOCK
    tq = 2 * tb

    def pick_bias(cols, blk):
        row = h * nb + blk
        base = pl.multiple_of((row // SUBLANES) * SUBLANES, SUBLANES)
        grp = sel_ref[0, pl.ds(base, SUBLANES), cols]
        sub = lax.broadcasted_iota(I32, grp.shape, 0)
        hit = jnp.max(jnp.where(sub == row - base, grp, 0.0), axis=0, keepdims=True)
        return jnp.where(hit > 0.5, 0.0, -jnp.inf)

    @pl.when(j == 0)
    def _():
        m_scr[...] = jnp.full_like(m_scr, -jnp.inf)
        l_scr[...] = jnp.zeros_like(l_scr)
        acc_scr[...] = jnp.zeros_like(acc_scr)

        def scores(cols):
            c0 = cols.start
            if c0 < tb:
                key = lax.broadcasted_iota(I32, (tb, ATTN_COLS), 0)
                qry = lax.broadcasted_iota(I32, (tb, ATTN_COLS), 1) + c0
                s = jnp.dot(k_ref[0, 0:tb, :], q_ref[0, :, cols], preferred_element_type=F32)
                return jnp.where(key <= qry, s, -jnp.inf), v_ref[0, :, 0:tb]
            key = lax.broadcasted_iota(I32, (tq, ATTN_COLS), 0)
            qry = lax.broadcasted_iota(I32, (tq, ATTN_COLS), 1) + c0
            s = jnp.dot(k_ref[0], q_ref[0, :, cols], preferred_element_type=F32)
            s = jnp.where(key < tb, s + pick_bias(cols, 2 * i), jnp.where(key <= qry, s, -jnp.inf))
            return s, v_ref[0]
        _flash_tiles_t(tq, scores, m_scr, l_scr, acc_scr)

    @pl.when((j > 0) & (j <= i))
    def _():
        kb = 2 * (i - j)
        key = lax.broadcasted_iota(I32, (tq, ATTN_COLS), 0)

        def scores(cols):
            bias = jnp.where(key < tb, pick_bias(cols, kb), pick_bias(cols, kb + 1))
            return jnp.dot(k_ref[0], q_ref[0, :, cols], preferred_element_type=F32) + bias, v_ref[0]
        _flash_tiles_t(tq, scores, m_scr, l_scr, acc_scr)

    @pl.when(j == i)
    def _():
        o_ref[0] = (acc_scr[...] / l_scr[...]).T.astype(o_ref.dtype)


def _moba_attn(qt, k, vt, sel_t):
    bsz, s, _ = k.shape
    tq = 2 * MOBA_BLOCK
    nb = s // MOBA_BLOCK
    assert s % tq == 0
    nq = s // tq
    return pl.pallas_call(
        functools.partial(_moba_attn_kernel, nb=nb),
        grid=(bsz, MOBA_HEADS, nq, nq),
        in_specs=[pl.BlockSpec((1, MOBA_DH, tq), lambda b, h, i, j: (b, h, i)),
                  pl.BlockSpec((1, tq, MOBA_DH), lambda b, h, i, j: (b, jnp.maximum(i - j, 0), h)),
                  pl.BlockSpec((1, MOBA_DH, tq), lambda b, h, i, j: (b, h, jnp.maximum(i - j, 0))),
                  pl.BlockSpec((1, LANES, tq), lambda b, h, i, j: (b, 0, i))],
        out_specs=pl.BlockSpec((1, tq, MOBA_DH), lambda b, h, i, j: (b, i, h)),
        out_shape=jax.ShapeDtypeStruct((bsz, s, C_W), BF16),
        scratch_shapes=[pltpu.VMEM((1, tq), F32), pltpu.VMEM((1, tq), F32), pltpu.VMEM((MOBA_DH, tq), F32)],
        compiler_params=_cparams("parallel", "parallel", "parallel", "arbitrary"),
        name="moba_attn",
    )(qt, k, vt, sel_t)


def _sgu_kernel(u_ref, v_ref, lng_ref, lnb_ref, w_ref, bt_ref, o_ref, *, ts):
    L = SGU_CHUNK
    r_i = lax.broadcasted_iota(I32, (L, L), 0)
    c_i = lax.broadcasted_iota(I32, (L, L), 1)
    tril = c_i <= r_i
    for g in range(SGU_GROUPS):
        sl = slice(g * SGU_CH, (g + 1) * SGU_CH)
        w = jnp.where(tril, w_ref[g], 0.0).astype(BF16)
        bias = bt_ref[:, g:g + 1]
        for c in range(ts // L):
            rows = slice(c * L, (c + 1) * L)
            vv = _gelu_tanh(v_ref[0, rows, sl])
            mu = jnp.mean(vv, axis=-1, keepdims=True)
            dv = vv - mu
            var = jnp.mean(dv * dv, axis=-1, keepdims=True)
            vn = dv * lax.rsqrt(var + EPS) * lng_ref[:, sl] + lnb_ref[:, sl]
            mixed = jnp.dot(w, vn.astype(BF16), preferred_element_type=F32) + bias
            o_ref[0, rows, sl] = (_gelu_tanh(u_ref[0, rows, sl]) * mixed).astype(o_ref.dtype)


def _sgu(proj, ln_g, ln_b, w_s, b_s, ts=256):
    bsz, s, _ = proj.shape
    ts = min(ts, s)
    base = 3 * C_W // D_W
    return pl.pallas_call(
        functools.partial(_sgu_kernel, ts=ts),
        grid=(bsz, s // ts),
        in_specs=[pl.BlockSpec((1, ts, D_W), lambda b, i: (b, i, base)),
                  pl.BlockSpec((1, ts, D_W), lambda b, i: (b, i, base + 1)),
                  pl.BlockSpec((1, D_W), lambda b, i: (0, 0)),
                  pl.BlockSpec((1, D_W), lambda b, i: (0, 0)),
                  pl.BlockSpec((SGU_GROUPS, SGU_CHUNK, SGU_CHUNK), lambda b, i: (0, 0, 0)),
                  pl.BlockSpec((SGU_CHUNK, SGU_GROUPS), lambda b, i: (0, 0))],
        out_specs=pl.BlockSpec((1, ts, D_W), lambda b, i: (b, i, 0)),
        out_shape=jax.ShapeDtypeStruct((bsz, s, D_W), BF16),
        compiler_params=_cparams("parallel", "parallel"),
        name="sgu",
    )(proj, proj, ln_g.reshape(1, -1), ln_b.reshape(1, -1), w_s, b_s.T)


def _topk_rows(s_scr, k, val_scr, idx_scr, payload_scr=None):
    n, t = s_scr.shape
    row = lax.broadcasted_iota(I32, (n, t), 0).astype(F32)

    def body(it, carry):
        s = s_scr[...]
        m = jnp.max(s, axis=0, keepdims=True)
        pos = jnp.min(jnp.where(s == m, row, float(n)), axis=0, keepdims=True)
        hit = row == pos
        _store_row(val_scr, it, m)
        if payload_scr is None:
            _store_row(idx_scr, it, pos)
        else:
            _store_row(idx_scr, it, jnp.max(jnp.where(hit, payload_scr[...], -1.0), axis=0, keepdims=True))
        s_scr[...] = jnp.where(hit, -jnp.inf, s)
        return carry

    val_scr[...] = jnp.zeros_like(val_scr)
    idx_scr[...] = jnp.zeros_like(idx_scr)
    lax.fori_loop(0, k, body, 0)


def _peer_topk_kernel(q_ref, keys_ref, idx_ref, g_ref, s_scr, v1, i1, v2, i2, cand, cidx, vt, it):
    K = PEER_TOPK
    half = PEER_DQ // 2
    for c, (vs, ix) in enumerate(((v1, i1), (v2, i2))):
        s_scr[...] = _dot_nt(keys_ref[0, c], q_ref[:, c * half:(c + 1) * half], precision=HIGHEST)
        _topk_rows(s_scr, K, vs, ix)
    cand[...] = jnp.full(cand.shape, -jnp.inf, F32)
    cidx[...] = jnp.zeros(cidx.shape, F32)
    off = 0
    for a in range(K):
        nb = K // (a + 1)
        cand[off:off + nb, :] = v1[a:a + 1, :] + v2[0:nb, :]
        cidx[off:off + nb, :] = i1[a:a + 1, :] * float(PEER_NKEYS) + i2[0:nb, :]
        off += nb
    _topk_rows(cand, K, vt, it, payload_scr=cidx)
    ts = vt[...]
    e = jnp.exp(ts - jnp.max(ts, axis=0, keepdims=True))
    g_ref[0] = e / jnp.sum(e, axis=0, keepdims=True)
    idx_ref[0] = it[...].astype(I32)


def _peer_topk(q2, sub_keys, tt=512):
    n_tok = q2.shape[0]
    tt = min(tt, n_tok)
    K = PEER_TOPK
    half = PEER_DQ // 2
    vs = lambda r, dt: pltpu.VMEM((r, tt), dt)
    n_cand = sum(K // (a + 1) for a in range(K))
    n_cand = -(-n_cand // SUBLANES) * SUBLANES
    return pl.pallas_call(
        _peer_topk_kernel,
        grid=(n_tok // tt, PEER_HEADS),
        in_specs=[pl.BlockSpec((tt, PEER_DQ), lambda i, h: (i, h)),
                  pl.BlockSpec((1, 2, PEER_NKEYS, half), lambda i, h: (h, 0, 0, 0))],
        out_specs=[pl.BlockSpec((1, K, tt), lambda i, h: (h, 0, i)),
                   pl.BlockSpec((1, K, tt), lambda i, h: (h, 0, i))],
        out_shape=[jax.ShapeDtypeStruct((PEER_HEADS, K, n_tok), I32),
                   jax.ShapeDtypeStruct((PEER_HEADS, K, n_tok), F32)],
        scratch_shapes=[vs(PEER_NKEYS, F32), vs(K, F32), vs(K, F32), vs(K, F32), vs(K, F32),
                        vs(n_cand, F32), vs(n_cand, F32), vs(K, F32), vs(K, F32)],
        compiler_params=_cparams("parallel", "parallel"),
        name="peer_topk",
    )(q2, sub_keys)


def _pack_kernel(u_ref, v_ref, o_ref):
    ub = lax.bitcast_convert_type(u_ref[...].astype(BF16).astype(F32), I32)
    vb = lax.bitcast_convert_type(v_ref[...].astype(BF16).astype(F32), I32)
    o_ref[:, 0, :] = (ub & jnp.int32(-65536)) | lax.shift_right_logical(vb, jnp.int32(16))


def _pack_tables(u, v, tm=128):
    n, d = u.shape
    spec = pl.BlockSpec((tm, d), lambda i: (i, 0))
    return pl.pallas_call(
        _pack_kernel,
        grid=(n // tm,),
        in_specs=[spec, spec],
        out_specs=pl.BlockSpec((tm, 1, d), lambda i: (i, 0, 0)),
        out_shape=jax.ShapeDtypeStruct((n, 1, d), I32),
        compiler_params=_cparams("parallel"),
        name="peer_pack",
    )(u, v)


PEER_SLOTS = SUBLANES
PEER_AHEAD = 6
PEER_DMA_QUEUES = 2


def _peer_gather_kernel(idx_ref, gate_ref, x_ref, h_ref, g2_ref, tab_hbm, o_ref, *rest, tt):
    R = PEER_SEL
    bufs, (stage, sem) = rest[:PEER_SLOTS], rest[PEER_SLOTS:]
    step = pl.program_id(0)
    d = x_ref.shape[1]
    nch = d // LANES
    ngrp = R // SUBLANES
    per_grp = R // ngrp

    def row_copy(t, j, slot):
        return pltpu.make_async_copy(tab_hbm.at[idx_ref[t, j]], bufs[slot].at[pl.ds(j, 1)], sem.at[slot])

    def wait(slot):
        pltpu.make_async_copy(bufs[(slot + 1) % PEER_SLOTS], bufs[slot], sem.at[slot]).wait()

    def tile(slot, g, c):
        return bufs[slot][g * SUBLANES:(g + 1) * SUBLANES, c * LANES:(c + 1) * LANES]

    lane = lax.broadcasted_iota(I32, (SUBLANES, LANES), 1)
    sub = lax.broadcasted_iota(I32, (SUBLANES, LANES), 0)
    hi_mask = jnp.int32(-65536)

    @pl.when(step == 0)
    def _():
        for t0 in range(PEER_AHEAD):
            def prime(j, carry, t0=t0):
                row_copy(t0, j, t0).start()
                return carry
            lax.fori_loop(0, R, prime, 0)

    def group(gi, carry):
        base = pl.multiple_of(gi * SUBLANES, SUBLANES)
        xg = x_ref[pl.ds(base, SUBLANES), :]
        gg = gate_ref[pl.ds(base, SUBLANES), :]
        for r in range(SUBLANES):
            slot = r
            nxt = base + r + PEER_AHEAD
            nslot = (r + PEER_AHEAD) % PEER_SLOTS
            wait(slot)
            xb = [jnp.broadcast_to(xg[r:r + 1, c * LANES:(c + 1) * LANES], (SUBLANES, LANES)) for c in range(nch)]
            grow = jnp.broadcast_to(gg[r:r + 1, :], (SUBLANES, LANES))
            wb = []
            for g in range(ngrp):
                for j in range(g * per_grp, (g + 1) * per_grp):
                    row_copy(nxt, j, nslot).start(priority=j % PEER_DMA_QUEUES)
                acc = lax.bitcast_convert_type(tile(slot, g, 0) & hi_mask, F32) * xb[0]
                for c in range(1, nch):
                    acc = acc + lax.bitcast_convert_type(tile(slot, g, c) & hi_mask, F32) * xb[c]
                act = jnp.sum(acc, axis=1, keepdims=True)
                gcol = jnp.sum(jnp.where(lane == g * SUBLANES + sub, grow, 0.0), axis=1, keepdims=True)
                wb.append(jnp.broadcast_to(gcol * _gelu_tanh(act), (SUBLANES, LANES)))
            for c in range(nch):
                acc = lax.bitcast_convert_type(tile(slot, 0, c) << 16, F32) * wb[0]
                for g in range(1, ngrp):
                    acc = acc + lax.bitcast_convert_type(tile(slot, g, c) << 16, F32) * wb[g]
                stage[r:r + 1, c * LANES:(c + 1) * LANES] = jnp.sum(acc, axis=0, keepdims=True)
        rows = pl.ds(base, SUBLANES)
        o_ref[rows, :] = h_ref[rows, :] + g2_ref[0] * stage[...]
        return carry

    lax.fori_loop(0, tt // SUBLANES, group, 0)

    @pl.when(step == pl.num_programs(0) - 1)
    def _():
        for k in range(PEER_AHEAD):
            wait((tt + k) % PEER_SLOTS)


def _peer_gather(idx, gates, hn2, h2, g2, table, s, tt=128):
    n_tok, d = hn2.shape
    tt = min(tt, s)
    assert s % tt == 0 and tt % PEER_SLOTS == 0 and d % (SUBLANES * LANES) == 0 and PEER_SEL % SUBLANES == 0
    assert PEER_SEL <= LANES and PEER_AHEAD < PEER_SLOTS - 1
    per_b = s // tt
    R = PEER_SEL
    nsteps = n_tok // tt
    idx3 = idx.reshape(nsteps, tt, R)
    head = jnp.concatenate([idx3[1:, :SUBLANES], jnp.zeros((1, SUBLANES, R), I32)], axis=0)
    idx = jnp.concatenate([idx3, head], axis=1).reshape(nsteps * (tt + SUBLANES), R)
    return pl.pallas_call(
        functools.partial(_peer_gather_kernel, tt=tt),
        grid=(nsteps,),
        in_specs=[pl.BlockSpec((tt + SUBLANES, R), lambda i: (i, 0), memory_space=pltpu.SMEM),
                  pl.BlockSpec((tt, R), lambda i: (i, 0)),
                  pl.BlockSpec((tt, d), lambda i: (i, 0)),
                  pl.BlockSpec((tt, d), lambda i: (i, 0)),
                  pl.BlockSpec((1, 1, d), lambda i: (i // per_b, 0, 0)),
                  pl.BlockSpec(memory_space=pl.ANY)],
        out_specs=pl.BlockSpec((tt, d), lambda i: (i, 0)),
        out_shape=jax.ShapeDtypeStruct((n_tok, d), F32),
        scratch_shapes=[pltpu.VMEM((R, d), I32)] * PEER_SLOTS
        + [pltpu.VMEM((SUBLANES, d), F32), pltpu.SemaphoreType.DMA((PEER_SLOTS,))],
        compiler_params=_cparams("arbitrary"),
        name="peer_gather",
    )(idx, gates, hn2, h2, g2.reshape(-1, 1, d), table)


def _peer_layer(h, g, sc, sh, g2, w_q, sub_keys, u, v):
    bsz, s, d = h.shape
    q, hn = _nm_matmul(h, g, sc, sh, w_q.astype(BF16), tm=512, tn=1024, emit_hn=True, name="peer_query")
    n_tok = bsz * s
    idx, gates = _peer_topk(q.reshape(n_tok, -1), sub_keys)
    idx = idx.transpose(2, 0, 1).reshape(n_tok, PEER_SEL)
    gates = gates.transpose(2, 0, 1).reshape(n_tok, PEER_SEL)
    out = _peer_gather(idx, gates, hn.reshape(n_tok, d), h.reshape(n_tok, d), g2, _pack_tables(u, v), s)
    return out.reshape(bsz, s, d)


def _rmsnorm_kernel(x_ref, g_ref, o_ref):
    x = x_ref[...]
    o_ref[...] = x * lax.rsqrt(jnp.mean(x * x, axis=-1, keepdims=True) + EPS) * g_ref[...]


def _final_norm(h, g, tm=512):
    bsz, s, d = h.shape
    n = bsz * s
    tm = min(tm, n)
    out = pl.pallas_call(
        _rmsnorm_kernel,
        grid=(n // tm,),
        in_specs=[pl.BlockSpec((tm, d), lambda i: (i, 0)), pl.BlockSpec((1, d), lambda i: (0, 0))],
        out_specs=pl.BlockSpec((tm, d), lambda i: (i, 0)),
        out_shape=jax.ShapeDtypeStruct((n, d), F32),
        compiler_params=_cparams("parallel"),
        name="final_norm",
    )(h.reshape(n, d), g.reshape(1, d))
    return out.reshape(bsz, s, d)


def _even_mixer(h, pos3, g, sc, sh, g1, layer, w_in, w_out, conv_w, conv_b, gate_b, mlstm_g, lam_vec, diff_g):
    lambda_init = 0.8 - 0.6 * math.exp(-0.3 * layer)
    d = h.shape[-1]
    cuts = np.cumsum([2 * A_QK, A_V, A_V, 2 * MLSTM_HEADS, B_QK, B_QK, B_V])
    n_gate = 2 * MLSTM_HEADS
    w_cat = jnp.concatenate([w_in[:, :cuts[2]], w_in[:, cuts[3]:], w_in[:, cuts[2]:cuts[3]],
                             jnp.zeros((d, LANES - n_gate), w_in.dtype)], axis=1).astype(BF16)
    proj = _nm_matmul(h, g, sc, sh, w_cat, tm=512, tn=896, name="even_in_proj")
    h_a = _mlstm(proj, gate_b, conv_w, conv_b, mlstm_g)
    bq, bk, bv = _diff_prep(proj, pos3)
    h_b = _diff_attn(bq, bk, bv, lam_vec, diff_g, lambda_init)
    return _out_proj(h_a, h_b, w_out, h, g1)


def _odd_mixer(h, pos3, g, sc, sh, g1, w_in, w_out, ln_g, ln_b, sgu_w, sgu_b):
    proj = _nm_matmul(h, g, sc, sh, w_in.astype(BF16), tm=512, tn=1024, name="odd_in_proj")
    cq, ck, cv, sel = _moba_prep(proj, pos3)
    h_c = _moba_attn(cq, ck, cv, sel)
    h_d = _sgu(proj, ln_g, ln_b, sgu_w, sgu_b)
    return _out_proj(h_c, h_d, w_out, h, g1)


def kernel(x, c, positions, mod_w, mod_b, norm_g, final_g, even_w_in, even_w_out, mlstm_conv_w, mlstm_conv_b, mlstm_gate_b, mlstm_head_g, diff_lambda, diff_head_g, odd_w_in, odd_w_out, sgu_ln_g, sgu_ln_b, sgu_w, sgu_b, peer_w_q, peer_sub_keys, peer_u, peer_v):
    depth = mod_w.shape[0]
    bsz, s, d = x.shape
    mod = _modulation(c, mod_w, mod_b)
    pos3 = positions.reshape(bsz, s, 1)
    h = x
    for layer in range(depth):
        sh1, sc1, g1, sh2, sc2, g2 = [mod[layer, :, i * d:(i + 1) * d] for i in range(6)]
        if layer % 2 == 0:
            e = layer // 2
            h = _even_mixer(h, pos3, norm_g[layer, 0], sc1, sh1, g1, layer, even_w_in[e], even_w_out[e],
                            mlstm_conv_w[e], mlstm_conv_b[e], mlstm_gate_b[e], mlstm_head_g[e],
                            diff_lambda[e], diff_head_g[e])
        else:
            o = layer // 2
            h = _odd_mixer(h, pos3, norm_g[layer, 0], sc1, sh1, g1, odd_w_in[o], odd_w_out[o],
                           sgu_ln_g[o], sgu_ln_b[o], sgu_w[o], sgu_b[o])
        h = _peer_layer(h, norm_g[layer, 1], sc2, sh2, g2, peer_w_q[layer], peer_sub_keys[layer],
                        peer_u[layer], peer_v[layer])
    return _final_norm(h, final_g)
```

```python
import functools
import math

import numpy as np
import jax
import jax.numpy as jnp
from jax import lax
from jax.experimental import pallas as pl
from jax.experimental.pallas import tpu as pltpu

F32 = jnp.float32
BF16 = jnp.bfloat16
I32 = jnp.int32
HIGHEST = lax.Precision.HIGHEST

EPS = 1e-6
ROPE_THETA = 500000.0
LANES = 128
SUBLANES = 8
VMEM_LIMIT_BYTES = 48 * 1024 * 1024

MLSTM_HEADS, MLSTM_DQK, MLSTM_DV, MLSTM_CHUNK, CONV_W = 4, 128, 256, 128, 4
DIFF_HEADS, DIFF_DQK, DIFF_DV = 8, 64, 128
MOBA_HEADS, MOBA_DH, MOBA_BLOCK, MOBA_TOPK = 8, 128, 256, 3
SGU_GROUPS, SGU_CH, SGU_CHUNK = 8, 128, 128
PEER_HEADS, PEER_NKEYS, PEER_DQ, PEER_TOPK = 8, 128, 256, 16
PEER_SEL = PEER_HEADS * PEER_TOPK

A_QK = MLSTM_HEADS * MLSTM_DQK
A_V = MLSTM_HEADS * MLSTM_DV
B_QK = DIFF_HEADS * 2 * DIFF_DQK
B_V = DIFF_HEADS * DIFF_DV
C_W = MOBA_HEADS * MOBA_DH
D_W = SGU_GROUPS * SGU_CH


def _cparams(*sem):
    return pltpu.CompilerParams(dimension_semantics=sem, vmem_limit_bytes=VMEM_LIMIT_BYTES)


def _dot_nt(a, b, precision=None):
    return lax.dot_general(a, b, (((1,), (1,)), ((), ())), preferred_element_type=F32, precision=precision)


def _store_row(ref, row, val, cols=slice(None)):
    base = pl.multiple_of((row // SUBLANES) * SUBLANES, SUBLANES)
    blk = ref[pl.ds(base, SUBLANES), cols]
    r = lax.broadcasted_iota(I32, blk.shape, 0)
    ref[pl.ds(base, SUBLANES), cols] = jnp.where(r == row - base, val, blk)


ATTN_ROWS = 128


def _flash_tiles(n_rows, score_fn, m_ref, l_ref, acc_ref, first=False):
    tiles = [slice(r0, r0 + ATTN_ROWS) for r0 in range(0, n_rows, ATTN_ROWS)]
    prev = None if first else [(m_ref[r, :], l_ref[r, :], acc_ref[r, :]) for r in tiles]
    scored = [score_fn(rows) for rows in tiles]
    soft = []
    for t, (s, v) in enumerate(scored):
        m_new = jnp.max(s, axis=1, keepdims=True)
        if not first:
            m_new = jnp.maximum(prev[t][0], m_new)
        p = jnp.exp(s - m_new)
        soft.append((m_new, jnp.sum(p, axis=1, keepdims=True), p.astype(BF16)))
    new = []
    for t, (m_new, l_new, p) in enumerate(soft):
        acc_new = jnp.dot(p, scored[t][1], preferred_element_type=F32)
        if not first:
            m_prev, l_prev, acc_prev = prev[t]
            alpha = jnp.exp(m_prev - m_new)
            l_new = alpha * l_prev + l_new
            acc_new = alpha * acc_prev + acc_new
        new.append((m_new, l_new, acc_new))
    for rows, (m_new, l_new, acc_new) in zip(tiles, new):
        m_ref[rows, :] = m_new
        l_ref[rows, :] = l_new
        acc_ref[rows, :] = acc_new


ATTN_COLS = 128


def _flash_tiles_t(n_cols, score_fn, m_ref, l_ref, acc_ref):
    tiles = [slice(c0, c0 + ATTN_COLS) for c0 in range(0, n_cols, ATTN_COLS)]
    prev = [(m_ref[:, c], l_ref[:, c], acc_ref[:, c]) for c in tiles]
    scored = [score_fn(cols) for cols in tiles]
    soft = []
    for t, (s, vt) in enumerate(scored):
        m_new = jnp.maximum(prev[t][0], jnp.max(s, axis=0, keepdims=True))
        p = jnp.exp(s - m_new)
        soft.append((m_new, jnp.sum(p, axis=0, keepdims=True), p.astype(BF16)))
    new = []
    for t, (m_new, l_new, p) in enumerate(soft):
        m_prev, l_prev, acc_prev = prev[t]
        alpha = jnp.exp(m_prev - m_new)
        pv = jnp.dot(scored[t][1], p, preferred_element_type=F32)
        new.append((m_new, alpha * l_prev + l_new, alpha * acc_prev + pv))
    for cols, (m_new, l_new, acc_new) in zip(tiles, new):
        m_ref[:, cols] = m_new
        l_ref[:, cols] = l_new
        acc_ref[:, cols] = acc_new


def _sigmoid(x):
    return 1.0 / (1.0 + jnp.exp(-x))


def _silu(x):
    return x * _sigmoid(x)


def _gelu_tanh(x):
    return 0.5 * x * (1.0 + jnp.tanh(math.sqrt(2.0 / math.pi) * (x + 0.044715 * (x * x * x))))


def _log_sigmoid(x):
    return jnp.minimum(x, 0.0) - jnp.log1p(jnp.exp(-jnp.abs(x)))


def _mod_kernel(c_ref, w_ref, b_ref, o_ref):
    ca = _silu(c_ref[...])
    o_ref[0] = jnp.dot(ca, w_ref[0], preferred_element_type=F32, precision=HIGHEST) + b_ref[0]


def _modulation(c, mod_w, mod_b):
    depth, d, n = mod_w.shape
    bsz = c.shape[0]
    rows = 8
    cp = jnp.zeros((rows, d), F32).at[:bsz].set(c)
    tn = 1024
    out = pl.pallas_call(
        _mod_kernel,
        grid=(depth, n // tn),
        in_specs=[pl.BlockSpec((rows, d), lambda l, j: (0, 0)),
                  pl.BlockSpec((1, d, tn), lambda l, j: (l, 0, j)),
                  pl.BlockSpec((1, 1, tn), lambda l, j: (l, 0, j))],
        out_specs=pl.BlockSpec((1, rows, tn), lambda l, j: (l, 0, j)),
        out_shape=jax.ShapeDtypeStruct((depth, rows, n), F32),
        compiler_params=_cparams("parallel", "parallel"),
        name="modulation",
    )(cp, mod_w, mod_b.reshape(depth, 1, n))
    return out[:, :bsz]


def _nm_matmul_kernel(h_ref, g_ref, sc_ref, sh_ref, w_ref, *rest, emit_hn):
    if emit_hn:
        o_ref, hn_ref, hn_scr = rest
    else:
        o_ref, hn_scr = rest

    @pl.when(pl.program_id(2) == 0)
    def _():
        x = h_ref[0]
        y = x * lax.rsqrt(jnp.mean(x * x, axis=-1, keepdims=True) + EPS) * g_ref[...]
        y = y * (1.0 + sc_ref[0]) + sh_ref[0]
        hn_scr[...] = y.astype(BF16)
        if emit_hn:
            hn_ref[0] = y

    o_ref[0] = jnp.dot(hn_scr[...], w_ref[...], preferred_element_type=F32).astype(o_ref.dtype)


def _nm_matmul(h, g, sc, sh, w, *, tm, tn, emit_hn=False, name="nm_matmul"):
    bsz, s, d = h.shape
    n = w.shape[1]
    tm = min(tm, s)
    out_shape = [jax.ShapeDtypeStruct((bsz, s, n), F32)]
    out_specs = [pl.BlockSpec((1, tm, tn), lambda b, i, j: (b, i, j))]
    if emit_hn:
        out_shape.append(jax.ShapeDtypeStruct((bsz, s, d), F32))
        out_specs.append(pl.BlockSpec((1, tm, d), lambda b, i, j: (b, i, 0)))
    res = pl.pallas_call(
        functools.partial(_nm_matmul_kernel, emit_hn=emit_hn),
        grid=(bsz, s // tm, n // tn),
        in_specs=[pl.BlockSpec((1, tm, d), lambda b, i, j: (b, i, 0)),
                  pl.BlockSpec((1, d), lambda b, i, j: (0, 0)),
                  pl.BlockSpec((1, 1, d), lambda b, i, j: (b, 0, 0)),
                  pl.BlockSpec((1, 1, d), lambda b, i, j: (b, 0, 0)),
                  pl.BlockSpec((d, tn), lambda b, i, j: (0, j))],
        out_specs=out_specs,
        out_shape=out_shape,
        scratch_shapes=[pltpu.VMEM((tm, d), BF16)],
        compiler_params=_cparams("parallel", "parallel", "arbitrary"),
        name=name,
    )(h, g.reshape(1, d), sc.reshape(bsz, 1, d), sh.reshape(bsz, 1, d), w)
    return res if emit_hn else res[0]


def _out_proj_kernel(xa_ref, xb_ref, wa_ref, wb_ref, h_ref, g_ref, o_ref):
    acc = jnp.dot(xa_ref[0], wa_ref[...], preferred_element_type=F32)
    acc += jnp.dot(xb_ref[0], wb_ref[...], preferred_element_type=F32)
    o_ref[0] = h_ref[0] + g_ref[0] * acc


def _out_proj(xa, xb, w, h, g1, *, tm=512, tn=1024):
    bsz, s, ka = xa.shape
    kb = xb.shape[2]
    n = w.shape[1]
    tm = min(tm, s)
    wa = w[:ka].astype(BF16)
    wb = w[ka:].astype(BF16)
    return pl.pallas_call(
        _out_proj_kernel,
        grid=(bsz, s // tm, n // tn),
        in_specs=[pl.BlockSpec((1, tm, ka), lambda b, i, j: (b, i, 0)),
                  pl.BlockSpec((1, tm, kb), lambda b, i, j: (b, i, 0)),
                  pl.BlockSpec((ka, tn), lambda b, i, j: (0, j)),
                  pl.BlockSpec((kb, tn), lambda b, i, j: (0, j)),
                  pl.BlockSpec((1, tm, tn), lambda b, i, j: (b, i, j)),
                  pl.BlockSpec((1, 1, tn), lambda b, i, j: (b, 0, j))],
        out_specs=pl.BlockSpec((1, tm, tn), lambda b, i, j: (b, i, j)),
        out_shape=jax.ShapeDtypeStruct((bsz, s, n), F32),
        compiler_params=_cparams("parallel", "parallel", "parallel"),
        name="out_proj",
    )(xa, xb, wa, wb, h, g1.reshape(bsz, 1, n))


def _mlstm_kernel(gb_ref, aqk_ref, v_ref, ao_ref, gcol_ref, cw_ref, cb_ref, hg_ref,
                  o_ref, xbuf, c_scr, n_scr, m_scr):
    L = MLSTM_CHUNK
    H, DQK, DV = MLSTM_HEADS, MLSTM_DQK, MLSTM_DV
    c = pl.program_id(1)

    @pl.when(c == 0)
    def _():
        xbuf[0:8, :] = jnp.zeros((8, 2 * A_QK), F32)
        c_scr[...] = jnp.zeros_like(c_scr)
        n_scr[...] = jnp.zeros_like(n_scr)
        m_scr[...] = jnp.zeros_like(m_scr)

    @pl.when(c > 0)
    def _():
        xbuf[0:8, :] = xbuf[L:L + 8, :]

    xbuf[8:L + 8, :] = aqk_ref[0]
    conv = cb_ref[...] + cw_ref[0:1, :] * xbuf[5:5 + L, :]
    for j in range(1, CONV_W):
        conv = conv + cw_ref[j:j + 1, :] * xbuf[5 + j:5 + j + L, :]
    qk = _silu(conv)

    r_i = lax.broadcasted_iota(I32, (L, L), 0)
    c_i = lax.broadcasted_iota(I32, (L, L), 1)
    tril = c_i <= r_i
    gcol = gcol_ref[0]
    grow = gcol.T

    for h in range(H):
        q = qk[:, h * DQK:(h + 1) * DQK] * (DQK ** -0.5)
        k = qk[:, A_QK + h * DQK:A_QK + (h + 1) * DQK]
        v = v_ref[0, :, h * DV:(h + 1) * DV]
        i_col = gcol[:, h:h + 1] + gb_ref[h]
        i_row = grow[h:h + 1, :] + gb_ref[h]
        lf_col = _log_sigmoid(gcol[:, H + h:H + h + 1] + gb_ref[H + h])
        lf_row = _log_sigmoid(grow[H + h:H + h + 1, :] + gb_ref[H + h])
        bcum_col = jnp.sum(jnp.where(tril, lf_row, 0.0), axis=1, keepdims=True)
        bcum_row = jnp.sum(jnp.where(r_i <= c_i, lf_col, 0.0), axis=0, keepdims=True)
        g_tot = jnp.sum(lf_row, axis=1, keepdims=True)
        m_prev = m_scr[h:h + 1, 0:1]
        d_intra = jnp.where(tril, bcum_col - bcum_row + i_row, -jnp.inf)
        d_inter = bcum_col + m_prev
        m_t = jnp.maximum(d_inter, jnp.max(d_intra, axis=1, keepdims=True))
        w_intra = jnp.exp(d_intra - m_t)
        w_inter = jnp.exp(d_inter - m_t)
        qb = q.astype(BF16)
        kb = k.astype(BF16)
        vb = v.astype(BF16)
        s = _dot_nt(qb, kb) * w_intra
        c_prev = c_scr[h]
        n_prev = n_scr[h:h + 1, :]
        num = jnp.dot(s.astype(BF16), vb, preferred_element_type=F32)
        num = num + w_inter * jnp.dot(qb, c_prev.astype(BF16), preferred_element_type=F32)
        den = jnp.sum(s, axis=1, keepdims=True) + w_inter * jnp.sum(q * n_prev, axis=1, keepdims=True)
        hh = num / jnp.maximum(jnp.abs(den), jnp.exp(-m_t))
        d_state = g_tot - bcum_col + i_col
        m_new = jnp.maximum(g_tot + m_prev, jnp.max(d_state, axis=0, keepdims=True))
        w_s = jnp.exp(d_state - m_new)
        w_c = jnp.exp(g_tot + m_prev - m_new)
        kw = k * w_s
        c_scr[h] = w_c * c_prev + jnp.dot(kw.T.astype(BF16), vb, preferred_element_type=F32)
        n_scr[h:h + 1, :] = w_c * n_prev + jnp.sum(kw, axis=0, keepdims=True)
        m_scr[h:h + 1, 0:1] = m_new
        y = hh * lax.rsqrt(jnp.mean(hh * hh, axis=-1, keepdims=True) + EPS) * hg_ref[:, h * DV:(h + 1) * DV]
        y = y * _sigmoid(ao_ref[0, :, h * DV:(h + 1) * DV])
        o_ref[0, :, h * DV:(h + 1) * DV] = y.astype(o_ref.dtype)


def _mlstm(proj, gate_b, conv_w, conv_b, head_g):
    bsz, s, _ = proj.shape
    L = MLSTM_CHUNK
    gate_blk = (2 * A_QK + 2 * A_V + 2 * B_QK + B_V) // LANES
    return pl.pallas_call(
        _mlstm_kernel,
        grid=(bsz, s // L),
        in_specs=[pl.BlockSpec(memory_space=pltpu.SMEM),
                  pl.BlockSpec((1, L, 2 * A_QK), lambda b, c: (b, c, 0)),
                  pl.BlockSpec((1, L, A_V), lambda b, c: (b, c, 1)),
                  pl.BlockSpec((1, L, A_V), lambda b, c: (b, c, 2)),
                  pl.BlockSpec((1, L, LANES), lambda b, c: (b, c, gate_blk)),
                  pl.BlockSpec((CONV_W, 2 * A_QK), lambda b, c: (0, 0)),
                  pl.BlockSpec((1, 2 * A_QK), lambda b, c: (0, 0)),
                  pl.BlockSpec((1, A_V), lambda b, c: (0, 0))],
        out_specs=pl.BlockSpec((1, L, A_V), lambda b, c: (b, c, 0)),
        out_shape=jax.ShapeDtypeStruct((bsz, s, A_V), BF16),
        scratch_shapes=[pltpu.VMEM((L + 8, 2 * A_QK), F32),
                        pltpu.VMEM((MLSTM_HEADS, MLSTM_DQK, MLSTM_DV), F32),
                        pltpu.VMEM((8, MLSTM_DQK), F32),
                        pltpu.VMEM((8, LANES), F32)],
        compiler_params=_cparams("parallel", "arbitrary"),
        name="mlstm",
    )(gate_b.reshape(-1), proj, proj, proj, proj, conv_w, conv_b.reshape(1, -1), head_g.reshape(1, -1))


def _rope_consts(dh):
    half = dh // 8
    lane = np.arange(LANES) % dh
    inv_half = ROPE_THETA ** (-jnp.arange(half, dtype=F32) / half)
    inv = jnp.where(jnp.asarray(lane < 2 * half), inv_half[lane % half], 0.0)
    lo = (lane < half).astype(np.float32)
    hi = ((lane >= half) & (lane < 2 * half)).astype(np.float32)
    return (inv.astype(F32).reshape(1, LANES), jnp.asarray(lo).reshape(1, LANES),
            jnp.asarray(hi).reshape(1, LANES))


def _rope_tables(pos_ref, inv_ref, lo_ref, hi_ref):
    ang = pos_ref[0].astype(F32) * inv_ref[...]
    cs = jnp.cos(ang)
    sn = jnp.sin(ang)
    return cs, -sn * lo_ref[...], sn * hi_ref[...]


def _rope_apply(x, cs, sa, sb, half):
    return x * cs + pltpu.roll(x, LANES - half, 1) * sa + pltpu.roll(x, half, 1) * sb


def _diff_prep_kernel(pos_ref, inv_ref, lo_ref, hi_ref, q_ref, k_ref, v_ref, qo_ref, ko_ref, vo_ref):
    cs, sa, sb = _rope_tables(pos_ref, inv_ref, lo_ref, hi_ref)
    half = DIFF_DQK // 8
    scale = DIFF_DQK ** -0.5
    for h in range(DIFF_HEADS):
        sl = slice(h * LANES, (h + 1) * LANES)
        qo_ref[0, sl, :] = (_rope_apply(q_ref[0, :, sl], cs, sa, sb, half) * scale).T.astype(BF16)
        ko_ref[0, :, sl] = _rope_apply(k_ref[0, :, sl], cs, sa, sb, half).astype(BF16)
        vo_ref[0, sl, :] = v_ref[0, :, sl].T.astype(BF16)


def _diff_prep(proj, pos3, ts=512):
    bsz, s, _ = proj.shape
    ts = min(ts, s)
    inv, lo, hi = _rope_consts(DIFF_DQK)
    base = (2 * A_QK + 2 * A_V) // B_QK
    cst = pl.BlockSpec((1, LANES), lambda b, i: (0, 0))
    osd = jax.ShapeDtypeStruct((bsz, s, B_QK), BF16)
    osd_t = jax.ShapeDtypeStruct((bsz, B_QK, s), BF16)
    spec_t = pl.BlockSpec((1, B_QK, ts), lambda b, i: (b, 0, i))
    return pl.pallas_call(
        _diff_prep_kernel,
        grid=(bsz, s // ts),
        in_specs=[pl.BlockSpec((1, ts, 1), lambda b, i: (b, i, 0)), cst, cst, cst,
                  pl.BlockSpec((1, ts, B_QK), lambda b, i: (b, i, base)),
                  pl.BlockSpec((1, ts, B_QK), lambda b, i: (b, i, base + 1)),
                  pl.BlockSpec((1, ts, B_V), lambda b, i: (b, i, base + 2))],
        out_specs=[spec_t, pl.BlockSpec((1, ts, B_QK), lambda b, i: (b, i, 0)), spec_t],
        out_shape=[osd_t, osd, osd_t],
        compiler_params=_cparams("parallel", "parallel"),
        name="diff_prep",
    )(pos3, inv, lo, hi, proj, proj, proj)


def _diff_attn_kernel(lam_ref, q_ref, k_ref, v_ref, g_ref, o_ref, qs_scr, m_scr, l_scr, acc_scr,
                      *, tq, tk, lambda_init):
    i = pl.program_id(2)
    j = pl.program_id(3)

    @pl.when(j == 0)
    def _():
        q = q_ref[0]
        feat = lax.broadcasted_iota(I32, q.shape, 0)
        zero = jnp.zeros_like(q)
        qs_scr[:, 0:tq] = jnp.where(feat < DIFF_DQK, q, zero)
        qs_scr[:, tq:2 * tq] = jnp.where(feat >= DIFF_DQK, q, zero)
        m_scr[...] = jnp.full_like(m_scr, -jnp.inf)
        l_scr[...] = jnp.zeros_like(l_scr)
        acc_scr[...] = jnp.zeros_like(acc_scr)

    def step(masked):
        def scores(cols):
            s = jnp.dot(k_ref[0], qs_scr[:, cols], preferred_element_type=F32)
            if masked:
                key = lax.broadcasted_iota(I32, (tk, ATTN_COLS), 0)
                qry = lax.broadcasted_iota(I32, (tk, ATTN_COLS), 1) + (cols.start % tq)
                s = jnp.where(key <= qry, s, -jnp.inf)
            return s, v_ref[0]
        _flash_tiles_t(2 * tq, scores, m_scr, l_scr, acc_scr)

    @pl.when(j < i)
    def _():
        step(False)

    @pl.when(j == i)
    def _():
        step(True)
        lv = lam_ref[...]
        lam = (jnp.exp(jnp.sum(lv[0:1] * lv[1:2], axis=1, keepdims=True))
               - jnp.exp(jnp.sum(lv[2:3] * lv[3:4], axis=1, keepdims=True)) + lambda_init)
        o0 = acc_scr[:, 0:tq] / l_scr[:, 0:tq]
        o1 = acc_scr[:, tq:2 * tq] / l_scr[:, tq:2 * tq]
        a = o0 - lam * o1
        y = a * lax.rsqrt(jnp.mean(a * a, axis=0, keepdims=True) + EPS) * g_ref[0]
        o_ref[0] = (y * (1.0 - lambda_init)).T.astype(o_ref.dtype)


def _diff_attn(qt, k, vt, lam_vec, head_g, lambda_init, *, tq=512):
    bsz, s, _ = k.shape
    tq = min(tq, s)
    nq = s // tq
    return pl.pallas_call(
        functools.partial(_diff_attn_kernel, tq=tq, tk=tq, lambda_init=lambda_init),
        grid=(bsz, DIFF_HEADS, nq, nq),
        in_specs=[pl.BlockSpec((4, DIFF_DQK), lambda b, h, i, j: (0, 0)),
                  pl.BlockSpec((1, LANES, tq), lambda b, h, i, j: (b, h, i)),
                  pl.BlockSpec((1, tq, LANES), lambda b, h, i, j: (b, jnp.minimum(j, i), h)),
                  pl.BlockSpec((1, DIFF_DV, tq), lambda b, h, i, j: (b, h, jnp.minimum(j, i))),
                  pl.BlockSpec((1, DIFF_DV, 1), lambda b, h, i, j: (h, 0, 0))],
        out_specs=pl.BlockSpec((1, tq, DIFF_DV), lambda b, h, i, j: (b, i, h)),
        out_shape=jax.ShapeDtypeStruct((bsz, s, B_V), BF16),
        scratch_shapes=[pltpu.VMEM((LANES, 2 * tq), BF16),
                        pltpu.VMEM((1, 2 * tq), F32),
                        pltpu.VMEM((1, 2 * tq), F32),
                        pltpu.VMEM((DIFF_DV, 2 * tq), F32)],
        compiler_params=_cparams("parallel", "parallel", "parallel", "arbitrary"),
        name="diff_attn",
    )(lam_vec, qt, k, vt, head_g.reshape(DIFF_HEADS, DIFF_DV, 1))


def _moba_prep_kernel(pos_ref, inv_ref, lo_ref, hi_ref, q_ref, k_ref, v_ref,
                      qo_ref, ko_ref, vo_ref, sel_ref, kmt_scr, qf_scr, *, nb):
    i = pl.program_id(1)
    tb = MOBA_BLOCK
    half = MOBA_DH // 8
    scale = MOBA_DH ** -0.5

    @pl.when(i == 0)
    def _():
        kmt_scr[...] = jnp.zeros_like(kmt_scr)

    cs, sa, sb = _rope_tables(pos_ref, inv_ref, lo_ref, hi_ref)
    kmeans = []
    for h in range(MOBA_HEADS):
        sl = slice(h * LANES, (h + 1) * LANES)
        qr = _rope_apply(q_ref[0, :, sl], cs, sa, sb, half)
        kr = _rope_apply(k_ref[0, :, sl], cs, sa, sb, half)
        qf_scr[:, sl] = qr
        qo_ref[0, sl, :] = (qr * scale).T.astype(BF16)
        ko_ref[0, :, sl] = kr.astype(BF16)
        vo_ref[0, sl, :] = v_ref[0, :, sl].T.astype(BF16)
        kmeans.append(jnp.mean(kr, axis=0, keepdims=True))

    gate = _dot_nt(qf_scr[...], kmt_scr[...], precision=HIGHEST)
    lane = lax.broadcasted_iota(I32, (tb, LANES), 1)
    n = lane % nb
    valid = n < i
    gm = jnp.where(valid, gate, -jnp.inf)
    rank = jnp.zeros((tb, LANES), I32)
    for d in range(1, nb):
        wraps = (n + d) >= nb
        partner = jnp.where(wraps, pltpu.roll(gm, nb - d, 1), pltpu.roll(gm, LANES - d, 1))
        pn = jnp.where(wraps, n + d - nb, n + d)
        beats = (partner > gm) | ((partner == gm) & (pn < n))
        rank = rank + beats.astype(I32)
    sel_ref[0] = jnp.where(valid & (rank < MOBA_TOPK), 1.0, 0.0).astype(F32).T

    for h in range(MOBA_HEADS):
        _store_row(kmt_scr, h * nb + i, kmeans[h], slice(h * LANES, (h + 1) * LANES))


def _moba_prep(proj, pos3):
    bsz, s, _ = proj.shape
    tb = MOBA_BLOCK
    nb = s // tb
    assert s % tb == 0 and MOBA_HEADS * nb <= LANES and nb >= MOBA_TOPK
    inv, lo, hi = _rope_consts(MOBA_DH)
    cst = pl.BlockSpec((1, LANES), lambda b, i: (0, 0))
    osd = jax.ShapeDtypeStruct((bsz, s, C_W), BF16)
    osd_t = jax.ShapeDtypeStruct((bsz, C_W, s), BF16)
    spec_t = pl.BlockSpec((1, C_W, tb), lambda b, i: (b, 0, i))
    return pl.pallas_call(
        functools.partial(_moba_prep_kernel, nb=nb),
        grid=(bsz, nb),
        in_specs=[pl.BlockSpec((1, tb, 1), lambda b, i: (b, i, 0)), cst, cst, cst,
                  pl.BlockSpec((1, tb, C_W), lambda b, i: (b, i, 0)),
                  pl.BlockSpec((1, tb, C_W), lambda b, i: (b, i, 1)),
                  pl.BlockSpec((1, tb, C_W), lambda b, i: (b, i, 2))],
        out_specs=[spec_t, pl.BlockSpec((1, tb, C_W), lambda b, i: (b, i, 0)), spec_t,
                   pl.BlockSpec((1, LANES, tb), lambda b, i: (b, 0, i))],
        out_shape=[osd_t, osd, osd_t, jax.ShapeDtypeStruct((bsz, LANES, s), F32)],
        scratch_shapes=[pltpu.VMEM((LANES, C_W), F32), pltpu.VMEM((tb, C_W), F32)],
        compiler_params=_cparams("parallel", "arbitrary"),
        name="moba_prep",
    )(pos3, inv, lo, hi, proj, proj, proj)


def _moba_attn_kernel(q_ref, k_ref, v_ref, sel_ref, o_ref, m_scr, l_scr, acc_scr, *, nb):
    h = pl.program_id(1)
    i = pl.program_id(2)
    j = pl.program_id(3)
    tb = MOBA_BLOCK
    tq = 2 * tb

    def pick_bias(cols, blk):
        row = h * nb + blk
        base = pl.multiple_of((row // SUBLANES) * SUBLANES, SUBLANES)
        grp = sel_ref[0, pl.ds(base, SUBLANES), cols]
        sub = lax.broadcasted_iota(I32, grp.shape, 0)
        hit = jnp.max(jnp.where(sub == row - base, grp, 0.0), axis=0, keepdims=True)
        return jnp.where(hit > 0.5, 0.0, -jnp.inf)

    @pl.when(j == 0)
    def _():
        m_scr[...] = jnp.full_like(m_scr, -jnp.inf)
        l_scr[...] = jnp.zeros_like(l_scr)
        acc_scr[...] = jnp.zeros_like(acc_scr)

        def scores(cols):
            c0 = cols.start
            if c0 < tb:
                key = lax.broadcasted_iota(I32, (tb, ATTN_COLS), 0)
                qry = lax.broadcasted_iota(I32, (tb, ATTN_COLS), 1) + c0
                s = jnp.dot(k_ref[0, 0:tb, :], q_ref[0, :, cols], preferred_element_type=F32)
                return jnp.where(key <= qry, s, -jnp.inf), v_ref[0, :, 0:tb]
            key = lax.broadcasted_iota(I32, (tq, ATTN_COLS), 0)
            qry = lax.broadcasted_iota(I32, (tq, ATTN_COLS), 1) + c0
            s = jnp.dot(k_ref[0], q_ref[0, :, cols], preferred_element_type=F32)
            s = jnp.where(key < tb, s + pick_bias(cols, 2 * i), jnp.where(key <= qry, s, -jnp.inf))
            return s, v_ref[0]
        _flash_tiles_t(tq, scores, m_scr, l_scr, acc_scr)

    @pl.when((j > 0) & (j <= i))
    def _():
        kb = 2 * (i - j)
        key = lax.broadcasted_iota(I32, (tq, ATTN_COLS), 0)

        def scores(cols):
            bias = jnp.where(key < tb, pick_bias(cols, kb), pick_bias(cols, kb + 1))
            return jnp.dot(k_ref[0], q_ref[0, :, cols], preferred_element_type=F32) + bias, v_ref[0]
        _flash_tiles_t(tq, scores, m_scr, l_scr, acc_scr)

    @pl.when(j == i)
    def _():
        o_ref[0] = (acc_scr[...] / l_scr[...]).T.astype(o_ref.dtype)


def _moba_attn(qt, k, vt, sel_t):
    bsz, s, _ = k.shape
    tq = 2 * MOBA_BLOCK
    nb = s // MOBA_BLOCK
    assert s % tq == 0
    nq = s // tq
    return pl.pallas_call(
        functools.partial(_moba_attn_kernel, nb=nb),
        grid=(bsz, MOBA_HEADS, nq, nq),
        in_specs=[pl.BlockSpec((1, MOBA_DH, tq), lambda b, h, i, j: (b, h, i)),
                  pl.BlockSpec((1, tq, MOBA_DH), lambda b, h, i, j: (b, jnp.maximum(i - j, 0), h)),
                  pl.BlockSpec((1, MOBA_DH, tq), lambda b, h, i, j: (b, h, jnp.maximum(i - j, 0))),
                  pl.BlockSpec((1, LANES, tq), lambda b, h, i, j: (b, 0, i))],
        out_specs=pl.BlockSpec((1, tq, MOBA_DH), lambda b, h, i, j: (b, i, h)),
        out_shape=jax.ShapeDtypeStruct((bsz, s, C_W), BF16),
        scratch_shapes=[pltpu.VMEM((1, tq), F32), pltpu.VMEM((1, tq), F32), pltpu.VMEM((MOBA_DH, tq), F32)],
        compiler_params=_cparams("parallel", "parallel", "parallel", "arbitrary"),
        name="moba_attn",
    )(qt, k, vt, sel_t)


def _sgu_kernel(u_ref, v_ref, lng_ref, lnb_ref, w_ref, bt_ref, o_ref, *, ts):
    L = SGU_CHUNK
    r_i = lax.broadcasted_iota(I32, (L, L), 0)
    c_i = lax.broadcasted_iota(I32, (L, L), 1)
    tril = c_i <= r_i
    for g in range(SGU_GROUPS):
        sl = slice(g * SGU_CH, (g + 1) * SGU_CH)
        w = jnp.where(tril, w_ref[g], 0.0).astype(BF16)
        bias = bt_ref[:, g:g + 1]
        for c in range(ts // L):
            rows = slice(c * L, (c + 1) * L)
            vv = _gelu_tanh(v_ref[0, rows, sl])
            mu = jnp.mean(vv, axis=-1, keepdims=True)
            dv = vv - mu
            var = jnp.mean(dv * dv, axis=-1, keepdims=True)
            vn = dv * lax.rsqrt(var + EPS) * lng_ref[:, sl] + lnb_ref[:, sl]
            mixed = jnp.dot(w, vn.astype(BF16), preferred_element_type=F32) + bias
            o_ref[0, rows, sl] = (_gelu_tanh(u_ref[0, rows, sl]) * mixed).astype(o_ref.dtype)


def _sgu(proj, ln_g, ln_b, w_s, b_s, ts=256):
    bsz, s, _ = proj.shape
    ts = min(ts, s)
    base = 3 * C_W // D_W
    return pl.pallas_call(
        functools.partial(_sgu_kernel, ts=ts),
        grid=(bsz, s // ts),
        in_specs=[pl.BlockSpec((1, ts, D_W), lambda b, i: (b, i, base)),
                  pl.BlockSpec((1, ts, D_W), lambda b, i: (b, i, base + 1)),
                  pl.BlockSpec((1, D_W), lambda b, i: (0, 0)),
                  pl.BlockSpec((1, D_W), lambda b, i: (0, 0)),
                  pl.BlockSpec((SGU_GROUPS, SGU_CHUNK, SGU_CHUNK), lambda b, i: (0, 0, 0)),
                  pl.BlockSpec((SGU_CHUNK, SGU_GROUPS), lambda b, i: (0, 0))],
        out_specs=pl.BlockSpec((1, ts, D_W), lambda b, i: (b, i, 0)),
        out_shape=jax.ShapeDtypeStruct((bsz, s, D_W), BF16),
        compiler_params=_cparams("parallel", "parallel"),
        name="sgu",
    )(proj, proj, ln_g.reshape(1, -1), ln_b.reshape(1, -1), w_s, b_s.T)


def _topk_rows(s_scr, k, val_scr, idx_scr, payload_scr=None):
    n, t = s_scr.shape
    row = lax.broadcasted_iota(I32, (n, t), 0).astype(F32)

    def body(it, carry):
        s = s_scr[...]
        m = jnp.max(s, axis=0, keepdims=True)
        pos = jnp.min(jnp.where(s == m, row, float(n)), axis=0, keepdims=True)
        hit = row == pos
        _store_row(val_scr, it, m)
        if payload_scr is None:
            _store_row(idx_scr, it, pos)
        else:
            _store_row(idx_scr, it, jnp.max(jnp.where(hit, payload_scr[...], -1.0), axis=0, keepdims=True))
        s_scr[...] = jnp.where(hit, -jnp.inf, s)
        return carry

    val_scr[...] = jnp.zeros_like(val_scr)
    idx_scr[...] = jnp.zeros_like(idx_scr)
    lax.fori_loop(0, k, body, 0)


def _peer_topk_kernel(q_ref, keys_ref, idx_ref, g_ref, s_scr, v1, i1, v2, i2, cand, cidx, vt, it):
    K = PEER_TOPK
    half = PEER_DQ // 2
    for c, (vs, ix) in enumerate(((v1, i1), (v2, i2))):
        s_scr[...] = _dot_nt(keys_ref[0, c], q_ref[:, c * half:(c + 1) * half], precision=HIGHEST)
        _topk_rows(s_scr, K, vs, ix)
    cand[...] = jnp.full(cand.shape, -jnp.inf, F32)
    cidx[...] = jnp.zeros(cidx.shape, F32)
    off = 0
    for a in range(K):
        nb = K // (a + 1)
        cand[off:off + nb, :] = v1[a:a + 1, :] + v2[0:nb, :]
        cidx[off:off + nb, :] = i1[a:a + 1, :] * float(PEER_NKEYS) + i2[0:nb, :]
        off += nb
    _topk_rows(cand, K, vt, it, payload_scr=cidx)
    ts = vt[...]
    e = jnp.exp(ts - jnp.max(ts, axis=0, keepdims=True))
    g_ref[0] = e / jnp.sum(e, axis=0, keepdims=True)
    idx_ref[0] = it[...].astype(I32)


def _peer_topk(q2, sub_keys, tt=512):
    n_tok = q2.shape[0]
    tt = min(tt, n_tok)
    K = PEER_TOPK
    half = PEER_DQ // 2
    vs = lambda r, dt: pltpu.VMEM((r, tt), dt)
    n_cand = sum(K // (a + 1) for a in range(K))
    n_cand = -(-n_cand // SUBLANES) * SUBLANES
    return pl.pallas_call(
        _peer_topk_kernel,
        grid=(n_tok // tt, PEER_HEADS),
        in_specs=[pl.BlockSpec((tt, PEER_DQ), lambda i, h: (i, h)),
                  pl.BlockSpec((1, 2, PEER_NKEYS, half), lambda i, h: (h, 0, 0, 0))],
        out_specs=[pl.BlockSpec((1, K, tt), lambda i, h: (h, 0, i)),
                   pl.BlockSpec((1, K, tt), lambda i, h: (h, 0, i))],
        out_shape=[jax.ShapeDtypeStruct((PEER_HEADS, K, n_tok), I32),
                   jax.ShapeDtypeStruct((PEER_HEADS, K, n_tok), F32)],
        scratch_shapes=[vs(PEER_NKEYS, F32), vs(K, F32), vs(K, F32), vs(K, F32), vs(K, F32),
                        vs(n_cand, F32), vs(n_cand, F32), vs(K, F32), vs(K, F32)],
        compiler_params=_cparams("parallel", "parallel"),
        name="peer_topk",
    )(q2, sub_keys)


def _pack_kernel(u_ref, v_ref, o_ref):
    ub = lax.bitcast_convert_type(u_ref[...].astype(BF16).astype(F32), I32)
    vb = lax.bitcast_convert_type(v_ref[...].astype(BF16).astype(F32), I32)
    o_ref[...] = (ub & jnp.int32(-65536)) | lax.shift_right_logical(vb, jnp.int32(16))


def _pack_tables(u_all, v_all, layer, tm=512):
    _, n, d = u_all.shape
    spec = pl.BlockSpec((None, tm, d), lambda i: (layer, i, 0))
    packed = pl.pallas_call(
        _pack_kernel,
        grid=(n // tm,),
        in_specs=[spec, spec],
        out_specs=pl.BlockSpec((tm, d), lambda i: (i, 0)),
        out_shape=jax.ShapeDtypeStruct((n, d), I32),
        compiler_params=_cparams("parallel"),
        name="peer_pack",
    )(u_all, v_all)
    return packed.reshape(n, d // LANES, LANES)


PEER_SLOTS = SUBLANES
PEER_AHEAD = 6
PEER_DMA_QUEUES = 2


def _peer_gather_kernel(idx_ref, gate_ref, x_ref, h_ref, g2_ref, tab_hbm, o_ref, *rest, tt):
    R = PEER_SEL
    bufs, (stage, sem) = rest[:PEER_SLOTS], rest[PEER_SLOTS:]
    step = pl.program_id(0)
    d = x_ref.shape[1]
    nch = d // LANES
    ngrp = R // SUBLANES
    per_grp = R // ngrp

    def row_copy(t, j, slot):
        return pltpu.make_async_copy(tab_hbm.at[idx_ref[t, j]], bufs[slot].at[:, j, :], sem.at[slot])

    def wait(slot):
        pltpu.make_async_copy(bufs[(slot + 1) % PEER_SLOTS], bufs[slot], sem.at[slot]).wait()

    def tile(slot, g, c):
        return bufs[slot][c, g * SUBLANES:(g + 1) * SUBLANES, :]

    lane = lax.broadcasted_iota(I32, (SUBLANES, LANES), 1)
    sub = lax.broadcasted_iota(I32, (SUBLANES, LANES), 0)
    hi_mask = jnp.int32(-65536)

    @pl.when(step == 0)
    def _():
        for t0 in range(PEER_AHEAD):
            def prime(j, carry, t0=t0):
                row_copy(t0, j, t0).start()
                return carry
            lax.fori_loop(0, R, prime, 0)

    def group(gi, carry):
        base = pl.multiple_of(gi * SUBLANES, SUBLANES)
        xg = x_ref[pl.ds(base, SUBLANES), :]
        gg = gate_ref[pl.ds(base, SUBLANES), :]
        for r in range(SUBLANES):
            slot = r
            nxt = base + r + PEER_AHEAD
            nslot = (r + PEER_AHEAD) % PEER_SLOTS
            wait(slot)
            xb = [jnp.broadcast_to(xg[r:r + 1, c * LANES:(c + 1) * LANES], (SUBLANES, LANES)) for c in range(nch)]
            grow = jnp.broadcast_to(gg[r:r + 1, :], (SUBLANES, LANES))
            wb = []
            for g in range(ngrp):
                for j in range(g * per_grp, (g + 1) * per_grp):
                    row_copy(nxt, j, nslot).start(priority=j % PEER_DMA_QUEUES)
                acc = lax.bitcast_convert_type(tile(slot, g, 0) & hi_mask, F32) * xb[0]
                for c in range(1, nch):
                    acc = acc + lax.bitcast_convert_type(tile(slot, g, c) & hi_mask, F32) * xb[c]
                act = jnp.sum(acc, axis=1, keepdims=True)
                gcol = jnp.sum(jnp.where(lane == g * SUBLANES + sub, grow, 0.0), axis=1, keepdims=True)
                wb.append(jnp.broadcast_to(gcol * _gelu_tanh(act), (SUBLANES, LANES)))
            for c in range(nch):
                acc = lax.bitcast_convert_type(tile(slot, 0, c) << 16, F32) * wb[0]
                for g in range(1, ngrp):
                    acc = acc + lax.bitcast_convert_type(tile(slot, g, c) << 16, F32) * wb[g]
                stage[r:r + 1, c * LANES:(c + 1) * LANES] = jnp.sum(acc, axis=0, keepdims=True)
        rows = pl.ds(base, SUBLANES)
        o_ref[rows, :] = h_ref[rows, :] + g2_ref[0] * stage[...]
        return carry

    lax.fori_loop(0, tt // SUBLANES, group, 0)

    @pl.when(step == pl.num_programs(0) - 1)
    def _():
        for k in range(PEER_AHEAD):
            wait((tt + k) % PEER_SLOTS)


def _peer_gather(idx, gates, hn2, h2, g2, table, s, tt=128):
    n_tok, d = hn2.shape
    tt = min(tt, s)
    assert s % tt == 0 and tt % PEER_SLOTS == 0 and d % (SUBLANES * LANES) == 0 and PEER_SEL % SUBLANES == 0
    assert PEER_SEL <= LANES and PEER_AHEAD < PEER_SLOTS - 1
    per_b = s // tt
    R = PEER_SEL
    nsteps = n_tok // tt
    idx3 = idx.reshape(nsteps, tt, R)
    head = jnp.concatenate([idx3[1:, :SUBLANES], jnp.zeros((1, SUBLANES, R), I32)], axis=0)
    idx = jnp.concatenate([idx3, head], axis=1).reshape(nsteps * (tt + SUBLANES), R)
    return pl.pallas_call(
        functools.partial(_peer_gather_kernel, tt=tt),
        grid=(nsteps,),
        in_specs=[pl.BlockSpec((tt + SUBLANES, R), lambda i: (i, 0), memory_space=pltpu.SMEM),
                  pl.BlockSpec((tt, R), lambda i: (i, 0)),
                  pl.BlockSpec((tt, d), lambda i: (i, 0)),
                  pl.BlockSpec((tt, d), lambda i: (i, 0)),
                  pl.BlockSpec((1, 1, d), lambda i: (i // per_b, 0, 0)),
                  pl.BlockSpec(memory_space=pl.ANY)],
        out_specs=pl.BlockSpec((tt, d), lambda i: (i, 0)),
        out_shape=jax.ShapeDtypeStruct((n_tok, d), F32),
        scratch_shapes=[pltpu.VMEM((d // LANES, R, LANES), I32)] * PEER_SLOTS
        + [pltpu.VMEM((SUBLANES, d), F32), pltpu.SemaphoreType.DMA((PEER_SLOTS,))],
        compiler_params=_cparams("arbitrary"),
        name="peer_gather",
    )(idx, gates, hn2, h2, g2.reshape(-1, 1, d), table)


def _peer_layer(h, g, sc, sh, g2, w_q, sub_keys, u_all, v_all, layer):
    bsz, s, d = h.shape
    q, hn = _nm_matmul(h, g, sc, sh, w_q.astype(BF16), tm=512, tn=1024, emit_hn=True, name="peer_query")
    n_tok = bsz * s
    idx, gates = _peer_topk(q.reshape(n_tok, -1), sub_keys)
    idx = idx.transpose(2, 0, 1).reshape(n_tok, PEER_SEL)
    gates = gates.transpose(2, 0, 1).reshape(n_tok, PEER_SEL)
    table = _pack_tables(u_all, v_all, layer)
    out = _peer_gather(idx, gates, hn.reshape(n_tok, d), h.reshape(n_tok, d), g2, table, s)
    return out.reshape(bsz, s, d)


def _rmsnorm_kernel(x_ref, g_ref, o_ref):
    x = x_ref[...]
    o_ref[...] = x * lax.rsqrt(jnp.mean(x * x, axis=-1, keepdims=True) + EPS) * g_ref[...]


def _final_norm(h, g, tm=512):
    bsz, s, d = h.shape
    n = bsz * s
    tm = min(tm, n)
    out = pl.pallas_call(
        _rmsnorm_kernel,
        grid=(n // tm,),
        in_specs=[pl.BlockSpec((tm, d), lambda i: (i, 0)), pl.BlockSpec((1, d), lambda i: (0, 0))],
        out_specs=pl.BlockSpec((tm, d), lambda i: (i, 0)),
        out_shape=jax.ShapeDtypeStruct((n, d), F32),
        compiler_params=_cparams("parallel"),
        name="final_norm",
    )(h.reshape(n, d), g.reshape(1, d))
    return out.reshape(bsz, s, d)


def _even_mixer(h, pos3, g, sc, sh, g1, layer, w_in, w_out, conv_w, conv_b, gate_b, mlstm_g, lam_vec, diff_g):
    lambda_init = 0.8 - 0.6 * math.exp(-0.3 * layer)
    d = h.shape[-1]
    cuts = np.cumsum([2 * A_QK, A_V, A_V, 2 * MLSTM_HEADS, B_QK, B_QK, B_V])
    n_gate = 2 * MLSTM_HEADS
    w_cat = jnp.concatenate([w_in[:, :cuts[2]], w_in[:, cuts[3]:], w_in[:, cuts[2]:cuts[3]],
                             jnp.zeros((d, LANES - n_gate), w_in.dtype)], axis=1).astype(BF16)
    proj = _nm_matmul(h, g, sc, sh, w_cat, tm=512, tn=896, name="even_in_proj")
    h_a = _mlstm(proj, gate_b, conv_w, conv_b, mlstm_g)
    bq, bk, bv = _diff_prep(proj, pos3)
    h_b = _diff_attn(bq, bk, bv, lam_vec, diff_g, lambda_init)
    return _out_proj(h_a, h_b, w_out, h, g1)


def _odd_mixer(h, pos3, g, sc, sh, g1, w_in, w_out, ln_g, ln_b, sgu_w, sgu_b):
    proj = _nm_matmul(h, g, sc, sh, w_in.astype(BF16), tm=512, tn=1024, name="odd_in_proj")
    cq, ck, cv, sel = _moba_prep(proj, pos3)
    h_c = _moba_attn(cq, ck, cv, sel)
    h_d = _sgu(proj, ln_g, ln_b, sgu_w, sgu_b)
    return _out_proj(h_c, h_d, w_out, h, g1)


def kernel(x, c, positions, mod_w, mod_b, norm_g, final_g, even_w_in, even_w_out, mlstm_conv_w, mlstm_conv_b, mlstm_gate_b, mlstm_head_g, diff_lambda, diff_head_g, odd_w_in, odd_w_out, sgu_ln_g, sgu_ln_b, sgu_w, sgu_b, peer_w_q, peer_sub_keys, peer_u, peer_v):
    depth = mod_w.shape[0]
    bsz, s, d = x.shape
    mod = _modulation(c, mod_w, mod_b)
    pos3 = positions.reshape(bsz, s, 1)
    h = x
    for layer in range(depth):
        sh1, sc1, g1, sh2, sc2, g2 = [mod[layer, :, i * d:(i + 1) * d] for i in range(6)]
        if layer % 2 == 0:
            e = layer // 2
            h = _even_mixer(h, pos3, norm_g[layer, 0], sc1, sh1, g1, layer, even_w_in[e], even_w_out[e],
                            mlstm_conv_w[e], mlstm_conv_b[e], mlstm_gate_b[e], mlstm_head_g[e],
                            diff_lambda[e], diff_head_g[e])
        else:
            o = layer // 2
            h = _odd_mixer(h, pos3, norm_g[layer, 0], sc1, sh1, g1, odd_w_in[o], odd_w_out[o],
                           sgu_ln_g[o], sgu_ln_b[o], sgu_w[o], sgu_b[o])
        h = _peer_layer(h, norm_g[layer, 1], sc2, sh2, g2, peer_w_q[layer], peer_sub_keys[layer],
                        peer_u, peer_v, layer)
    return _final_norm(h, final_g)
```

```python
import functools
import math

import numpy as np
import jax
import jax.numpy as jnp
from jax import lax
from jax.experimental import pallas as pl
from jax.experimental.pallas import tpu as pltpu

F32 = jnp.float32
BF16 = jnp.bfloat16
I32 = jnp.int32
HIGHEST = lax.Precision.HIGHEST

EPS = 1e-6
ROPE_THETA = 500000.0
LANES = 128
SUBLANES = 8
VMEM_LIMIT_BYTES = 48 * 1024 * 1024

MLSTM_HEADS, MLSTM_DQK, MLSTM_DV, MLSTM_CHUNK, CONV_W = 4, 128, 256, 128, 4
DIFF_HEADS, DIFF_DQK, DIFF_DV = 8, 64, 128
MOBA_HEADS, MOBA_DH, MOBA_BLOCK, MOBA_TOPK = 8, 128, 256, 3
SGU_GROUPS, SGU_CH, SGU_CHUNK = 8, 128, 128
PEER_HEADS, PEER_NKEYS, PEER_DQ, PEER_TOPK = 8, 128, 256, 16
PEER_SEL = PEER_HEADS * PEER_TOPK

A_QK = MLSTM_HEADS * MLSTM_DQK
A_V = MLSTM_HEADS * MLSTM_DV
B_QK = DIFF_HEADS * 2 * DIFF_DQK
B_V = DIFF_HEADS * DIFF_DV
C_W = MOBA_HEADS * MOBA_DH
D_W = SGU_GROUPS * SGU_CH


def _cparams(*sem):
    return pltpu.CompilerParams(dimension_semantics=sem, vmem_limit_bytes=VMEM_LIMIT_BYTES)


def _dot_nt(a, b, precision=None):
    return lax.dot_general(a, b, (((1,), (1,)), ((), ())), preferred_element_type=F32, precision=precision)


def _store_row(ref, row, val, cols=slice(None)):
    base = pl.multiple_of((row // SUBLANES) * SUBLANES, SUBLANES)
    blk = ref[pl.ds(base, SUBLANES), cols]
    r = lax.broadcasted_iota(I32, blk.shape, 0)
    ref[pl.ds(base, SUBLANES), cols] = jnp.where(r == row - base, val, blk)


ATTN_ROWS = 128


def _flash_tiles(n_rows, score_fn, m_ref, l_ref, acc_ref, first=False):
    tiles = [slice(r0, r0 + ATTN_ROWS) for r0 in range(0, n_rows, ATTN_ROWS)]
    prev = None if first else [(m_ref[r, :], l_ref[r, :], acc_ref[r, :]) for r in tiles]
    scored = [score_fn(rows) for rows in tiles]
    soft = []
    for t, (s, v) in enumerate(scored):
        m_new = jnp.max(s, axis=1, keepdims=True)
        if not first:
            m_new = jnp.maximum(prev[t][0], m_new)
        p = jnp.exp(s - m_new)
        soft.append((m_new, jnp.sum(p, axis=1, keepdims=True), p.astype(BF16)))
    new = []
    for t, (m_new, l_new, p) in enumerate(soft):
        acc_new = jnp.dot(p, scored[t][1], preferred_element_type=F32)
        if not first:
            m_prev, l_prev, acc_prev = prev[t]
            alpha = jnp.exp(m_prev - m_new)
            l_new = alpha * l_prev + l_new
            acc_new = alpha * acc_prev + acc_new
        new.append((m_new, l_new, acc_new))
    for rows, (m_new, l_new, acc_new) in zip(tiles, new):
        m_ref[rows, :] = m_new
        l_ref[rows, :] = l_new
        acc_ref[rows, :] = acc_new


ATTN_COLS = 128


def _flash_tiles_t(n_cols, score_fn, m_ref, l_ref, acc_ref):
    tiles = [slice(c0, c0 + ATTN_COLS) for c0 in range(0, n_cols, ATTN_COLS)]
    prev = [(m_ref[:, c], l_ref[:, c], acc_ref[:, c]) for c in tiles]
    scored = [score_fn(cols) for cols in tiles]
    soft = []
    for t, (s, vt) in enumerate(scored):
        m_new = jnp.maximum(prev[t][0], jnp.max(s, axis=0, keepdims=True))
        p = jnp.exp(s - m_new)
        soft.append((m_new, jnp.sum(p, axis=0, keepdims=True), p.astype(BF16)))
    new = []
    for t, (m_new, l_new, p) in enumerate(soft):
        m_prev, l_prev, acc_prev = prev[t]
        alpha = jnp.exp(m_prev - m_new)
        pv = jnp.dot(scored[t][1], p, preferred_element_type=F32)
        new.append((m_new, alpha * l_prev + l_new, alpha * acc_prev + pv))
    for cols, (m_new, l_new, acc_new) in zip(tiles, new):
        m_ref[:, cols] = m_new
        l_ref[:, cols] = l_new
        acc_ref[:, cols] = acc_new


def _sigmoid(x):
    return 1.0 / (1.0 + jnp.exp(-x))


def _silu(x):
    return x * _sigmoid(x)


def _gelu_tanh(x):
    return 0.5 * x * (1.0 + jnp.tanh(math.sqrt(2.0 / math.pi) * (x + 0.044715 * (x * x * x))))


def _log_sigmoid(x):
    return jnp.minimum(x, 0.0) - jnp.log1p(jnp.exp(-jnp.abs(x)))


def _mod_kernel(c_ref, w_ref, b_ref, o_ref):
    ca = _silu(c_ref[...])
    o_ref[0] = jnp.dot(ca, w_ref[0], preferred_element_type=F32, precision=HIGHEST) + b_ref[0]


def _modulation(c, mod_w, mod_b):
    depth, d, n = mod_w.shape
    bsz = c.shape[0]
    rows = 8
    cp = jnp.zeros((rows, d), F32).at[:bsz].set(c)
    tn = 1024
    out = pl.pallas_call(
        _mod_kernel,
        grid=(depth, n // tn),
        in_specs=[pl.BlockSpec((rows, d), lambda l, j: (0, 0)),
                  pl.BlockSpec((1, d, tn), lambda l, j: (l, 0, j)),
                  pl.BlockSpec((1, 1, tn), lambda l, j: (l, 0, j))],
        out_specs=pl.BlockSpec((1, rows, tn), lambda l, j: (l, 0, j)),
        out_shape=jax.ShapeDtypeStruct((depth, rows, n), F32),
        compiler_params=_cparams("parallel", "parallel"),
        name="modulation",
    )(cp, mod_w, mod_b.reshape(depth, 1, n))
    return out[:, :bsz]


def _nm_matmul_kernel(h_ref, g_ref, sc_ref, sh_ref, w_ref, *rest, emit_hn):
    if emit_hn:
        o_ref, hn_ref, hn_scr = rest
    else:
        o_ref, hn_scr = rest

    @pl.when(pl.program_id(2) == 0)
    def _():
        x = h_ref[0]
        y = x * lax.rsqrt(jnp.mean(x * x, axis=-1, keepdims=True) + EPS) * g_ref[...]
        y = y * (1.0 + sc_ref[0]) + sh_ref[0]
        hn_scr[...] = y.astype(BF16)
        if emit_hn:
            hn_ref[0] = y

    o_ref[0] = jnp.dot(hn_scr[...], w_ref[...], preferred_element_type=F32).astype(o_ref.dtype)


def _nm_matmul(h, g, sc, sh, w, *, tm, tn, emit_hn=False, name="nm_matmul"):
    bsz, s, d = h.shape
    n = w.shape[1]
    tm = min(tm, s)
    out_shape = [jax.ShapeDtypeStruct((bsz, s, n), F32)]
    out_specs = [pl.BlockSpec((1, tm, tn), lambda b, i, j: (b, i, j))]
    if emit_hn:
        out_shape.append(jax.ShapeDtypeStruct((bsz, s, d), F32))
        out_specs.append(pl.BlockSpec((1, tm, d), lambda b, i, j: (b, i, 0)))
    res = pl.pallas_call(
        functools.partial(_nm_matmul_kernel, emit_hn=emit_hn),
        grid=(bsz, s // tm, n // tn),
        in_specs=[pl.BlockSpec((1, tm, d), lambda b, i, j: (b, i, 0)),
                  pl.BlockSpec((1, d), lambda b, i, j: (0, 0)),
                  pl.BlockSpec((1, 1, d), lambda b, i, j: (b, 0, 0)),
                  pl.BlockSpec((1, 1, d), lambda b, i, j: (b, 0, 0)),
                  pl.BlockSpec((d, tn), lambda b, i, j: (0, j))],
        out_specs=out_specs,
        out_shape=out_shape,
        scratch_shapes=[pltpu.VMEM((tm, d), BF16)],
        compiler_params=_cparams("parallel", "parallel", "arbitrary"),
        name=name,
    )(h, g.reshape(1, d), sc.reshape(bsz, 1, d), sh.reshape(bsz, 1, d), w)
    return res if emit_hn else res[0]


def _out_proj_kernel(xa_ref, xb_ref, wa_ref, wb_ref, h_ref, g_ref, o_ref):
    acc = jnp.dot(xa_ref[0], wa_ref[...], preferred_element_type=F32)
    acc += jnp.dot(xb_ref[0], wb_ref[...], preferred_element_type=F32)
    o_ref[0] = h_ref[0] + g_ref[0] * acc


def _out_proj(xa, xb, w, h, g1, *, tm=512, tn=1024):
    bsz, s, ka = xa.shape
    kb = xb.shape[2]
    n = w.shape[1]
    tm = min(tm, s)
    wa = w[:ka].astype(BF16)
    wb = w[ka:].astype(BF16)
    return pl.pallas_call(
        _out_proj_kernel,
        grid=(bsz, s // tm, n // tn),
        in_specs=[pl.BlockSpec((1, tm, ka), lambda b, i, j: (b, i, 0)),
                  pl.BlockSpec((1, tm, kb), lambda b, i, j: (b, i, 0)),
                  pl.BlockSpec((ka, tn), lambda b, i, j: (0, j)),
                  pl.BlockSpec((kb, tn), lambda b, i, j: (0, j)),
                  pl.BlockSpec((1, tm, tn), lambda b, i, j: (b, i, j)),
                  pl.BlockSpec((1, 1, tn), lambda b, i, j: (b, 0, j))],
        out_specs=pl.BlockSpec((1, tm, tn), lambda b, i, j: (b, i, j)),
        out_shape=jax.ShapeDtypeStruct((bsz, s, n), F32),
        compiler_params=_cparams("parallel", "parallel", "parallel"),
        name="out_proj",
    )(xa, xb, wa, wb, h, g1.reshape(bsz, 1, n))


def _mlstm_kernel(gb_ref, aqk_ref, v_ref, ao_ref, gcol_ref, cw_ref, cb_ref, hg_ref,
                  o_ref, xbuf, c_scr, n_scr, m_scr):
    L = MLSTM_CHUNK
    H, DQK, DV = MLSTM_HEADS, MLSTM_DQK, MLSTM_DV
    c = pl.program_id(1)

    @pl.when(c == 0)
    def _():
        xbuf[0:8, :] = jnp.zeros((8, 2 * A_QK), F32)
        c_scr[...] = jnp.zeros_like(c_scr)
        n_scr[...] = jnp.zeros_like(n_scr)
        m_scr[...] = jnp.zeros_like(m_scr)

    @pl.when(c > 0)
    def _():
        xbuf[0:8, :] = xbuf[L:L + 8, :]

    xbuf[8:L + 8, :] = aqk_ref[0]
    conv = cb_ref[...] + cw_ref[0:1, :] * xbuf[5:5 + L, :]
    for j in range(1, CONV_W):
        conv = conv + cw_ref[j:j + 1, :] * xbuf[5 + j:5 + j + L, :]
    qk = _silu(conv)

    r_i = lax.broadcasted_iota(I32, (L, L), 0)
    c_i = lax.broadcasted_iota(I32, (L, L), 1)
    tril = c_i <= r_i
    gcol = gcol_ref[0]
    grow = gcol.T

    for h in range(H):
        q = qk[:, h * DQK:(h + 1) * DQK] * (DQK ** -0.5)
        k = qk[:, A_QK + h * DQK:A_QK + (h + 1) * DQK]
        v = v_ref[0, :, h * DV:(h + 1) * DV]
        i_col = gcol[:, h:h + 1] + gb_ref[h]
        i_row = grow[h:h + 1, :] + gb_ref[h]
        lf_col = _log_sigmoid(gcol[:, H + h:H + h + 1] + gb_ref[H + h])
        lf_row = _log_sigmoid(grow[H + h:H + h + 1, :] + gb_ref[H + h])
        bcum_col = jnp.sum(jnp.where(tril, lf_row, 0.0), axis=1, keepdims=True)
        bcum_row = jnp.sum(jnp.where(r_i <= c_i, lf_col, 0.0), axis=0, keepdims=True)
        g_tot = jnp.sum(lf_row, axis=1, keepdims=True)
        m_prev = m_scr[h:h + 1, 0:1]
        d_intra = jnp.where(tril, bcum_col - bcum_row + i_row, -jnp.inf)
        d_inter = bcum_col + m_prev
        m_t = jnp.maximum(d_inter, jnp.max(d_intra, axis=1, keepdims=True))
        w_intra = jnp.exp(d_intra - m_t)
        w_inter = jnp.exp(d_inter - m_t)
        qb = q.astype(BF16)
        kb = k.astype(BF16)
        vb = v.astype(BF16)
        s = _dot_nt(qb, kb) * w_intra
        c_prev = c_scr[h]
        n_prev = n_scr[h:h + 1, :]
        num = jnp.dot(s.astype(BF16), vb, preferred_element_type=F32)
        num = num + w_inter * jnp.dot(qb, c_prev.astype(BF16), preferred_element_type=F32)
        den = jnp.sum(s, axis=1, keepdims=True) + w_inter * jnp.sum(q * n_prev, axis=1, keepdims=True)
        hh = num / jnp.maximum(jnp.abs(den), jnp.exp(-m_t))
        d_state = g_tot - bcum_col + i_col
        m_new = jnp.maximum(g_tot + m_prev, jnp.max(d_state, axis=0, keepdims=True))
        w_s = jnp.exp(d_state - m_new)
        w_c = jnp.exp(g_tot + m_prev - m_new)
        kw = k * w_s
        c_scr[h] = w_c * c_prev + jnp.dot(kw.T.astype(BF16), vb, preferred_element_type=F32)
        n_scr[h:h + 1, :] = w_c * n_prev + jnp.sum(kw, axis=0, keepdims=True)
        m_scr[h:h + 1, 0:1] = m_new
        y = hh * lax.rsqrt(jnp.mean(hh * hh, axis=-1, keepdims=True) + EPS) * hg_ref[:, h * DV:(h + 1) * DV]
        y = y * _sigmoid(ao_ref[0, :, h * DV:(h + 1) * DV])
        o_ref[0, :, h * DV:(h + 1) * DV] = y.astype(o_ref.dtype)


def _mlstm(proj, gate_b, conv_w, conv_b, head_g):
    bsz, s, _ = proj.shape
    L = MLSTM_CHUNK
    gate_blk = (2 * A_QK + 2 * A_V + 2 * B_QK + B_V) // LANES
    return pl.pallas_call(
        _mlstm_kernel,
        grid=(bsz, s // L),
        in_specs=[pl.BlockSpec(memory_space=pltpu.SMEM),
                  pl.BlockSpec((1, L, 2 * A_QK), lambda b, c: (b, c, 0)),
                  pl.BlockSpec((1, L, A_V), lambda b, c: (b, c, 1)),
                  pl.BlockSpec((1, L, A_V), lambda b, c: (b, c, 2)),
                  pl.BlockSpec((1, L, LANES), lambda b, c: (b, c, gate_blk)),
                  pl.BlockSpec((CONV_W, 2 * A_QK), lambda b, c: (0, 0)),
                  pl.BlockSpec((1, 2 * A_QK), lambda b, c: (0, 0)),
                  pl.BlockSpec((1, A_V), lambda b, c: (0, 0))],
        out_specs=pl.BlockSpec((1, L, A_V), lambda b, c: (b, c, 0)),
        out_shape=jax.ShapeDtypeStruct((bsz, s, A_V), BF16),
        scratch_shapes=[pltpu.VMEM((L + 8, 2 * A_QK), F32),
                        pltpu.VMEM((MLSTM_HEADS, MLSTM_DQK, MLSTM_DV), F32),
                        pltpu.VMEM((8, MLSTM_DQK), F32),
                        pltpu.VMEM((8, LANES), F32)],
        compiler_params=_cparams("parallel", "arbitrary"),
        name="mlstm",
    )(gate_b.reshape(-1), proj, proj, proj, proj, conv_w, conv_b.reshape(1, -1), head_g.reshape(1, -1))


def _rope_consts(dh):
    half = dh // 8
    lane = np.arange(LANES) % dh
    inv_half = ROPE_THETA ** (-jnp.arange(half, dtype=F32) / half)
    inv = jnp.where(jnp.asarray(lane < 2 * half), inv_half[lane % half], 0.0)
    lo = (lane < half).astype(np.float32)
    hi = ((lane >= half) & (lane < 2 * half)).astype(np.float32)
    return (inv.astype(F32).reshape(1, LANES), jnp.asarray(lo).reshape(1, LANES),
            jnp.asarray(hi).reshape(1, LANES))


def _rope_tables(pos_ref, inv_ref, lo_ref, hi_ref):
    ang = pos_ref[0].astype(F32) * inv_ref[...]
    cs = jnp.cos(ang)
    sn = jnp.sin(ang)
    return cs, -sn * lo_ref[...], sn * hi_ref[...]


def _rope_apply(x, cs, sa, sb, half):
    return x * cs + pltpu.roll(x, LANES - half, 1) * sa + pltpu.roll(x, half, 1) * sb


def _diff_prep_kernel(pos_ref, inv_ref, lo_ref, hi_ref, q_ref, k_ref, v_ref, qo_ref, ko_ref, vo_ref):
    cs, sa, sb = _rope_tables(pos_ref, inv_ref, lo_ref, hi_ref)
    half = DIFF_DQK // 8
    scale = DIFF_DQK ** -0.5
    for h in range(DIFF_HEADS):
        sl = slice(h * LANES, (h + 1) * LANES)
        qo_ref[0, sl, :] = (_rope_apply(q_ref[0, :, sl], cs, sa, sb, half) * scale).T.astype(BF16)
        ko_ref[0, :, sl] = _rope_apply(k_ref[0, :, sl], cs, sa, sb, half).astype(BF16)
        vo_ref[0, sl, :] = v_ref[0, :, sl].T.astype(BF16)


def _diff_prep(proj, pos3, ts=512):
    bsz, s, _ = proj.shape
    ts = min(ts, s)
    inv, lo, hi = _rope_consts(DIFF_DQK)
    base = (2 * A_QK + 2 * A_V) // B_QK
    cst = pl.BlockSpec((1, LANES), lambda b, i: (0, 0))
    osd = jax.ShapeDtypeStruct((bsz, s, B_QK), BF16)
    osd_t = jax.ShapeDtypeStruct((bsz, B_QK, s), BF16)
    spec_t = pl.BlockSpec((1, B_QK, ts), lambda b, i: (b, 0, i))
    return pl.pallas_call(
        _diff_prep_kernel,
        grid=(bsz, s // ts),
        in_specs=[pl.BlockSpec((1, ts, 1), lambda b, i: (b, i, 0)), cst, cst, cst,
                  pl.BlockSpec((1, ts, B_QK), lambda b, i: (b, i, base)),
                  pl.BlockSpec((1, ts, B_QK), lambda b, i: (b, i, base + 1)),
                  pl.BlockSpec((1, ts, B_V), lambda b, i: (b, i, base + 2))],
        out_specs=[spec_t, pl.BlockSpec((1, ts, B_QK), lambda b, i: (b, i, 0)), spec_t],
        out_shape=[osd_t, osd, osd_t],
        compiler_params=_cparams("parallel", "parallel"),
        name="diff_prep",
    )(pos3, inv, lo, hi, proj, proj, proj)


def _causal_pairs(nq):
    pairs = [(i, j) for i in range(nq) for j in range(i + 1)]
    return (jnp.asarray([p[0] for p in pairs], I32), jnp.asarray([p[1] for p in pairs], I32))


def _diff_attn_kernel(it_ref, jt_ref, lam_ref, q_ref, k_ref, v_ref, g_ref, o_ref, qs_scr, m_scr, l_scr, acc_scr,
                      *, tq, tk, lambda_init):
    i = it_ref[pl.program_id(2)]
    j = jt_ref[pl.program_id(2)]

    @pl.when(j == 0)
    def _():
        q = q_ref[0]
        feat = lax.broadcasted_iota(I32, q.shape, 0)
        zero = jnp.zeros_like(q)
        qs_scr[:, 0:tq] = jnp.where(feat < DIFF_DQK, q, zero)
        qs_scr[:, tq:2 * tq] = jnp.where(feat >= DIFF_DQK, q, zero)
        m_scr[...] = jnp.full_like(m_scr, -jnp.inf)
        l_scr[...] = jnp.zeros_like(l_scr)
        acc_scr[...] = jnp.zeros_like(acc_scr)

    def step(masked):
        def scores(cols):
            s = jnp.dot(k_ref[0], qs_scr[:, cols], preferred_element_type=F32)
            if masked:
                key = lax.broadcasted_iota(I32, (tk, ATTN_COLS), 0)
                qry = lax.broadcasted_iota(I32, (tk, ATTN_COLS), 1) + (cols.start % tq)
                s = jnp.where(key <= qry, s, -jnp.inf)
            return s, v_ref[0]
        _flash_tiles_t(2 * tq, scores, m_scr, l_scr, acc_scr)

    @pl.when(j < i)
    def _():
        step(False)

    @pl.when(j == i)
    def _():
        step(True)
        lv = lam_ref[...]
        lam = (jnp.exp(jnp.sum(lv[0:1] * lv[1:2], axis=1, keepdims=True))
               - jnp.exp(jnp.sum(lv[2:3] * lv[3:4], axis=1, keepdims=True)) + lambda_init)
        o0 = acc_scr[:, 0:tq] / l_scr[:, 0:tq]
        o1 = acc_scr[:, tq:2 * tq] / l_scr[:, tq:2 * tq]
        a = o0 - lam * o1
        y = a * lax.rsqrt(jnp.mean(a * a, axis=0, keepdims=True) + EPS) * g_ref[0]
        o_ref[0] = (y * (1.0 - lambda_init)).T.astype(o_ref.dtype)


def _diff_attn(qt, k, vt, lam_vec, head_g, lambda_init, *, tq=512):
    bsz, s, _ = k.shape
    tq = min(tq, s)
    it, jt = _causal_pairs(s // tq)
    grid_spec = pltpu.PrefetchScalarGridSpec(
        num_scalar_prefetch=2,
        grid=(bsz, DIFF_HEADS, it.shape[0]),
        in_specs=[pl.BlockSpec((4, DIFF_DQK), lambda b, h, t, it, jt: (0, 0)),
                  pl.BlockSpec((1, LANES, tq), lambda b, h, t, it, jt: (b, h, it[t])),
                  pl.BlockSpec((1, tq, LANES), lambda b, h, t, it, jt: (b, jt[t], h)),
                  pl.BlockSpec((1, DIFF_DV, tq), lambda b, h, t, it, jt: (b, h, jt[t])),
                  pl.BlockSpec((1, DIFF_DV, 1), lambda b, h, t, it, jt: (h, 0, 0))],
        out_specs=pl.BlockSpec((1, tq, DIFF_DV), lambda b, h, t, it, jt: (b, it[t], h)),
        scratch_shapes=[pltpu.VMEM((LANES, 2 * tq), BF16),
                        pltpu.VMEM((1, 2 * tq), F32),
                        pltpu.VMEM((1, 2 * tq), F32),
                        pltpu.VMEM((DIFF_DV, 2 * tq), F32)])
    return pl.pallas_call(
        functools.partial(_diff_attn_kernel, tq=tq, tk=tq, lambda_init=lambda_init),
        grid_spec=grid_spec,
        out_shape=jax.ShapeDtypeStruct((bsz, s, B_V), BF16),
        compiler_params=_cparams("parallel", "parallel", "arbitrary"),
        name="diff_attn",
    )(it, jt, lam_vec, qt, k, vt, head_g.reshape(DIFF_HEADS, DIFF_DV, 1))


def _moba_prep_kernel(pos_ref, inv_ref, lo_ref, hi_ref, q_ref, k_ref, v_ref,
                      qo_ref, ko_ref, vo_ref, sel_ref, kmt_scr, qf_scr, *, nb):
    i = pl.program_id(1)
    tb = MOBA_BLOCK
    half = MOBA_DH // 8
    scale = MOBA_DH ** -0.5

    @pl.when(i == 0)
    def _():
        kmt_scr[...] = jnp.zeros_like(kmt_scr)

    cs, sa, sb = _rope_tables(pos_ref, inv_ref, lo_ref, hi_ref)
    kmeans = []
    for h in range(MOBA_HEADS):
        sl = slice(h * LANES, (h + 1) * LANES)
        qr = _rope_apply(q_ref[0, :, sl], cs, sa, sb, half)
        kr = _rope_apply(k_ref[0, :, sl], cs, sa, sb, half)
        qf_scr[:, sl] = qr
        qo_ref[0, sl, :] = (qr * scale).T.astype(BF16)
        ko_ref[0, :, sl] = kr.astype(BF16)
        vo_ref[0, sl, :] = v_ref[0, :, sl].T.astype(BF16)
        kmeans.append(jnp.mean(kr, axis=0, keepdims=True))

    gate = _dot_nt(qf_scr[...], kmt_scr[...], precision=HIGHEST)
    lane = lax.broadcasted_iota(I32, (tb, LANES), 1)
    n = lane % nb
    valid = n < i
    gm = jnp.where(valid, gate, -jnp.inf)
    rank = jnp.zeros((tb, LANES), I32)
    for d in range(1, nb):
        wraps = (n + d) >= nb
        partner = jnp.where(wraps, pltpu.roll(gm, nb - d, 1), pltpu.roll(gm, LANES - d, 1))
        pn = jnp.where(wraps, n + d - nb, n + d)
        beats = (partner > gm) | ((partner == gm) & (pn < n))
        rank = rank + beats.astype(I32)
    sel_ref[0] = jnp.where(valid & (rank < MOBA_TOPK), 1.0, 0.0).astype(F32).T

    for h in range(MOBA_HEADS):
        _store_row(kmt_scr, h * nb + i, kmeans[h], slice(h * LANES, (h + 1) * LANES))


def _moba_prep(proj, pos3):
    bsz, s, _ = proj.shape
    tb = MOBA_BLOCK
    nb = s // tb
    assert s % tb == 0 and MOBA_HEADS * nb <= LANES and nb >= MOBA_TOPK
    inv, lo, hi = _rope_consts(MOBA_DH)
    cst = pl.BlockSpec((1, LANES), lambda b, i: (0, 0))
    osd = jax.ShapeDtypeStruct((bsz, s, C_W), BF16)
    osd_t = jax.ShapeDtypeStruct((bsz, C_W, s), BF16)
    spec_t = pl.BlockSpec((1, C_W, tb), lambda b, i: (b, 0, i))
    return pl.pallas_call(
        functools.partial(_moba_prep_kernel, nb=nb),
        grid=(bsz, nb),
        in_specs=[pl.BlockSpec((1, tb, 1), lambda b, i: (b, i, 0)), cst, cst, cst,
                  pl.BlockSpec((1, tb, C_W), lambda b, i: (b, i, 0)),
                  pl.BlockSpec((1, tb, C_W), lambda b, i: (b, i, 1)),
                  pl.BlockSpec((1, tb, C_W), lambda b, i: (b, i, 2))],
        out_specs=[spec_t, pl.BlockSpec((1, tb, C_W), lambda b, i: (b, i, 0)), spec_t,
                   pl.BlockSpec((1, LANES, tb), lambda b, i: (b, 0, i))],
        out_shape=[osd_t, osd, osd_t, jax.ShapeDtypeStruct((bsz, LANES, s), F32)],
        scratch_shapes=[pltpu.VMEM((LANES, C_W), F32), pltpu.VMEM((tb, C_W), F32)],
        compiler_params=_cparams("parallel", "arbitrary"),
        name="moba_prep",
    )(pos3, inv, lo, hi, proj, proj, proj)


def _moba_attn_kernel(it_ref, jt_ref, q_ref, k_ref, v_ref, sel_ref, o_ref, m_scr, l_scr, acc_scr, *, nb):
    h = pl.program_id(1)
    i = it_ref[pl.program_id(2)]
    j = jt_ref[pl.program_id(2)]
    tb = MOBA_BLOCK
    tq = 2 * tb

    def pick_bias(cols, blk):
        row = h * nb + blk
        base = pl.multiple_of((row // SUBLANES) * SUBLANES, SUBLANES)
        grp = sel_ref[0, pl.ds(base, SUBLANES), cols]
        sub = lax.broadcasted_iota(I32, grp.shape, 0)
        hit = jnp.max(jnp.where(sub == row - base, grp, 0.0), axis=0, keepdims=True)
        return jnp.where(hit > 0.5, 0.0, -jnp.inf)

    @pl.when(j == 0)
    def _():
        m_scr[...] = jnp.full_like(m_scr, -jnp.inf)
        l_scr[...] = jnp.zeros_like(l_scr)
        acc_scr[...] = jnp.zeros_like(acc_scr)

        def scores(cols):
            c0 = cols.start
            if c0 < tb:
                key = lax.broadcasted_iota(I32, (tb, ATTN_COLS), 0)
                qry = lax.broadcasted_iota(I32, (tb, ATTN_COLS), 1) + c0
                s = jnp.dot(k_ref[0, 0:tb, :], q_ref[0, :, cols], preferred_element_type=F32)
                return jnp.where(key <= qry, s, -jnp.inf), v_ref[0, :, 0:tb]
            key = lax.broadcasted_iota(I32, (tq, ATTN_COLS), 0)
            qry = lax.broadcasted_iota(I32, (tq, ATTN_COLS), 1) + c0
            s = jnp.dot(k_ref[0], q_ref[0, :, cols], preferred_element_type=F32)
            s = jnp.where(key < tb, s + pick_bias(cols, 2 * i), jnp.where(key <= qry, s, -jnp.inf))
            return s, v_ref[0]
        _flash_tiles_t(tq, scores, m_scr, l_scr, acc_scr)

    @pl.when((j > 0) & (j <= i))
    def _():
        kb = 2 * (i - j)
        key = lax.broadcasted_iota(I32, (tq, ATTN_COLS), 0)

        def scores(cols):
            bias = jnp.where(key < tb, pick_bias(cols, kb), pick_bias(cols, kb + 1))
            return jnp.dot(k_ref[0], q_ref[0, :, cols], preferred_element_type=F32) + bias, v_ref[0]
        _flash_tiles_t(tq, scores, m_scr, l_scr, acc_scr)

    @pl.when(j == i)
    def _():
        o_ref[0] = (acc_scr[...] / l_scr[...]).T.astype(o_ref.dtype)


def _moba_attn(qt, k, vt, sel_t):
    bsz, s, _ = k.shape
    tq = 2 * MOBA_BLOCK
    nb = s // MOBA_BLOCK
    assert s % tq == 0
    it, jt = _causal_pairs(s // tq)
    grid_spec = pltpu.PrefetchScalarGridSpec(
        num_scalar_prefetch=2,
        grid=(bsz, MOBA_HEADS, it.shape[0]),
        in_specs=[pl.BlockSpec((1, MOBA_DH, tq), lambda b, h, t, it, jt: (b, h, it[t])),
                  pl.BlockSpec((1, tq, MOBA_DH), lambda b, h, t, it, jt: (b, it[t] - jt[t], h)),
                  pl.BlockSpec((1, MOBA_DH, tq), lambda b, h, t, it, jt: (b, h, it[t] - jt[t])),
                  pl.BlockSpec((1, LANES, tq), lambda b, h, t, it, jt: (b, 0, it[t]))],
        out_specs=pl.BlockSpec((1, tq, MOBA_DH), lambda b, h, t, it, jt: (b, it[t], h)),
        scratch_shapes=[pltpu.VMEM((1, tq), F32), pltpu.VMEM((1, tq), F32), pltpu.VMEM((MOBA_DH, tq), F32)])
    return pl.pallas_call(
        functools.partial(_moba_attn_kernel, nb=nb),
        grid_spec=grid_spec,
        out_shape=jax.ShapeDtypeStruct((bsz, s, C_W), BF16),
        compiler_params=_cparams("parallel", "parallel", "arbitrary"),
        name="moba_attn",
    )(it, jt, qt, k, vt, sel_t)


def _sgu_kernel(u_ref, v_ref, lng_ref, lnb_ref, w_ref, bt_ref, o_ref, *, ts):
    L = SGU_CHUNK
    r_i = lax.broadcasted_iota(I32, (L, L), 0)
    c_i = lax.broadcasted_iota(I32, (L, L), 1)
    tril = c_i <= r_i
    for g in range(SGU_GROUPS):
        sl = slice(g * SGU_CH, (g + 1) * SGU_CH)
        w = jnp.where(tril, w_ref[g], 0.0).astype(BF16)
        bias = bt_ref[:, g:g + 1]
        for c in range(ts // L):
            rows = slice(c * L, (c + 1) * L)
            vv = _gelu_tanh(v_ref[0, rows, sl])
            mu = jnp.mean(vv, axis=-1, keepdims=True)
            dv = vv - mu
            var = jnp.mean(dv * dv, axis=-1, keepdims=True)
            vn = dv * lax.rsqrt(var + EPS) * lng_ref[:, sl] + lnb_ref[:, sl]
            mixed = jnp.dot(w, vn.astype(BF16), preferred_element_type=F32) + bias
            o_ref[0, rows, sl] = (_gelu_tanh(u_ref[0, rows, sl]) * mixed).astype(o_ref.dtype)


def _sgu(proj, ln_g, ln_b, w_s, b_s, ts=256):
    bsz, s, _ = proj.shape
    ts = min(ts, s)
    base = 3 * C_W // D_W
    return pl.pallas_call(
        functools.partial(_sgu_kernel, ts=ts),
        grid=(bsz, s // ts),
        in_specs=[pl.BlockSpec((1, ts, D_W), lambda b, i: (b, i, base)),
                  pl.BlockSpec((1, ts, D_W), lambda b, i: (b, i, base + 1)),
                  pl.BlockSpec((1, D_W), lambda b, i: (0, 0)),
                  pl.BlockSpec((1, D_W), lambda b, i: (0, 0)),
                  pl.BlockSpec((SGU_GROUPS, SGU_CHUNK, SGU_CHUNK), lambda b, i: (0, 0, 0)),
                  pl.BlockSpec((SGU_CHUNK, SGU_GROUPS), lambda b, i: (0, 0))],
        out_specs=pl.BlockSpec((1, ts, D_W), lambda b, i: (b, i, 0)),
        out_shape=jax.ShapeDtypeStruct((bsz, s, D_W), BF16),
        compiler_params=_cparams("parallel", "parallel"),
        name="sgu",
    )(proj, proj, ln_g.reshape(1, -1), ln_b.reshape(1, -1), w_s, b_s.T)


def _topk_rows(s_scr, k, val_scr, idx_scr, payload_scr=None):
    n, t = s_scr.shape
    row = lax.broadcasted_iota(I32, (n, t), 0).astype(F32)

    def body(it, carry):
        s = s_scr[...]
        m = jnp.max(s, axis=0, keepdims=True)
        pos = jnp.min(jnp.where(s == m, row, float(n)), axis=0, keepdims=True)
        hit = row == pos
        _store_row(val_scr, it, m)
        if payload_scr is None:
            _store_row(idx_scr, it, pos)
        else:
            _store_row(idx_scr, it, jnp.max(jnp.where(hit, payload_scr[...], -1.0), axis=0, keepdims=True))
        s_scr[...] = jnp.where(hit, -jnp.inf, s)
        return carry

    val_scr[...] = jnp.zeros_like(val_scr)
    idx_scr[...] = jnp.zeros_like(idx_scr)
    lax.fori_loop(0, k, body, 0)


def _peer_topk_kernel(q_ref, keys_ref, idx_ref, g_ref, s_scr, v1, i1, v2, i2, cand, cidx, vt, it):
    K = PEER_TOPK
    half = PEER_DQ // 2
    for c, (vs, ix) in enumerate(((v1, i1), (v2, i2))):
        s_scr[...] = _dot_nt(keys_ref[0, c], q_ref[:, c * half:(c + 1) * half], precision=HIGHEST)
        _topk_rows(s_scr, K, vs, ix)
    cand[...] = jnp.full(cand.shape, -jnp.inf, F32)
    cidx[...] = jnp.zeros(cidx.shape, F32)
    off = 0
    for a in range(K):
        nb = K // (a + 1)
        cand[off:off + nb, :] = v1[a:a + 1, :] + v2[0:nb, :]
        cidx[off:off + nb, :] = i1[a:a + 1, :] * float(PEER_NKEYS) + i2[0:nb, :]
        off += nb
    _topk_rows(cand, K, vt, it, payload_scr=cidx)
    ts = vt[...]
    e = jnp.exp(ts - jnp.max(ts, axis=0, keepdims=True))
    g_ref[0] = e / jnp.sum(e, axis=0, keepdims=True)
    idx_ref[0] = it[...].astype(I32)


def _peer_topk(q2, sub_keys, tt=512):
    n_tok = q2.shape[0]
    tt = min(tt, n_tok)
    K = PEER_TOPK
    half = PEER_DQ // 2
    vs = lambda r, dt: pltpu.VMEM((r, tt), dt)
    n_cand = sum(K // (a + 1) for a in range(K))
    n_cand = -(-n_cand // SUBLANES) * SUBLANES
    return pl.pallas_call(
        _peer_topk_kernel,
        grid=(n_tok // tt, PEER_HEADS),
        in_specs=[pl.BlockSpec((tt, PEER_DQ), lambda i, h: (i, h)),
                  pl.BlockSpec((1, 2, PEER_NKEYS, half), lambda i, h: (h, 0, 0, 0))],
        out_specs=[pl.BlockSpec((1, K, tt), lambda i, h: (h, 0, i)),
                   pl.BlockSpec((1, K, tt), lambda i, h: (h, 0, i))],
        out_shape=[jax.ShapeDtypeStruct((PEER_HEADS, K, n_tok), I32),
                   jax.ShapeDtypeStruct((PEER_HEADS, K, n_tok), F32)],
        scratch_shapes=[vs(PEER_NKEYS, F32), vs(K, F32), vs(K, F32), vs(K, F32), vs(K, F32),
                        vs(n_cand, F32), vs(n_cand, F32), vs(K, F32), vs(K, F32)],
        compiler_params=_cparams("parallel", "parallel"),
        name="peer_topk",
    )(q2, sub_keys)


def _pack_kernel(u_ref, v_ref, o_ref):
    ub = lax.bitcast_convert_type(u_ref[...].astype(BF16).astype(F32), I32)
    vb = lax.bitcast_convert_type(v_ref[...].astype(BF16).astype(F32), I32)
    o_ref[...] = (ub & jnp.int32(-65536)) | lax.shift_right_logical(vb, jnp.int32(16))


def _pack_tables(u_all, v_all, layer, tm=512):
    _, n, d = u_all.shape
    spec = pl.BlockSpec((None, tm, d), lambda i: (layer, i, 0))
    packed = pl.pallas_call(
        _pack_kernel,
        grid=(n // tm,),
        in_specs=[spec, spec],
        out_specs=pl.BlockSpec((tm, d), lambda i: (i, 0)),
        out_shape=jax.ShapeDtypeStruct((n, d), I32),
        compiler_params=_cparams("parallel"),
        name="peer_pack",
    )(u_all, v_all)
    return packed.reshape(n, d // LANES, LANES)


PEER_SLOTS = SUBLANES
PEER_AHEAD = 6
PEER_DMA_QUEUES = 2


def _peer_gather_kernel(idx_ref, gate_ref, x_ref, h_ref, g2_ref, tab_hbm, o_ref, *rest, tt):
    R = PEER_SEL
    bufs, (stage, sem) = rest[:PEER_SLOTS], rest[PEER_SLOTS:]
    step = pl.program_id(0)
    d = x_ref.shape[1]
    nch = d // LANES
    ngrp = R // SUBLANES
    per_grp = R // ngrp

    def row_copy(t, j, slot):
        return pltpu.make_async_copy(tab_hbm.at[idx_ref[t, j]], bufs[slot].at[:, j, :], sem.at[slot])

    def wait(slot):
        pltpu.make_async_copy(bufs[(slot + 1) % PEER_SLOTS], bufs[slot], sem.at[slot]).wait()

    def tile(slot, g, c):
        return bufs[slot][c, g * SUBLANES:(g + 1) * SUBLANES, :]

    lane = lax.broadcasted_iota(I32, (SUBLANES, LANES), 1)
    sub = lax.broadcasted_iota(I32, (SUBLANES, LANES), 0)
    hi_mask = jnp.int32(-65536)

    @pl.when(step == 0)
    def _():
        for t0 in range(PEER_AHEAD):
            def prime(j, carry, t0=t0):
                row_copy(t0, j, t0).start()
                return carry
            lax.fori_loop(0, R, prime, 0)

    def group(gi, carry):
        base = pl.multiple_of(gi * SUBLANES, SUBLANES)
        xg = x_ref[pl.ds(base, SUBLANES), :]
        gg = gate_ref[pl.ds(base, SUBLANES), :]
        for r in range(SUBLANES):
            slot = r
            nxt = base + r + PEER_AHEAD
            nslot = (r + PEER_AHEAD) % PEER_SLOTS
            wait(slot)
            xb = [jnp.broadcast_to(xg[r:r + 1, c * LANES:(c + 1) * LANES], (SUBLANES, LANES)) for c in range(nch)]
            grow = jnp.broadcast_to(gg[r:r + 1, :], (SUBLANES, LANES))
            wb = []
            for g in range(ngrp):
                for j in range(g * per_grp, (g + 1) * per_grp):
                    row_copy(nxt, j, nslot).start(priority=j % PEER_DMA_QUEUES)
                acc = lax.bitcast_convert_type(tile(slot, g, 0) & hi_mask, F32) * xb[0]
                for c in range(1, nch):
                    acc = acc + lax.bitcast_convert_type(tile(slot, g, c) & hi_mask, F32) * xb[c]
                act = jnp.sum(acc, axis=1, keepdims=True)
                gcol = jnp.sum(jnp.where(lane == g * SUBLANES + sub, grow, 0.0), axis=1, keepdims=True)
                wb.append(jnp.broadcast_to(gcol * _gelu_tanh(act), (SUBLANES, LANES)))
            for c in range(nch):
                acc = lax.bitcast_convert_type(tile(slot, 0, c) << 16, F32) * wb[0]
                for g in range(1, ngrp):
                    acc = acc + lax.bitcast_convert_type(tile(slot, g, c) << 16, F32) * wb[g]
                stage[r:r + 1, c * LANES:(c + 1) * LANES] = jnp.sum(acc, axis=0, keepdims=True)
        rows = pl.ds(base, SUBLANES)
        o_ref[rows, :] = h_ref[rows, :] + g2_ref[0] * stage[...]
        return carry

    lax.fori_loop(0, tt // SUBLANES, group, 0)

    @pl.when(step == pl.num_programs(0) - 1)
    def _():
        for k in range(PEER_AHEAD):
            wait((tt + k) % PEER_SLOTS)


def _peer_gather(idx, gates, hn2, h2, g2, table, s, tt=128):
    n_tok, d = hn2.shape
    tt = min(tt, s)
    assert s % tt == 0 and tt % PEER_SLOTS == 0 and d % (SUBLANES * LANES) == 0 and PEER_SEL % SUBLANES == 0
    assert PEER_SEL <= LANES and PEER_AHEAD < PEER_SLOTS - 1
    per_b = s // tt
    R = PEER_SEL
    nsteps = n_tok // tt
    idx3 = idx.reshape(nsteps, tt, R)
    head = jnp.concatenate([idx3[1:, :SUBLANES], jnp.zeros((1, SUBLANES, R), I32)], axis=0)
    idx = jnp.concatenate([idx3, head], axis=1).reshape(nsteps * (tt + SUBLANES), R)
    return pl.pallas_call(
        functools.partial(_peer_gather_kernel, tt=tt),
        grid=(nsteps,),
        in_specs=[pl.BlockSpec((tt + SUBLANES, R), lambda i: (i, 0), memory_space=pltpu.SMEM),
                  pl.BlockSpec((tt, R), lambda i: (i, 0)),
                  pl.BlockSpec((tt, d), lambda i: (i, 0)),
                  pl.BlockSpec((tt, d), lambda i: (i, 0)),
                  pl.BlockSpec((1, 1, d), lambda i: (i // per_b, 0, 0)),
                  pl.BlockSpec(memory_space=pl.ANY)],
        out_specs=pl.BlockSpec((tt, d), lambda i: (i, 0)),
        out_shape=jax.ShapeDtypeStruct((n_tok, d), F32),
        scratch_shapes=[pltpu.VMEM((d // LANES, R, LANES), I32)] * PEER_SLOTS
        + [pltpu.VMEM((SUBLANES, d), F32), pltpu.SemaphoreType.DMA((PEER_SLOTS,))],
        compiler_params=_cparams("arbitrary"),
        name="peer_gather",
    )(idx, gates, hn2, h2, g2.reshape(-1, 1, d), table)


def _peer_layer(h, g, sc, sh, g2, w_q, sub_keys, u_all, v_all, layer):
    bsz, s, d = h.shape
    q, hn = _nm_matmul(h, g, sc, sh, w_q.astype(BF16), tm=512, tn=1024, emit_hn=True, name="peer_query")
    n_tok = bsz * s
    idx, gates = _peer_topk(q.reshape(n_tok, -1), sub_keys)
    idx = idx.transpose(2, 0, 1).reshape(n_tok, PEER_SEL)
    gates = gates.transpose(2, 0, 1).reshape(n_tok, PEER_SEL)
    table = _pack_tables(u_all, v_all, layer)
    out = _peer_gather(idx, gates, hn.reshape(n_tok, d), h.reshape(n_tok, d), g2, table, s)
    return out.reshape(bsz, s, d)


def _rmsnorm_kernel(x_ref, g_ref, o_ref):
    x = x_ref[...]
    o_ref[...] = x * lax.rsqrt(jnp.mean(x * x, axis=-1, keepdims=True) + EPS) * g_ref[...]


def _final_norm(h, g, tm=512):
    bsz, s, d = h.shape
    n = bsz * s
    tm = min(tm, n)
    out = pl.pallas_call(
        _rmsnorm_kernel,
        grid=(n // tm,),
        in_specs=[pl.BlockSpec((tm, d), lambda i: (i, 0)), pl.BlockSpec((1, d), lambda i: (0, 0))],
        out_specs=pl.BlockSpec((tm, d), lambda i: (i, 0)),
        out_shape=jax.ShapeDtypeStruct((n, d), F32),
        compiler_params=_cparams("parallel"),
        name="final_norm",
    )(h.reshape(n, d), g.reshape(1, d))
    return out.reshape(bsz, s, d)


def _even_mixer(h, pos3, g, sc, sh, g1, layer, w_in, w_out, conv_w, conv_b, gate_b, mlstm_g, lam_vec, diff_g):
    lambda_init = 0.8 - 0.6 * math.exp(-0.3 * layer)
    d = h.shape[-1]
    cuts = np.cumsum([2 * A_QK, A_V, A_V, 2 * MLSTM_HEADS, B_QK, B_QK, B_V])
    n_gate = 2 * MLSTM_HEADS
    w_cat = jnp.concatenate([w_in[:, :cuts[2]], w_in[:, cuts[3]:], w_in[:, cuts[2]:cuts[3]],
                             jnp.zeros((d, LANES - n_gate), w_in.dtype)], axis=1).astype(BF16)
    proj = _nm_matmul(h, g, sc, sh, w_cat, tm=1024, tn=896, name="even_in_proj")
    h_a = _mlstm(proj, gate_b, conv_w, conv_b, mlstm_g)
    bq, bk, bv = _diff_prep(proj, pos3)
    h_b = _diff_attn(bq, bk, bv, lam_vec, diff_g, lambda_init)
    return _out_proj(h_a, h_b, w_out, h, g1)


def _odd_mixer(h, pos3, g, sc, sh, g1, w_in, w_out, ln_g, ln_b, sgu_w, sgu_b):
    proj = _nm_matmul(h, g, sc, sh, w_in.astype(BF16), tm=1024, tn=1024, name="odd_in_proj")
    cq, ck, cv, sel = _moba_prep(proj, pos3)
    h_c = _moba_attn(cq, ck, cv, sel)
    h_d = _sgu(proj, ln_g, ln_b, sgu_w, sgu_b)
    return _out_proj(h_c, h_d, w_out, h, g1)


def kernel(x, c, positions, mod_w, mod_b, norm_g, final_g, even_w_in, even_w_out, mlstm_conv_w, mlstm_conv_b, mlstm_gate_b, mlstm_head_g, diff_lambda, diff_head_g, odd_w_in, odd_w_out, sgu_ln_g, sgu_ln_b, sgu_w, sgu_b, peer_w_q, peer_sub_keys, peer_u, peer_v):
    depth = mod_w.shape[0]
    bsz, s, d = x.shape
    mod = _modulation(c, mod_w, mod_b)
    pos3 = positions.reshape(bsz, s, 1)
    h = x
    for layer in range(depth):
        sh1, sc1, g1, sh2, sc2, g2 = [mod[layer, :, i * d:(i + 1) * d] for i in range(6)]
        if layer % 2 == 0:
            e = layer // 2
            h = _even_mixer(h, pos3, norm_g[layer, 0], sc1, sh1, g1, layer, even_w_in[e], even_w_out[e],
                            mlstm_conv_w[e], mlstm_conv_b[e], mlstm_gate_b[e], mlstm_head_g[e],
                            diff_lambda[e], diff_head_g[e])
        else:
            o = layer // 2
            h = _odd_mixer(h, pos3, norm_g[layer, 0], sc1, sh1, g1, odd_w_in[o], odd_w_out[o],
                           sgu_ln_g[o], sgu_ln_b[o], sgu_w[o], sgu_b[o])
        h = _peer_layer(h, norm_g[layer, 1], sc2, sh2, g2, peer_w_q[layer], peer_sub_keys[layer],
                        peer_u, peer_v, layer)
    return _final_norm(h, final_g)
```

```python
import functools
import math

import numpy as np
import jax
import jax.numpy as jnp
from jax import lax
from jax.experimental import pallas as pl
from jax.experimental.pallas import tpu as pltpu

F32 = jnp.float32
BF16 = jnp.bfloat16
I32 = jnp.int32
HIGHEST = lax.Precision.HIGHEST

EPS = 1e-6
ROPE_THETA = 500000.0
LANES = 128
SUBLANES = 8
VMEM_LIMIT_BYTES = 48 * 1024 * 1024

MLSTM_HEADS, MLSTM_DQK, MLSTM_DV, MLSTM_CHUNK, CONV_W = 4, 128, 256, 128, 4
DIFF_HEADS, DIFF_DQK, DIFF_DV = 8, 64, 128
MOBA_HEADS, MOBA_DH, MOBA_BLOCK, MOBA_TOPK = 8, 128, 256, 3
SGU_GROUPS, SGU_CH, SGU_CHUNK = 8, 128, 128
PEER_HEADS, PEER_NKEYS, PEER_DQ, PEER_TOPK = 8, 128, 256, 16
PEER_SEL = PEER_HEADS * PEER_TOPK

A_QK = MLSTM_HEADS * MLSTM_DQK
A_V = MLSTM_HEADS * MLSTM_DV
B_QK = DIFF_HEADS * 2 * DIFF_DQK
B_V = DIFF_HEADS * DIFF_DV
C_W = MOBA_HEADS * MOBA_DH
D_W = SGU_GROUPS * SGU_CH


def _cparams(*sem):
    return pltpu.CompilerParams(dimension_semantics=sem, vmem_limit_bytes=VMEM_LIMIT_BYTES)


def _dot_nt(a, b, precision=None):
    return lax.dot_general(a, b, (((1,), (1,)), ((), ())), preferred_element_type=F32, precision=precision)


def _store_row(ref, row, val, cols=slice(None)):
    base = pl.multiple_of((row // SUBLANES) * SUBLANES, SUBLANES)
    blk = ref[pl.ds(base, SUBLANES), cols]
    r = lax.broadcasted_iota(I32, blk.shape, 0)
    ref[pl.ds(base, SUBLANES), cols] = jnp.where(r == row - base, val, blk)


ATTN_ROWS = 128


def _flash_tiles(n_rows, score_fn, m_ref, l_ref, acc_ref, first=False):
    tiles = [slice(r0, r0 + ATTN_ROWS) for r0 in range(0, n_rows, ATTN_ROWS)]
    prev = None if first else [(m_ref[r, :], l_ref[r, :], acc_ref[r, :]) for r in tiles]
    scored = [score_fn(rows) for rows in tiles]
    soft = []
    for t, (s, v) in enumerate(scored):
        m_new = jnp.max(s, axis=1, keepdims=True)
        if not first:
            m_new = jnp.maximum(prev[t][0], m_new)
        p = jnp.exp(s - m_new)
        soft.append((m_new, jnp.sum(p, axis=1, keepdims=True), p.astype(BF16)))
    new = []
    for t, (m_new, l_new, p) in enumerate(soft):
        acc_new = jnp.dot(p, scored[t][1], preferred_element_type=F32)
        if not first:
            m_prev, l_prev, acc_prev = prev[t]
            alpha = jnp.exp(m_prev - m_new)
            l_new = alpha * l_prev + l_new
            acc_new = alpha * acc_prev + acc_new
        new.append((m_new, l_new, acc_new))
    for rows, (m_new, l_new, acc_new) in zip(tiles, new):
        m_ref[rows, :] = m_new
        l_ref[rows, :] = l_new
        acc_ref[rows, :] = acc_new


ATTN_COLS = 128


def _flash_tiles_t(n_cols, score_fn, m_ref, l_ref, acc_ref):
    tiles = [slice(c0, c0 + ATTN_COLS) for c0 in range(0, n_cols, ATTN_COLS)]
    prev = [(m_ref[:, c], l_ref[:, c], acc_ref[:, c]) for c in tiles]
    scored = [score_fn(cols) for cols in tiles]
    soft = []
    for t, (s, vt) in enumerate(scored):
        m_new = jnp.maximum(prev[t][0], jnp.max(s, axis=0, keepdims=True))
        p = jnp.exp(s - m_new)
        soft.append((m_new, jnp.sum(p, axis=0, keepdims=True), p.astype(BF16)))
    new = []
    for t, (m_new, l_new, p) in enumerate(soft):
        m_prev, l_prev, acc_prev = prev[t]
        alpha = jnp.exp(m_prev - m_new)
        pv = jnp.dot(scored[t][1], p, preferred_element_type=F32)
        new.append((m_new, alpha * l_prev + l_new, alpha * acc_prev + pv))
    for cols, (m_new, l_new, acc_new) in zip(tiles, new):
        m_ref[:, cols] = m_new
        l_ref[:, cols] = l_new
        acc_ref[:, cols] = acc_new


def _sigmoid(x):
    return 1.0 / (1.0 + jnp.exp(-x))


def _silu(x):
    return x * _sigmoid(x)


def _gelu_tanh(x):
    return 0.5 * x * (1.0 + jnp.tanh(math.sqrt(2.0 / math.pi) * (x + 0.044715 * (x * x * x))))


def _log_sigmoid(x):
    return jnp.minimum(x, 0.0) - jnp.log1p(jnp.exp(-jnp.abs(x)))


def _mod_kernel(c_ref, w_ref, b_ref, o_ref):
    ca = _silu(c_ref[...])
    o_ref[0] = jnp.dot(ca, w_ref[0], preferred_element_type=F32, precision=HIGHEST) + b_ref[0]


def _modulation(c, mod_w, mod_b):
    depth, d, n = mod_w.shape
    bsz = c.shape[0]
    rows = 8
    cp = jnp.zeros((rows, d), F32).at[:bsz].set(c)
    tn = 1024
    out = pl.pallas_call(
        _mod_kernel,
        grid=(depth, n // tn),
        in_specs=[pl.BlockSpec((rows, d), lambda l, j: (0, 0)),
                  pl.BlockSpec((1, d, tn), lambda l, j: (l, 0, j)),
                  pl.BlockSpec((1, 1, tn), lambda l, j: (l, 0, j))],
        out_specs=pl.BlockSpec((1, rows, tn), lambda l, j: (l, 0, j)),
        out_shape=jax.ShapeDtypeStruct((depth, rows, n), F32),
        compiler_params=_cparams("parallel", "parallel"),
        name="modulation",
    )(cp, mod_w, mod_b.reshape(depth, 1, n))
    return out[:, :bsz]


def _nm_matmul_kernel(h_ref, g_ref, sc_ref, sh_ref, w_ref, *rest, emit_hn):
    if emit_hn:
        o_ref, hn_ref, hn_scr = rest
    else:
        o_ref, hn_scr = rest

    @pl.when(pl.program_id(2) == 0)
    def _():
        x = h_ref[0]
        y = x * lax.rsqrt(jnp.mean(x * x, axis=-1, keepdims=True) + EPS) * g_ref[...]
        y = y * (1.0 + sc_ref[0]) + sh_ref[0]
        hn_scr[...] = y.astype(BF16)
        if emit_hn:
            hn_ref[0] = y

    o_ref[0] = jnp.dot(hn_scr[...], w_ref[...], preferred_element_type=F32).astype(o_ref.dtype)


def _nm_matmul(h, g, sc, sh, w, *, tm, tn, emit_hn=False, name="nm_matmul"):
    bsz, s, d = h.shape
    n = w.shape[1]
    tm = min(tm, s)
    out_shape = [jax.ShapeDtypeStruct((bsz, s, n), F32)]
    out_specs = [pl.BlockSpec((1, tm, tn), lambda b, i, j: (b, i, j))]
    if emit_hn:
        out_shape.append(jax.ShapeDtypeStruct((bsz, s, d), F32))
        out_specs.append(pl.BlockSpec((1, tm, d), lambda b, i, j: (b, i, 0)))
    res = pl.pallas_call(
        functools.partial(_nm_matmul_kernel, emit_hn=emit_hn),
        grid=(bsz, s // tm, n // tn),
        in_specs=[pl.BlockSpec((1, tm, d), lambda b, i, j: (b, i, 0)),
                  pl.BlockSpec((1, d), lambda b, i, j: (0, 0)),
                  pl.BlockSpec((1, 1, d), lambda b, i, j: (b, 0, 0)),
                  pl.BlockSpec((1, 1, d), lambda b, i, j: (b, 0, 0)),
                  pl.BlockSpec((d, tn), lambda b, i, j: (0, j))],
        out_specs=out_specs,
        out_shape=out_shape,
        scratch_shapes=[pltpu.VMEM((tm, d), BF16)],
        compiler_params=_cparams("parallel", "parallel", "arbitrary"),
        name=name,
    )(h, g.reshape(1, d), sc.reshape(bsz, 1, d), sh.reshape(bsz, 1, d), w)
    return res if emit_hn else res[0]


def _out_proj_kernel(xa_ref, xb_ref, wa_ref, wb_ref, h_ref, g_ref, o_ref):
    acc = jnp.dot(xa_ref[0], wa_ref[...], preferred_element_type=F32)
    acc += jnp.dot(xb_ref[0], wb_ref[...], preferred_element_type=F32)
    o_ref[0] = h_ref[0] + g_ref[0] * acc


def _out_proj(xa, xb, w, h, g1, *, tm=512, tn=1024):
    bsz, s, ka = xa.shape
    kb = xb.shape[2]
    n = w.shape[1]
    tm = min(tm, s)
    wa = w[:ka].astype(BF16)
    wb = w[ka:].astype(BF16)
    return pl.pallas_call(
        _out_proj_kernel,
        grid=(bsz, s // tm, n // tn),
        in_specs=[pl.BlockSpec((1, tm, ka), lambda b, i, j: (b, i, 0)),
                  pl.BlockSpec((1, tm, kb), lambda b, i, j: (b, i, 0)),
                  pl.BlockSpec((ka, tn), lambda b, i, j: (0, j)),
                  pl.BlockSpec((kb, tn), lambda b, i, j: (0, j)),
                  pl.BlockSpec((1, tm, tn), lambda b, i, j: (b, i, j)),
                  pl.BlockSpec((1, 1, tn), lambda b, i, j: (b, 0, j))],
        out_specs=pl.BlockSpec((1, tm, tn), lambda b, i, j: (b, i, j)),
        out_shape=jax.ShapeDtypeStruct((bsz, s, n), F32),
        compiler_params=_cparams("parallel", "parallel", "parallel"),
        name="out_proj",
    )(xa, xb, wa, wb, h, g1.reshape(bsz, 1, n))


def _mlstm_kernel(gb_ref, aqk_ref, v_ref, ao_ref, gcol_ref, cw_ref, cb_ref, hg_ref,
                  o_ref, xbuf, c_scr, n_scr, m_scr):
    L = MLSTM_CHUNK
    H, DQK, DV = MLSTM_HEADS, MLSTM_DQK, MLSTM_DV
    c = pl.program_id(1)

    @pl.when(c == 0)
    def _():
        xbuf[0:8, :] = jnp.zeros((8, 2 * A_QK), F32)
        c_scr[...] = jnp.zeros_like(c_scr)
        n_scr[...] = jnp.zeros_like(n_scr)
        m_scr[...] = jnp.zeros_like(m_scr)

    @pl.when(c > 0)
    def _():
        xbuf[0:8, :] = xbuf[L:L + 8, :]

    xbuf[8:L + 8, :] = aqk_ref[0]
    conv = cb_ref[...] + cw_ref[0:1, :] * xbuf[5:5 + L, :]
    for j in range(1, CONV_W):
        conv = conv + cw_ref[j:j + 1, :] * xbuf[5 + j:5 + j + L, :]
    qk = _silu(conv)

    r_i = lax.broadcasted_iota(I32, (L, L), 0)
    c_i = lax.broadcasted_iota(I32, (L, L), 1)
    tril = c_i <= r_i
    gcol = gcol_ref[0]
    grow = gcol.T

    for h in range(H):
        q = qk[:, h * DQK:(h + 1) * DQK] * (DQK ** -0.5)
        k = qk[:, A_QK + h * DQK:A_QK + (h + 1) * DQK]
        v = v_ref[0, :, h * DV:(h + 1) * DV]
        i_col = gcol[:, h:h + 1] + gb_ref[h]
        i_row = grow[h:h + 1, :] + gb_ref[h]
        lf_col = _log_sigmoid(gcol[:, H + h:H + h + 1] + gb_ref[H + h])
        lf_row = _log_sigmoid(grow[H + h:H + h + 1, :] + gb_ref[H + h])
        bcum_col = jnp.sum(jnp.where(tril, lf_row, 0.0), axis=1, keepdims=True)
        bcum_row = jnp.sum(jnp.where(r_i <= c_i, lf_col, 0.0), axis=0, keepdims=True)
        g_tot = jnp.sum(lf_row, axis=1, keepdims=True)
        m_prev = m_scr[h:h + 1, 0:1]
        d_intra = jnp.where(tril, bcum_col - bcum_row + i_row, -jnp.inf)
        d_inter = bcum_col + m_prev
        m_t = jnp.maximum(d_inter, jnp.max(d_intra, axis=1, keepdims=True))
        w_intra = jnp.exp(d_intra - m_t)
        w_inter = jnp.exp(d_inter - m_t)
        qb = q.astype(BF16)
        kb = k.astype(BF16)
        vb = v.astype(BF16)
        s = _dot_nt(qb, kb) * w_intra
        c_prev = c_scr[h]
        n_prev = n_scr[h:h + 1, :]
        num = jnp.dot(s.astype(BF16), vb, preferred_element_type=F32)
        num = num + w_inter * jnp.dot(qb, c_prev.astype(BF16), preferred_element_type=F32)
        den = jnp.sum(s, axis=1, keepdims=True) + w_inter * jnp.sum(q * n_prev, axis=1, keepdims=True)
        hh = num / jnp.maximum(jnp.abs(den), jnp.exp(-m_t))
        d_state = g_tot - bcum_col + i_col
        m_new = jnp.maximum(g_tot + m_prev, jnp.max(d_state, axis=0, keepdims=True))
        w_s = jnp.exp(d_state - m_new)
        w_c = jnp.exp(g_tot + m_prev - m_new)
        kw = k * w_s
        c_scr[h] = w_c * c_prev + jnp.dot(kw.T.astype(BF16), vb, preferred_element_type=F32)
        n_scr[h:h + 1, :] = w_c * n_prev + jnp.sum(kw, axis=0, keepdims=True)
        m_scr[h:h + 1, 0:1] = m_new
        y = hh * lax.rsqrt(jnp.mean(hh * hh, axis=-1, keepdims=True) + EPS) * hg_ref[:, h * DV:(h + 1) * DV]
        y = y * _sigmoid(ao_ref[0, :, h * DV:(h + 1) * DV])
        o_ref[0, :, h * DV:(h + 1) * DV] = y.astype(o_ref.dtype)


def _mlstm(proj, gate_b, conv_w, conv_b, head_g):
    bsz, s, _ = proj.shape
    L = MLSTM_CHUNK
    gate_blk = (2 * A_QK + 2 * A_V + 2 * B_QK + B_V) // LANES
    return pl.pallas_call(
        _mlstm_kernel,
        grid=(bsz, s // L),
        in_specs=[pl.BlockSpec(memory_space=pltpu.SMEM),
                  pl.BlockSpec((1, L, 2 * A_QK), lambda b, c: (b, c, 0)),
                  pl.BlockSpec((1, L, A_V), lambda b, c: (b, c, 1)),
                  pl.BlockSpec((1, L, A_V), lambda b, c: (b, c, 2)),
                  pl.BlockSpec((1, L, LANES), lambda b, c: (b, c, gate_blk)),
                  pl.BlockSpec((CONV_W, 2 * A_QK), lambda b, c: (0, 0)),
                  pl.BlockSpec((1, 2 * A_QK), lambda b, c: (0, 0)),
                  pl.BlockSpec((1, A_V), lambda b, c: (0, 0))],
        out_specs=pl.BlockSpec((1, L, A_V), lambda b, c: (b, c, 0)),
        out_shape=jax.ShapeDtypeStruct((bsz, s, A_V), BF16),
        scratch_shapes=[pltpu.VMEM((L + 8, 2 * A_QK), F32),
                        pltpu.VMEM((MLSTM_HEADS, MLSTM_DQK, MLSTM_DV), F32),
                        pltpu.VMEM((8, MLSTM_DQK), F32),
                        pltpu.VMEM((8, LANES), F32)],
        compiler_params=_cparams("parallel", "arbitrary"),
        name="mlstm",
    )(gate_b.reshape(-1), proj, proj, proj, proj, conv_w, conv_b.reshape(1, -1), head_g.reshape(1, -1))


def _rope_consts(dh):
    half = dh // 8
    lane = np.arange(LANES) % dh
    inv_half = ROPE_THETA ** (-jnp.arange(half, dtype=F32) / half)
    inv = jnp.where(jnp.asarray(lane < 2 * half), inv_half[lane % half], 0.0)
    lo = (lane < half).astype(np.float32)
    hi = ((lane >= half) & (lane < 2 * half)).astype(np.float32)
    return (inv.astype(F32).reshape(1, LANES), jnp.asarray(lo).reshape(1, LANES),
            jnp.asarray(hi).reshape(1, LANES))


def _rope_tables(pos_ref, inv_ref, lo_ref, hi_ref):
    ang = pos_ref[0].astype(F32) * inv_ref[...]
    cs = jnp.cos(ang)
    sn = jnp.sin(ang)
    return cs, -sn * lo_ref[...], sn * hi_ref[...]


def _rope_apply(x, cs, sa, sb, half):
    return x * cs + pltpu.roll(x, LANES - half, 1) * sa + pltpu.roll(x, half, 1) * sb


def _diff_prep_kernel(pos_ref, inv_ref, lo_ref, hi_ref, q_ref, k_ref, v_ref, qo_ref, ko_ref, vo_ref):
    cs, sa, sb = _rope_tables(pos_ref, inv_ref, lo_ref, hi_ref)
    half = DIFF_DQK // 8
    scale = DIFF_DQK ** -0.5
    for h in range(DIFF_HEADS):
        sl = slice(h * LANES, (h + 1) * LANES)
        qo_ref[0, sl, :] = (_rope_apply(q_ref[0, :, sl], cs, sa, sb, half) * scale).T.astype(BF16)
        ko_ref[0, :, sl] = _rope_apply(k_ref[0, :, sl], cs, sa, sb, half).astype(BF16)
        vo_ref[0, sl, :] = v_ref[0, :, sl].T.astype(BF16)


def _diff_prep(proj, pos3, ts=512):
    bsz, s, _ = proj.shape
    ts = min(ts, s)
    inv, lo, hi = _rope_consts(DIFF_DQK)
    base = (2 * A_QK + 2 * A_V) // B_QK
    cst = pl.BlockSpec((1, LANES), lambda b, i: (0, 0))
    osd = jax.ShapeDtypeStruct((bsz, s, B_QK), BF16)
    osd_t = jax.ShapeDtypeStruct((bsz, B_QK, s), BF16)
    spec_t = pl.BlockSpec((1, B_QK, ts), lambda b, i: (b, 0, i))
    return pl.pallas_call(
        _diff_prep_kernel,
        grid=(bsz, s // ts),
        in_specs=[pl.BlockSpec((1, ts, 1), lambda b, i: (b, i, 0)), cst, cst, cst,
                  pl.BlockSpec((1, ts, B_QK), lambda b, i: (b, i, base)),
                  pl.BlockSpec((1, ts, B_QK), lambda b, i: (b, i, base + 1)),
                  pl.BlockSpec((1, ts, B_V), lambda b, i: (b, i, base + 2))],
        out_specs=[spec_t, pl.BlockSpec((1, ts, B_QK), lambda b, i: (b, i, 0)), spec_t],
        out_shape=[osd_t, osd, osd_t],
        compiler_params=_cparams("parallel", "parallel"),
        name="diff_prep",
    )(pos3, inv, lo, hi, proj, proj, proj)


def _causal_pairs(nq):
    pairs = [(i, j) for i in range(nq) for j in range(i + 1)]
    return (jnp.asarray([p[0] for p in pairs], I32), jnp.asarray([p[1] for p in pairs], I32))


def _diff_attn_kernel(it_ref, jt_ref, lam_ref, q_ref, k_ref, v_ref, g_ref, o_ref, qs_scr, m_scr, l_scr, acc_scr,
                      *, tq, tk, lambda_init):
    i = it_ref[pl.program_id(2)]
    j = jt_ref[pl.program_id(2)]

    @pl.when(j == 0)
    def _():
        q = q_ref[0]
        feat = lax.broadcasted_iota(I32, q.shape, 0)
        zero = jnp.zeros_like(q)
        qs_scr[:, 0:tq] = jnp.where(feat < DIFF_DQK, q, zero)
        qs_scr[:, tq:2 * tq] = jnp.where(feat >= DIFF_DQK, q, zero)
        m_scr[...] = jnp.full_like(m_scr, -jnp.inf)
        l_scr[...] = jnp.zeros_like(l_scr)
        acc_scr[...] = jnp.zeros_like(acc_scr)

    def step(masked):
        def scores(cols):
            s = jnp.dot(k_ref[0], qs_scr[:, cols], preferred_element_type=F32)
            if masked:
                key = lax.broadcasted_iota(I32, (tk, ATTN_COLS), 0)
                qry = lax.broadcasted_iota(I32, (tk, ATTN_COLS), 1) + (cols.start % tq)
                s = jnp.where(key <= qry, s, -jnp.inf)
            return s, v_ref[0]
        _flash_tiles_t(2 * tq, scores, m_scr, l_scr, acc_scr)

    @pl.when(j < i)
    def _():
        step(False)

    @pl.when(j == i)
    def _():
        step(True)
        lv = lam_ref[...]
        lam = (jnp.exp(jnp.sum(lv[0:1] * lv[1:2], axis=1, keepdims=True))
               - jnp.exp(jnp.sum(lv[2:3] * lv[3:4], axis=1, keepdims=True)) + lambda_init)
        o0 = acc_scr[:, 0:tq] / l_scr[:, 0:tq]
        o1 = acc_scr[:, tq:2 * tq] / l_scr[:, tq:2 * tq]
        a = o0 - lam * o1
        y = a * lax.rsqrt(jnp.mean(a * a, axis=0, keepdims=True) + EPS) * g_ref[0]
        o_ref[0] = (y * (1.0 - lambda_init)).T.astype(o_ref.dtype)


def _diff_attn(qt, k, vt, lam_vec, head_g, lambda_init, *, tq=512):
    bsz, s, _ = k.shape
    tq = min(tq, s)
    it, jt = _causal_pairs(s // tq)
    grid_spec = pltpu.PrefetchScalarGridSpec(
        num_scalar_prefetch=2,
        grid=(bsz, DIFF_HEADS, it.shape[0]),
        in_specs=[pl.BlockSpec((4, DIFF_DQK), lambda b, h, t, it, jt: (0, 0)),
                  pl.BlockSpec((1, LANES, tq), lambda b, h, t, it, jt: (b, h, it[t])),
                  pl.BlockSpec((1, tq, LANES), lambda b, h, t, it, jt: (b, jt[t], h)),
                  pl.BlockSpec((1, DIFF_DV, tq), lambda b, h, t, it, jt: (b, h, jt[t])),
                  pl.BlockSpec((1, DIFF_DV, 1), lambda b, h, t, it, jt: (h, 0, 0))],
        out_specs=pl.BlockSpec((1, tq, DIFF_DV), lambda b, h, t, it, jt: (b, it[t], h)),
        scratch_shapes=[pltpu.VMEM((LANES, 2 * tq), BF16),
                        pltpu.VMEM((1, 2 * tq), F32),
                        pltpu.VMEM((1, 2 * tq), F32),
                        pltpu.VMEM((DIFF_DV, 2 * tq), F32)])
    return pl.pallas_call(
        functools.partial(_diff_attn_kernel, tq=tq, tk=tq, lambda_init=lambda_init),
        grid_spec=grid_spec,
        out_shape=jax.ShapeDtypeStruct((bsz, s, B_V), BF16),
        compiler_params=_cparams("parallel", "parallel", "arbitrary"),
        name="diff_attn",
    )(it, jt, lam_vec, qt, k, vt, head_g.reshape(DIFF_HEADS, DIFF_DV, 1))


def _moba_prep_kernel(pos_ref, inv_ref, lo_ref, hi_ref, q_ref, k_ref, v_ref,
                      qo_ref, ko_ref, vo_ref, sel_ref, kmt_scr, qf_scr, *, nb):
    i = pl.program_id(1)
    tb = MOBA_BLOCK
    half = MOBA_DH // 8
    scale = MOBA_DH ** -0.5

    @pl.when(i == 0)
    def _():
        kmt_scr[...] = jnp.zeros_like(kmt_scr)

    cs, sa, sb = _rope_tables(pos_ref, inv_ref, lo_ref, hi_ref)
    kmeans = []
    for h in range(MOBA_HEADS):
        sl = slice(h * LANES, (h + 1) * LANES)
        qr = _rope_apply(q_ref[0, :, sl], cs, sa, sb, half)
        kr = _rope_apply(k_ref[0, :, sl], cs, sa, sb, half)
        qf_scr[:, sl] = qr
        qo_ref[0, sl, :] = (qr * scale).T.astype(BF16)
        ko_ref[0, :, sl] = kr.astype(BF16)
        vo_ref[0, sl, :] = v_ref[0, :, sl].T.astype(BF16)
        kmeans.append(jnp.mean(kr, axis=0, keepdims=True))

    gate = _dot_nt(qf_scr[...], kmt_scr[...], precision=HIGHEST)
    lane = lax.broadcasted_iota(I32, (tb, LANES), 1)
    n = lane % nb
    valid = n < i
    gm = jnp.where(valid, gate, -jnp.inf)
    rank = jnp.zeros((tb, LANES), I32)
    for d in range(1, nb):
        wraps = (n + d) >= nb
        partner = jnp.where(wraps, pltpu.roll(gm, nb - d, 1), pltpu.roll(gm, LANES - d, 1))
        pn = jnp.where(wraps, n + d - nb, n + d)
        beats = (partner > gm) | ((partner == gm) & (pn < n))
        rank = rank + beats.astype(I32)
    sel_ref[0] = jnp.where(valid & (rank < MOBA_TOPK), 1.0, 0.0).astype(F32).T

    for h in range(MOBA_HEADS):
        _store_row(kmt_scr, h * nb + i, kmeans[h], slice(h * LANES, (h + 1) * LANES))


def _moba_prep(proj, pos3):
    bsz, s, _ = proj.shape
    tb = MOBA_BLOCK
    nb = s // tb
    assert s % tb == 0 and MOBA_HEADS * nb <= LANES and nb >= MOBA_TOPK
    inv, lo, hi = _rope_consts(MOBA_DH)
    cst = pl.BlockSpec((1, LANES), lambda b, i: (0, 0))
    osd = jax.ShapeDtypeStruct((bsz, s, C_W), BF16)
    osd_t = jax.ShapeDtypeStruct((bsz, C_W, s), BF16)
    spec_t = pl.BlockSpec((1, C_W, tb), lambda b, i: (b, 0, i))
    return pl.pallas_call(
        functools.partial(_moba_prep_kernel, nb=nb),
        grid=(bsz, nb),
        in_specs=[pl.BlockSpec((1, tb, 1), lambda b, i: (b, i, 0)), cst, cst, cst,
                  pl.BlockSpec((1, tb, C_W), lambda b, i: (b, i, 0)),
                  pl.BlockSpec((1, tb, C_W), lambda b, i: (b, i, 1)),
                  pl.BlockSpec((1, tb, C_W), lambda b, i: (b, i, 2))],
        out_specs=[spec_t, pl.BlockSpec((1, tb, C_W), lambda b, i: (b, i, 0)), spec_t,
                   pl.BlockSpec((1, LANES, tb), lambda b, i: (b, 0, i))],
        out_shape=[osd_t, osd, osd_t, jax.ShapeDtypeStruct((bsz, LANES, s), F32)],
        scratch_shapes=[pltpu.VMEM((LANES, C_W), F32), pltpu.VMEM((tb, C_W), F32)],
        compiler_params=_cparams("parallel", "arbitrary"),
        name="moba_prep",
    )(pos3, inv, lo, hi, proj, proj, proj)


def _moba_attn_kernel(it_ref, jt_ref, q_ref, k_ref, v_ref, sel_ref, o_ref, m_scr, l_scr, acc_scr, *, nb):
    h = pl.program_id(1)
    i = it_ref[pl.program_id(2)]
    j = jt_ref[pl.program_id(2)]
    tb = MOBA_BLOCK
    tq = 2 * tb

    def pick_bias(cols, blk):
        row = h * nb + blk
        base = pl.multiple_of((row // SUBLANES) * SUBLANES, SUBLANES)
        grp = sel_ref[0, pl.ds(base, SUBLANES), cols]
        sub = lax.broadcasted_iota(I32, grp.shape, 0)
        hit = jnp.max(jnp.where(sub == row - base, grp, 0.0), axis=0, keepdims=True)
        return jnp.where(hit > 0.5, 0.0, -jnp.inf)

    @pl.when(j == 0)
    def _():
        m_scr[...] = jnp.full_like(m_scr, -jnp.inf)
        l_scr[...] = jnp.zeros_like(l_scr)
        acc_scr[...] = jnp.zeros_like(acc_scr)

        def scores(cols):
            c0 = cols.start
            if c0 < tb:
                key = lax.broadcasted_iota(I32, (tb, ATTN_COLS), 0)
                qry = lax.broadcasted_iota(I32, (tb, ATTN_COLS), 1) + c0
                s = jnp.dot(k_ref[0, 0:tb, :], q_ref[0, :, cols], preferred_element_type=F32)
                return jnp.where(key <= qry, s, -jnp.inf), v_ref[0, :, 0:tb]
            key = lax.broadcasted_iota(I32, (tq, ATTN_COLS), 0)
            qry = lax.broadcasted_iota(I32, (tq, ATTN_COLS), 1) + c0
            s = jnp.dot(k_ref[0], q_ref[0, :, cols], preferred_element_type=F32)
            s = jnp.where(key < tb, s + pick_bias(cols, 2 * i), jnp.where(key <= qry, s, -jnp.inf))
            return s, v_ref[0]
        _flash_tiles_t(tq, scores, m_scr, l_scr, acc_scr)

    @pl.when((j > 0) & (j <= i))
    def _():
        kb = 2 * (i - j)
        key = lax.broadcasted_iota(I32, (tq, ATTN_COLS), 0)

        def scores(cols):
            bias = jnp.where(key < tb, pick_bias(cols, kb), pick_bias(cols, kb + 1))
            return jnp.dot(k_ref[0], q_ref[0, :, cols], preferred_element_type=F32) + bias, v_ref[0]
        _flash_tiles_t(tq, scores, m_scr, l_scr, acc_scr)

    @pl.when(j == i)
    def _():
        o_ref[0] = (acc_scr[...] / l_scr[...]).T.astype(o_ref.dtype)


def _moba_attn(qt, k, vt, sel_t):
    bsz, s, _ = k.shape
    tq = 2 * MOBA_BLOCK
    nb = s // MOBA_BLOCK
    assert s % tq == 0
    it, jt = _causal_pairs(s // tq)
    grid_spec = pltpu.PrefetchScalarGridSpec(
        num_scalar_prefetch=2,
        grid=(bsz, MOBA_HEADS, it.shape[0]),
        in_specs=[pl.BlockSpec((1, MOBA_DH, tq), lambda b, h, t, it, jt: (b, h, it[t])),
                  pl.BlockSpec((1, tq, MOBA_DH), lambda b, h, t, it, jt: (b, it[t] - jt[t], h)),
                  pl.BlockSpec((1, MOBA_DH, tq), lambda b, h, t, it, jt: (b, h, it[t] - jt[t])),
                  pl.BlockSpec((1, LANES, tq), lambda b, h, t, it, jt: (b, 0, it[t]))],
        out_specs=pl.BlockSpec((1, tq, MOBA_DH), lambda b, h, t, it, jt: (b, it[t], h)),
        scratch_shapes=[pltpu.VMEM((1, tq), F32), pltpu.VMEM((1, tq), F32), pltpu.VMEM((MOBA_DH, tq), F32)])
    return pl.pallas_call(
        functools.partial(_moba_attn_kernel, nb=nb),
        grid_spec=grid_spec,
        out_shape=jax.ShapeDtypeStruct((bsz, s, C_W), BF16),
        compiler_params=_cparams("parallel", "parallel", "arbitrary"),
        name="moba_attn",
    )(it, jt, qt, k, vt, sel_t)


def _sgu_kernel(u_ref, v_ref, lng_ref, lnb_ref, w_ref, bt_ref, o_ref, *, ts):
    L = SGU_CHUNK
    r_i = lax.broadcasted_iota(I32, (L, L), 0)
    c_i = lax.broadcasted_iota(I32, (L, L), 1)
    tril = c_i <= r_i
    for g in range(SGU_GROUPS):
        sl = slice(g * SGU_CH, (g + 1) * SGU_CH)
        w = jnp.where(tril, w_ref[g], 0.0).astype(BF16)
        bias = bt_ref[:, g:g + 1]
        for c in range(ts // L):
            rows = slice(c * L, (c + 1) * L)
            vv = _gelu_tanh(v_ref[0, rows, sl])
            mu = jnp.mean(vv, axis=-1, keepdims=True)
            dv = vv - mu
            var = jnp.mean(dv * dv, axis=-1, keepdims=True)
            vn = dv * lax.rsqrt(var + EPS) * lng_ref[:, sl] + lnb_ref[:, sl]
            mixed = jnp.dot(w, vn.astype(BF16), preferred_element_type=F32) + bias
            o_ref[0, rows, sl] = (_gelu_tanh(u_ref[0, rows, sl]) * mixed).astype(o_ref.dtype)


def _sgu(proj, ln_g, ln_b, w_s, b_s, ts=256):
    bsz, s, _ = proj.shape
    ts = min(ts, s)
    base = 3 * C_W // D_W
    return pl.pallas_call(
        functools.partial(_sgu_kernel, ts=ts),
        grid=(bsz, s // ts),
        in_specs=[pl.BlockSpec((1, ts, D_W), lambda b, i: (b, i, base)),
                  pl.BlockSpec((1, ts, D_W), lambda b, i: (b, i, base + 1)),
                  pl.BlockSpec((1, D_W), lambda b, i: (0, 0)),
                  pl.BlockSpec((1, D_W), lambda b, i: (0, 0)),
                  pl.BlockSpec((SGU_GROUPS, SGU_CHUNK, SGU_CHUNK), lambda b, i: (0, 0, 0)),
                  pl.BlockSpec((SGU_CHUNK, SGU_GROUPS), lambda b, i: (0, 0))],
        out_specs=pl.BlockSpec((1, ts, D_W), lambda b, i: (b, i, 0)),
        out_shape=jax.ShapeDtypeStruct((bsz, s, D_W), BF16),
        compiler_params=_cparams("parallel", "parallel"),
        name="sgu",
    )(proj, proj, ln_g.reshape(1, -1), ln_b.reshape(1, -1), w_s, b_s.T)


def _topk_rows(s_scr, k, val_scr, idx_scr, payload_scr=None):
    n, t = s_scr.shape
    row = lax.broadcasted_iota(I32, (n, t), 0).astype(F32)

    def body(it, carry):
        s = s_scr[...]
        m = jnp.max(s, axis=0, keepdims=True)
        pos = jnp.min(jnp.where(s == m, row, float(n)), axis=0, keepdims=True)
        hit = row == pos
        _store_row(val_scr, it, m)
        if payload_scr is None:
            _store_row(idx_scr, it, pos)
        else:
            _store_row(idx_scr, it, jnp.max(jnp.where(hit, payload_scr[...], -1.0), axis=0, keepdims=True))
        s_scr[...] = jnp.where(hit, -jnp.inf, s)
        return carry

    val_scr[...] = jnp.zeros_like(val_scr)
    idx_scr[...] = jnp.zeros_like(idx_scr)
    lax.fori_loop(0, k, body, 0)


def _peer_topk_kernel(q_ref, keys_ref, idx_ref, g_ref, s_scr, v1, i1, v2, i2, cand, cidx, vt, it):
    K = PEER_TOPK
    half = PEER_DQ // 2
    for c, (vs, ix) in enumerate(((v1, i1), (v2, i2))):
        s_scr[...] = _dot_nt(keys_ref[0, c], q_ref[:, c * half:(c + 1) * half], precision=HIGHEST)
        _topk_rows(s_scr, K, vs, ix)
    cand[...] = jnp.full(cand.shape, -jnp.inf, F32)
    cidx[...] = jnp.zeros(cidx.shape, F32)
    off = 0
    for a in range(K):
        nb = K // (a + 1)
        cand[off:off + nb, :] = v1[a:a + 1, :] + v2[0:nb, :]
        cidx[off:off + nb, :] = i1[a:a + 1, :] * float(PEER_NKEYS) + i2[0:nb, :]
        off += nb
    _topk_rows(cand, K, vt, it, payload_scr=cidx)
    ts = vt[...]
    e = jnp.exp(ts - jnp.max(ts, axis=0, keepdims=True))
    g_ref[0] = e / jnp.sum(e, axis=0, keepdims=True)
    idx_ref[0] = it[...].astype(I32)


def _peer_topk(q2, sub_keys, tt=512):
    n_tok = q2.shape[0]
    tt = min(tt, n_tok)
    K = PEER_TOPK
    half = PEER_DQ // 2
    vs = lambda r, dt: pltpu.VMEM((r, tt), dt)
    n_cand = sum(K // (a + 1) for a in range(K))
    n_cand = -(-n_cand // SUBLANES) * SUBLANES
    return pl.pallas_call(
        _peer_topk_kernel,
        grid=(n_tok // tt, PEER_HEADS),
        in_specs=[pl.BlockSpec((tt, PEER_DQ), lambda i, h: (i, h)),
                  pl.BlockSpec((1, 2, PEER_NKEYS, half), lambda i, h: (h, 0, 0, 0))],
        out_specs=[pl.BlockSpec((1, K, tt), lambda i, h: (h, 0, i)),
                   pl.BlockSpec((1, K, tt), lambda i, h: (h, 0, i))],
        out_shape=[jax.ShapeDtypeStruct((PEER_HEADS, K, n_tok), I32),
                   jax.ShapeDtypeStruct((PEER_HEADS, K, n_tok), F32)],
        scratch_shapes=[vs(PEER_NKEYS, F32), vs(K, F32), vs(K, F32), vs(K, F32), vs(K, F32),
                        vs(n_cand, F32), vs(n_cand, F32), vs(K, F32), vs(K, F32)],
        compiler_params=_cparams("parallel", "parallel"),
        name="peer_topk",
    )(q2, sub_keys)


def _pack_kernel(u_ref, v_ref, o_ref):
    ub = lax.bitcast_convert_type(u_ref[...].astype(BF16).astype(F32), I32)
    vb = lax.bitcast_convert_type(v_ref[...].astype(BF16).astype(F32), I32)
    o_ref[...] = (ub & jnp.int32(-65536)) | lax.shift_right_logical(vb, jnp.int32(16))


def _pack_tables(u_all, v_all, layer, tm=512):
    _, n, d = u_all.shape
    spec = pl.BlockSpec((None, tm, d), lambda i: (layer, i, 0))
    packed = pl.pallas_call(
        _pack_kernel,
        grid=(n // tm,),
        in_specs=[spec, spec],
        out_specs=pl.BlockSpec((tm, d), lambda i: (i, 0)),
        out_shape=jax.ShapeDtypeStruct((n, d), I32),
        compiler_params=_cparams("parallel"),
        name="peer_pack",
    )(u_all, v_all)
    return packed.reshape(n, d // LANES, LANES)


PEER_SLOTS = SUBLANES
PEER_AHEAD = 6
PEER_DMA_QUEUES = 2


def _fold_order():
    tiles = [[e] * SUBLANES for e in range(SUBLANES)]
    f = SUBLANES
    while len(tiles) > 1:
        tiles = [[x[s] if (s % f) < f // 2 else y[s] for s in range(SUBLANES)]
                 for x, y in zip(tiles[0::2], tiles[1::2])]
        f //= 2
    return tiles[0]


def _peer_gather_kernel(idx_ref, gate_ref, x_ref, h_ref, g2_ref, tab_hbm, o_ref, *rest, tt):
    R = PEER_SEL
    bufs, (stage, wscr, sem) = rest[:PEER_SLOTS], rest[PEER_SLOTS:]
    step = pl.program_id(0)
    d = x_ref.shape[1]
    nch = d // LANES
    nv = nch // SUBLANES
    ngrp = R // SUBLANES
    per_grp = R // ngrp
    order = _fold_order()

    def row_copy(t, j, slot):
        return pltpu.make_async_copy(tab_hbm.at[idx_ref[t, j]], bufs[slot].at[pl.ds(j * nch, nch)], sem.at[slot])

    def wait(slot):
        pltpu.make_async_copy(bufs[(slot + 1) % PEER_SLOTS], bufs[slot], sem.at[slot]).wait()

    def tile(slot, j, v):
        return bufs[slot][(j * nv + v) * SUBLANES:(j * nv + v + 1) * SUBLANES, :]

    lane = lax.broadcasted_iota(I32, (SUBLANES, LANES), 1)
    sub = lax.broadcasted_iota(I32, (SUBLANES, LANES), 0)
    hi_mask = jnp.int32(-65536)
    fold_masks = {f: (sub % f) < f // 2 for f in (8, 4, 2)}
    order_vec = jnp.zeros((SUBLANES, LANES), I32)
    for s_, e_ in enumerate(order):
        order_vec = jnp.where(sub == s_, e_, order_vec)

    def fold(x, y, f):
        m = fold_masks[f]
        return (jnp.where(m, x, pltpu.roll(y, f // 2, 0))
                + jnp.where(m, pltpu.roll(x, SUBLANES - f // 2, 0), y))

    @pl.when(step == 0)
    def _():
        for t0 in range(PEER_AHEAD):
            def prime(j, carry, t0=t0):
                row_copy(t0, j, t0).start()
                return carry
            lax.fori_loop(0, R, prime, 0)

    def group(gi, carry):
        base = pl.multiple_of(gi * SUBLANES, SUBLANES)
        xg = x_ref[pl.ds(base, SUBLANES), :]
        gg = gate_ref[pl.ds(base, SUBLANES), :]
        for r in range(SUBLANES):
            slot = r
            nxt = base + r + PEER_AHEAD
            nslot = (r + PEER_AHEAD) % PEER_SLOTS
            wait(slot)
            xs = []
            for v in range(nv):
                xv = jnp.zeros((SUBLANES, LANES), F32)
                for s_ in range(SUBLANES):
                    c = v * SUBLANES + s_
                    xv = jnp.where(sub == s_, jnp.broadcast_to(xg[r:r + 1, c * LANES:(c + 1) * LANES],
                                                                (SUBLANES, LANES)), xv)
                xs.append(xv)
            grow = jnp.broadcast_to(gg[r:r + 1, :], (SUBLANES, LANES))
            for g in range(ngrp):
                for j in range(g * per_grp, (g + 1) * per_grp):
                    row_copy(nxt, j, nslot).start(priority=j % PEER_DMA_QUEUES)
                parts = []
                for e_ in range(SUBLANES):
                    j = g * SUBLANES + e_
                    p = lax.bitcast_convert_type(tile(slot, j, 0) & hi_mask, F32) * xs[0]
                    for v in range(1, nv):
                        p = p + lax.bitcast_convert_type(tile(slot, j, v) & hi_mask, F32) * xs[v]
                    parts.append(p)
                f = SUBLANES
                while len(parts) > 1:
                    parts = [fold(x, y, f) for x, y in zip(parts[0::2], parts[1::2])]
                    f //= 2
                act = jnp.sum(parts[0], axis=1, keepdims=True)
                gcol = jnp.sum(jnp.where(lane == g * SUBLANES + order_vec, grow, 0.0), axis=1, keepdims=True)
                wscr[g * SUBLANES:(g + 1) * SUBLANES, :] = jnp.broadcast_to(gcol * _gelu_tanh(act),
                                                                             (SUBLANES, LANES))
            n_acc = 4
            accs = [[None] * nv for _ in range(n_acc)]
            for j in range(R):
                g, e_ = divmod(j, SUBLANES)
                row = g * SUBLANES + order.index(e_)
                wj = wscr[row:row + 1, :]
                for v in range(nv):
                    term = lax.bitcast_convert_type(tile(slot, j, v) << 16, F32) * wj
                    k = j % n_acc
                    accs[k][v] = term if accs[k][v] is None else accs[k][v] + term
            for v in range(nv):
                out_v = (accs[0][v] + accs[1][v]) + (accs[2][v] + accs[3][v])
                for s_ in range(SUBLANES):
                    c = v * SUBLANES + s_
                    stage[r:r + 1, c * LANES:(c + 1) * LANES] = out_v[s_:s_ + 1, :]
        rows = pl.ds(base, SUBLANES)
        o_ref[rows, :] = h_ref[rows, :] + g2_ref[0] * stage[...]
        return carry

    lax.fori_loop(0, tt // SUBLANES, group, 0)

    @pl.when(step == pl.num_programs(0) - 1)
    def _():
        for k in range(PEER_AHEAD):
            wait((tt + k) % PEER_SLOTS)


def _peer_gather(idx, gates, hn2, h2, g2, table, s, tt=128):
    n_tok, d = hn2.shape
    tt = min(tt, s)
    assert s % tt == 0 and tt % PEER_SLOTS == 0 and d % (SUBLANES * LANES) == 0 and PEER_SEL % SUBLANES == 0
    assert PEER_SEL <= LANES and PEER_AHEAD < PEER_SLOTS - 1
    per_b = s // tt
    R = PEER_SEL
    nsteps = n_tok // tt
    idx3 = idx.reshape(nsteps, tt, R)
    head = jnp.concatenate([idx3[1:, :SUBLANES], jnp.zeros((1, SUBLANES, R), I32)], axis=0)
    idx = jnp.concatenate([idx3, head], axis=1).reshape(nsteps * (tt + SUBLANES), R)
    return pl.pallas_call(
        functools.partial(_peer_gather_kernel, tt=tt),
        grid=(nsteps,),
        in_specs=[pl.BlockSpec((tt + SUBLANES, R), lambda i: (i, 0), memory_space=pltpu.SMEM),
                  pl.BlockSpec((tt, R), lambda i: (i, 0)),
                  pl.BlockSpec((tt, d), lambda i: (i, 0)),
                  pl.BlockSpec((tt, d), lambda i: (i, 0)),
                  pl.BlockSpec((1, 1, d), lambda i: (i // per_b, 0, 0)),
                  pl.BlockSpec(memory_space=pl.ANY)],
        out_specs=pl.BlockSpec((tt, d), lambda i: (i, 0)),
        out_shape=jax.ShapeDtypeStruct((n_tok, d), F32),
        scratch_shapes=[pltpu.VMEM((R * d // LANES, LANES), I32)] * PEER_SLOTS
        + [pltpu.VMEM((SUBLANES, d), F32), pltpu.VMEM((R, LANES), F32), pltpu.SemaphoreType.DMA((PEER_SLOTS,))],
        compiler_params=_cparams("arbitrary"),
        name="peer_gather",
    )(idx, gates, hn2, h2, g2.reshape(-1, 1, d), table)


def _peer_layer(h, g, sc, sh, g2, w_q, sub_keys, u_all, v_all, layer):
    bsz, s, d = h.shape
    q, hn = _nm_matmul(h, g, sc, sh, w_q.astype(BF16), tm=512, tn=1024, emit_hn=True, name="peer_query")
    n_tok = bsz * s
    idx, gates = _peer_topk(q.reshape(n_tok, -1), sub_keys)
    idx = idx.transpose(2, 0, 1).reshape(n_tok, PEER_SEL)
    gates = gates.transpose(2, 0, 1).reshape(n_tok, PEER_SEL)
    table = _pack_tables(u_all, v_all, layer)
    out = _peer_gather(idx, gates, hn.reshape(n_tok, d), h.reshape(n_tok, d), g2, table, s)
    return out.reshape(bsz, s, d)


def _rmsnorm_kernel(x_ref, g_ref, o_ref):
    x = x_ref[...]
    o_ref[...] = x * lax.rsqrt(jnp.mean(x * x, axis=-1, keepdims=True) + EPS) * g_ref[...]


def _final_norm(h, g, tm=512):
    bsz, s, d = h.shape
    n = bsz * s
    tm = min(tm, n)
    out = pl.pallas_call(
        _rmsnorm_kernel,
        grid=(n // tm,),
        in_specs=[pl.BlockSpec((tm, d), lambda i: (i, 0)), pl.BlockSpec((1, d), lambda i: (0, 0))],
        out_specs=pl.BlockSpec((tm, d), lambda i: (i, 0)),
        out_shape=jax.ShapeDtypeStruct((n, d), F32),
        compiler_params=_cparams("parallel"),
        name="final_norm",
    )(h.reshape(n, d), g.reshape(1, d))
    return out.reshape(bsz, s, d)


def _even_mixer(h, pos3, g, sc, sh, g1, layer, w_in, w_out, conv_w, conv_b, gate_b, mlstm_g, lam_vec, diff_g):
    lambda_init = 0.8 - 0.6 * math.exp(-0.3 * layer)
    d = h.shape[-1]
    cuts = np.cumsum([2 * A_QK, A_V, A_V, 2 * MLSTM_HEADS, B_QK, B_QK, B_V])
    n_gate = 2 * MLSTM_HEADS
    w_cat = jnp.concatenate([w_in[:, :cuts[2]], w_in[:, cuts[3]:], w_in[:, cuts[2]:cuts[3]],
                             jnp.zeros((d, LANES - n_gate), w_in.dtype)], axis=1).astype(BF16)
    proj = _nm_matmul(h, g, sc, sh, w_cat, tm=1024, tn=896, name="even_in_proj")
    h_a = _mlstm(proj, gate_b, conv_w, conv_b, mlstm_g)
    bq, bk, bv = _diff_prep(proj, pos3)
    h_b = _diff_attn(bq, bk, bv, lam_vec, diff_g, lambda_init)
    return _out_proj(h_a, h_b, w_out, h, g1)


def _odd_mixer(h, pos3, g, sc, sh, g1, w_in, w_out, ln_g, ln_b, sgu_w, sgu_b):
    proj = _nm_matmul(h, g, sc, sh, w_in.astype(BF16), tm=1024, tn=1024, name="odd_in_proj")
    cq, ck, cv, sel = _moba_prep(proj, pos3)
    h_c = _moba_attn(cq, ck, cv, sel)
    h_d = _sgu(proj, ln_g, ln_b, sgu_w, sgu_b)
    return _out_proj(h_c, h_d, w_out, h, g1)


def kernel(x, c, positions, mod_w, mod_b, norm_g, final_g, even_w_in, even_w_out, mlstm_conv_w, mlstm_conv_b, mlstm_gate_b, mlstm_head_g, diff_lambda, diff_head_g, odd_w_in, odd_w_out, sgu_ln_g, sgu_ln_b, sgu_w, sgu_b, peer_w_q, peer_sub_keys, peer_u, peer_v):
    depth = mod_w.shape[0]
    bsz, s, d = x.shape
    mod = _modulation(c, mod_w, mod_b)
    pos3 = positions.reshape(bsz, s, 1)
    h = x
    for layer in range(depth):
        sh1, sc1, g1, sh2, sc2, g2 = [mod[layer, :, i * d:(i + 1) * d] for i in range(6)]
        if layer % 2 == 0:
            e = layer // 2
            h = _even_mixer(h, pos3, norm_g[layer, 0], sc1, sh1, g1, layer, even_w_in[e], even_w_out[e],
                            mlstm_conv_w[e], mlstm_conv_b[e], mlstm_gate_b[e], mlstm_head_g[e],
                            diff_lambda[e], diff_head_g[e])
        else:
            o = layer // 2
            h = _odd_mixer(h, pos3, norm_g[layer, 0], sc1, sh1, g1, odd_w_in[o], odd_w_out[o],
                           sgu_ln_g[o], sgu_ln_b[o], sgu_w[o], sgu_b[o])
        h = _peer_layer(h, norm_g[layer, 1], sc2, sh2, g2, peer_w_q[layer], peer_sub_keys[layer],
                        peer_u, peer_v, layer)
    return _final_norm(h, final_g)
```

```python
import functools
import math

import numpy as np
import jax
import jax.numpy as jnp
from jax import lax
from jax.experimental import pallas as pl
from jax.experimental.pallas import tpu as pltpu

F32 = jnp.float32
BF16 = jnp.bfloat16
I32 = jnp.int32
HIGHEST = lax.Precision.HIGHEST

EPS = 1e-6
ROPE_THETA = 500000.0
LANES = 128
SUBLANES = 8
VMEM_LIMIT_BYTES = 48 * 1024 * 1024

MLSTM_HEADS, MLSTM_DQK, MLSTM_DV, MLSTM_CHUNK, CONV_W = 4, 128, 256, 128, 4
DIFF_HEADS, DIFF_DQK, DIFF_DV = 8, 64, 128
MOBA_HEADS, MOBA_DH, MOBA_BLOCK, MOBA_TOPK = 8, 128, 256, 3
SGU_GROUPS, SGU_CH, SGU_CHUNK = 8, 128, 128
PEER_HEADS, PEER_NKEYS, PEER_DQ, PEER_TOPK = 8, 128, 256, 16
PEER_SEL = PEER_HEADS * PEER_TOPK

A_QK = MLSTM_HEADS * MLSTM_DQK
A_V = MLSTM_HEADS * MLSTM_DV
B_QK = DIFF_HEADS * 2 * DIFF_DQK
B_V = DIFF_HEADS * DIFF_DV
C_W = MOBA_HEADS * MOBA_DH
D_W = SGU_GROUPS * SGU_CH


def _cparams(*sem):
    return pltpu.CompilerParams(dimension_semantics=sem, vmem_limit_bytes=VMEM_LIMIT_BYTES)


def _dot_nt(a, b, precision=None):
    return lax.dot_general(a, b, (((1,), (1,)), ((), ())), preferred_element_type=F32, precision=precision)


def _store_row(ref, row, val, cols=slice(None)):
    base = pl.multiple_of((row // SUBLANES) * SUBLANES, SUBLANES)
    blk = ref[pl.ds(base, SUBLANES), cols]
    r = lax.broadcasted_iota(I32, blk.shape, 0)
    ref[pl.ds(base, SUBLANES), cols] = jnp.where(r == row - base, val, blk)


ATTN_ROWS = 128


def _flash_tiles(n_rows, score_fn, m_ref, l_ref, acc_ref, first=False):
    tiles = [slice(r0, r0 + ATTN_ROWS) for r0 in range(0, n_rows, ATTN_ROWS)]
    prev = None if first else [(m_ref[r, :], l_ref[r, :], acc_ref[r, :]) for r in tiles]
    scored = [score_fn(rows) for rows in tiles]
    soft = []
    for t, (s, v) in enumerate(scored):
        m_new = jnp.max(s, axis=1, keepdims=True)
        if not first:
            m_new = jnp.maximum(prev[t][0], m_new)
        p = jnp.exp(s - m_new)
        soft.append((m_new, jnp.sum(p, axis=1, keepdims=True), p.astype(BF16)))
    new = []
    for t, (m_new, l_new, p) in enumerate(soft):
        acc_new = jnp.dot(p, scored[t][1], preferred_element_type=F32)
        if not first:
            m_prev, l_prev, acc_prev = prev[t]
            alpha = jnp.exp(m_prev - m_new)
            l_new = alpha * l_prev + l_new
            acc_new = alpha * acc_prev + acc_new
        new.append((m_new, l_new, acc_new))
    for rows, (m_new, l_new, acc_new) in zip(tiles, new):
        m_ref[rows, :] = m_new
        l_ref[rows, :] = l_new
        acc_ref[rows, :] = acc_new


ATTN_COLS = 128


def _flash_tiles_t(n_cols, score_fn, m_ref, l_ref, acc_ref):
    tiles = [slice(c0, c0 + ATTN_COLS) for c0 in range(0, n_cols, ATTN_COLS)]
    prev = [(m_ref[:, c], l_ref[:, c], acc_ref[:, c]) for c in tiles]
    scored = [score_fn(cols) for cols in tiles]
    soft = []
    for t, (s, vt) in enumerate(scored):
        m_new = jnp.maximum(prev[t][0], jnp.max(s, axis=0, keepdims=True))
        p = jnp.exp(s - m_new)
        soft.append((m_new, jnp.sum(p, axis=0, keepdims=True), p.astype(BF16)))
    new = []
    for t, (m_new, l_new, p) in enumerate(soft):
        m_prev, l_prev, acc_prev = prev[t]
        alpha = jnp.exp(m_prev - m_new)
        pv = jnp.dot(scored[t][1], p, preferred_element_type=F32)
        new.append((m_new, alpha * l_prev + l_new, alpha * acc_prev + pv))
    for cols, (m_new, l_new, acc_new) in zip(tiles, new):
        m_ref[:, cols] = m_new
        l_ref[:, cols] = l_new
        acc_ref[:, cols] = acc_new


def _sigmoid(x):
    return 1.0 / (1.0 + jnp.exp(-x))


def _silu(x):
    return x * _sigmoid(x)


def _gelu_tanh(x):
    return 0.5 * x * (1.0 + jnp.tanh(math.sqrt(2.0 / math.pi) * (x + 0.044715 * (x * x * x))))


def _log_sigmoid(x):
    return jnp.minimum(x, 0.0) - jnp.log1p(jnp.exp(-jnp.abs(x)))


def _mod_kernel(c_ref, w_ref, b_ref, o_ref):
    ca = _silu(c_ref[...])
    o_ref[0] = jnp.dot(ca, w_ref[0], preferred_element_type=F32, precision=HIGHEST) + b_ref[0]


def _modulation(c, mod_w, mod_b):
    depth, d, n = mod_w.shape
    bsz = c.shape[0]
    rows = 8
    cp = jnp.zeros((rows, d), F32).at[:bsz].set(c)
    tn = 1024
    out = pl.pallas_call(
        _mod_kernel,
        grid=(depth, n // tn),
        in_specs=[pl.BlockSpec((rows, d), lambda l, j: (0, 0)),
                  pl.BlockSpec((1, d, tn), lambda l, j: (l, 0, j)),
                  pl.BlockSpec((1, 1, tn), lambda l, j: (l, 0, j))],
        out_specs=pl.BlockSpec((1, rows, tn), lambda l, j: (l, 0, j)),
        out_shape=jax.ShapeDtypeStruct((depth, rows, n), F32),
        compiler_params=_cparams("parallel", "parallel"),
        name="modulation",
    )(cp, mod_w, mod_b.reshape(depth, 1, n))
    return out[:, :bsz]


def _nm_matmul_kernel(h_ref, g_ref, sc_ref, sh_ref, w_ref, *rest, emit_hn):
    if emit_hn:
        o_ref, hn_ref, hn_scr = rest
    else:
        o_ref, hn_scr = rest

    @pl.when(pl.program_id(2) == 0)
    def _():
        x = h_ref[0]
        y = x * lax.rsqrt(jnp.mean(x * x, axis=-1, keepdims=True) + EPS) * g_ref[...]
        y = y * (1.0 + sc_ref[0]) + sh_ref[0]
        hn_scr[...] = y.astype(BF16)
        if emit_hn:
            hn_ref[0] = y

    o_ref[0] = jnp.dot(hn_scr[...], w_ref[...], preferred_element_type=F32).astype(o_ref.dtype)


def _nm_matmul(h, g, sc, sh, w, *, tm, tn, emit_hn=False, name="nm_matmul"):
    bsz, s, d = h.shape
    n = w.shape[1]
    tm = min(tm, s)
    out_shape = [jax.ShapeDtypeStruct((bsz, s, n), F32)]
    out_specs = [pl.BlockSpec((1, tm, tn), lambda b, i, j: (b, i, j))]
    if emit_hn:
        out_shape.append(jax.ShapeDtypeStruct((bsz, s, d), F32))
        out_specs.append(pl.BlockSpec((1, tm, d), lambda b, i, j: (b, i, 0)))
    res = pl.pallas_call(
        functools.partial(_nm_matmul_kernel, emit_hn=emit_hn),
        grid=(bsz, s // tm, n // tn),
        in_specs=[pl.BlockSpec((1, tm, d), lambda b, i, j: (b, i, 0)),
                  pl.BlockSpec((1, d), lambda b, i, j: (0, 0)),
                  pl.BlockSpec((1, 1, d), lambda b, i, j: (b, 0, 0)),
                  pl.BlockSpec((1, 1, d), lambda b, i, j: (b, 0, 0)),
                  pl.BlockSpec((d, tn), lambda b, i, j: (0, j))],
        out_specs=out_specs,
        out_shape=out_shape,
        scratch_shapes=[pltpu.VMEM((tm, d), BF16)],
        compiler_params=_cparams("parallel", "parallel", "arbitrary"),
        name=name,
    )(h, g.reshape(1, d), sc.reshape(bsz, 1, d), sh.reshape(bsz, 1, d), w)
    return res if emit_hn else res[0]


def _out_proj_kernel(xa_ref, xb_ref, wa_ref, wb_ref, h_ref, g_ref, o_ref):
    acc = jnp.dot(xa_ref[0], wa_ref[...], preferred_element_type=F32)
    acc += jnp.dot(xb_ref[0], wb_ref[...], preferred_element_type=F32)
    o_ref[0] = h_ref[0] + g_ref[0] * acc


def _out_proj(xa, xb, w, h, g1, *, tm=512, tn=1024):
    bsz, s, ka = xa.shape
    kb = xb.shape[2]
    n = w.shape[1]
    tm = min(tm, s)
    wa = w[:ka].astype(BF16)
    wb = w[ka:].astype(BF16)
    return pl.pallas_call(
        _out_proj_kernel,
        grid=(bsz, s // tm, n // tn),
        in_specs=[pl.BlockSpec((1, tm, ka), lambda b, i, j: (b, i, 0)),
                  pl.BlockSpec((1, tm, kb), lambda b, i, j: (b, i, 0)),
                  pl.BlockSpec((ka, tn), lambda b, i, j: (0, j)),
                  pl.BlockSpec((kb, tn), lambda b, i, j: (0, j)),
                  pl.BlockSpec((1, tm, tn), lambda b, i, j: (b, i, j)),
                  pl.BlockSpec((1, 1, tn), lambda b, i, j: (b, 0, j))],
        out_specs=pl.BlockSpec((1, tm, tn), lambda b, i, j: (b, i, j)),
        out_shape=jax.ShapeDtypeStruct((bsz, s, n), F32),
        compiler_params=_cparams("parallel", "parallel", "parallel"),
        name="out_proj",
    )(xa, xb, wa, wb, h, g1.reshape(bsz, 1, n))


def _mlstm_kernel(gb_ref, aqk_ref, v_ref, ao_ref, gcol_ref, cw_ref, cb_ref, hg_ref,
                  o_ref, xbuf, c_scr, n_scr, m_scr):
    L = MLSTM_CHUNK
    H, DQK, DV = MLSTM_HEADS, MLSTM_DQK, MLSTM_DV
    c = pl.program_id(1)

    @pl.when(c == 0)
    def _():
        xbuf[0:8, :] = jnp.zeros((8, 2 * A_QK), F32)
        c_scr[...] = jnp.zeros_like(c_scr)
        n_scr[...] = jnp.zeros_like(n_scr)
        m_scr[...] = jnp.zeros_like(m_scr)

    @pl.when(c > 0)
    def _():
        xbuf[0:8, :] = xbuf[L:L + 8, :]

    xbuf[8:L + 8, :] = aqk_ref[0]
    conv = cb_ref[...] + cw_ref[0:1, :] * xbuf[5:5 + L, :]
    for j in range(1, CONV_W):
        conv = conv + cw_ref[j:j + 1, :] * xbuf[5 + j:5 + j + L, :]
    qk = _silu(conv)

    r_i = lax.broadcasted_iota(I32, (L, L), 0)
    c_i = lax.broadcasted_iota(I32, (L, L), 1)
    tril = c_i <= r_i
    gcol = gcol_ref[0]
    grow = gcol.T

    for h in range(H):
        q = qk[:, h * DQK:(h + 1) * DQK] * (DQK ** -0.5)
        k = qk[:, A_QK + h * DQK:A_QK + (h + 1) * DQK]
        v = v_ref[0, :, h * DV:(h + 1) * DV]
        i_col = gcol[:, h:h + 1] + gb_ref[h]
        i_row = grow[h:h + 1, :] + gb_ref[h]
        lf_col = _log_sigmoid(gcol[:, H + h:H + h + 1] + gb_ref[H + h])
        lf_row = _log_sigmoid(grow[H + h:H + h + 1, :] + gb_ref[H + h])
        bcum_col = jnp.sum(jnp.where(tril, lf_row, 0.0), axis=1, keepdims=True)
        bcum_row = jnp.sum(jnp.where(r_i <= c_i, lf_col, 0.0), axis=0, keepdims=True)
        g_tot = jnp.sum(lf_row, axis=1, keepdims=True)
        m_prev = m_scr[h:h + 1, 0:1]
        d_intra = jnp.where(tril, bcum_col - bcum_row + i_row, -jnp.inf)
        d_inter = bcum_col + m_prev
        m_t = jnp.maximum(d_inter, jnp.max(d_intra, axis=1, keepdims=True))
        w_intra = jnp.exp(d_intra - m_t)
        w_inter = jnp.exp(d_inter - m_t)
        qb = q.astype(BF16)
        kb = k.astype(BF16)
        vb = v.astype(BF16)
        s = _dot_nt(qb, kb) * w_intra
        c_prev = c_scr[h]
        n_prev = n_scr[h:h + 1, :]
        num = jnp.dot(s.astype(BF16), vb, preferred_element_type=F32)
        num = num + w_inter * jnp.dot(qb, c_prev.astype(BF16), preferred_element_type=F32)
        den = jnp.sum(s, axis=1, keepdims=True) + w_inter * jnp.sum(q * n_prev, axis=1, keepdims=True)
        hh = num / jnp.maximum(jnp.abs(den), jnp.exp(-m_t))
        d_state = g_tot - bcum_col + i_col
        m_new = jnp.maximum(g_tot + m_prev, jnp.max(d_state, axis=0, keepdims=True))
        w_s = jnp.exp(d_state - m_new)
        w_c = jnp.exp(g_tot + m_prev - m_new)
        kw = k * w_s
        c_scr[h] = w_c * c_prev + jnp.dot(kw.T.astype(BF16), vb, preferred_element_type=F32)
        n_scr[h:h + 1, :] = w_c * n_prev + jnp.sum(kw, axis=0, keepdims=True)
        m_scr[h:h + 1, 0:1] = m_new
        y = hh * lax.rsqrt(jnp.mean(hh * hh, axis=-1, keepdims=True) + EPS) * hg_ref[:, h * DV:(h + 1) * DV]
        y = y * _sigmoid(ao_ref[0, :, h * DV:(h + 1) * DV])
        o_ref[0, :, h * DV:(h + 1) * DV] = y.astype(o_ref.dtype)


def _mlstm(proj, gate_b, conv_w, conv_b, head_g):
    bsz, s, _ = proj.shape
    L = MLSTM_CHUNK
    gate_blk = (2 * A_QK + 2 * A_V + 2 * B_QK + B_V) // LANES
    return pl.pallas_call(
        _mlstm_kernel,
        grid=(bsz, s // L),
        in_specs=[pl.BlockSpec(memory_space=pltpu.SMEM),
                  pl.BlockSpec((1, L, 2 * A_QK), lambda b, c: (b, c, 0)),
                  pl.BlockSpec((1, L, A_V), lambda b, c: (b, c, 1)),
                  pl.BlockSpec((1, L, A_V), lambda b, c: (b, c, 2)),
                  pl.BlockSpec((1, L, LANES), lambda b, c: (b, c, gate_blk)),
                  pl.BlockSpec((CONV_W, 2 * A_QK), lambda b, c: (0, 0)),
                  pl.BlockSpec((1, 2 * A_QK), lambda b, c: (0, 0)),
                  pl.BlockSpec((1, A_V), lambda b, c: (0, 0))],
        out_specs=pl.BlockSpec((1, L, A_V), lambda b, c: (b, c, 0)),
        out_shape=jax.ShapeDtypeStruct((bsz, s, A_V), BF16),
        scratch_shapes=[pltpu.VMEM((L + 8, 2 * A_QK), F32),
                        pltpu.VMEM((MLSTM_HEADS, MLSTM_DQK, MLSTM_DV), F32),
                        pltpu.VMEM((8, MLSTM_DQK), F32),
                        pltpu.VMEM((8, LANES), F32)],
        compiler_params=_cparams("parallel", "arbitrary"),
        name="mlstm",
    )(gate_b.reshape(-1), proj, proj, proj, proj, conv_w, conv_b.reshape(1, -1), head_g.reshape(1, -1))


def _rope_consts(dh):
    half = dh // 8
    lane = np.arange(LANES) % dh
    inv_half = ROPE_THETA ** (-jnp.arange(half, dtype=F32) / half)
    inv = jnp.where(jnp.asarray(lane < 2 * half), inv_half[lane % half], 0.0)
    lo = (lane < half).astype(np.float32)
    hi = ((lane >= half) & (lane < 2 * half)).astype(np.float32)
    return (inv.astype(F32).reshape(1, LANES), jnp.asarray(lo).reshape(1, LANES),
            jnp.asarray(hi).reshape(1, LANES))


def _rope_tables(pos_ref, inv_ref, lo_ref, hi_ref):
    ang = pos_ref[0].astype(F32) * inv_ref[...]
    cs = jnp.cos(ang)
    sn = jnp.sin(ang)
    return cs, -sn * lo_ref[...], sn * hi_ref[...]


def _rope_apply(x, cs, sa, sb, half):
    return x * cs + pltpu.roll(x, LANES - half, 1) * sa + pltpu.roll(x, half, 1) * sb


def _diff_prep_kernel(pos_ref, inv_ref, lo_ref, hi_ref, q_ref, k_ref, v_ref, qo_ref, ko_ref, vo_ref):
    cs, sa, sb = _rope_tables(pos_ref, inv_ref, lo_ref, hi_ref)
    half = DIFF_DQK // 8
    scale = DIFF_DQK ** -0.5
    for h in range(DIFF_HEADS):
        sl = slice(h * LANES, (h + 1) * LANES)
        qo_ref[0, sl, :] = (_rope_apply(q_ref[0, :, sl], cs, sa, sb, half) * scale).T.astype(BF16)
        ko_ref[0, :, sl] = _rope_apply(k_ref[0, :, sl], cs, sa, sb, half).astype(BF16)
        vo_ref[0, sl, :] = v_ref[0, :, sl].T.astype(BF16)


def _diff_prep(proj, pos3, ts=512):
    bsz, s, _ = proj.shape
    ts = min(ts, s)
    inv, lo, hi = _rope_consts(DIFF_DQK)
    base = (2 * A_QK + 2 * A_V) // B_QK
    cst = pl.BlockSpec((1, LANES), lambda b, i: (0, 0))
    osd = jax.ShapeDtypeStruct((bsz, s, B_QK), BF16)
    osd_t = jax.ShapeDtypeStruct((bsz, B_QK, s), BF16)
    spec_t = pl.BlockSpec((1, B_QK, ts), lambda b, i: (b, 0, i))
    return pl.pallas_call(
        _diff_prep_kernel,
        grid=(bsz, s // ts),
        in_specs=[pl.BlockSpec((1, ts, 1), lambda b, i: (b, i, 0)), cst, cst, cst,
                  pl.BlockSpec((1, ts, B_QK), lambda b, i: (b, i, base)),
                  pl.BlockSpec((1, ts, B_QK), lambda b, i: (b, i, base + 1)),
                  pl.BlockSpec((1, ts, B_V), lambda b, i: (b, i, base + 2))],
        out_specs=[spec_t, pl.BlockSpec((1, ts, B_QK), lambda b, i: (b, i, 0)), spec_t],
        out_shape=[osd_t, osd, osd_t],
        compiler_params=_cparams("parallel", "parallel"),
        name="diff_prep",
    )(pos3, inv, lo, hi, proj, proj, proj)


def _causal_pairs(nq):
    pairs = [(i, j) for i in range(nq) for j in range(i + 1)]
    return (jnp.asarray([p[0] for p in pairs], I32), jnp.asarray([p[1] for p in pairs], I32))


def _diff_attn_kernel(it_ref, jt_ref, lam_ref, q_ref, k_ref, v_ref, g_ref, o_ref, qs_scr, m_scr, l_scr, acc_scr,
                      *, tq, tk, lambda_init):
    i = it_ref[pl.program_id(2)]
    j = jt_ref[pl.program_id(2)]

    @pl.when(j == 0)
    def _():
        q = q_ref[0]
        feat = lax.broadcasted_iota(I32, q.shape, 0)
        zero = jnp.zeros_like(q)
        qs_scr[:, 0:tq] = jnp.where(feat < DIFF_DQK, q, zero)
        qs_scr[:, tq:2 * tq] = jnp.where(feat >= DIFF_DQK, q, zero)
        m_scr[...] = jnp.full_like(m_scr, -jnp.inf)
        l_scr[...] = jnp.zeros_like(l_scr)
        acc_scr[...] = jnp.zeros_like(acc_scr)

    def step(masked):
        def scores(cols):
            s = jnp.dot(k_ref[0], qs_scr[:, cols], preferred_element_type=F32)
            if masked:
                key = lax.broadcasted_iota(I32, (tk, ATTN_COLS), 0)
                qry = lax.broadcasted_iota(I32, (tk, ATTN_COLS), 1) + (cols.start % tq)
                s = jnp.where(key <= qry, s, -jnp.inf)
            return s, v_ref[0]
        _flash_tiles_t(2 * tq, scores, m_scr, l_scr, acc_scr)

    @pl.when(j < i)
    def _():
        step(False)

    @pl.when(j == i)
    def _():
        step(True)
        lv = lam_ref[...]
        lam = (jnp.exp(jnp.sum(lv[0:1] * lv[1:2], axis=1, keepdims=True))
               - jnp.exp(jnp.sum(lv[2:3] * lv[3:4], axis=1, keepdims=True)) + lambda_init)
        o0 = acc_scr[:, 0:tq] / l_scr[:, 0:tq]
        o1 = acc_scr[:, tq:2 * tq] / l_scr[:, tq:2 * tq]
        a = o0 - lam * o1
        y = a * lax.rsqrt(jnp.mean(a * a, axis=0, keepdims=True) + EPS) * g_ref[0]
        o_ref[0] = (y * (1.0 - lambda_init)).T.astype(o_ref.dtype)


def _diff_attn(qt, k, vt, lam_vec, head_g, lambda_init, *, tq=512):
    bsz, s, _ = k.shape
    tq = min(tq, s)
    it, jt = _causal_pairs(s // tq)
    grid_spec = pltpu.PrefetchScalarGridSpec(
        num_scalar_prefetch=2,
        grid=(bsz, DIFF_HEADS, it.shape[0]),
        in_specs=[pl.BlockSpec((4, DIFF_DQK), lambda b, h, t, it, jt: (0, 0)),
                  pl.BlockSpec((1, LANES, tq), lambda b, h, t, it, jt: (b, h, it[t])),
                  pl.BlockSpec((1, tq, LANES), lambda b, h, t, it, jt: (b, jt[t], h)),
                  pl.BlockSpec((1, DIFF_DV, tq), lambda b, h, t, it, jt: (b, h, jt[t])),
                  pl.BlockSpec((1, DIFF_DV, 1), lambda b, h, t, it, jt: (h, 0, 0))],
        out_specs=pl.BlockSpec((1, tq, DIFF_DV), lambda b, h, t, it, jt: (b, it[t], h)),
        scratch_shapes=[pltpu.VMEM((LANES, 2 * tq), BF16),
                        pltpu.VMEM((1, 2 * tq), F32),
                        pltpu.VMEM((1, 2 * tq), F32),
                        pltpu.VMEM((DIFF_DV, 2 * tq), F32)])
    return pl.pallas_call(
        functools.partial(_diff_attn_kernel, tq=tq, tk=tq, lambda_init=lambda_init),
        grid_spec=grid_spec,
        out_shape=jax.ShapeDtypeStruct((bsz, s, B_V), BF16),
        compiler_params=_cparams("parallel", "parallel", "arbitrary"),
        name="diff_attn",
    )(it, jt, lam_vec, qt, k, vt, head_g.reshape(DIFF_HEADS, DIFF_DV, 1))


def _moba_prep_kernel(pos_ref, inv_ref, lo_ref, hi_ref, q_ref, k_ref, v_ref,
                      qo_ref, ko_ref, vo_ref, sel_ref, kmt_scr, qf_scr, *, nb):
    i = pl.program_id(1)
    tb = MOBA_BLOCK
    half = MOBA_DH // 8
    scale = MOBA_DH ** -0.5

    @pl.when(i == 0)
    def _():
        kmt_scr[...] = jnp.zeros_like(kmt_scr)

    cs, sa, sb = _rope_tables(pos_ref, inv_ref, lo_ref, hi_ref)
    kmeans = []
    for h in range(MOBA_HEADS):
        sl = slice(h * LANES, (h + 1) * LANES)
        qr = _rope_apply(q_ref[0, :, sl], cs, sa, sb, half)
        kr = _rope_apply(k_ref[0, :, sl], cs, sa, sb, half)
        qf_scr[:, sl] = qr
        qo_ref[0, sl, :] = (qr * scale).T.astype(BF16)
        ko_ref[0, :, sl] = kr.astype(BF16)
        vo_ref[0, sl, :] = v_ref[0, :, sl].T.astype(BF16)
        kmeans.append(jnp.mean(kr, axis=0, keepdims=True))

    gate = _dot_nt(qf_scr[...], kmt_scr[...], precision=HIGHEST)
    lane = lax.broadcasted_iota(I32, (tb, LANES), 1)
    n = lane % nb
    valid = n < i
    gm = jnp.where(valid, gate, -jnp.inf)
    rank = jnp.zeros((tb, LANES), I32)
    for d in range(1, nb):
        wraps = (n + d) >= nb
        partner = jnp.where(wraps, pltpu.roll(gm, nb - d, 1), pltpu.roll(gm, LANES - d, 1))
        pn = jnp.where(wraps, n + d - nb, n + d)
        beats = (partner > gm) | ((partner == gm) & (pn < n))
        rank = rank + beats.astype(I32)
    sel_ref[0] = jnp.where(valid & (rank < MOBA_TOPK), 1.0, 0.0).astype(F32).T

    for h in range(MOBA_HEADS):
        _store_row(kmt_scr, h * nb + i, kmeans[h], slice(h * LANES, (h + 1) * LANES))


def _moba_prep(proj, pos3):
    bsz, s, _ = proj.shape
    tb = MOBA_BLOCK
    nb = s // tb
    assert s % tb == 0 and MOBA_HEADS * nb <= LANES and nb >= MOBA_TOPK
    inv, lo, hi = _rope_consts(MOBA_DH)
    cst = pl.BlockSpec((1, LANES), lambda b, i: (0, 0))
    osd = jax.ShapeDtypeStruct((bsz, s, C_W), BF16)
    osd_t = jax.ShapeDtypeStruct((bsz, C_W, s), BF16)
    spec_t = pl.BlockSpec((1, C_W, tb), lambda b, i: (b, 0, i))
    return pl.pallas_call(
        functools.partial(_moba_prep_kernel, nb=nb),
        grid=(bsz, nb),
        in_specs=[pl.BlockSpec((1, tb, 1), lambda b, i: (b, i, 0)), cst, cst, cst,
                  pl.BlockSpec((1, tb, C_W), lambda b, i: (b, i, 0)),
                  pl.BlockSpec((1, tb, C_W), lambda b, i: (b, i, 1)),
                  pl.BlockSpec((1, tb, C_W), lambda b, i: (b, i, 2))],
        out_specs=[spec_t, pl.BlockSpec((1, tb, C_W), lambda b, i: (b, i, 0)), spec_t,
                   pl.BlockSpec((1, LANES, tb), lambda b, i: (b, 0, i))],
        out_shape=[osd_t, osd, osd_t, jax.ShapeDtypeStruct((bsz, LANES, s), F32)],
        scratch_shapes=[pltpu.VMEM((LANES, C_W), F32), pltpu.VMEM((tb, C_W), F32)],
        compiler_params=_cparams("parallel", "arbitrary"),
        name="moba_prep",
    )(pos3, inv, lo, hi, proj, proj, proj)


def _moba_attn_kernel(it_ref, jt_ref, q_ref, k_ref, v_ref, sel_ref, o_ref, m_scr, l_scr, acc_scr, *, nb):
    h = pl.program_id(1)
    i = it_ref[pl.program_id(2)]
    j = jt_ref[pl.program_id(2)]
    tb = MOBA_BLOCK
    tq = 2 * tb

    def pick_bias(cols, blk):
        row = h * nb + blk
        base = pl.multiple_of((row // SUBLANES) * SUBLANES, SUBLANES)
        grp = sel_ref[0, pl.ds(base, SUBLANES), cols]
        sub = lax.broadcasted_iota(I32, grp.shape, 0)
        hit = jnp.max(jnp.where(sub == row - base, grp, 0.0), axis=0, keepdims=True)
        return jnp.where(hit > 0.5, 0.0, -jnp.inf)

    @pl.when(j == 0)
    def _():
        m_scr[...] = jnp.full_like(m_scr, -jnp.inf)
        l_scr[...] = jnp.zeros_like(l_scr)
        acc_scr[...] = jnp.zeros_like(acc_scr)

        def scores(cols):
            c0 = cols.start
            if c0 < tb:
                key = lax.broadcasted_iota(I32, (tb, ATTN_COLS), 0)
                qry = lax.broadcasted_iota(I32, (tb, ATTN_COLS), 1) + c0
                s = jnp.dot(k_ref[0, 0:tb, :], q_ref[0, :, cols], preferred_element_type=F32)
                return jnp.where(key <= qry, s, -jnp.inf), v_ref[0, :, 0:tb]
            key = lax.broadcasted_iota(I32, (tq, ATTN_COLS), 0)
            qry = lax.broadcasted_iota(I32, (tq, ATTN_COLS), 1) + c0
            s = jnp.dot(k_ref[0], q_ref[0, :, cols], preferred_element_type=F32)
            s = jnp.where(key < tb, s + pick_bias(cols, 2 * i), jnp.where(key <= qry, s, -jnp.inf))
            return s, v_ref[0]
        _flash_tiles_t(tq, scores, m_scr, l_scr, acc_scr)

    @pl.when((j > 0) & (j <= i))
    def _():
        kb = 2 * (i - j)
        key = lax.broadcasted_iota(I32, (tq, ATTN_COLS), 0)

        def scores(cols):
            bias = jnp.where(key < tb, pick_bias(cols, kb), pick_bias(cols, kb + 1))
            return jnp.dot(k_ref[0], q_ref[0, :, cols], preferred_element_type=F32) + bias, v_ref[0]
        _flash_tiles_t(tq, scores, m_scr, l_scr, acc_scr)

    @pl.when(j == i)
    def _():
        o_ref[0] = (acc_scr[...] / l_scr[...]).T.astype(o_ref.dtype)


def _moba_attn(qt, k, vt, sel_t):
    bsz, s, _ = k.shape
    tq = 2 * MOBA_BLOCK
    nb = s // MOBA_BLOCK
    assert s % tq == 0
    it, jt = _causal_pairs(s // tq)
    grid_spec = pltpu.PrefetchScalarGridSpec(
        num_scalar_prefetch=2,
        grid=(bsz, MOBA_HEADS, it.shape[0]),
        in_specs=[pl.BlockSpec((1, MOBA_DH, tq), lambda b, h, t, it, jt: (b, h, it[t])),
                  pl.BlockSpec((1, tq, MOBA_DH), lambda b, h, t, it, jt: (b, it[t] - jt[t], h)),
                  pl.BlockSpec((1, MOBA_DH, tq), lambda b, h, t, it, jt: (b, h, it[t] - jt[t])),
                  pl.BlockSpec((1, LANES, tq), lambda b, h, t, it, jt: (b, 0, it[t]))],
        out_specs=pl.BlockSpec((1, tq, MOBA_DH), lambda b, h, t, it, jt: (b, it[t], h)),
        scratch_shapes=[pltpu.VMEM((1, tq), F32), pltpu.VMEM((1, tq), F32), pltpu.VMEM((MOBA_DH, tq), F32)])
    return pl.pallas_call(
        functools.partial(_moba_attn_kernel, nb=nb),
        grid_spec=grid_spec,
        out_shape=jax.ShapeDtypeStruct((bsz, s, C_W), BF16),
        compiler_params=_cparams("parallel", "parallel", "arbitrary"),
        name="moba_attn",
    )(it, jt, qt, k, vt, sel_t)


def _sgu_kernel(u_ref, v_ref, lng_ref, lnb_ref, w_ref, bt_ref, o_ref, *, ts):
    L = SGU_CHUNK
    r_i = lax.broadcasted_iota(I32, (L, L), 0)
    c_i = lax.broadcasted_iota(I32, (L, L), 1)
    tril = c_i <= r_i
    for g in range(SGU_GROUPS):
        sl = slice(g * SGU_CH, (g + 1) * SGU_CH)
        w = jnp.where(tril, w_ref[g], 0.0).astype(BF16)
        bias = bt_ref[:, g:g + 1]
        for c in range(ts // L):
            rows = slice(c * L, (c + 1) * L)
            vv = _gelu_tanh(v_ref[0, rows, sl])
            mu = jnp.mean(vv, axis=-1, keepdims=True)
            dv = vv - mu
            var = jnp.mean(dv * dv, axis=-1, keepdims=True)
            vn = dv * lax.rsqrt(var + EPS) * lng_ref[:, sl] + lnb_ref[:, sl]
            mixed = jnp.dot(w, vn.astype(BF16), preferred_element_type=F32) + bias
            o_ref[0, rows, sl] = (_gelu_tanh(u_ref[0, rows, sl]) * mixed).astype(o_ref.dtype)


def _sgu(proj, ln_g, ln_b, w_s, b_s, ts=256):
    bsz, s, _ = proj.shape
    ts = min(ts, s)
    base = 3 * C_W // D_W
    return pl.pallas_call(
        functools.partial(_sgu_kernel, ts=ts),
        grid=(bsz, s // ts),
        in_specs=[pl.BlockSpec((1, ts, D_W), lambda b, i: (b, i, base)),
                  pl.BlockSpec((1, ts, D_W), lambda b, i: (b, i, base + 1)),
                  pl.BlockSpec((1, D_W), lambda b, i: (0, 0)),
                  pl.BlockSpec((1, D_W), lambda b, i: (0, 0)),
                  pl.BlockSpec((SGU_GROUPS, SGU_CHUNK, SGU_CHUNK), lambda b, i: (0, 0, 0)),
                  pl.BlockSpec((SGU_CHUNK, SGU_GROUPS), lambda b, i: (0, 0))],
        out_specs=pl.BlockSpec((1, ts, D_W), lambda b, i: (b, i, 0)),
        out_shape=jax.ShapeDtypeStruct((bsz, s, D_W), BF16),
        compiler_params=_cparams("parallel", "parallel"),
        name="sgu",
    )(proj, proj, ln_g.reshape(1, -1), ln_b.reshape(1, -1), w_s, b_s.T)


def _topk_rows(jobs, k):
    def body(it, carry):
        for s_scr, val_scr, idx_scr, payload_scr in jobs:
            n, t = s_scr.shape
            row = lax.broadcasted_iota(I32, (n, t), 0).astype(F32)
            s = s_scr[...]
            m = jnp.max(s, axis=0, keepdims=True)
            pos = jnp.min(jnp.where(s == m, row, float(n)), axis=0, keepdims=True)
            hit = row == pos
            _store_row(val_scr, it, m)
            if payload_scr is None:
                _store_row(idx_scr, it, pos)
            else:
                _store_row(idx_scr, it, jnp.max(jnp.where(hit, payload_scr[...], -1.0), axis=0, keepdims=True))
            s_scr[...] = jnp.where(hit, -jnp.inf, s)
        return carry

    for _, val_scr, idx_scr, _ in jobs:
        val_scr[...] = jnp.zeros_like(val_scr)
        idx_scr[...] = jnp.zeros_like(idx_scr)
    lax.fori_loop(0, k, body, 0)


def _peer_topk_kernel(q_ref, keys_ref, idx_ref, g_ref, s1_scr, s2_scr, v1, i1, v2, i2, cand, cidx, vt, it):
    K = PEER_TOPK
    half = PEER_DQ // 2
    for c, s_scr in enumerate((s1_scr, s2_scr)):
        s_scr[...] = _dot_nt(keys_ref[0, c], q_ref[:, c * half:(c + 1) * half], precision=HIGHEST)
    _topk_rows([(s1_scr, v1, i1, None), (s2_scr, v2, i2, None)], K)
    cand[...] = jnp.full(cand.shape, -jnp.inf, F32)
    cidx[...] = jnp.zeros(cidx.shape, F32)
    off = 0
    for a in range(K):
        nb = K // (a + 1)
        cand[off:off + nb, :] = v1[a:a + 1, :] + v2[0:nb, :]
        cidx[off:off + nb, :] = i1[a:a + 1, :] * float(PEER_NKEYS) + i2[0:nb, :]
        off += nb
    _topk_rows([(cand, vt, it, cidx)], K)
    ts = vt[...]
    e = jnp.exp(ts - jnp.max(ts, axis=0, keepdims=True))
    g_ref[0] = e / jnp.sum(e, axis=0, keepdims=True)
    idx_ref[0] = it[...].astype(I32)


def _peer_topk(q2, sub_keys, tt=512):
    n_tok = q2.shape[0]
    tt = min(tt, n_tok)
    K = PEER_TOPK
    half = PEER_DQ // 2
    vs = lambda r, dt: pltpu.VMEM((r, tt), dt)
    n_cand = sum(K // (a + 1) for a in range(K))
    n_cand = -(-n_cand // SUBLANES) * SUBLANES
    return pl.pallas_call(
        _peer_topk_kernel,
        grid=(n_tok // tt, PEER_HEADS),
        in_specs=[pl.BlockSpec((tt, PEER_DQ), lambda i, h: (i, h)),
                  pl.BlockSpec((1, 2, PEER_NKEYS, half), lambda i, h: (h, 0, 0, 0))],
        out_specs=[pl.BlockSpec((1, K, tt), lambda i, h: (h, 0, i)),
                   pl.BlockSpec((1, K, tt), lambda i, h: (h, 0, i))],
        out_shape=[jax.ShapeDtypeStruct((PEER_HEADS, K, n_tok), I32),
                   jax.ShapeDtypeStruct((PEER_HEADS, K, n_tok), F32)],
        scratch_shapes=[vs(PEER_NKEYS, F32), vs(PEER_NKEYS, F32), vs(K, F32), vs(K, F32), vs(K, F32), vs(K, F32),
                        vs(n_cand, F32), vs(n_cand, F32), vs(K, F32), vs(K, F32)],
        compiler_params=_cparams("parallel", "parallel"),
        name="peer_topk",
    )(q2, sub_keys)


def _pack_kernel(u_ref, v_ref, o_ref):
    ub = lax.bitcast_convert_type(u_ref[...].astype(BF16).astype(F32), I32)
    vb = lax.bitcast_convert_type(v_ref[...].astype(BF16).astype(F32), I32)
    o_ref[...] = (ub & jnp.int32(-65536)) | lax.shift_right_logical(vb, jnp.int32(16))


def _pack_tables(u_all, v_all, layer, tm=512):
    _, n, d = u_all.shape
    spec = pl.BlockSpec((None, tm, d), lambda i: (layer, i, 0))
    packed = pl.pallas_call(
        _pack_kernel,
        grid=(n // tm,),
        in_specs=[spec, spec],
        out_specs=pl.BlockSpec((tm, d), lambda i: (i, 0)),
        out_shape=jax.ShapeDtypeStruct((n, d), I32),
        compiler_params=_cparams("parallel"),
        name="peer_pack",
    )(u_all, v_all)
    return packed.reshape(n, d // LANES, LANES)


PEER_SLOTS = SUBLANES
PEER_AHEAD = 6
PEER_DMA_QUEUES = 2


def _fold_order():
    tiles = [[e] * SUBLANES for e in range(SUBLANES)]
    f = SUBLANES
    while len(tiles) > 1:
        tiles = [[x[s] if (s % f) < f // 2 else y[s] for s in range(SUBLANES)]
                 for x, y in zip(tiles[0::2], tiles[1::2])]
        f //= 2
    return tiles[0]


def _peer_gather_kernel(idx_ref, gate_ref, x_ref, h_ref, g2_ref, tab_hbm, o_ref, *rest, tt):
    R = PEER_SEL
    bufs, (stage, wscr, sem) = rest[:PEER_SLOTS], rest[PEER_SLOTS:]
    step = pl.program_id(0)
    d = x_ref.shape[1]
    nch = d // LANES
    nv = nch // SUBLANES
    ngrp = R // SUBLANES
    per_grp = R // ngrp
    order = _fold_order()

    def row_copy(t, j, slot):
        return pltpu.make_async_copy(tab_hbm.at[idx_ref[t, j]], bufs[slot].at[pl.ds(j * nch, nch)], sem.at[slot])

    def wait(slot):
        pltpu.make_async_copy(bufs[(slot + 1) % PEER_SLOTS], bufs[slot], sem.at[slot]).wait()

    def tile(slot, j, v):
        return bufs[slot][(j * nv + v) * SUBLANES:(j * nv + v + 1) * SUBLANES, :]

    lane = lax.broadcasted_iota(I32, (SUBLANES, LANES), 1)
    sub = lax.broadcasted_iota(I32, (SUBLANES, LANES), 0)
    hi_mask = jnp.int32(-65536)
    fold_masks = {f: (sub % f) < f // 2 for f in (8, 4, 2)}
    order_vec = jnp.zeros((SUBLANES, LANES), I32)
    for s_, e_ in enumerate(order):
        order_vec = jnp.where(sub == s_, e_, order_vec)

    def fold(x, y, f):
        m = fold_masks[f]
        if f == SUBLANES:
            return jnp.where(m, x, y) + pltpu.roll(jnp.where(m, y, x), f // 2, 0)
        return (jnp.where(m, x, pltpu.roll(y, f // 2, 0))
                + jnp.where(m, pltpu.roll(x, SUBLANES - f // 2, 0), y))

    @pl.when(step == 0)
    def _():
        for t0 in range(PEER_AHEAD):
            def prime(j, carry, t0=t0):
                row_copy(t0, j, t0).start()
                return carry
            lax.fori_loop(0, R, prime, 0)

    def group(gi, carry):
        base = pl.multiple_of(gi * SUBLANES, SUBLANES)
        xg = x_ref[pl.ds(base, SUBLANES), :]
        gg = gate_ref[pl.ds(base, SUBLANES), :]
        for r in range(SUBLANES):
            slot = r
            nxt = base + r + PEER_AHEAD
            nslot = (r + PEER_AHEAD) % PEER_SLOTS
            wait(slot)
            xs = []
            for v in range(nv):
                xv = jnp.zeros((SUBLANES, LANES), F32)
                for s_ in range(SUBLANES):
                    c = v * SUBLANES + s_
                    xv = jnp.where(sub == s_, jnp.broadcast_to(xg[r:r + 1, c * LANES:(c + 1) * LANES],
                                                                (SUBLANES, LANES)), xv)
                xs.append(xv)
            grow = jnp.broadcast_to(gg[r:r + 1, :], (SUBLANES, LANES))
            for g in range(ngrp):
                for j in range(g * per_grp, (g + 1) * per_grp):
                    row_copy(nxt, j, nslot).start(priority=j % PEER_DMA_QUEUES)
                parts = []
                for e_ in range(SUBLANES):
                    j = g * SUBLANES + e_
                    p = lax.bitcast_convert_type(tile(slot, j, 0) & hi_mask, F32) * xs[0]
                    for v in range(1, nv):
                        p = p + lax.bitcast_convert_type(tile(slot, j, v) & hi_mask, F32) * xs[v]
                    parts.append(p)
                f = SUBLANES
                while len(parts) > 1:
                    parts = [fold(x, y, f) for x, y in zip(parts[0::2], parts[1::2])]
                    f //= 2
                act = jnp.sum(parts[0], axis=1, keepdims=True)
                gcol = jnp.sum(jnp.where(lane == g * SUBLANES + order_vec, grow, 0.0), axis=1, keepdims=True)
                wscr[g * SUBLANES:(g + 1) * SUBLANES, :] = jnp.broadcast_to(gcol * _gelu_tanh(act),
                                                                             (SUBLANES, LANES))
            n_acc = 4
            accs = [[None] * nv for _ in range(n_acc)]
            for j in range(R):
                g, e_ = divmod(j, SUBLANES)
                row = g * SUBLANES + order.index(e_)
                wj = wscr[row:row + 1, :]
                for v in range(nv):
                    term = lax.bitcast_convert_type(tile(slot, j, v) << 16, F32) * wj
                    k = j % n_acc
                    accs[k][v] = term if accs[k][v] is None else accs[k][v] + term
            for v in range(nv):
                out_v = (accs[0][v] + accs[1][v]) + (accs[2][v] + accs[3][v])
                for s_ in range(SUBLANES):
                    c = v * SUBLANES + s_
                    stage[r:r + 1, c * LANES:(c + 1) * LANES] = out_v[s_:s_ + 1, :]
        rows = pl.ds(base, SUBLANES)
        o_ref[rows, :] = h_ref[rows, :] + g2_ref[0] * stage[...]
        return carry

    lax.fori_loop(0, tt // SUBLANES, group, 0)

    @pl.when(step == pl.num_programs(0) - 1)
    def _():
        for k in range(PEER_AHEAD):
            wait((tt + k) % PEER_SLOTS)


def _peer_gather(idx, gates, hn2, h2, g2, table, s, tt=128):
    n_tok, d = hn2.shape
    tt = min(tt, s)
    assert s % tt == 0 and tt % PEER_SLOTS == 0 and d % (SUBLANES * LANES) == 0 and PEER_SEL % SUBLANES == 0
    assert PEER_SEL <= LANES and PEER_AHEAD < PEER_SLOTS - 1
    per_b = s // tt
    R = PEER_SEL
    nsteps = n_tok // tt
    idx3 = idx.reshape(nsteps, tt, R)
    head = jnp.concatenate([idx3[1:, :SUBLANES], jnp.zeros((1, SUBLANES, R), I32)], axis=0)
    idx = jnp.concatenate([idx3, head], axis=1).reshape(nsteps * (tt + SUBLANES), R)
    return pl.pallas_call(
        functools.partial(_peer_gather_kernel, tt=tt),
        grid=(nsteps,),
        in_specs=[pl.BlockSpec((tt + SUBLANES, R), lambda i: (i, 0), memory_space=pltpu.SMEM),
                  pl.BlockSpec((tt, R), lambda i: (i, 0)),
                  pl.BlockSpec((tt, d), lambda i: (i, 0)),
                  pl.BlockSpec((tt, d), lambda i: (i, 0)),
                  pl.BlockSpec((1, 1, d), lambda i: (i // per_b, 0, 0)),
                  pl.BlockSpec(memory_space=pl.ANY)],
        out_specs=pl.BlockSpec((tt, d), lambda i: (i, 0)),
        out_shape=jax.ShapeDtypeStruct((n_tok, d), F32),
        scratch_shapes=[pltpu.VMEM((R * d // LANES, LANES), I32)] * PEER_SLOTS
        + [pltpu.VMEM((SUBLANES, d), F32), pltpu.VMEM((R, LANES), F32), pltpu.SemaphoreType.DMA((PEER_SLOTS,))],
        compiler_params=_cparams("arbitrary"),
        name="peer_gather",
    )(idx, gates, hn2, h2, g2.reshape(-1, 1, d), table)


def _peer_layer(h, g, sc, sh, g2, w_q, sub_keys, u_all, v_all, layer):
    bsz, s, d = h.shape
    q, hn = _nm_matmul(h, g, sc, sh, w_q.astype(BF16), tm=512, tn=1024, emit_hn=True, name="peer_query")
    n_tok = bsz * s
    idx, gates = _peer_topk(q.reshape(n_tok, -1), sub_keys)
    idx = idx.transpose(2, 0, 1).reshape(n_tok, PEER_SEL)
    gates = gates.transpose(2, 0, 1).reshape(n_tok, PEER_SEL)
    table = _pack_tables(u_all, v_all, layer)
    out = _peer_gather(idx, gates, hn.reshape(n_tok, d), h.reshape(n_tok, d), g2, table, s)
    return out.reshape(bsz, s, d)


def _rmsnorm_kernel(x_ref, g_ref, o_ref):
    x = x_ref[...]
    o_ref[...] = x * lax.rsqrt(jnp.mean(x * x, axis=-1, keepdims=True) + EPS) * g_ref[...]


def _final_norm(h, g, tm=512):
    bsz, s, d = h.shape
    n = bsz * s
    tm = min(tm, n)
    out = pl.pallas_call(
        _rmsnorm_kernel,
        grid=(n // tm,),
        in_specs=[pl.BlockSpec((tm, d), lambda i: (i, 0)), pl.BlockSpec((1, d), lambda i: (0, 0))],
        out_specs=pl.BlockSpec((tm, d), lambda i: (i, 0)),
        out_shape=jax.ShapeDtypeStruct((n, d), F32),
        compiler_params=_cparams("parallel"),
        name="final_norm",
    )(h.reshape(n, d), g.reshape(1, d))
    return out.reshape(bsz, s, d)


def _even_mixer(h, pos3, g, sc, sh, g1, layer, w_in, w_out, conv_w, conv_b, gate_b, mlstm_g, lam_vec, diff_g):
    lambda_init = 0.8 - 0.6 * math.exp(-0.3 * layer)
    d = h.shape[-1]
    cuts = np.cumsum([2 * A_QK, A_V, A_V, 2 * MLSTM_HEADS, B_QK, B_QK, B_V])
    n_gate = 2 * MLSTM_HEADS
    w_cat = jnp.concatenate([w_in[:, :cuts[2]], w_in[:, cuts[3]:], w_in[:, cuts[2]:cuts[3]],
                             jnp.zeros((d, LANES - n_gate), w_in.dtype)], axis=1).astype(BF16)
    proj = _nm_matmul(h, g, sc, sh, w_cat, tm=1024, tn=896, name="even_in_proj")
    h_a = _mlstm(proj, gate_b, conv_w, conv_b, mlstm_g)
    bq, bk, bv = _diff_prep(proj, pos3)
    h_b = _diff_attn(bq, bk, bv, lam_vec, diff_g, lambda_init)
    return _out_proj(h_a, h_b, w_out, h, g1)


def _odd_mixer(h, pos3, g, sc, sh, g1, w_in, w_out, ln_g, ln_b, sgu_w, sgu_b):
    proj = _nm_matmul(h, g, sc, sh, w_in.astype(BF16), tm=1024, tn=1024, name="odd_in_proj")
    cq, ck, cv, sel = _moba_prep(proj, pos3)
    h_c = _moba_attn(cq, ck, cv, sel)
    h_d = _sgu(proj, ln_g, ln_b, sgu_w, sgu_b)
    return _out_proj(h_c, h_d, w_out, h, g1)


def kernel(x, c, positions, mod_w, mod_b, norm_g, final_g, even_w_in, even_w_out, mlstm_conv_w, mlstm_conv_b, mlstm_gate_b, mlstm_head_g, diff_lambda, diff_head_g, odd_w_in, odd_w_out, sgu_ln_g, sgu_ln_b, sgu_w, sgu_b, peer_w_q, peer_sub_keys, peer_u, peer_v):
    depth = mod_w.shape[0]
    bsz, s, d = x.shape
    mod = _modulation(c, mod_w, mod_b)
    pos3 = positions.reshape(bsz, s, 1)
    h = x
    for layer in range(depth):
        sh1, sc1, g1, sh2, sc2, g2 = [mod[layer, :, i * d:(i + 1) * d] for i in range(6)]
        if layer % 2 == 0:
            e = layer // 2
            h = _even_mixer(h, pos3, norm_g[layer, 0], sc1, sh1, g1, layer, even_w_in[e], even_w_out[e],
                            mlstm_conv_w[e], mlstm_conv_b[e], mlstm_gate_b[e], mlstm_head_g[e],
                            diff_lambda[e], diff_head_g[e])
        else:
            o = layer // 2
            h = _odd_mixer(h, pos3, norm_g[layer, 0], sc1, sh1, g1, odd_w_in[o], odd_w_out[o],
                           sgu_ln_g[o], sgu_ln_b[o], sgu_w[o], sgu_b[o])
        h = _peer_layer(h, norm_g[layer, 1], sc2, sh2, g2, peer_w_q[layer], peer_sub_keys[layer],
                        peer_u, peer_v, layer)
    return _final_norm(h, final_g)
```

```python
import functools
import math

import numpy as np
import jax
import jax.numpy as jnp
from jax import lax
from jax.experimental import pallas as pl
from jax.experimental.pallas import tpu as pltpu

F32 = jnp.float32
BF16 = jnp.bfloat16
I32 = jnp.int32
HIGHEST = lax.Precision.HIGHEST

EPS = 1e-6
ROPE_THETA = 500000.0
LANES = 128
SUBLANES = 8
VMEM_LIMIT_BYTES = 48 * 1024 * 1024

MLSTM_HEADS, MLSTM_DQK, MLSTM_DV, MLSTM_CHUNK, CONV_W = 4, 128, 256, 128, 4
DIFF_HEADS, DIFF_DQK, DIFF_DV = 8, 64, 128
MOBA_HEADS, MOBA_DH, MOBA_BLOCK, MOBA_TOPK = 8, 128, 256, 3
SGU_GROUPS, SGU_CH, SGU_CHUNK = 8, 128, 128
PEER_HEADS, PEER_NKEYS, PEER_DQ, PEER_TOPK = 8, 128, 256, 16
PEER_SEL = PEER_HEADS * PEER_TOPK

A_QK = MLSTM_HEADS * MLSTM_DQK
A_V = MLSTM_HEADS * MLSTM_DV
B_QK = DIFF_HEADS * 2 * DIFF_DQK
B_V = DIFF_HEADS * DIFF_DV
C_W = MOBA_HEADS * MOBA_DH
D_W = SGU_GROUPS * SGU_CH


def _cparams(*sem):
    return pltpu.CompilerParams(dimension_semantics=sem, vmem_limit_bytes=VMEM_LIMIT_BYTES)


def _dot_nt(a, b, precision=None):
    return lax.dot_general(a, b, (((1,), (1,)), ((), ())), preferred_element_type=F32, precision=precision)


def _store_row(ref, row, val, cols=slice(None)):
    base = pl.multiple_of((row // SUBLANES) * SUBLANES, SUBLANES)
    blk = ref[pl.ds(base, SUBLANES), cols]
    r = lax.broadcasted_iota(I32, blk.shape, 0)
    ref[pl.ds(base, SUBLANES), cols] = jnp.where(r == row - base, val, blk)


ATTN_COLS = 128


def _flash_tiles_t(n_cols, score_fn, m_ref, l_ref, acc_ref):
    tiles = [slice(c0, c0 + ATTN_COLS) for c0 in range(0, n_cols, ATTN_COLS)]
    prev = [(m_ref[:, c], l_ref[:, c], acc_ref[:, c]) for c in tiles]
    scored = [score_fn(cols) for cols in tiles]
    soft = []
    for t, (s, vt) in enumerate(scored):
        m_new = jnp.maximum(prev[t][0], jnp.max(s, axis=0, keepdims=True))
        p = jnp.exp(s - m_new)
        soft.append((m_new, jnp.sum(p, axis=0, keepdims=True), p.astype(BF16)))
    new = []
    for t, (m_new, l_new, p) in enumerate(soft):
        m_prev, l_prev, acc_prev = prev[t]
        alpha = jnp.exp(m_prev - m_new)
        pv = jnp.dot(scored[t][1], p, preferred_element_type=F32)
        new.append((m_new, alpha * l_prev + l_new, alpha * acc_prev + pv))
    for cols, (m_new, l_new, acc_new) in zip(tiles, new):
        m_ref[:, cols] = m_new
        l_ref[:, cols] = l_new
        acc_ref[:, cols] = acc_new


def _sigmoid(x):
    return 1.0 / (1.0 + jnp.exp(-x))


def _silu(x):
    return x * _sigmoid(x)


def _gelu_tanh(x):
    return 0.5 * x * (1.0 + jnp.tanh(math.sqrt(2.0 / math.pi) * (x + 0.044715 * (x * x * x))))


def _log_sigmoid(x):
    return jnp.minimum(x, 0.0) - jnp.log1p(jnp.exp(-jnp.abs(x)))


def _mod_kernel(c_ref, w_ref, b_ref, o_ref):
    ca = _silu(c_ref[...])
    o_ref[0] = jnp.dot(ca, w_ref[0], preferred_element_type=F32, precision=HIGHEST) + b_ref[0]


def _modulation(c, mod_w, mod_b):
    depth, d, n = mod_w.shape
    bsz = c.shape[0]
    rows = 8
    cp = jnp.zeros((rows, d), F32).at[:bsz].set(c)
    tn = 1024
    out = pl.pallas_call(
        _mod_kernel,
        grid=(depth, n // tn),
        in_specs=[pl.BlockSpec((rows, d), lambda l, j: (0, 0)),
                  pl.BlockSpec((1, d, tn), lambda l, j: (l, 0, j)),
                  pl.BlockSpec((1, 1, tn), lambda l, j: (l, 0, j))],
        out_specs=pl.BlockSpec((1, rows, tn), lambda l, j: (l, 0, j)),
        out_shape=jax.ShapeDtypeStruct((depth, rows, n), F32),
        compiler_params=_cparams("parallel", "parallel"),
        name="modulation",
    )(cp, mod_w, mod_b.reshape(depth, 1, n))
    return out[:, :bsz]


def _nm_matmul_kernel(h_ref, g_ref, sc_ref, sh_ref, w_ref, *rest, emit_hn):
    if emit_hn:
        o_ref, hn_ref, hn_scr = rest
    else:
        o_ref, hn_scr = rest

    @pl.when(pl.program_id(2) == 0)
    def _():
        x = h_ref[0]
        y = x * lax.rsqrt(jnp.mean(x * x, axis=-1, keepdims=True) + EPS) * g_ref[...]
        y = y * (1.0 + sc_ref[0]) + sh_ref[0]
        hn_scr[...] = y.astype(BF16)
        if emit_hn:
            hn_ref[0] = y

    o_ref[0] = jnp.dot(hn_scr[...], w_ref[...], preferred_element_type=F32).astype(o_ref.dtype)


def _nm_matmul(h, g, sc, sh, w, *, tm, tn, emit_hn=False, name="nm_matmul"):
    bsz, s, d = h.shape
    n = w.shape[1]
    tm = min(tm, s)
    out_shape = [jax.ShapeDtypeStruct((bsz, s, n), F32)]
    out_specs = [pl.BlockSpec((1, tm, tn), lambda b, i, j: (b, i, j))]
    if emit_hn:
        out_shape.append(jax.ShapeDtypeStruct((bsz, s, d), F32))
        out_specs.append(pl.BlockSpec((1, tm, d), lambda b, i, j: (b, i, 0)))
    res = pl.pallas_call(
        functools.partial(_nm_matmul_kernel, emit_hn=emit_hn),
        grid=(bsz, s // tm, n // tn),
        in_specs=[pl.BlockSpec((1, tm, d), lambda b, i, j: (b, i, 0)),
                  pl.BlockSpec((1, d), lambda b, i, j: (0, 0)),
                  pl.BlockSpec((1, 1, d), lambda b, i, j: (b, 0, 0)),
                  pl.BlockSpec((1, 1, d), lambda b, i, j: (b, 0, 0)),
                  pl.BlockSpec((d, tn), lambda b, i, j: (0, j))],
        out_specs=out_specs,
        out_shape=out_shape,
        scratch_shapes=[pltpu.VMEM((tm, d), BF16)],
        compiler_params=_cparams("parallel", "parallel", "arbitrary"),
        name=name,
    )(h, g.reshape(1, d), sc.reshape(bsz, 1, d), sh.reshape(bsz, 1, d), w)
    return res if emit_hn else res[0]


def _out_proj_kernel(xa_ref, xb_ref, wa_ref, wb_ref, h_ref, g_ref, o_ref):
    acc = jnp.dot(xa_ref[0], wa_ref[...], preferred_element_type=F32)
    acc += jnp.dot(xb_ref[0], wb_ref[...], preferred_element_type=F32)
    o_ref[0] = h_ref[0] + g_ref[0] * acc


def _out_proj(xa, xb, w, h, g1, *, tm=1024, tn=1024):
    bsz, s, ka = xa.shape
    kb = xb.shape[2]
    n = w.shape[1]
    tm = min(tm, s)
    wa = w[:ka].astype(BF16)
    wb = w[ka:].astype(BF16)
    return pl.pallas_call(
        _out_proj_kernel,
        grid=(bsz, s // tm, n // tn),
        in_specs=[pl.BlockSpec((1, tm, ka), lambda b, i, j: (b, i, 0)),
                  pl.BlockSpec((1, tm, kb), lambda b, i, j: (b, i, 0)),
                  pl.BlockSpec((ka, tn), lambda b, i, j: (0, j)),
                  pl.BlockSpec((kb, tn), lambda b, i, j: (0, j)),
                  pl.BlockSpec((1, tm, tn), lambda b, i, j: (b, i, j)),
                  pl.BlockSpec((1, 1, tn), lambda b, i, j: (b, 0, j))],
        out_specs=pl.BlockSpec((1, tm, tn), lambda b, i, j: (b, i, j)),
        out_shape=jax.ShapeDtypeStruct((bsz, s, n), F32),
        compiler_params=_cparams("parallel", "parallel", "parallel"),
        name="out_proj",
    )(xa, xb, wa, wb, h, g1.reshape(bsz, 1, n))


def _mlstm_kernel(gb_ref, aqk_ref, v_ref, ao_ref, gcol_ref, cw_ref, cb_ref, hg_ref,
                  o_ref, xbuf, c_scr, n_scr, m_scr):
    L = MLSTM_CHUNK
    H, DQK, DV = MLSTM_HEADS, MLSTM_DQK, MLSTM_DV
    c = pl.program_id(1)

    @pl.when(c == 0)
    def _():
        xbuf[0:8, :] = jnp.zeros((8, 2 * A_QK), F32)
        c_scr[...] = jnp.zeros_like(c_scr)
        n_scr[...] = jnp.zeros_like(n_scr)
        m_scr[...] = jnp.zeros_like(m_scr)

    @pl.when(c > 0)
    def _():
        xbuf[0:8, :] = xbuf[L:L + 8, :]

    xbuf[8:L + 8, :] = aqk_ref[0]
    conv = cb_ref[...] + cw_ref[0:1, :] * xbuf[5:5 + L, :]
    for j in range(1, CONV_W):
        conv = conv + cw_ref[j:j + 1, :] * xbuf[5 + j:5 + j + L, :]
    qk = _silu(conv)

    r_i = lax.broadcasted_iota(I32, (L, L), 0)
    c_i = lax.broadcasted_iota(I32, (L, L), 1)
    tril = c_i <= r_i
    gcol = gcol_ref[0]
    grow = gcol.T

    for h in range(H):
        q = qk[:, h * DQK:(h + 1) * DQK] * (DQK ** -0.5)
        k = qk[:, A_QK + h * DQK:A_QK + (h + 1) * DQK]
        v = v_ref[0, :, h * DV:(h + 1) * DV]
        i_col = gcol[:, h:h + 1] + gb_ref[h]
        i_row = grow[h:h + 1, :] + gb_ref[h]
        lf_col = _log_sigmoid(gcol[:, H + h:H + h + 1] + gb_ref[H + h])
        lf_row = _log_sigmoid(grow[H + h:H + h + 1, :] + gb_ref[H + h])
        bcum_col = jnp.sum(jnp.where(tril, lf_row, 0.0), axis=1, keepdims=True)
        bcum_row = jnp.sum(jnp.where(r_i <= c_i, lf_col, 0.0), axis=0, keepdims=True)
        g_tot = jnp.sum(lf_row, axis=1, keepdims=True)
        m_prev = m_scr[h:h + 1, 0:1]
        d_intra = jnp.where(tril, bcum_col - bcum_row + i_row, -jnp.inf)
        d_inter = bcum_col + m_prev
        m_t = jnp.maximum(d_inter, jnp.max(d_intra, axis=1, keepdims=True))
        w_intra = jnp.exp(d_intra - m_t)
        w_inter = jnp.exp(d_inter - m_t)
        qb = q.astype(BF16)
        kb = k.astype(BF16)
        vb = v.astype(BF16)
        s = _dot_nt(qb, kb) * w_intra
        c_prev = c_scr[h]
        n_prev = n_scr[h:h + 1, :]
        num = jnp.dot(s.astype(BF16), vb, preferred_element_type=F32)
        num = num + w_inter * jnp.dot(qb, c_prev.astype(BF16), preferred_element_type=F32)
        den = jnp.sum(s, axis=1, keepdims=True) + w_inter * jnp.sum(q * n_prev, axis=1, keepdims=True)
        hh = num / jnp.maximum(jnp.abs(den), jnp.exp(-m_t))
        d_state = g_tot - bcum_col + i_col
        m_new = jnp.maximum(g_tot + m_prev, jnp.max(d_state, axis=0, keepdims=True))
        w_s = jnp.exp(d_state - m_new)
        w_c = jnp.exp(g_tot + m_prev - m_new)
        kw = k * w_s
        c_scr[h] = w_c * c_prev + jnp.dot(kw.T.astype(BF16), vb, preferred_element_type=F32)
        n_scr[h:h + 1, :] = w_c * n_prev + jnp.sum(kw, axis=0, keepdims=True)
        m_scr[h:h + 1, 0:1] = m_new
        y = hh * lax.rsqrt(jnp.mean(hh * hh, axis=-1, keepdims=True) + EPS) * hg_ref[:, h * DV:(h + 1) * DV]
        y = y * _sigmoid(ao_ref[0, :, h * DV:(h + 1) * DV])
        o_ref[0, :, h * DV:(h + 1) * DV] = y.astype(o_ref.dtype)


def _mlstm(proj, gate_b, conv_w, conv_b, head_g):
    bsz, s, _ = proj.shape
    L = MLSTM_CHUNK
    gate_blk = (2 * A_QK + 2 * A_V + 2 * B_QK + B_V) // LANES
    return pl.pallas_call(
        _mlstm_kernel,
        grid=(bsz, s // L),
        in_specs=[pl.BlockSpec(memory_space=pltpu.SMEM),
                  pl.BlockSpec((1, L, 2 * A_QK), lambda b, c: (b, c, 0)),
                  pl.BlockSpec((1, L, A_V), lambda b, c: (b, c, 1)),
                  pl.BlockSpec((1, L, A_V), lambda b, c: (b, c, 2)),
                  pl.BlockSpec((1, L, LANES), lambda b, c: (b, c, gate_blk)),
                  pl.BlockSpec((CONV_W, 2 * A_QK), lambda b, c: (0, 0)),
                  pl.BlockSpec((1, 2 * A_QK), lambda b, c: (0, 0)),
                  pl.BlockSpec((1, A_V), lambda b, c: (0, 0))],
        out_specs=pl.BlockSpec((1, L, A_V), lambda b, c: (b, c, 0)),
        out_shape=jax.ShapeDtypeStruct((bsz, s, A_V), BF16),
        scratch_shapes=[pltpu.VMEM((L + 8, 2 * A_QK), F32),
                        pltpu.VMEM((MLSTM_HEADS, MLSTM_DQK, MLSTM_DV), F32),
                        pltpu.VMEM((8, MLSTM_DQK), F32),
                        pltpu.VMEM((8, LANES), F32)],
        compiler_params=_cparams("parallel", "arbitrary"),
        name="mlstm",
    )(gate_b.reshape(-1), proj, proj, proj, proj, conv_w, conv_b.reshape(1, -1), head_g.reshape(1, -1))


def _rope_consts(dh):
    half = dh // 8
    lane = np.arange(LANES) % dh
    inv_half = ROPE_THETA ** (-jnp.arange(half, dtype=F32) / half)
    inv = jnp.where(jnp.asarray(lane < 2 * half), inv_half[lane % half], 0.0)
    lo = (lane < half).astype(np.float32)
    hi = ((lane >= half) & (lane < 2 * half)).astype(np.float32)
    return (inv.astype(F32).reshape(1, LANES), jnp.asarray(lo).reshape(1, LANES),
            jnp.asarray(hi).reshape(1, LANES))


def _rope_tables(pos_ref, inv_ref, lo_ref, hi_ref):
    ang = pos_ref[0].astype(F32) * inv_ref[...]
    cs = jnp.cos(ang)
    sn = jnp.sin(ang)
    return cs, -sn * lo_ref[...], sn * hi_ref[...]


def _rope_apply(x, cs, sa, sb, half):
    return x * cs + pltpu.roll(x, LANES - half, 1) * sa + pltpu.roll(x, half, 1) * sb


def _diff_prep_kernel(pos_ref, inv_ref, lo_ref, hi_ref, q_ref, k_ref, v_ref, qo_ref, ko_ref, vo_ref):
    cs, sa, sb = _rope_tables(pos_ref, inv_ref, lo_ref, hi_ref)
    half = DIFF_DQK // 8
    scale = DIFF_DQK ** -0.5
    for h in range(DIFF_HEADS):
        sl = slice(h * LANES, (h + 1) * LANES)
        qo_ref[0, sl, :] = (_rope_apply(q_ref[0, :, sl], cs, sa, sb, half) * scale).T.astype(BF16)
        ko_ref[0, :, sl] = _rope_apply(k_ref[0, :, sl], cs, sa, sb, half).astype(BF16)
        vo_ref[0, sl, :] = v_ref[0, :, sl].T.astype(BF16)


def _diff_prep(proj, pos3, ts=512):
    bsz, s, _ = proj.shape
    ts = min(ts, s)
    inv, lo, hi = _rope_consts(DIFF_DQK)
    base = (2 * A_QK + 2 * A_V) // B_QK
    cst = pl.BlockSpec((1, LANES), lambda b, i: (0, 0))
    osd = jax.ShapeDtypeStruct((bsz, s, B_QK), BF16)
    osd_t = jax.ShapeDtypeStruct((bsz, B_QK, s), BF16)
    spec_t = pl.BlockSpec((1, B_QK, ts), lambda b, i: (b, 0, i))
    return pl.pallas_call(
        _diff_prep_kernel,
        grid=(bsz, s // ts),
        in_specs=[pl.BlockSpec((1, ts, 1), lambda b, i: (b, i, 0)), cst, cst, cst,
                  pl.BlockSpec((1, ts, B_QK), lambda b, i: (b, i, base)),
                  pl.BlockSpec((1, ts, B_QK), lambda b, i: (b, i, base + 1)),
                  pl.BlockSpec((1, ts, B_V), lambda b, i: (b, i, base + 2))],
        out_specs=[spec_t, pl.BlockSpec((1, ts, B_QK), lambda b, i: (b, i, 0)), spec_t],
        out_shape=[osd_t, osd, osd_t],
        compiler_params=_cparams("parallel", "parallel"),
        name="diff_prep",
    )(pos3, inv, lo, hi, proj, proj, proj)


def _causal_pairs(nq):
    pairs = [(i, j) for i in range(nq) for j in range(i + 1)]
    return (jnp.asarray([p[0] for p in pairs], I32), jnp.asarray([p[1] for p in pairs], I32))


def _diff_attn_kernel(it_ref, jt_ref, lam_ref, q_ref, k_ref, v_ref, g_ref, o_ref, qs_scr, m_scr, l_scr, acc_scr,
                      *, tq, tk, lambda_init):
    i = it_ref[pl.program_id(2)]
    j = jt_ref[pl.program_id(2)]

    @pl.when(j == 0)
    def _():
        q = q_ref[0]
        feat = lax.broadcasted_iota(I32, q.shape, 0)
        zero = jnp.zeros_like(q)
        qs_scr[:, 0:tq] = jnp.where(feat < DIFF_DQK, q, zero)
        qs_scr[:, tq:2 * tq] = jnp.where(feat >= DIFF_DQK, q, zero)
        m_scr[...] = jnp.full_like(m_scr, -jnp.inf)
        l_scr[...] = jnp.zeros_like(l_scr)
        acc_scr[...] = jnp.zeros_like(acc_scr)

    def step(masked):
        def scores(cols):
            s = jnp.dot(k_ref[0], qs_scr[:, cols], preferred_element_type=F32)
            if masked:
                key = lax.broadcasted_iota(I32, (tk, ATTN_COLS), 0)
                qry = lax.broadcasted_iota(I32, (tk, ATTN_COLS), 1) + (cols.start % tq)
                s = jnp.where(key <= qry, s, -jnp.inf)
            return s, v_ref[0]
        _flash_tiles_t(2 * tq, scores, m_scr, l_scr, acc_scr)

    @pl.when(j < i)
    def _():
        step(False)

    @pl.when(j == i)
    def _():
        step(True)
        lv = lam_ref[...]
        lam = (jnp.exp(jnp.sum(lv[0:1] * lv[1:2], axis=1, keepdims=True))
               - jnp.exp(jnp.sum(lv[2:3] * lv[3:4], axis=1, keepdims=True)) + lambda_init)
        o0 = acc_scr[:, 0:tq] / l_scr[:, 0:tq]
        o1 = acc_scr[:, tq:2 * tq] / l_scr[:, tq:2 * tq]
        a = o0 - lam * o1
        y = a * lax.rsqrt(jnp.mean(a * a, axis=0, keepdims=True) + EPS) * g_ref[0]
        o_ref[0] = (y * (1.0 - lambda_init)).T.astype(o_ref.dtype)


def _diff_attn(qt, k, vt, lam_vec, head_g, lambda_init, *, tq=512):
    bsz, s, _ = k.shape
    tq = min(tq, s)
    it, jt = _causal_pairs(s // tq)
    grid_spec = pltpu.PrefetchScalarGridSpec(
        num_scalar_prefetch=2,
        grid=(bsz, DIFF_HEADS, it.shape[0]),
        in_specs=[pl.BlockSpec((4, DIFF_DQK), lambda b, h, t, it, jt: (0, 0)),
                  pl.BlockSpec((1, LANES, tq), lambda b, h, t, it, jt: (b, h, it[t])),
                  pl.BlockSpec((1, tq, LANES), lambda b, h, t, it, jt: (b, jt[t], h)),
                  pl.BlockSpec((1, DIFF_DV, tq), lambda b, h, t, it, jt: (b, h, jt[t])),
                  pl.BlockSpec((1, DIFF_DV, 1), lambda b, h, t, it, jt: (h, 0, 0))],
        out_specs=pl.BlockSpec((1, tq, DIFF_DV), lambda b, h, t, it, jt: (b, it[t], h)),
        scratch_shapes=[pltpu.VMEM((LANES, 2 * tq), BF16),
                        pltpu.VMEM((1, 2 * tq), F32),
                        pltpu.VMEM((1, 2 * tq), F32),
                        pltpu.VMEM((DIFF_DV, 2 * tq), F32)])
    return pl.pallas_call(
        functools.partial(_diff_attn_kernel, tq=tq, tk=tq, lambda_init=lambda_init),
        grid_spec=grid_spec,
        out_shape=jax.ShapeDtypeStruct((bsz, s, B_V), BF16),
        compiler_params=_cparams("parallel", "parallel", "arbitrary"),
        name="diff_attn",
    )(it, jt, lam_vec, qt, k, vt, head_g.reshape(DIFF_HEADS, DIFF_DV, 1))


def _moba_prep_kernel(pos_ref, inv_ref, lo_ref, hi_ref, q_ref, k_ref, v_ref,
                      qo_ref, ko_ref, vo_ref, sel_ref, kmt_scr, qf_scr, *, nb):
    i = pl.program_id(1)
    tb = MOBA_BLOCK
    half = MOBA_DH // 8
    scale = MOBA_DH ** -0.5

    @pl.when(i == 0)
    def _():
        kmt_scr[...] = jnp.zeros_like(kmt_scr)

    cs, sa, sb = _rope_tables(pos_ref, inv_ref, lo_ref, hi_ref)
    kmeans = []
    for h in range(MOBA_HEADS):
        sl = slice(h * LANES, (h + 1) * LANES)
        qr = _rope_apply(q_ref[0, :, sl], cs, sa, sb, half)
        kr = _rope_apply(k_ref[0, :, sl], cs, sa, sb, half)
        qf_scr[:, sl] = qr
        qo_ref[0, sl, :] = (qr * scale).T.astype(BF16)
        ko_ref[0, :, sl] = kr.astype(BF16)
        vo_ref[0, sl, :] = v_ref[0, :, sl].T.astype(BF16)
        kmeans.append(jnp.mean(kr, axis=0, keepdims=True))

    gate = _dot_nt(qf_scr[...], kmt_scr[...], precision=HIGHEST)
    lane = lax.broadcasted_iota(I32, (tb, LANES), 1)
    n = lane % nb
    valid = n < i
    gm = jnp.where(valid, gate, -jnp.inf)
    rank = jnp.zeros((tb, LANES), I32)
    for d in range(1, nb):
        wraps = (n + d) >= nb
        partner = jnp.where(wraps, pltpu.roll(gm, nb - d, 1), pltpu.roll(gm, LANES - d, 1))
        pn = jnp.where(wraps, n + d - nb, n + d)
        beats = (partner > gm) | ((partner == gm) & (pn < n))
        rank = rank + beats.astype(I32)
    sel_ref[0] = jnp.where(valid & (rank < MOBA_TOPK), 1.0, 0.0).astype(F32).T

    for h in range(MOBA_HEADS):
        _store_row(kmt_scr, h * nb + i, kmeans[h], slice(h * LANES, (h + 1) * LANES))


def _moba_prep(proj, pos3):
    bsz, s, _ = proj.shape
    tb = MOBA_BLOCK
    nb = s // tb
    assert s % tb == 0 and MOBA_HEADS * nb <= LANES and nb >= MOBA_TOPK
    inv, lo, hi = _rope_consts(MOBA_DH)
    cst = pl.BlockSpec((1, LANES), lambda b, i: (0, 0))
    osd = jax.ShapeDtypeStruct((bsz, s, C_W), BF16)
    osd_t = jax.ShapeDtypeStruct((bsz, C_W, s), BF16)
    spec_t = pl.BlockSpec((1, C_W, tb), lambda b, i: (b, 0, i))
    return pl.pallas_call(
        functools.partial(_moba_prep_kernel, nb=nb),
        grid=(bsz, nb),
        in_specs=[pl.BlockSpec((1, tb, 1), lambda b, i: (b, i, 0)), cst, cst, cst,
                  pl.BlockSpec((1, tb, C_W), lambda b, i: (b, i, 0)),
                  pl.BlockSpec((1, tb, C_W), lambda b, i: (b, i, 1)),
                  pl.BlockSpec((1, tb, C_W), lambda b, i: (b, i, 2))],
        out_specs=[spec_t, pl.BlockSpec((1, tb, C_W), lambda b, i: (b, i, 0)), spec_t,
                   pl.BlockSpec((1, LANES, tb), lambda b, i: (b, 0, i))],
        out_shape=[osd_t, osd, osd_t, jax.ShapeDtypeStruct((bsz, LANES, s), F32)],
        scratch_shapes=[pltpu.VMEM((LANES, C_W), F32), pltpu.VMEM((tb, C_W), F32)],
        compiler_params=_cparams("parallel", "arbitrary"),
        name="moba_prep",
    )(pos3, inv, lo, hi, proj, proj, proj)


def _moba_attn_kernel(it_ref, jt_ref, q_ref, k_ref, v_ref, sel_ref, o_ref, m_scr, l_scr, acc_scr, *, nb):
    h = pl.program_id(1)
    i = it_ref[pl.program_id(2)]
    j = jt_ref[pl.program_id(2)]
    tb = MOBA_BLOCK
    tq = 2 * tb

    def pick_bias(cols, blk):
        row = h * nb + blk
        base = pl.multiple_of((row // SUBLANES) * SUBLANES, SUBLANES)
        grp = sel_ref[0, pl.ds(base, SUBLANES), cols]
        sub = lax.broadcasted_iota(I32, grp.shape, 0)
        hit = jnp.max(jnp.where(sub == row - base, grp, 0.0), axis=0, keepdims=True)
        return jnp.where(hit > 0.5, 0.0, -jnp.inf)

    @pl.when(j == 0)
    def _():
        m_scr[...] = jnp.full_like(m_scr, -jnp.inf)
        l_scr[...] = jnp.zeros_like(l_scr)
        acc_scr[...] = jnp.zeros_like(acc_scr)

        def scores(cols):
            c0 = cols.start
            if c0 < tb:
                key = lax.broadcasted_iota(I32, (tb, ATTN_COLS), 0)
                qry = lax.broadcasted_iota(I32, (tb, ATTN_COLS), 1) + c0
                s = jnp.dot(k_ref[0, 0:tb, :], q_ref[0, :, cols], preferred_element_type=F32)
                return jnp.where(key <= qry, s, -jnp.inf), v_ref[0, :, 0:tb]
            key = lax.broadcasted_iota(I32, (tq, ATTN_COLS), 0)
            qry = lax.broadcasted_iota(I32, (tq, ATTN_COLS), 1) + c0
            s = jnp.dot(k_ref[0], q_ref[0, :, cols], preferred_element_type=F32)
            s = jnp.where(key < tb, s + pick_bias(cols, 2 * i), jnp.where(key <= qry, s, -jnp.inf))
            return s, v_ref[0]
        _flash_tiles_t(tq, scores, m_scr, l_scr, acc_scr)

    @pl.when((j > 0) & (j <= i))
    def _():
        kb = 2 * (i - j)
        key = lax.broadcasted_iota(I32, (tq, ATTN_COLS), 0)

        def scores(cols):
            bias = jnp.where(key < tb, pick_bias(cols, kb), pick_bias(cols, kb + 1))
            return jnp.dot(k_ref[0], q_ref[0, :, cols], preferred_element_type=F32) + bias, v_ref[0]
        _flash_tiles_t(tq, scores, m_scr, l_scr, acc_scr)

    @pl.when(j == i)
    def _():
        o_ref[0] = (acc_scr[...] / l_scr[...]).T.astype(o_ref.dtype)


def _moba_attn(qt, k, vt, sel_t):
    bsz, s, _ = k.shape
    tq = 2 * MOBA_BLOCK
    nb = s // MOBA_BLOCK
    assert s % tq == 0
    it, jt = _causal_pairs(s // tq)
    grid_spec = pltpu.PrefetchScalarGridSpec(
        num_scalar_prefetch=2,
        grid=(bsz, MOBA_HEADS, it.shape[0]),
        in_specs=[pl.BlockSpec((1, MOBA_DH, tq), lambda b, h, t, it, jt: (b, h, it[t])),
                  pl.BlockSpec((1, tq, MOBA_DH), lambda b, h, t, it, jt: (b, it[t] - jt[t], h)),
                  pl.BlockSpec((1, MOBA_DH, tq), lambda b, h, t, it, jt: (b, h, it[t] - jt[t])),
                  pl.BlockSpec((1, LANES, tq), lambda b, h, t, it, jt: (b, 0, it[t]))],
        out_specs=pl.BlockSpec((1, tq, MOBA_DH), lambda b, h, t, it, jt: (b, it[t], h)),
        scratch_shapes=[pltpu.VMEM((1, tq), F32), pltpu.VMEM((1, tq), F32), pltpu.VMEM((MOBA_DH, tq), F32)])
    return pl.pallas_call(
        functools.partial(_moba_attn_kernel, nb=nb),
        grid_spec=grid_spec,
        out_shape=jax.ShapeDtypeStruct((bsz, s, C_W), BF16),
        compiler_params=_cparams("parallel", "parallel", "arbitrary"),
        name="moba_attn",
    )(it, jt, qt, k, vt, sel_t)


def _sgu_kernel(u_ref, v_ref, lng_ref, lnb_ref, w_ref, bt_ref, o_ref, *, ts):
    L = SGU_CHUNK
    r_i = lax.broadcasted_iota(I32, (L, L), 0)
    c_i = lax.broadcasted_iota(I32, (L, L), 1)
    tril = c_i <= r_i
    for g in range(SGU_GROUPS):
        sl = slice(g * SGU_CH, (g + 1) * SGU_CH)
        w = jnp.where(tril, w_ref[g], 0.0).astype(BF16)
        bias = bt_ref[:, g:g + 1]
        for c in range(ts // L):
            rows = slice(c * L, (c + 1) * L)
            vv = _gelu_tanh(v_ref[0, rows, sl])
            mu = jnp.mean(vv, axis=-1, keepdims=True)
            dv = vv - mu
            var = jnp.mean(dv * dv, axis=-1, keepdims=True)
            vn = dv * lax.rsqrt(var + EPS) * lng_ref[:, sl] + lnb_ref[:, sl]
            mixed = jnp.dot(w, vn.astype(BF16), preferred_element_type=F32) + bias
            o_ref[0, rows, sl] = (_gelu_tanh(u_ref[0, rows, sl]) * mixed).astype(o_ref.dtype)


def _sgu(proj, ln_g, ln_b, w_s, b_s, ts=256):
    bsz, s, _ = proj.shape
    ts = min(ts, s)
    base = 3 * C_W // D_W
    return pl.pallas_call(
        functools.partial(_sgu_kernel, ts=ts),
        grid=(bsz, s // ts),
        in_specs=[pl.BlockSpec((1, ts, D_W), lambda b, i: (b, i, base)),
                  pl.BlockSpec((1, ts, D_W), lambda b, i: (b, i, base + 1)),
                  pl.BlockSpec((1, D_W), lambda b, i: (0, 0)),
                  pl.BlockSpec((1, D_W), lambda b, i: (0, 0)),
                  pl.BlockSpec((SGU_GROUPS, SGU_CHUNK, SGU_CHUNK), lambda b, i: (0, 0, 0)),
                  pl.BlockSpec((SGU_CHUNK, SGU_GROUPS), lambda b, i: (0, 0))],
        out_specs=pl.BlockSpec((1, ts, D_W), lambda b, i: (b, i, 0)),
        out_shape=jax.ShapeDtypeStruct((bsz, s, D_W), BF16),
        compiler_params=_cparams("parallel", "parallel"),
        name="sgu",
    )(proj, proj, ln_g.reshape(1, -1), ln_b.reshape(1, -1), w_s, b_s.T)


def _topk_rows(jobs, k):
    def body(it, carry):
        for s_scr, val_scr, idx_scr, payload_scr in jobs:
            n, t = s_scr.shape
            row = lax.broadcasted_iota(I32, (n, t), 0).astype(F32)
            s = s_scr[...]
            m = jnp.max(s, axis=0, keepdims=True)
            pos = jnp.min(jnp.where(s == m, row, float(n)), axis=0, keepdims=True)
            hit = row == pos
            _store_row(val_scr, it, m)
            if payload_scr is None:
                _store_row(idx_scr, it, pos)
            else:
                _store_row(idx_scr, it, jnp.max(jnp.where(hit, payload_scr[...], -1.0), axis=0, keepdims=True))
            s_scr[...] = jnp.where(hit, -jnp.inf, s)
        return carry

    for _, val_scr, idx_scr, _ in jobs:
        val_scr[...] = jnp.zeros_like(val_scr)
        idx_scr[...] = jnp.zeros_like(idx_scr)
    lax.fori_loop(0, k, body, 0)


def _peer_topk_kernel(q_ref, keys_ref, idx_ref, g_ref, s1_scr, s2_scr, v1, i1, v2, i2, cand, cidx, vt, it):
    K = PEER_TOPK
    half = PEER_DQ // 2
    for c, s_scr in enumerate((s1_scr, s2_scr)):
        s_scr[...] = _dot_nt(keys_ref[0, c], q_ref[:, c * half:(c + 1) * half], precision=HIGHEST)
    _topk_rows([(s1_scr, v1, i1, None), (s2_scr, v2, i2, None)], K)
    cand[...] = jnp.full(cand.shape, -jnp.inf, F32)
    cidx[...] = jnp.zeros(cidx.shape, F32)
    off = 0
    for a in range(K):
        nb = K // (a + 1)
        cand[off:off + nb, :] = v1[a:a + 1, :] + v2[0:nb, :]
        cidx[off:off + nb, :] = i1[a:a + 1, :] * float(PEER_NKEYS) + i2[0:nb, :]
        off += nb
    _topk_rows([(cand, vt, it, cidx)], K)
    ts = vt[...]
    e = jnp.exp(ts - jnp.max(ts, axis=0, keepdims=True))
    g_ref[0] = e / jnp.sum(e, axis=0, keepdims=True)
    idx_ref[0] = it[...].astype(I32)


def _peer_topk(q2, sub_keys, tt=512):
    n_tok = q2.shape[0]
    tt = min(tt, n_tok)
    K = PEER_TOPK
    half = PEER_DQ // 2
    vs = lambda r, dt: pltpu.VMEM((r, tt), dt)
    n_cand = sum(K // (a + 1) for a in range(K))
    n_cand = -(-n_cand // SUBLANES) * SUBLANES
    return pl.pallas_call(
        _peer_topk_kernel,
        grid=(n_tok // tt, PEER_HEADS),
        in_specs=[pl.BlockSpec((tt, PEER_DQ), lambda i, h: (i, h)),
                  pl.BlockSpec((1, 2, PEER_NKEYS, half), lambda i, h: (h, 0, 0, 0))],
        out_specs=[pl.BlockSpec((1, K, tt), lambda i, h: (h, 0, i)),
                   pl.BlockSpec((1, K, tt), lambda i, h: (h, 0, i))],
        out_shape=[jax.ShapeDtypeStruct((PEER_HEADS, K, n_tok), I32),
                   jax.ShapeDtypeStruct((PEER_HEADS, K, n_tok), F32)],
        scratch_shapes=[vs(PEER_NKEYS, F32), vs(PEER_NKEYS, F32), vs(K, F32), vs(K, F32), vs(K, F32), vs(K, F32),
                        vs(n_cand, F32), vs(n_cand, F32), vs(K, F32), vs(K, F32)],
        compiler_params=_cparams("parallel", "parallel"),
        name="peer_topk",
    )(q2, sub_keys)


def _pack_kernel(u_ref, v_ref, o_ref):
    ub = lax.bitcast_convert_type(u_ref[...].astype(BF16).astype(F32), I32)
    vb = lax.bitcast_convert_type(v_ref[...].astype(BF16).astype(F32), I32)
    o_ref[...] = (ub & jnp.int32(-65536)) | lax.shift_right_logical(vb, jnp.int32(16))


def _pack_tables(u_all, v_all, layer, tm=512):
    _, n, d = u_all.shape
    spec = pl.BlockSpec((None, tm, d), lambda i: (layer, i, 0))
    packed = pl.pallas_call(
        _pack_kernel,
        grid=(n // tm,),
        in_specs=[spec, spec],
        out_specs=pl.BlockSpec((tm, d), lambda i: (i, 0)),
        out_shape=jax.ShapeDtypeStruct((n, d), I32),
        compiler_params=_cparams("parallel"),
        name="peer_pack",
    )(u_all, v_all)
    return packed.reshape(n, d // LANES, LANES)


PEER_SLOTS = SUBLANES
PEER_AHEAD = 6
PEER_DMA_QUEUES = 2


def _fold_order():
    tiles = [[e] * SUBLANES for e in range(SUBLANES)]
    f = SUBLANES
    while len(tiles) > 1:
        tiles = [[x[s] if (s % f) < f // 2 else y[s] for s in range(SUBLANES)]
                 for x, y in zip(tiles[0::2], tiles[1::2])]
        f //= 2
    return tiles[0]


def _peer_gather_kernel(idx_ref, gate_ref, x_ref, h_ref, g2_ref, tab_hbm, o_ref, *rest, tt):
    R = PEER_SEL
    bufs, (stage, wscr, sem) = rest[:PEER_SLOTS], rest[PEER_SLOTS:]
    step = pl.program_id(0)
    d = x_ref.shape[1]
    nch = d // LANES
    nv = nch // SUBLANES
    ngrp = R // SUBLANES
    per_grp = R // ngrp
    order = _fold_order()

    def row_copy(t, j, slot):
        return pltpu.make_async_copy(tab_hbm.at[idx_ref[t, j]], bufs[slot].at[pl.ds(j * nch, nch)], sem.at[slot])

    def wait(slot):
        pltpu.make_async_copy(bufs[(slot + 1) % PEER_SLOTS], bufs[slot], sem.at[slot]).wait()

    def tile(slot, j, v):
        return bufs[slot][(j * nv + v) * SUBLANES:(j * nv + v + 1) * SUBLANES, :]

    lane = lax.broadcasted_iota(I32, (SUBLANES, LANES), 1)
    sub = lax.broadcasted_iota(I32, (SUBLANES, LANES), 0)
    hi_mask = jnp.int32(-65536)
    fold_masks = {f: (sub % f) < f // 2 for f in (8, 4, 2)}
    order_vec = jnp.zeros((SUBLANES, LANES), I32)
    for s_, e_ in enumerate(order):
        order_vec = jnp.where(sub == s_, e_, order_vec)

    def fold(x, y, f):
        m = fold_masks[f]
        if f == SUBLANES:
            return jnp.where(m, x, y) + pltpu.roll(jnp.where(m, y, x), f // 2, 0)
        return (jnp.where(m, x, pltpu.roll(y, f // 2, 0))
                + jnp.where(m, pltpu.roll(x, SUBLANES - f // 2, 0), y))

    @pl.when(step == 0)
    def _():
        for t0 in range(PEER_AHEAD):
            def prime(j, carry, t0=t0):
                row_copy(t0, j, t0).start()
                return carry
            lax.fori_loop(0, R, prime, 0)

    def group(gi, carry):
        base = pl.multiple_of(gi * SUBLANES, SUBLANES)
        xg = x_ref[pl.ds(base, SUBLANES), :]
        gg = gate_ref[pl.ds(base, SUBLANES), :]
        for r in range(SUBLANES):
            slot = r
            nxt = base + r + PEER_AHEAD
            nslot = (r + PEER_AHEAD) % PEER_SLOTS
            wait(slot)
            xs = []
            for v in range(nv):
                xv = jnp.zeros((SUBLANES, LANES), F32)
                for s_ in range(SUBLANES):
                    c = v * SUBLANES + s_
                    xv = jnp.where(sub == s_, jnp.broadcast_to(xg[r:r + 1, c * LANES:(c + 1) * LANES],
                                                                (SUBLANES, LANES)), xv)
                xs.append(xv)
            grow = jnp.broadcast_to(gg[r:r + 1, :], (SUBLANES, LANES))
            for g in range(ngrp):
                for j in range(g * per_grp, (g + 1) * per_grp):
                    row_copy(nxt, j, nslot).start(priority=j % PEER_DMA_QUEUES)
                parts = []
                for e_ in range(SUBLANES):
                    j = g * SUBLANES + e_
                    p = lax.bitcast_convert_type(tile(slot, j, 0) & hi_mask, F32) * xs[0]
                    for v in range(1, nv):
                        p = p + lax.bitcast_convert_type(tile(slot, j, v) & hi_mask, F32) * xs[v]
                    parts.append(p)
                f = SUBLANES
                while len(parts) > 1:
                    parts = [fold(x, y, f) for x, y in zip(parts[0::2], parts[1::2])]
                    f //= 2
                act = jnp.sum(parts[0], axis=1, keepdims=True)
                gcol = jnp.sum(jnp.where(lane == g * SUBLANES + order_vec, grow, 0.0), axis=1, keepdims=True)
                wscr[g * SUBLANES:(g + 1) * SUBLANES, :] = jnp.broadcast_to(gcol * _gelu_tanh(act),
                                                                             (SUBLANES, LANES))
            n_acc = 4
            accs = [[None] * nv for _ in range(n_acc)]
            for j in range(R):
                g, e_ = divmod(j, SUBLANES)
                row = g * SUBLANES + order.index(e_)
                wj = wscr[row:row + 1, :]
                for v in range(nv):
                    term = lax.bitcast_convert_type(tile(slot, j, v) << 16, F32) * wj
                    k = j % n_acc
                    accs[k][v] = term if accs[k][v] is None else accs[k][v] + term
            for v in range(nv):
                out_v = (accs[0][v] + accs[1][v]) + (accs[2][v] + accs[3][v])
                for s_ in range(SUBLANES):
                    c = v * SUBLANES + s_
                    stage[r:r + 1, c * LANES:(c + 1) * LANES] = out_v[s_:s_ + 1, :]
        rows = pl.ds(base, SUBLANES)
        o_ref[rows, :] = h_ref[rows, :] + g2_ref[0] * stage[...]
        return carry

    lax.fori_loop(0, tt // SUBLANES, group, 0)

    @pl.when(step == pl.num_programs(0) - 1)
    def _():
        for k in range(PEER_AHEAD):
            wait((tt + k) % PEER_SLOTS)


def _peer_gather(idx, gates, hn2, h2, g2, table, s, tt=128):
    n_tok, d = hn2.shape
    tt = min(tt, s)
    assert s % tt == 0 and tt % PEER_SLOTS == 0 and d % (SUBLANES * LANES) == 0 and PEER_SEL % SUBLANES == 0
    assert PEER_SEL <= LANES and PEER_AHEAD < PEER_SLOTS - 1
    per_b = s // tt
    R = PEER_SEL
    nsteps = n_tok // tt
    idx3 = idx.reshape(nsteps, tt, R)
    head = jnp.concatenate([idx3[1:, :SUBLANES], jnp.zeros((1, SUBLANES, R), I32)], axis=0)
    idx = jnp.concatenate([idx3, head], axis=1).reshape(nsteps * (tt + SUBLANES), R)
    return pl.pallas_call(
        functools.partial(_peer_gather_kernel, tt=tt),
        grid=(nsteps,),
        in_specs=[pl.BlockSpec((tt + SUBLANES, R), lambda i: (i, 0), memory_space=pltpu.SMEM),
                  pl.BlockSpec((tt, R), lambda i: (i, 0)),
                  pl.BlockSpec((tt, d), lambda i: (i, 0)),
                  pl.BlockSpec((tt, d), lambda i: (i, 0)),
                  pl.BlockSpec((1, 1, d), lambda i: (i // per_b, 0, 0)),
                  pl.BlockSpec(memory_space=pl.ANY)],
        out_specs=pl.BlockSpec((tt, d), lambda i: (i, 0)),
        out_shape=jax.ShapeDtypeStruct((n_tok, d), F32),
        scratch_shapes=[pltpu.VMEM((R * d // LANES, LANES), I32)] * PEER_SLOTS
        + [pltpu.VMEM((SUBLANES, d), F32), pltpu.VMEM((R, LANES), F32), pltpu.SemaphoreType.DMA((PEER_SLOTS,))],
        compiler_params=_cparams("arbitrary"),
        name="peer_gather",
    )(idx, gates, hn2, h2, g2.reshape(-1, 1, d), table)


def _peer_layer(h, g, sc, sh, g2, w_q, sub_keys, u_all, v_all, layer):
    bsz, s, d = h.shape
    q, hn = _nm_matmul(h, g, sc, sh, w_q.astype(BF16), tm=512, tn=1024, emit_hn=True, name="peer_query")
    n_tok = bsz * s
    idx, gates = _peer_topk(q.reshape(n_tok, -1), sub_keys)
    idx = idx.transpose(2, 0, 1).reshape(n_tok, PEER_SEL)
    gates = gates.transpose(2, 0, 1).reshape(n_tok, PEER_SEL)
    table = _pack_tables(u_all, v_all, layer)
    out = _peer_gather(idx, gates, hn.reshape(n_tok, d), h.reshape(n_tok, d), g2, table, s)
    return out.reshape(bsz, s, d)


def _rmsnorm_kernel(x_ref, g_ref, o_ref):
    x = x_ref[...]
    o_ref[...] = x * lax.rsqrt(jnp.mean(x * x, axis=-1, keepdims=True) + EPS) * g_ref[...]


def _final_norm(h, g, tm=512):
    bsz, s, d = h.shape
    n = bsz * s
    tm = min(tm, n)
    out = pl.pallas_call(
        _rmsnorm_kernel,
        grid=(n // tm,),
        in_specs=[pl.BlockSpec((tm, d), lambda i: (i, 0)), pl.BlockSpec((1, d), lambda i: (0, 0))],
        out_specs=pl.BlockSpec((tm, d), lambda i: (i, 0)),
        out_shape=jax.ShapeDtypeStruct((n, d), F32),
        compiler_params=_cparams("parallel"),
        name="final_norm",
    )(h.reshape(n, d), g.reshape(1, d))
    return out.reshape(bsz, s, d)


def _even_mixer(h, pos3, g, sc, sh, g1, layer, w_in, w_out, conv_w, conv_b, gate_b, mlstm_g, lam_vec, diff_g):
    lambda_init = 0.8 - 0.6 * math.exp(-0.3 * layer)
    d = h.shape[-1]
    cuts = np.cumsum([2 * A_QK, A_V, A_V, 2 * MLSTM_HEADS, B_QK, B_QK, B_V])
    n_gate = 2 * MLSTM_HEADS
    w_cat = jnp.concatenate([w_in[:, :cuts[2]], w_in[:, cuts[3]:], w_in[:, cuts[2]:cuts[3]],
                             jnp.zeros((d, LANES - n_gate), w_in.dtype)], axis=1).astype(BF16)
    proj = _nm_matmul(h, g, sc, sh, w_cat, tm=1024, tn=896, name="even_in_proj")
    h_a = _mlstm(proj, gate_b, conv_w, conv_b, mlstm_g)
    bq, bk, bv = _diff_prep(proj, pos3)
    h_b = _diff_attn(bq, bk, bv, lam_vec, diff_g, lambda_init)
    return _out_proj(h_a, h_b, w_out, h, g1)


def _odd_mixer(h, pos3, g, sc, sh, g1, w_in, w_out, ln_g, ln_b, sgu_w, sgu_b):
    proj = _nm_matmul(h, g, sc, sh, w_in.astype(BF16), tm=1024, tn=1024, name="odd_in_proj")
    cq, ck, cv, sel = _moba_prep(proj, pos3)
    h_c = _moba_attn(cq, ck, cv, sel)
    h_d = _sgu(proj, ln_g, ln_b, sgu_w, sgu_b)
    return _out_proj(h_c, h_d, w_out, h, g1)


def kernel(x, c, positions, mod_w, mod_b, norm_g, final_g, even_w_in, even_w_out, mlstm_conv_w, mlstm_conv_b, mlstm_gate_b, mlstm_head_g, diff_lambda, diff_head_g, odd_w_in, odd_w_out, sgu_ln_g, sgu_ln_b, sgu_w, sgu_b, peer_w_q, peer_sub_keys, peer_u, peer_v):
    depth = mod_w.shape[0]
    bsz, s, d = x.shape
    mod = _modulation(c, mod_w, mod_b)
    pos3 = positions.reshape(bsz, s, 1)
    h = x
    for layer in range(depth):
        sh1, sc1, g1, sh2, sc2, g2 = [mod[layer, :, i * d:(i + 1) * d] for i in range(6)]
        if layer % 2 == 0:
            e = layer // 2
            h = _even_mixer(h, pos3, norm_g[layer, 0], sc1, sh1, g1, layer, even_w_in[e], even_w_out[e],
                            mlstm_conv_w[e], mlstm_conv_b[e], mlstm_gate_b[e], mlstm_head_g[e],
                            diff_lambda[e], diff_head_g[e])
        else:
            o = layer // 2
            h = _odd_mixer(h, pos3, norm_g[layer, 0], sc1, sh1, g1, odd_w_in[o], odd_w_out[o],
                           sgu_ln_g[o], sgu_ln_b[o], sgu_w[o], sgu_b[o])
        h = _peer_layer(h, norm_g[layer, 1], sc2, sh2, g2, peer_w_q[layer], peer_sub_keys[layer],
                        peer_u, peer_v, layer)
    return _final_norm(h, final_g)
```

```python
import functools
import math

import numpy as np
import jax
import jax.numpy as jnp
from jax import lax
from jax.experimental import pallas as pl
from jax.experimental.pallas import tpu as pltpu

F32 = jnp.float32
BF16 = jnp.bfloat16
I32 = jnp.int32
HIGHEST = lax.Precision.HIGHEST

EPS = 1e-6
ROPE_THETA = 500000.0
LANES = 128
SUBLANES = 8
VMEM_LIMIT_BYTES = 48 * 1024 * 1024

MLSTM_HEADS, MLSTM_DQK, MLSTM_DV, MLSTM_CHUNK, CONV_W = 4, 128, 256, 128, 4
DIFF_HEADS, DIFF_DQK, DIFF_DV = 8, 64, 128
MOBA_HEADS, MOBA_DH, MOBA_BLOCK, MOBA_TOPK = 8, 128, 256, 3
SGU_GROUPS, SGU_CH, SGU_CHUNK = 8, 128, 128
PEER_HEADS, PEER_NKEYS, PEER_DQ, PEER_TOPK = 8, 128, 256, 16
PEER_SEL = PEER_HEADS * PEER_TOPK

A_QK = MLSTM_HEADS * MLSTM_DQK
A_V = MLSTM_HEADS * MLSTM_DV
B_QK = DIFF_HEADS * 2 * DIFF_DQK
B_V = DIFF_HEADS * DIFF_DV
C_W = MOBA_HEADS * MOBA_DH
D_W = SGU_GROUPS * SGU_CH


def _cparams(*sem):
    return pltpu.CompilerParams(dimension_semantics=sem, vmem_limit_bytes=VMEM_LIMIT_BYTES)


def _dot_nt(a, b, precision=None):
    return lax.dot_general(a, b, (((1,), (1,)), ((), ())), preferred_element_type=F32, precision=precision)


def _store_row(ref, row, val, cols=slice(None)):
    base = pl.multiple_of((row // SUBLANES) * SUBLANES, SUBLANES)
    blk = ref[pl.ds(base, SUBLANES), cols]
    r = lax.broadcasted_iota(I32, blk.shape, 0)
    ref[pl.ds(base, SUBLANES), cols] = jnp.where(r == row - base, val, blk)


ATTN_COLS = 128


def _flash_tiles_t(n_cols, score_fn, m_ref, l_ref, acc_ref):
    tiles = [slice(c0, c0 + ATTN_COLS) for c0 in range(0, n_cols, ATTN_COLS)]
    prev = [(m_ref[:, c], l_ref[:, c], acc_ref[:, c]) for c in tiles]
    scored = [score_fn(cols) for cols in tiles]
    soft = []
    for t, (s, vt) in enumerate(scored):
        m_new = jnp.maximum(prev[t][0], jnp.max(s, axis=0, keepdims=True))
        p = jnp.exp(s - m_new)
        soft.append((m_new, jnp.sum(p, axis=0, keepdims=True), p.astype(BF16)))
    new = []
    for t, (m_new, l_new, p) in enumerate(soft):
        m_prev, l_prev, acc_prev = prev[t]
        alpha = jnp.exp(m_prev - m_new)
        pv = jnp.dot(scored[t][1], p, preferred_element_type=F32)
        new.append((m_new, alpha * l_prev + l_new, alpha * acc_prev + pv))
    for cols, (m_new, l_new, acc_new) in zip(tiles, new):
        m_ref[:, cols] = m_new
        l_ref[:, cols] = l_new
        acc_ref[:, cols] = acc_new


def _sigmoid(x):
    return 1.0 / (1.0 + jnp.exp(-x))


def _silu(x):
    return x * _sigmoid(x)


def _gelu_tanh(x):
    return 0.5 * x * (1.0 + jnp.tanh(math.sqrt(2.0 / math.pi) * (x + 0.044715 * (x * x * x))))


def _log_sigmoid(x):
    return jnp.minimum(x, 0.0) - jnp.log1p(jnp.exp(-jnp.abs(x)))


def _mod_kernel(c_ref, w_ref, b_ref, o_ref):
    ca = _silu(c_ref[...])
    o_ref[0] = jnp.dot(ca, w_ref[0], preferred_element_type=F32, precision=HIGHEST) + b_ref[0]


def _modulation(c, mod_w, mod_b):
    depth, d, n = mod_w.shape
    bsz = c.shape[0]
    rows = 8
    cp = jnp.zeros((rows, d), F32).at[:bsz].set(c)
    tn = 1024
    out = pl.pallas_call(
        _mod_kernel,
        grid=(depth, n // tn),
        in_specs=[pl.BlockSpec((rows, d), lambda l, j: (0, 0)),
                  pl.BlockSpec((1, d, tn), lambda l, j: (l, 0, j)),
                  pl.BlockSpec((1, 1, tn), lambda l, j: (l, 0, j))],
        out_specs=pl.BlockSpec((1, rows, tn), lambda l, j: (l, 0, j)),
        out_shape=jax.ShapeDtypeStruct((depth, rows, n), F32),
        compiler_params=_cparams("parallel", "parallel"),
        name="modulation",
    )(cp, mod_w, mod_b.reshape(depth, 1, n))
    return out[:, :bsz]


def _nm_matmul_kernel(h_ref, g_ref, sc_ref, sh_ref, w_ref, *rest, emit_hn):
    if emit_hn:
        o_ref, hn_ref, hn_scr = rest
    else:
        o_ref, hn_scr = rest

    @pl.when(pl.program_id(2) == 0)
    def _():
        x = h_ref[0]
        y = x * lax.rsqrt(jnp.mean(x * x, axis=-1, keepdims=True) + EPS) * g_ref[...]
        y = y * (1.0 + sc_ref[0]) + sh_ref[0]
        hn_scr[...] = y.astype(BF16)
        if emit_hn:
            hn_ref[0] = y

    o_ref[0] = jnp.dot(hn_scr[...], w_ref[...], preferred_element_type=F32).astype(o_ref.dtype)


def _nm_matmul(h, g, sc, sh, w, *, tm, tn, emit_hn=False, name="nm_matmul"):
    bsz, s, d = h.shape
    n = w.shape[1]
    tm = min(tm, s)
    out_shape = [jax.ShapeDtypeStruct((bsz, s, n), F32)]
    out_specs = [pl.BlockSpec((1, tm, tn), lambda b, i, j: (b, i, j))]
    if emit_hn:
        out_shape.append(jax.ShapeDtypeStruct((bsz, s, d), F32))
        out_specs.append(pl.BlockSpec((1, tm, d), lambda b, i, j: (b, i, 0)))
    res = pl.pallas_call(
        functools.partial(_nm_matmul_kernel, emit_hn=emit_hn),
        grid=(bsz, s // tm, n // tn),
        in_specs=[pl.BlockSpec((1, tm, d), lambda b, i, j: (b, i, 0)),
                  pl.BlockSpec((1, d), lambda b, i, j: (0, 0)),
                  pl.BlockSpec((1, 1, d), lambda b, i, j: (b, 0, 0)),
                  pl.BlockSpec((1, 1, d), lambda b, i, j: (b, 0, 0)),
                  pl.BlockSpec((d, tn), lambda b, i, j: (0, j))],
        out_specs=out_specs,
        out_shape=out_shape,
        scratch_shapes=[pltpu.VMEM((tm, d), BF16)],
        compiler_params=_cparams("parallel", "parallel", "arbitrary"),
        name=name,
    )(h, g.reshape(1, d), sc.reshape(bsz, 1, d), sh.reshape(bsz, 1, d), w)
    return res if emit_hn else res[0]


def _out_proj_kernel(xa_ref, xb_ref, wa_ref, wb_ref, h_ref, g_ref, o_ref):
    acc = jnp.dot(xa_ref[0], wa_ref[...], preferred_element_type=F32)
    acc += jnp.dot(xb_ref[0], wb_ref[...], preferred_element_type=F32)
    o_ref[0] = h_ref[0] + g_ref[0] * acc


def _out_proj(xa, xb, w, h, g1, *, tm=1024, tn=1024):
    bsz, s, ka = xa.shape
    kb = xb.shape[2]
    n = w.shape[1]
    tm = min(tm, s)
    wa = w[:ka].astype(BF16)
    wb = w[ka:].astype(BF16)
    return pl.pallas_call(
        _out_proj_kernel,
        grid=(bsz, s // tm, n // tn),
        in_specs=[pl.BlockSpec((1, tm, ka), lambda b, i, j: (b, i, 0)),
                  pl.BlockSpec((1, tm, kb), lambda b, i, j: (b, i, 0)),
                  pl.BlockSpec((ka, tn), lambda b, i, j: (0, j)),
                  pl.BlockSpec((kb, tn), lambda b, i, j: (0, j)),
                  pl.BlockSpec((1, tm, tn), lambda b, i, j: (b, i, j)),
                  pl.BlockSpec((1, 1, tn), lambda b, i, j: (b, 0, j))],
        out_specs=pl.BlockSpec((1, tm, tn), lambda b, i, j: (b, i, j)),
        out_shape=jax.ShapeDtypeStruct((bsz, s, n), F32),
        compiler_params=_cparams("parallel", "parallel", "parallel"),
        name="out_proj",
    )(xa, xb, wa, wb, h, g1.reshape(bsz, 1, n))


def _mlstm_kernel(gb_ref, aqk_ref, v_ref, ao_ref, gcol_ref, cw_ref, cb_ref, hg_ref,
                  o_ref, xbuf, c_scr, n_scr, m_scr):
    L = MLSTM_CHUNK
    H, DQK, DV = MLSTM_HEADS, MLSTM_DQK, MLSTM_DV
    c = pl.program_id(1)

    @pl.when(c == 0)
    def _():
        xbuf[0:8, :] = jnp.zeros((8, 2 * A_QK), F32)
        c_scr[...] = jnp.zeros_like(c_scr)
        n_scr[...] = jnp.zeros_like(n_scr)
        m_scr[...] = jnp.zeros_like(m_scr)

    @pl.when(c > 0)
    def _():
        xbuf[0:8, :] = xbuf[L:L + 8, :]

    xbuf[8:L + 8, :] = aqk_ref[0]
    conv = cb_ref[...] + cw_ref[0:1, :] * xbuf[5:5 + L, :]
    for j in range(1, CONV_W):
        conv = conv + cw_ref[j:j + 1, :] * xbuf[5 + j:5 + j + L, :]
    qk = _silu(conv)

    r_i = lax.broadcasted_iota(I32, (L, L), 0)
    c_i = lax.broadcasted_iota(I32, (L, L), 1)
    tril = c_i <= r_i
    gcol = gcol_ref[0]
    grow = gcol.T

    for h in range(H):
        q = qk[:, h * DQK:(h + 1) * DQK] * (DQK ** -0.5)
        k = qk[:, A_QK + h * DQK:A_QK + (h + 1) * DQK]
        v = v_ref[0, :, h * DV:(h + 1) * DV]
        i_col = gcol[:, h:h + 1] + gb_ref[h]
        i_row = grow[h:h + 1, :] + gb_ref[h]
        lf_col = _log_sigmoid(gcol[:, H + h:H + h + 1] + gb_ref[H + h])
        lf_row = _log_sigmoid(grow[H + h:H + h + 1, :] + gb_ref[H + h])
        bcum_col = jnp.sum(jnp.where(tril, lf_row, 0.0), axis=1, keepdims=True)
        bcum_row = jnp.sum(jnp.where(r_i <= c_i, lf_col, 0.0), axis=0, keepdims=True)
        g_tot = jnp.sum(lf_row, axis=1, keepdims=True)
        m_prev = m_scr[h:h + 1, 0:1]
        d_intra = jnp.where(tril, bcum_col - bcum_row + i_row, -jnp.inf)
        d_inter = bcum_col + m_prev
        m_t = jnp.maximum(d_inter, jnp.max(d_intra, axis=1, keepdims=True))
        w_intra = jnp.exp(d_intra - m_t)
        w_inter = jnp.exp(d_inter - m_t)
        qb = q.astype(BF16)
        kb = k.astype(BF16)
        vb = v.astype(BF16)
        s = _dot_nt(qb, kb) * w_intra
        c_prev = c_scr[h]
        n_prev = n_scr[h:h + 1, :]
        num = jnp.dot(s.astype(BF16), vb, preferred_element_type=F32)
        num = num + w_inter * jnp.dot(qb, c_prev.astype(BF16), preferred_element_type=F32)
        den = jnp.sum(s, axis=1, keepdims=True) + w_inter * jnp.sum(q * n_prev, axis=1, keepdims=True)
        hh = num / jnp.maximum(jnp.abs(den), jnp.exp(-m_t))
        d_state = g_tot - bcum_col + i_col
        m_new = jnp.maximum(g_tot + m_prev, jnp.max(d_state, axis=0, keepdims=True))
        w_s = jnp.exp(d_state - m_new)
        w_c = jnp.exp(g_tot + m_prev - m_new)
        kw = k * w_s
        c_scr[h] = w_c * c_prev + jnp.dot(kw.T.astype(BF16), vb, preferred_element_type=F32)
        n_scr[h:h + 1, :] = w_c * n_prev + jnp.sum(kw, axis=0, keepdims=True)
        m_scr[h:h + 1, 0:1] = m_new
        y = hh * lax.rsqrt(jnp.mean(hh * hh, axis=-1, keepdims=True) + EPS) * hg_ref[:, h * DV:(h + 1) * DV]
        y = y * _sigmoid(ao_ref[0, :, h * DV:(h + 1) * DV])
        o_ref[0, :, h * DV:(h + 1) * DV] = y.astype(o_ref.dtype)


def _mlstm(proj, gate_b, conv_w, conv_b, head_g):
    bsz, s, _ = proj.shape
    L = MLSTM_CHUNK
    gate_blk = (2 * A_QK + 2 * A_V + 2 * B_QK + B_V) // LANES
    return pl.pallas_call(
        _mlstm_kernel,
        grid=(bsz, s // L),
        in_specs=[pl.BlockSpec(memory_space=pltpu.SMEM),
                  pl.BlockSpec((1, L, 2 * A_QK), lambda b, c: (b, c, 0)),
                  pl.BlockSpec((1, L, A_V), lambda b, c: (b, c, 1)),
                  pl.BlockSpec((1, L, A_V), lambda b, c: (b, c, 2)),
                  pl.BlockSpec((1, L, LANES), lambda b, c: (b, c, gate_blk)),
                  pl.BlockSpec((CONV_W, 2 * A_QK), lambda b, c: (0, 0)),
                  pl.BlockSpec((1, 2 * A_QK), lambda b, c: (0, 0)),
                  pl.BlockSpec((1, A_V), lambda b, c: (0, 0))],
        out_specs=pl.BlockSpec((1, L, A_V), lambda b, c: (b, c, 0)),
        out_shape=jax.ShapeDtypeStruct((bsz, s, A_V), BF16),
        scratch_shapes=[pltpu.VMEM((L + 8, 2 * A_QK), F32),
                        pltpu.VMEM((MLSTM_HEADS, MLSTM_DQK, MLSTM_DV), F32),
                        pltpu.VMEM((8, MLSTM_DQK), F32),
                        pltpu.VMEM((8, LANES), F32)],
        compiler_params=_cparams("parallel", "arbitrary"),
        name="mlstm",
    )(gate_b.reshape(-1), proj, proj, proj, proj, conv_w, conv_b.reshape(1, -1), head_g.reshape(1, -1))


def _rope_consts(dh):
    half = dh // 8
    lane = np.arange(LANES) % dh
    inv_half = ROPE_THETA ** (-jnp.arange(half, dtype=F32) / half)
    inv = jnp.where(jnp.asarray(lane < 2 * half), inv_half[lane % half], 0.0)
    lo = (lane < half).astype(np.float32)
    hi = ((lane >= half) & (lane < 2 * half)).astype(np.float32)
    return (inv.astype(F32).reshape(1, LANES), jnp.asarray(lo).reshape(1, LANES),
            jnp.asarray(hi).reshape(1, LANES))


def _rope_tables(pos_ref, inv_ref, lo_ref, hi_ref):
    ang = pos_ref[0].astype(F32) * inv_ref[...]
    cs = jnp.cos(ang)
    sn = jnp.sin(ang)
    return cs, -sn * lo_ref[...], sn * hi_ref[...]


def _rope_apply(x, cs, sa, sb, half):
    return x * cs + pltpu.roll(x, LANES - half, 1) * sa + pltpu.roll(x, half, 1) * sb


def _diff_prep_kernel(pos_ref, inv_ref, lo_ref, hi_ref, q_ref, k_ref, v_ref, qo_ref, ko_ref, vo_ref):
    cs, sa, sb = _rope_tables(pos_ref, inv_ref, lo_ref, hi_ref)
    half = DIFF_DQK // 8
    scale = DIFF_DQK ** -0.5
    for h in range(DIFF_HEADS):
        sl = slice(h * LANES, (h + 1) * LANES)
        qo_ref[0, sl, :] = (_rope_apply(q_ref[0, :, sl], cs, sa, sb, half) * scale).T.astype(BF16)
        ko_ref[0, :, sl] = _rope_apply(k_ref[0, :, sl], cs, sa, sb, half).astype(BF16)
        vo_ref[0, sl, :] = v_ref[0, :, sl].T.astype(BF16)


def _diff_prep(proj, pos3, ts=512):
    bsz, s, _ = proj.shape
    ts = min(ts, s)
    inv, lo, hi = _rope_consts(DIFF_DQK)
    base = (2 * A_QK + 2 * A_V) // B_QK
    cst = pl.BlockSpec((1, LANES), lambda b, i: (0, 0))
    osd = jax.ShapeDtypeStruct((bsz, s, B_QK), BF16)
    osd_t = jax.ShapeDtypeStruct((bsz, B_QK, s), BF16)
    spec_t = pl.BlockSpec((1, B_QK, ts), lambda b, i: (b, 0, i))
    return pl.pallas_call(
        _diff_prep_kernel,
        grid=(bsz, s // ts),
        in_specs=[pl.BlockSpec((1, ts, 1), lambda b, i: (b, i, 0)), cst, cst, cst,
                  pl.BlockSpec((1, ts, B_QK), lambda b, i: (b, i, base)),
                  pl.BlockSpec((1, ts, B_QK), lambda b, i: (b, i, base + 1)),
                  pl.BlockSpec((1, ts, B_V), lambda b, i: (b, i, base + 2))],
        out_specs=[spec_t, pl.BlockSpec((1, ts, B_QK), lambda b, i: (b, i, 0)), spec_t],
        out_shape=[osd_t, osd, osd_t],
        compiler_params=_cparams("parallel", "parallel"),
        name="diff_prep",
    )(pos3, inv, lo, hi, proj, proj, proj)


def _causal_pairs(nq):
    pairs = [(i, j) for i in range(nq) for j in range(i + 1)]
    return (jnp.asarray([p[0] for p in pairs], I32), jnp.asarray([p[1] for p in pairs], I32))


def _diff_attn_kernel(it_ref, jt_ref, lam_ref, q_ref, k_ref, v_ref, g_ref, o_ref, qs_scr, m_scr, l_scr, acc_scr,
                      *, tq, tk, lambda_init):
    i = it_ref[pl.program_id(2)]
    j = jt_ref[pl.program_id(2)]

    @pl.when(j == 0)
    def _():
        q = q_ref[0]
        feat = lax.broadcasted_iota(I32, q.shape, 0)
        zero = jnp.zeros_like(q)
        qs_scr[:, 0:tq] = jnp.where(feat < DIFF_DQK, q, zero)
        qs_scr[:, tq:2 * tq] = jnp.where(feat >= DIFF_DQK, q, zero)
        m_scr[...] = jnp.full_like(m_scr, -jnp.inf)
        l_scr[...] = jnp.zeros_like(l_scr)
        acc_scr[...] = jnp.zeros_like(acc_scr)

    def step(masked):
        def scores(cols):
            s = jnp.dot(k_ref[0], qs_scr[:, cols], preferred_element_type=F32)
            if masked:
                key = lax.broadcasted_iota(I32, (tk, ATTN_COLS), 0)
                qry = lax.broadcasted_iota(I32, (tk, ATTN_COLS), 1) + (cols.start % tq)
                s = jnp.where(key <= qry, s, -jnp.inf)
            return s, v_ref[0]
        _flash_tiles_t(2 * tq, scores, m_scr, l_scr, acc_scr)

    @pl.when(j < i)
    def _():
        step(False)

    @pl.when(j == i)
    def _():
        step(True)
        lv = lam_ref[...]
        lam = (jnp.exp(jnp.sum(lv[0:1] * lv[1:2], axis=1, keepdims=True))
               - jnp.exp(jnp.sum(lv[2:3] * lv[3:4], axis=1, keepdims=True)) + lambda_init)
        o0 = acc_scr[:, 0:tq] / l_scr[:, 0:tq]
        o1 = acc_scr[:, tq:2 * tq] / l_scr[:, tq:2 * tq]
        a = o0 - lam * o1
        y = a * lax.rsqrt(jnp.mean(a * a, axis=0, keepdims=True) + EPS) * g_ref[0]
        o_ref[0] = (y * (1.0 - lambda_init)).T.astype(o_ref.dtype)


def _diff_attn(qt, k, vt, lam_vec, head_g, lambda_init, *, tq=512):
    bsz, s, _ = k.shape
    tq = min(tq, s)
    it, jt = _causal_pairs(s // tq)
    grid_spec = pltpu.PrefetchScalarGridSpec(
        num_scalar_prefetch=2,
        grid=(bsz, DIFF_HEADS, it.shape[0]),
        in_specs=[pl.BlockSpec((4, DIFF_DQK), lambda b, h, t, it, jt: (0, 0)),
                  pl.BlockSpec((1, LANES, tq), lambda b, h, t, it, jt: (b, h, it[t])),
                  pl.BlockSpec((1, tq, LANES), lambda b, h, t, it, jt: (b, jt[t], h)),
                  pl.BlockSpec((1, DIFF_DV, tq), lambda b, h, t, it, jt: (b, h, jt[t])),
                  pl.BlockSpec((1, DIFF_DV, 1), lambda b, h, t, it, jt: (h, 0, 0))],
        out_specs=pl.BlockSpec((1, tq, DIFF_DV), lambda b, h, t, it, jt: (b, it[t], h)),
        scratch_shapes=[pltpu.VMEM((LANES, 2 * tq), BF16),
                        pltpu.VMEM((1, 2 * tq), F32),
                        pltpu.VMEM((1, 2 * tq), F32),
                        pltpu.VMEM((DIFF_DV, 2 * tq), F32)])
    return pl.pallas_call(
        functools.partial(_diff_attn_kernel, tq=tq, tk=tq, lambda_init=lambda_init),
        grid_spec=grid_spec,
        out_shape=jax.ShapeDtypeStruct((bsz, s, B_V), BF16),
        compiler_params=_cparams("parallel", "parallel", "arbitrary"),
        name="diff_attn",
    )(it, jt, lam_vec, qt, k, vt, head_g.reshape(DIFF_HEADS, DIFF_DV, 1))


def _moba_prep_kernel(pos_ref, inv_ref, lo_ref, hi_ref, q_ref, k_ref, v_ref,
                      qo_ref, ko_ref, vo_ref, sel_ref, kmt_scr, qf_scr, *, nb):
    i = pl.program_id(1)
    tb = MOBA_BLOCK
    half = MOBA_DH // 8
    scale = MOBA_DH ** -0.5

    @pl.when(i == 0)
    def _():
        kmt_scr[...] = jnp.zeros_like(kmt_scr)

    cs, sa, sb = _rope_tables(pos_ref, inv_ref, lo_ref, hi_ref)
    kmeans = []
    for h in range(MOBA_HEADS):
        sl = slice(h * LANES, (h + 1) * LANES)
        qr = _rope_apply(q_ref[0, :, sl], cs, sa, sb, half)
        kr = _rope_apply(k_ref[0, :, sl], cs, sa, sb, half)
        qf_scr[:, sl] = qr
        qo_ref[0, sl, :] = (qr * scale).T.astype(BF16)
        ko_ref[0, :, sl] = kr.astype(BF16)
        vo_ref[0, sl, :] = v_ref[0, :, sl].T.astype(BF16)
        kmeans.append(jnp.mean(kr, axis=0, keepdims=True))

    gate = _dot_nt(qf_scr[...], kmt_scr[...], precision=HIGHEST)
    lane = lax.broadcasted_iota(I32, (tb, LANES), 1)
    n = lane % nb
    valid = n < i
    gm = jnp.where(valid, gate, -jnp.inf)
    rank = jnp.zeros((tb, LANES), I32)
    for d in range(1, nb):
        wraps = (n + d) >= nb
        partner = jnp.where(wraps, pltpu.roll(gm, nb - d, 1), pltpu.roll(gm, LANES - d, 1))
        pn = jnp.where(wraps, n + d - nb, n + d)
        beats = (partner > gm) | ((partner == gm) & (pn < n))
        rank = rank + beats.astype(I32)
    sel_ref[0] = jnp.where(valid & (rank < MOBA_TOPK), 1.0, 0.0).astype(F32).T

    for h in range(MOBA_HEADS):
        _store_row(kmt_scr, h * nb + i, kmeans[h], slice(h * LANES, (h + 1) * LANES))


def _moba_prep(proj, pos3):
    bsz, s, _ = proj.shape
    tb = MOBA_BLOCK
    nb = s // tb
    assert s % tb == 0 and MOBA_HEADS * nb <= LANES and nb >= MOBA_TOPK
    inv, lo, hi = _rope_consts(MOBA_DH)
    cst = pl.BlockSpec((1, LANES), lambda b, i: (0, 0))
    osd = jax.ShapeDtypeStruct((bsz, s, C_W), BF16)
    osd_t = jax.ShapeDtypeStruct((bsz, C_W, s), BF16)
    spec_t = pl.BlockSpec((1, C_W, tb), lambda b, i: (b, 0, i))
    return pl.pallas_call(
        functools.partial(_moba_prep_kernel, nb=nb),
        grid=(bsz, nb),
        in_specs=[pl.BlockSpec((1, tb, 1), lambda b, i: (b, i, 0)), cst, cst, cst,
                  pl.BlockSpec((1, tb, C_W), lambda b, i: (b, i, 0)),
                  pl.BlockSpec((1, tb, C_W), lambda b, i: (b, i, 1)),
                  pl.BlockSpec((1, tb, C_W), lambda b, i: (b, i, 2))],
        out_specs=[spec_t, pl.BlockSpec((1, tb, C_W), lambda b, i: (b, i, 0)), spec_t,
                   pl.BlockSpec((1, LANES, tb), lambda b, i: (b, 0, i))],
        out_shape=[osd_t, osd, osd_t, jax.ShapeDtypeStruct((bsz, LANES, s), F32)],
        scratch_shapes=[pltpu.VMEM((LANES, C_W), F32), pltpu.VMEM((tb, C_W), F32)],
        compiler_params=_cparams("parallel", "arbitrary"),
        name="moba_prep",
    )(pos3, inv, lo, hi, proj, proj, proj)


def _moba_attn_kernel(it_ref, jt_ref, q_ref, k_ref, v_ref, sel_ref, o_ref, m_scr, l_scr, acc_scr, *, nb):
    h = pl.program_id(1)
    i = it_ref[pl.program_id(2)]
    j = jt_ref[pl.program_id(2)]
    tb = MOBA_BLOCK
    tq = 2 * tb

    def pick_bias(cols, blk):
        row = h * nb + blk
        base = pl.multiple_of((row // SUBLANES) * SUBLANES, SUBLANES)
        grp = sel_ref[0, pl.ds(base, SUBLANES), cols]
        sub = lax.broadcasted_iota(I32, grp.shape, 0)
        hit = jnp.max(jnp.where(sub == row - base, grp, 0.0), axis=0, keepdims=True)
        return jnp.where(hit > 0.5, 0.0, -jnp.inf)

    @pl.when(j == 0)
    def _():
        m_scr[...] = jnp.full_like(m_scr, -jnp.inf)
        l_scr[...] = jnp.zeros_like(l_scr)
        acc_scr[...] = jnp.zeros_like(acc_scr)

        def scores(cols):
            c0 = cols.start
            if c0 < tb:
                key = lax.broadcasted_iota(I32, (tb, ATTN_COLS), 0)
                qry = lax.broadcasted_iota(I32, (tb, ATTN_COLS), 1) + c0
                s = jnp.dot(k_ref[0, 0:tb, :], q_ref[0, :, cols], preferred_element_type=F32)
                return jnp.where(key <= qry, s, -jnp.inf), v_ref[0, :, 0:tb]
            key = lax.broadcasted_iota(I32, (tq, ATTN_COLS), 0)
            qry = lax.broadcasted_iota(I32, (tq, ATTN_COLS), 1) + c0
            s = jnp.dot(k_ref[0], q_ref[0, :, cols], preferred_element_type=F32)
            s = jnp.where(key < tb, s + pick_bias(cols, 2 * i), jnp.where(key <= qry, s, -jnp.inf))
            return s, v_ref[0]
        _flash_tiles_t(tq, scores, m_scr, l_scr, acc_scr)

    @pl.when((j > 0) & (j <= i))
    def _():
        kb = 2 * (i - j)
        key = lax.broadcasted_iota(I32, (tq, ATTN_COLS), 0)

        def scores(cols):
            bias = jnp.where(key < tb, pick_bias(cols, kb), pick_bias(cols, kb + 1))
            return jnp.dot(k_ref[0], q_ref[0, :, cols], preferred_element_type=F32) + bias, v_ref[0]
        _flash_tiles_t(tq, scores, m_scr, l_scr, acc_scr)

    @pl.when(j == i)
    def _():
        o_ref[0] = (acc_scr[...] / l_scr[...]).T.astype(o_ref.dtype)


def _moba_attn(qt, k, vt, sel_t):
    bsz, s, _ = k.shape
    tq = 2 * MOBA_BLOCK
    nb = s // MOBA_BLOCK
    assert s % tq == 0
    it, jt = _causal_pairs(s // tq)
    grid_spec = pltpu.PrefetchScalarGridSpec(
        num_scalar_prefetch=2,
        grid=(bsz, MOBA_HEADS, it.shape[0]),
        in_specs=[pl.BlockSpec((1, MOBA_DH, tq), lambda b, h, t, it, jt: (b, h, it[t])),
                  pl.BlockSpec((1, tq, MOBA_DH), lambda b, h, t, it, jt: (b, it[t] - jt[t], h)),
                  pl.BlockSpec((1, MOBA_DH, tq), lambda b, h, t, it, jt: (b, h, it[t] - jt[t])),
                  pl.BlockSpec((1, LANES, tq), lambda b, h, t, it, jt: (b, 0, it[t]))],
        out_specs=pl.BlockSpec((1, tq, MOBA_DH), lambda b, h, t, it, jt: (b, it[t], h)),
        scratch_shapes=[pltpu.VMEM((1, tq), F32), pltpu.VMEM((1, tq), F32), pltpu.VMEM((MOBA_DH, tq), F32)])
    return pl.pallas_call(
        functools.partial(_moba_attn_kernel, nb=nb),
        grid_spec=grid_spec,
        out_shape=jax.ShapeDtypeStruct((bsz, s, C_W), BF16),
        compiler_params=_cparams("parallel", "parallel", "arbitrary"),
        name="moba_attn",
    )(it, jt, qt, k, vt, sel_t)


def _sgu_kernel(u_ref, v_ref, lng_ref, lnb_ref, w_ref, bt_ref, o_ref, *, ts):
    L = SGU_CHUNK
    r_i = lax.broadcasted_iota(I32, (L, L), 0)
    c_i = lax.broadcasted_iota(I32, (L, L), 1)
    tril = c_i <= r_i
    for g in range(SGU_GROUPS):
        sl = slice(g * SGU_CH, (g + 1) * SGU_CH)
        w = jnp.where(tril, w_ref[g], 0.0).astype(BF16)
        bias = bt_ref[:, g:g + 1]
        for c in range(ts // L):
            rows = slice(c * L, (c + 1) * L)
            vv = _gelu_tanh(v_ref[0, rows, sl])
            mu = jnp.mean(vv, axis=-1, keepdims=True)
            dv = vv - mu
            var = jnp.mean(dv * dv, axis=-1, keepdims=True)
            vn = dv * lax.rsqrt(var + EPS) * lng_ref[:, sl] + lnb_ref[:, sl]
            mixed = jnp.dot(w, vn.astype(BF16), preferred_element_type=F32) + bias
            o_ref[0, rows, sl] = (_gelu_tanh(u_ref[0, rows, sl]) * mixed).astype(o_ref.dtype)


def _sgu(proj, ln_g, ln_b, w_s, b_s, ts=256):
    bsz, s, _ = proj.shape
    ts = min(ts, s)
    base = 3 * C_W // D_W
    return pl.pallas_call(
        functools.partial(_sgu_kernel, ts=ts),
        grid=(bsz, s // ts),
        in_specs=[pl.BlockSpec((1, ts, D_W), lambda b, i: (b, i, base)),
                  pl.BlockSpec((1, ts, D_W), lambda b, i: (b, i, base + 1)),
                  pl.BlockSpec((1, D_W), lambda b, i: (0, 0)),
                  pl.BlockSpec((1, D_W), lambda b, i: (0, 0)),
                  pl.BlockSpec((SGU_GROUPS, SGU_CHUNK, SGU_CHUNK), lambda b, i: (0, 0, 0)),
                  pl.BlockSpec((SGU_CHUNK, SGU_GROUPS), lambda b, i: (0, 0))],
        out_specs=pl.BlockSpec((1, ts, D_W), lambda b, i: (b, i, 0)),
        out_shape=jax.ShapeDtypeStruct((bsz, s, D_W), BF16),
        compiler_params=_cparams("parallel", "parallel"),
        name="sgu",
    )(proj, proj, ln_g.reshape(1, -1), ln_b.reshape(1, -1), w_s, b_s.T)


def _topk_rows(jobs, k):
    def body(it, carry):
        for s_scr, val_scr, idx_scr, payload_scr in jobs:
            n, t = s_scr.shape
            row = lax.broadcasted_iota(I32, (n, t), 0).astype(F32)
            s = s_scr[...]
            m = jnp.max(s, axis=0, keepdims=True)
            pos = jnp.min(jnp.where(s == m, row, float(n)), axis=0, keepdims=True)
            hit = row == pos
            _store_row(val_scr, it, m)
            if payload_scr is None:
                _store_row(idx_scr, it, pos)
            else:
                _store_row(idx_scr, it, jnp.max(jnp.where(hit, payload_scr[...], -1.0), axis=0, keepdims=True))
            s_scr[...] = jnp.where(hit, -jnp.inf, s)
        return carry

    for _, val_scr, idx_scr, _ in jobs:
        val_scr[...] = jnp.zeros_like(val_scr)
        idx_scr[...] = jnp.zeros_like(idx_scr)
    lax.fori_loop(0, k, body, 0)


def _peer_topk_kernel(q_ref, keys_ref, idx_ref, g_ref, s1_scr, s2_scr, v1, i1, v2, i2, cand, cidx, vt, it):
    K = PEER_TOPK
    half = PEER_DQ // 2
    for c, s_scr in enumerate((s1_scr, s2_scr)):
        s_scr[...] = _dot_nt(keys_ref[0, c], q_ref[:, c * half:(c + 1) * half], precision=HIGHEST)
    _topk_rows([(s1_scr, v1, i1, None), (s2_scr, v2, i2, None)], K)
    cand[...] = jnp.full(cand.shape, -jnp.inf, F32)
    cidx[...] = jnp.zeros(cidx.shape, F32)
    off = 0
    for a in range(K):
        nb = K // (a + 1)
        cand[off:off + nb, :] = v1[a:a + 1, :] + v2[0:nb, :]
        cidx[off:off + nb, :] = i1[a:a + 1, :] * float(PEER_NKEYS) + i2[0:nb, :]
        off += nb
    _topk_rows([(cand, vt, it, cidx)], K)
    ts = vt[...]
    e = jnp.exp(ts - jnp.max(ts, axis=0, keepdims=True))
    g_ref[0] = e / jnp.sum(e, axis=0, keepdims=True)
    idx_ref[0] = it[...].astype(I32)


def _peer_topk(q2, sub_keys, tt=512):
    n_tok = q2.shape[0]
    tt = min(tt, n_tok)
    K = PEER_TOPK
    half = PEER_DQ // 2
    vs = lambda r, dt: pltpu.VMEM((r, tt), dt)
    n_cand = sum(K // (a + 1) for a in range(K))
    n_cand = -(-n_cand // SUBLANES) * SUBLANES
    return pl.pallas_call(
        _peer_topk_kernel,
        grid=(n_tok // tt, PEER_HEADS),
        in_specs=[pl.BlockSpec((tt, PEER_DQ), lambda i, h: (i, h)),
                  pl.BlockSpec((1, 2, PEER_NKEYS, half), lambda i, h: (h, 0, 0, 0))],
        out_specs=[pl.BlockSpec((1, K, tt), lambda i, h: (h, 0, i)),
                   pl.BlockSpec((1, K, tt), lambda i, h: (h, 0, i))],
        out_shape=[jax.ShapeDtypeStruct((PEER_HEADS, K, n_tok), I32),
                   jax.ShapeDtypeStruct((PEER_HEADS, K, n_tok), F32)],
        scratch_shapes=[vs(PEER_NKEYS, F32), vs(PEER_NKEYS, F32), vs(K, F32), vs(K, F32), vs(K, F32), vs(K, F32),
                        vs(n_cand, F32), vs(n_cand, F32), vs(K, F32), vs(K, F32)],
        compiler_params=_cparams("parallel", "parallel"),
        name="peer_topk",
    )(q2, sub_keys)


def _pack_kernel(u_ref, v_ref, o_ref):
    ub = lax.bitcast_convert_type(u_ref[...].astype(BF16).astype(F32), I32)
    vb = lax.bitcast_convert_type(v_ref[...].astype(BF16).astype(F32), I32)
    o_ref[...] = (ub & jnp.int32(-65536)) | lax.shift_right_logical(vb, jnp.int32(16))


def _pack_tables(u_all, v_all, layer, tm=512):
    _, n, d = u_all.shape
    spec = pl.BlockSpec((None, tm, d), lambda i: (layer, i, 0))
    packed = pl.pallas_call(
        _pack_kernel,
        grid=(n // tm,),
        in_specs=[spec, spec],
        out_specs=pl.BlockSpec((tm, d), lambda i: (i, 0)),
        out_shape=jax.ShapeDtypeStruct((n, d), I32),
        compiler_params=_cparams("parallel"),
        name="peer_pack",
    )(u_all, v_all)
    return packed.reshape(n, d // LANES, LANES)


PEER_SLOTS = SUBLANES
PEER_AHEAD = 6
PEER_DMA_QUEUES = 2


def _fold_order():
    tiles = [[e] * SUBLANES for e in range(SUBLANES)]
    f = SUBLANES
    while len(tiles) > 1:
        tiles = [[x[s] if (s % f) < f // 2 else y[s] for s in range(SUBLANES)]
                 for x, y in zip(tiles[0::2], tiles[1::2])]
        f //= 2
    return tiles[0]


def _peer_gather_kernel(idx_ref, gate_ref, x_ref, h_ref, g2_ref, tab_hbm, o_ref, *rest, tt):
    R = PEER_SEL
    bufs, (stage, wscr, sem) = rest[:PEER_SLOTS], rest[PEER_SLOTS:]
    step = pl.program_id(0)
    d = x_ref.shape[1]
    nch = d // LANES
    nv = nch // SUBLANES
    ngrp = R // SUBLANES
    per_grp = R // ngrp
    order = _fold_order()

    def row_copy(t, j, slot):
        return pltpu.make_async_copy(tab_hbm.at[idx_ref[t, j]], bufs[slot].at[pl.ds(j * nch, nch)], sem.at[slot])

    def wait(slot):
        pltpu.make_async_copy(bufs[(slot + 1) % PEER_SLOTS], bufs[slot], sem.at[slot]).wait()

    def tile(slot, j, v):
        return bufs[slot][(j * nv + v) * SUBLANES:(j * nv + v + 1) * SUBLANES, :]

    lane = lax.broadcasted_iota(I32, (SUBLANES, LANES), 1)
    sub = lax.broadcasted_iota(I32, (SUBLANES, LANES), 0)
    hi_mask = jnp.int32(-65536)
    fold_masks = {f: (sub % f) < f // 2 for f in (8, 4, 2)}
    order_vec = jnp.zeros((SUBLANES, LANES), I32)
    for s_, e_ in enumerate(order):
        order_vec = jnp.where(sub == s_, e_, order_vec)

    def fold(x, y, f):
        m = fold_masks[f]
        if f == SUBLANES:
            return jnp.where(m, x, y) + pltpu.roll(jnp.where(m, y, x), f // 2, 0)
        return (jnp.where(m, x, pltpu.roll(y, f // 2, 0))
                + jnp.where(m, pltpu.roll(x, SUBLANES - f // 2, 0), y))

    @pl.when(step == 0)
    def _():
        for t0 in range(PEER_AHEAD):
            def prime(j, carry, t0=t0):
                row_copy(t0, j, t0).start()
                return carry
            lax.fori_loop(0, R, prime, 0)

    def group(gi, carry):
        base = pl.multiple_of(gi * SUBLANES, SUBLANES)
        xg = x_ref[pl.ds(base, SUBLANES), :]
        gg = gate_ref[pl.ds(base, SUBLANES), :]
        for r in range(SUBLANES):
            slot = r
            nxt = base + r + PEER_AHEAD
            nslot = (r + PEER_AHEAD) % PEER_SLOTS
            wait(slot)
            xs = []
            for v in range(nv):
                xv = jnp.zeros((SUBLANES, LANES), F32)
                for s_ in range(SUBLANES):
                    c = v * SUBLANES + s_
                    xv = jnp.where(sub == s_, jnp.broadcast_to(xg[r:r + 1, c * LANES:(c + 1) * LANES],
                                                                (SUBLANES, LANES)), xv)
                xs.append(xv)
            grow = jnp.broadcast_to(gg[r:r + 1, :], (SUBLANES, LANES))
            for g in range(ngrp):
                for j in range(g * per_grp, (g + 1) * per_grp):
                    row_copy(nxt, j, nslot).start(priority=j % PEER_DMA_QUEUES)
                parts = []
                for e_ in range(SUBLANES):
                    j = g * SUBLANES + e_
                    p = lax.bitcast_convert_type(tile(slot, j, 0) & hi_mask, F32) * xs[0]
                    for v in range(1, nv):
                        p = p + lax.bitcast_convert_type(tile(slot, j, v) & hi_mask, F32) * xs[v]
                    parts.append(p)
                f = SUBLANES
                while len(parts) > 1:
                    parts = [fold(x, y, f) for x, y in zip(parts[0::2], parts[1::2])]
                    f //= 2
                act = jnp.sum(parts[0], axis=1, keepdims=True)
                gcol = jnp.sum(jnp.where(lane == g * SUBLANES + order_vec, grow, 0.0), axis=1, keepdims=True)
                wscr[g * SUBLANES:(g + 1) * SUBLANES, :] = jnp.broadcast_to(gcol * _gelu_tanh(act),
                                                                             (SUBLANES, LANES))
            n_acc = 4
            accs = [[None] * nv for _ in range(n_acc)]
            for j in range(R):
                g, e_ = divmod(j, SUBLANES)
                row = g * SUBLANES + order.index(e_)
                wj = wscr[row:row + 1, :]
                for v in range(nv):
                    term = lax.bitcast_convert_type(tile(slot, j, v) << 16, F32) * wj
                    k = j % n_acc
                    accs[k][v] = term if accs[k][v] is None else accs[k][v] + term
            for v in range(nv):
                out_v = (accs[0][v] + accs[1][v]) + (accs[2][v] + accs[3][v])
                for s_ in range(SUBLANES):
                    c = v * SUBLANES + s_
                    stage[r:r + 1, c * LANES:(c + 1) * LANES] = out_v[s_:s_ + 1, :]
        rows = pl.ds(base, SUBLANES)
        o_ref[rows, :] = h_ref[rows, :] + g2_ref[0] * stage[...]
        return carry

    lax.fori_loop(0, tt // SUBLANES, group, 0)

    @pl.when(step == pl.num_programs(0) - 1)
    def _():
        for k in range(PEER_AHEAD):
            wait((tt + k) % PEER_SLOTS)


def _peer_gather(idx, gates, hn2, h2, g2, table, s, tt=128):
    n_tok, d = hn2.shape
    tt = min(tt, s)
    assert s % tt == 0 and tt % PEER_SLOTS == 0 and d % (SUBLANES * LANES) == 0 and PEER_SEL % SUBLANES == 0
    assert PEER_SEL <= LANES and PEER_AHEAD < PEER_SLOTS - 1
    per_b = s // tt
    R = PEER_SEL
    nsteps = n_tok // tt
    idx3 = idx.reshape(nsteps, tt, R)
    head = jnp.concatenate([idx3[1:, :SUBLANES], jnp.zeros((1, SUBLANES, R), I32)], axis=0)
    idx = jnp.concatenate([idx3, head], axis=1).reshape(nsteps * (tt + SUBLANES), R)
    return pl.pallas_call(
        functools.partial(_peer_gather_kernel, tt=tt),
        grid=(nsteps,),
        in_specs=[pl.BlockSpec((tt + SUBLANES, R), lambda i: (i, 0), memory_space=pltpu.SMEM),
                  pl.BlockSpec((tt, R), lambda i: (i, 0)),
                  pl.BlockSpec((tt, d), lambda i: (i, 0)),
                  pl.BlockSpec((tt, d), lambda i: (i, 0)),
                  pl.BlockSpec((1, 1, d), lambda i: (i // per_b, 0, 0)),
                  pl.BlockSpec(memory_space=pl.ANY)],
        out_specs=pl.BlockSpec((tt, d), lambda i: (i, 0)),
        out_shape=jax.ShapeDtypeStruct((n_tok, d), F32),
        scratch_shapes=[pltpu.VMEM((R * d // LANES, LANES), I32)] * PEER_SLOTS
        + [pltpu.VMEM((SUBLANES, d), F32), pltpu.VMEM((R, LANES), F32), pltpu.SemaphoreType.DMA((PEER_SLOTS,))],
        compiler_params=_cparams("arbitrary"),
        name="peer_gather",
    )(idx, gates, hn2, h2, g2.reshape(-1, 1, d), table)


def _peer_layer(h, g, sc, sh, g2, w_q, sub_keys, u_all, v_all, layer):
    bsz, s, d = h.shape
    q, hn = _nm_matmul(h, g, sc, sh, w_q.astype(BF16), tm=512, tn=2048, emit_hn=True, name="peer_query")
    n_tok = bsz * s
    idx, gates = _peer_topk(q.reshape(n_tok, -1), sub_keys)
    idx = idx.transpose(2, 0, 1).reshape(n_tok, PEER_SEL)
    gates = gates.transpose(2, 0, 1).reshape(n_tok, PEER_SEL)
    table = _pack_tables(u_all, v_all, layer)
    out = _peer_gather(idx, gates, hn.reshape(n_tok, d), h.reshape(n_tok, d), g2, table, s)
    return out.reshape(bsz, s, d)


def _rmsnorm_kernel(x_ref, g_ref, o_ref):
    x = x_ref[...]
    o_ref[...] = x * lax.rsqrt(jnp.mean(x * x, axis=-1, keepdims=True) + EPS) * g_ref[...]


def _final_norm(h, g, tm=512):
    bsz, s, d = h.shape
    n = bsz * s
    tm = min(tm, n)
    out = pl.pallas_call(
        _rmsnorm_kernel,
        grid=(n // tm,),
        in_specs=[pl.BlockSpec((tm, d), lambda i: (i, 0)), pl.BlockSpec((1, d), lambda i: (0, 0))],
        out_specs=pl.BlockSpec((tm, d), lambda i: (i, 0)),
        out_shape=jax.ShapeDtypeStruct((n, d), F32),
        compiler_params=_cparams("parallel"),
        name="final_norm",
    )(h.reshape(n, d), g.reshape(1, d))
    return out.reshape(bsz, s, d)


def _even_mixer(h, pos3, g, sc, sh, g1, layer, w_in, w_out, conv_w, conv_b, gate_b, mlstm_g, lam_vec, diff_g):
    lambda_init = 0.8 - 0.6 * math.exp(-0.3 * layer)
    d = h.shape[-1]
    cuts = np.cumsum([2 * A_QK, A_V, A_V, 2 * MLSTM_HEADS, B_QK, B_QK, B_V])
    n_gate = 2 * MLSTM_HEADS
    w_cat = jnp.concatenate([w_in[:, :cuts[2]], w_in[:, cuts[3]:], w_in[:, cuts[2]:cuts[3]],
                             jnp.zeros((d, LANES - n_gate), w_in.dtype)], axis=1).astype(BF16)
    proj = _nm_matmul(h, g, sc, sh, w_cat, tm=1024, tn=896, name="even_in_proj")
    h_a = _mlstm(proj, gate_b, conv_w, conv_b, mlstm_g)
    bq, bk, bv = _diff_prep(proj, pos3)
    h_b = _diff_attn(bq, bk, bv, lam_vec, diff_g, lambda_init)
    return _out_proj(h_a, h_b, w_out, h, g1)


def _odd_mixer(h, pos3, g, sc, sh, g1, w_in, w_out, ln_g, ln_b, sgu_w, sgu_b):
    proj = _nm_matmul(h, g, sc, sh, w_in.astype(BF16), tm=1024, tn=1024, name="odd_in_proj")
    cq, ck, cv, sel = _moba_prep(proj, pos3)
    h_c = _moba_attn(cq, ck, cv, sel)
    h_d = _sgu(proj, ln_g, ln_b, sgu_w, sgu_b)
    return _out_proj(h_c, h_d, w_out, h, g1)


def kernel(x, c, positions, mod_w, mod_b, norm_g, final_g, even_w_in, even_w_out, mlstm_conv_w, mlstm_conv_b, mlstm_gate_b, mlstm_head_g, diff_lambda, diff_head_g, odd_w_in, odd_w_out, sgu_ln_g, sgu_ln_b, sgu_w, sgu_b, peer_w_q, peer_sub_keys, peer_u, peer_v):
    depth = mod_w.shape[0]
    bsz, s, d = x.shape
    mod = _modulation(c, mod_w, mod_b)
    pos3 = positions.reshape(bsz, s, 1)
    h = x
    for layer in range(depth):
        sh1, sc1, g1, sh2, sc2, g2 = [mod[layer, :, i * d:(i + 1) * d] for i in range(6)]
        if layer % 2 == 0:
            e = layer // 2
            h = _even_mixer(h, pos3, norm_g[layer, 0], sc1, sh1, g1, layer, even_w_in[e], even_w_out[e],
                            mlstm_conv_w[e], mlstm_conv_b[e], mlstm_gate_b[e], mlstm_head_g[e],
                            diff_lambda[e], diff_head_g[e])
        else:
            o = layer // 2
            h = _odd_mixer(h, pos3, norm_g[layer, 0], sc1, sh1, g1, odd_w_in[o], odd_w_out[o],
                           sgu_ln_g[o], sgu_ln_b[o], sgu_w[o], sgu_b[o])
        h = _peer_layer(h, norm_g[layer, 1], sc2, sh2, g2, peer_w_q[layer], peer_sub_keys[layer],
                        peer_u, peer_v, layer)
    return _final_norm(h, final_g)
```

```python
import functools
import math

import numpy as np
import jax
import jax.numpy as jnp
from jax import lax
from jax.experimental import pallas as pl
from jax.experimental.pallas import tpu as pltpu

F32 = jnp.float32
BF16 = jnp.bfloat16
I32 = jnp.int32
HIGHEST = lax.Precision.HIGHEST

EPS = 1e-6
ROPE_THETA = 500000.0
LANES = 128
SUBLANES = 8
VMEM_LIMIT_BYTES = 48 * 1024 * 1024

MLSTM_HEADS, MLSTM_DQK, MLSTM_DV, MLSTM_CHUNK, CONV_W = 4, 128, 256, 128, 4
DIFF_HEADS, DIFF_DQK, DIFF_DV = 8, 64, 128
MOBA_HEADS, MOBA_DH, MOBA_BLOCK, MOBA_TOPK = 8, 128, 256, 3
SGU_GROUPS, SGU_CH, SGU_CHUNK = 8, 128, 128
PEER_HEADS, PEER_NKEYS, PEER_DQ, PEER_TOPK = 8, 128, 256, 16
PEER_SEL = PEER_HEADS * PEER_TOPK

A_QK = MLSTM_HEADS * MLSTM_DQK
A_V = MLSTM_HEADS * MLSTM_DV
B_QK = DIFF_HEADS * 2 * DIFF_DQK
B_V = DIFF_HEADS * DIFF_DV
C_W = MOBA_HEADS * MOBA_DH
D_W = SGU_GROUPS * SGU_CH


def _cparams(*sem):
    return pltpu.CompilerParams(dimension_semantics=sem, vmem_limit_bytes=VMEM_LIMIT_BYTES)


def _dot_nt(a, b, precision=None):
    return lax.dot_general(a, b, (((1,), (1,)), ((), ())), preferred_element_type=F32, precision=precision)


def _store_row(ref, row, val, cols=slice(None)):
    base = pl.multiple_of((row // SUBLANES) * SUBLANES, SUBLANES)
    blk = ref[pl.ds(base, SUBLANES), cols]
    r = lax.broadcasted_iota(I32, blk.shape, 0)
    ref[pl.ds(base, SUBLANES), cols] = jnp.where(r == row - base, val, blk)


ATTN_COLS = 128


def _flash_tiles_t(n_cols, score_fn, m_ref, l_ref, acc_ref):
    tiles = [slice(c0, c0 + ATTN_COLS) for c0 in range(0, n_cols, ATTN_COLS)]
    prev = [(m_ref[:, c], l_ref[:, c], acc_ref[:, c]) for c in tiles]
    scored = [score_fn(cols) for cols in tiles]
    soft = []
    for t, (s, vt) in enumerate(scored):
        m_new = jnp.maximum(prev[t][0], jnp.max(s, axis=0, keepdims=True))
        p = jnp.exp(s - m_new)
        soft.append((m_new, jnp.sum(p, axis=0, keepdims=True), p.astype(BF16)))
    new = []
    for t, (m_new, l_new, p) in enumerate(soft):
        m_prev, l_prev, acc_prev = prev[t]
        alpha = jnp.exp(m_prev - m_new)
        pv = jnp.dot(scored[t][1], p, preferred_element_type=F32)
        new.append((m_new, alpha * l_prev + l_new, alpha * acc_prev + pv))
    for cols, (m_new, l_new, acc_new) in zip(tiles, new):
        m_ref[:, cols] = m_new
        l_ref[:, cols] = l_new
        acc_ref[:, cols] = acc_new


def _sigmoid(x):
    return 1.0 / (1.0 + jnp.exp(-x))


def _silu(x):
    return x * _sigmoid(x)


def _gelu_tanh(x):
    return 0.5 * x * (1.0 + jnp.tanh(math.sqrt(2.0 / math.pi) * (x + 0.044715 * (x * x * x))))


def _log_sigmoid(x):
    return jnp.minimum(x, 0.0) - jnp.log1p(jnp.exp(-jnp.abs(x)))


def _mod_kernel(c_ref, w_ref, b_ref, o_ref):
    ca = _silu(c_ref[...])
    o_ref[0] = jnp.dot(ca, w_ref[0], preferred_element_type=F32, precision=HIGHEST) + b_ref[0]


def _modulation(c, mod_w, mod_b):
    depth, d, n = mod_w.shape
    bsz = c.shape[0]
    rows = 8
    cp = jnp.zeros((rows, d), F32).at[:bsz].set(c)
    tn = 1024
    out = pl.pallas_call(
        _mod_kernel,
        grid=(depth, n // tn),
        in_specs=[pl.BlockSpec((rows, d), lambda l, j: (0, 0)),
                  pl.BlockSpec((1, d, tn), lambda l, j: (l, 0, j)),
                  pl.BlockSpec((1, 1, tn), lambda l, j: (l, 0, j))],
        out_specs=pl.BlockSpec((1, rows, tn), lambda l, j: (l, 0, j)),
        out_shape=jax.ShapeDtypeStruct((depth, rows, n), F32),
        compiler_params=_cparams("parallel", "parallel"),
        name="modulation",
    )(cp, mod_w, mod_b.reshape(depth, 1, n))
    return out[:, :bsz]


def _nm_matmul_kernel(h_ref, g_ref, sc_ref, sh_ref, w_ref, *rest, emit_hn):
    if emit_hn:
        o_ref, hn_ref, hn_scr = rest
    else:
        o_ref, hn_scr = rest

    @pl.when(pl.program_id(2) == 0)
    def _():
        x = h_ref[0]
        y = x * lax.rsqrt(jnp.mean(x * x, axis=-1, keepdims=True) + EPS) * g_ref[...]
        y = y * (1.0 + sc_ref[0]) + sh_ref[0]
        hn_scr[...] = y.astype(BF16)
        if emit_hn:
            hn_ref[0] = y

    o_ref[0] = jnp.dot(hn_scr[...], w_ref[...], preferred_element_type=F32).astype(o_ref.dtype)


def _nm_matmul(h, g, sc, sh, w, *, tm, tn, emit_hn=False, name="nm_matmul"):
    bsz, s, d = h.shape
    n = w.shape[1]
    tm = min(tm, s)
    out_shape = [jax.ShapeDtypeStruct((bsz, s, n), F32)]
    out_specs = [pl.BlockSpec((1, tm, tn), lambda b, i, j: (b, i, j))]
    if emit_hn:
        out_shape.append(jax.ShapeDtypeStruct((bsz, s, d), F32))
        out_specs.append(pl.BlockSpec((1, tm, d), lambda b, i, j: (b, i, 0)))
    res = pl.pallas_call(
        functools.partial(_nm_matmul_kernel, emit_hn=emit_hn),
        grid=(bsz, s // tm, n // tn),
        in_specs=[pl.BlockSpec((1, tm, d), lambda b, i, j: (b, i, 0)),
                  pl.BlockSpec((1, d), lambda b, i, j: (0, 0)),
                  pl.BlockSpec((1, 1, d), lambda b, i, j: (b, 0, 0)),
                  pl.BlockSpec((1, 1, d), lambda b, i, j: (b, 0, 0)),
                  pl.BlockSpec((d, tn), lambda b, i, j: (0, j))],
        out_specs=out_specs,
        out_shape=out_shape,
        scratch_shapes=[pltpu.VMEM((tm, d), BF16)],
        compiler_params=_cparams("parallel", "parallel", "arbitrary"),
        name=name,
    )(h, g.reshape(1, d), sc.reshape(bsz, 1, d), sh.reshape(bsz, 1, d), w)
    return res if emit_hn else res[0]


def _out_proj_kernel(xa_ref, xb_ref, wa_ref, wb_ref, h_ref, g_ref, o_ref):
    acc = jnp.dot(xa_ref[0], wa_ref[...], preferred_element_type=F32)
    acc += jnp.dot(xb_ref[0], wb_ref[...], preferred_element_type=F32)
    o_ref[0] = h_ref[0] + g_ref[0] * acc


def _out_proj(xa, xb, w, h, g1, *, tm=1024, tn=1024):
    bsz, s, ka = xa.shape
    kb = xb.shape[2]
    n = w.shape[1]
    tm = min(tm, s)
    wa = w[:ka].astype(BF16)
    wb = w[ka:].astype(BF16)
    return pl.pallas_call(
        _out_proj_kernel,
        grid=(bsz, s // tm, n // tn),
        in_specs=[pl.BlockSpec((1, tm, ka), lambda b, i, j: (b, i, 0)),
                  pl.BlockSpec((1, tm, kb), lambda b, i, j: (b, i, 0)),
                  pl.BlockSpec((ka, tn), lambda b, i, j: (0, j)),
                  pl.BlockSpec((kb, tn), lambda b, i, j: (0, j)),
                  pl.BlockSpec((1, tm, tn), lambda b, i, j: (b, i, j)),
                  pl.BlockSpec((1, 1, tn), lambda b, i, j: (b, 0, j))],
        out_specs=pl.BlockSpec((1, tm, tn), lambda b, i, j: (b, i, j)),
        out_shape=jax.ShapeDtypeStruct((bsz, s, n), F32),
        compiler_params=_cparams("parallel", "parallel", "parallel"),
        name="out_proj",
    )(xa, xb, wa, wb, h, g1.reshape(bsz, 1, n))


def _mlstm_kernel(gb_ref, aqk_ref, v_ref, ao_ref, gcol_ref, cw_ref, cb_ref, hg_ref,
                  o_ref, xbuf, c_scr, n_scr, m_scr):
    L = MLSTM_CHUNK
    H, DQK, DV = MLSTM_HEADS, MLSTM_DQK, MLSTM_DV
    c = pl.program_id(1)

    @pl.when(c == 0)
    def _():
        xbuf[0:8, :] = jnp.zeros((8, 2 * A_QK), F32)
        c_scr[...] = jnp.zeros_like(c_scr)
        n_scr[...] = jnp.zeros_like(n_scr)
        m_scr[...] = jnp.zeros_like(m_scr)

    @pl.when(c > 0)
    def _():
        xbuf[0:8, :] = xbuf[L:L + 8, :]

    xbuf[8:L + 8, :] = aqk_ref[0]
    conv = cb_ref[...] + cw_ref[0:1, :] * xbuf[5:5 + L, :]
    for j in range(1, CONV_W):
        conv = conv + cw_ref[j:j + 1, :] * xbuf[5 + j:5 + j + L, :]
    qk = _silu(conv)

    r_i = lax.broadcasted_iota(I32, (L, L), 0)
    c_i = lax.broadcasted_iota(I32, (L, L), 1)
    tril = c_i <= r_i
    gcol = gcol_ref[0]
    grow = gcol.T

    for h in range(H):
        q = qk[:, h * DQK:(h + 1) * DQK] * (DQK ** -0.5)
        k = qk[:, A_QK + h * DQK:A_QK + (h + 1) * DQK]
        v = v_ref[0, :, h * DV:(h + 1) * DV]
        i_col = gcol[:, h:h + 1] + gb_ref[h]
        i_row = grow[h:h + 1, :] + gb_ref[h]
        lf_col = _log_sigmoid(gcol[:, H + h:H + h + 1] + gb_ref[H + h])
        lf_row = _log_sigmoid(grow[H + h:H + h + 1, :] + gb_ref[H + h])
        bcum_col = jnp.sum(jnp.where(tril, lf_row, 0.0), axis=1, keepdims=True)
        bcum_row = jnp.sum(jnp.where(r_i <= c_i, lf_col, 0.0), axis=0, keepdims=True)
        g_tot = jnp.sum(lf_row, axis=1, keepdims=True)
        m_prev = m_scr[h:h + 1, 0:1]
        d_intra = jnp.where(tril, bcum_col - bcum_row + i_row, -jnp.inf)
        d_inter = bcum_col + m_prev
        m_t = jnp.maximum(d_inter, jnp.max(d_intra, axis=1, keepdims=True))
        w_intra = jnp.exp(d_intra - m_t)
        w_inter = jnp.exp(d_inter - m_t)
        qb = q.astype(BF16)
        kb = k.astype(BF16)
        vb = v.astype(BF16)
        s = _dot_nt(qb, kb) * w_intra
        c_prev = c_scr[h]
        n_prev = n_scr[h:h + 1, :]
        num = jnp.dot(s.astype(BF16), vb, preferred_element_type=F32)
        num = num + w_inter * jnp.dot(qb, c_prev.astype(BF16), preferred_element_type=F32)
        den = jnp.sum(s, axis=1, keepdims=True) + w_inter * jnp.sum(q * n_prev, axis=1, keepdims=True)
        hh = num / jnp.maximum(jnp.abs(den), jnp.exp(-m_t))
        d_state = g_tot - bcum_col + i_col
        m_new = jnp.maximum(g_tot + m_prev, jnp.max(d_state, axis=0, keepdims=True))
        w_s = jnp.exp(d_state - m_new)
        w_c = jnp.exp(g_tot + m_prev - m_new)
        kw = k * w_s
        c_scr[h] = w_c * c_prev + jnp.dot(kw.T.astype(BF16), vb, preferred_element_type=F32)
        n_scr[h:h + 1, :] = w_c * n_prev + jnp.sum(kw, axis=0, keepdims=True)
        m_scr[h:h + 1, 0:1] = m_new
        y = hh * lax.rsqrt(jnp.mean(hh * hh, axis=-1, keepdims=True) + EPS) * hg_ref[:, h * DV:(h + 1) * DV]
        y = y * _sigmoid(ao_ref[0, :, h * DV:(h + 1) * DV])
        o_ref[0, :, h * DV:(h + 1) * DV] = y.astype(o_ref.dtype)


def _mlstm(proj, gate_b, conv_w, conv_b, head_g):
    bsz, s, _ = proj.shape
    L = MLSTM_CHUNK
    gate_blk = (2 * A_QK + 2 * A_V + 2 * B_QK + B_V) // LANES
    return pl.pallas_call(
        _mlstm_kernel,
        grid=(bsz, s // L),
        in_specs=[pl.BlockSpec(memory_space=pltpu.SMEM),
                  pl.BlockSpec((1, L, 2 * A_QK), lambda b, c: (b, c, 0)),
                  pl.BlockSpec((1, L, A_V), lambda b, c: (b, c, 1)),
                  pl.BlockSpec((1, L, A_V), lambda b, c: (b, c, 2)),
                  pl.BlockSpec((1, L, LANES), lambda b, c: (b, c, gate_blk)),
                  pl.BlockSpec((CONV_W, 2 * A_QK), lambda b, c: (0, 0)),
                  pl.BlockSpec((1, 2 * A_QK), lambda b, c: (0, 0)),
                  pl.BlockSpec((1, A_V), lambda b, c: (0, 0))],
        out_specs=pl.BlockSpec((1, L, A_V), lambda b, c: (b, c, 0)),
        out_shape=jax.ShapeDtypeStruct((bsz, s, A_V), BF16),
        scratch_shapes=[pltpu.VMEM((L + 8, 2 * A_QK), F32),
                        pltpu.VMEM((MLSTM_HEADS, MLSTM_DQK, MLSTM_DV), F32),
                        pltpu.VMEM((8, MLSTM_DQK), F32),
                        pltpu.VMEM((8, LANES), F32)],
        compiler_params=_cparams("parallel", "arbitrary"),
        name="mlstm",
    )(gate_b.reshape(-1), proj, proj, proj, proj, conv_w, conv_b.reshape(1, -1), head_g.reshape(1, -1))


def _rope_consts(dh):
    half = dh // 8
    lane = np.arange(LANES) % dh
    inv_half = ROPE_THETA ** (-jnp.arange(half, dtype=F32) / half)
    inv = jnp.where(jnp.asarray(lane < 2 * half), inv_half[lane % half], 0.0)
    lo = (lane < half).astype(np.float32)
    hi = ((lane >= half) & (lane < 2 * half)).astype(np.float32)
    return (inv.astype(F32).reshape(1, LANES), jnp.asarray(lo).reshape(1, LANES),
            jnp.asarray(hi).reshape(1, LANES))


def _rope_tables(pos_ref, inv_ref, lo_ref, hi_ref):
    ang = pos_ref[0].astype(F32) * inv_ref[...]
    cs = jnp.cos(ang)
    sn = jnp.sin(ang)
    return cs, -sn * lo_ref[...], sn * hi_ref[...]


def _rope_apply(x, cs, sa, sb, half):
    return x * cs + pltpu.roll(x, LANES - half, 1) * sa + pltpu.roll(x, half, 1) * sb


def _diff_prep_kernel(pos_ref, inv_ref, lo_ref, hi_ref, q_ref, k_ref, v_ref, qo_ref, ko_ref, vo_ref):
    cs, sa, sb = _rope_tables(pos_ref, inv_ref, lo_ref, hi_ref)
    half = DIFF_DQK // 8
    scale = DIFF_DQK ** -0.5
    for h in range(DIFF_HEADS):
        sl = slice(h * LANES, (h + 1) * LANES)
        qo_ref[0, sl, :] = (_rope_apply(q_ref[0, :, sl], cs, sa, sb, half) * scale).T.astype(BF16)
        ko_ref[0, :, sl] = _rope_apply(k_ref[0, :, sl], cs, sa, sb, half).astype(BF16)
        vo_ref[0, sl, :] = v_ref[0, :, sl].T.astype(BF16)


def _diff_prep(proj, pos3, ts=512):
    bsz, s, _ = proj.shape
    ts = min(ts, s)
    inv, lo, hi = _rope_consts(DIFF_DQK)
    base = (2 * A_QK + 2 * A_V) // B_QK
    cst = pl.BlockSpec((1, LANES), lambda b, i: (0, 0))
    osd = jax.ShapeDtypeStruct((bsz, s, B_QK), BF16)
    osd_t = jax.ShapeDtypeStruct((bsz, B_QK, s), BF16)
    spec_t = pl.BlockSpec((1, B_QK, ts), lambda b, i: (b, 0, i))
    return pl.pallas_call(
        _diff_prep_kernel,
        grid=(bsz, s // ts),
        in_specs=[pl.BlockSpec((1, ts, 1), lambda b, i: (b, i, 0)), cst, cst, cst,
                  pl.BlockSpec((1, ts, B_QK), lambda b, i: (b, i, base)),
                  pl.BlockSpec((1, ts, B_QK), lambda b, i: (b, i, base + 1)),
                  pl.BlockSpec((1, ts, B_V), lambda b, i: (b, i, base + 2))],
        out_specs=[spec_t, pl.BlockSpec((1, ts, B_QK), lambda b, i: (b, i, 0)), spec_t],
        out_shape=[osd_t, osd, osd_t],
        compiler_params=_cparams("parallel", "parallel"),
        name="diff_prep",
    )(pos3, inv, lo, hi, proj, proj, proj)


def _causal_pairs(nq):
    pairs = [(i, j) for i in range(nq) for j in range(i + 1)]
    return (jnp.asarray([p[0] for p in pairs], I32), jnp.asarray([p[1] for p in pairs], I32))


def _diff_attn_kernel(it_ref, jt_ref, lam_ref, q_ref, k_ref, v_ref, g_ref, o_ref, qs_scr, m_scr, l_scr, acc_scr,
                      *, tq, tk, lambda_init):
    i = it_ref[pl.program_id(2)]
    j = jt_ref[pl.program_id(2)]

    @pl.when(j == 0)
    def _():
        q = q_ref[0]
        feat = lax.broadcasted_iota(I32, q.shape, 0)
        zero = jnp.zeros_like(q)
        qs_scr[:, 0:tq] = jnp.where(feat < DIFF_DQK, q, zero)
        qs_scr[:, tq:2 * tq] = jnp.where(feat >= DIFF_DQK, q, zero)
        m_scr[...] = jnp.full_like(m_scr, -jnp.inf)
        l_scr[...] = jnp.zeros_like(l_scr)
        acc_scr[...] = jnp.zeros_like(acc_scr)

    def step(masked):
        def scores(cols):
            s = jnp.dot(k_ref[0], qs_scr[:, cols], preferred_element_type=F32)
            if masked:
                key = lax.broadcasted_iota(I32, (tk, ATTN_COLS), 0)
                qry = lax.broadcasted_iota(I32, (tk, ATTN_COLS), 1) + (cols.start % tq)
                s = jnp.where(key <= qry, s, -jnp.inf)
            return s, v_ref[0]
        _flash_tiles_t(2 * tq, scores, m_scr, l_scr, acc_scr)

    @pl.when(j < i)
    def _():
        step(False)

    @pl.when(j == i)
    def _():
        step(True)
        lv = lam_ref[...]
        lam = (jnp.exp(jnp.sum(lv[0:1] * lv[1:2], axis=1, keepdims=True))
               - jnp.exp(jnp.sum(lv[2:3] * lv[3:4], axis=1, keepdims=True)) + lambda_init)
        o0 = acc_scr[:, 0:tq] / l_scr[:, 0:tq]
        o1 = acc_scr[:, tq:2 * tq] / l_scr[:, tq:2 * tq]
        a = o0 - lam * o1
        y = a * lax.rsqrt(jnp.mean(a * a, axis=0, keepdims=True) + EPS) * g_ref[0]
        o_ref[0] = (y * (1.0 - lambda_init)).T.astype(o_ref.dtype)


def _diff_attn(qt, k, vt, lam_vec, head_g, lambda_init, *, tq=512):
    bsz, s, _ = k.shape
    tq = min(tq, s)
    it, jt = _causal_pairs(s // tq)
    grid_spec = pltpu.PrefetchScalarGridSpec(
        num_scalar_prefetch=2,
        grid=(bsz, DIFF_HEADS, it.shape[0]),
        in_specs=[pl.BlockSpec((4, DIFF_DQK), lambda b, h, t, it, jt: (0, 0)),
                  pl.BlockSpec((1, LANES, tq), lambda b, h, t, it, jt: (b, h, it[t])),
                  pl.BlockSpec((1, tq, LANES), lambda b, h, t, it, jt: (b, jt[t], h)),
                  pl.BlockSpec((1, DIFF_DV, tq), lambda b, h, t, it, jt: (b, h, jt[t])),
                  pl.BlockSpec((1, DIFF_DV, 1), lambda b, h, t, it, jt: (h, 0, 0))],
        out_specs=pl.BlockSpec((1, tq, DIFF_DV), lambda b, h, t, it, jt: (b, it[t], h)),
        scratch_shapes=[pltpu.VMEM((LANES, 2 * tq), BF16),
                        pltpu.VMEM((1, 2 * tq), F32),
                        pltpu.VMEM((1, 2 * tq), F32),
                        pltpu.VMEM((DIFF_DV, 2 * tq), F32)])
    return pl.pallas_call(
        functools.partial(_diff_attn_kernel, tq=tq, tk=tq, lambda_init=lambda_init),
        grid_spec=grid_spec,
        out_shape=jax.ShapeDtypeStruct((bsz, s, B_V), BF16),
        compiler_params=_cparams("parallel", "parallel", "arbitrary"),
        name="diff_attn",
    )(it, jt, lam_vec, qt, k, vt, head_g.reshape(DIFF_HEADS, DIFF_DV, 1))


def _moba_prep_kernel(pos_ref, inv_ref, lo_ref, hi_ref, q_ref, k_ref, v_ref,
                      qo_ref, ko_ref, vo_ref, sel_ref, kmt_scr, qf_scr, *, nb):
    i = pl.program_id(1)
    tb = MOBA_BLOCK
    half = MOBA_DH // 8
    scale = MOBA_DH ** -0.5

    @pl.when(i == 0)
    def _():
        kmt_scr[...] = jnp.zeros_like(kmt_scr)

    cs, sa, sb = _rope_tables(pos_ref, inv_ref, lo_ref, hi_ref)
    kmeans = []
    for h in range(MOBA_HEADS):
        sl = slice(h * LANES, (h + 1) * LANES)
        qr = _rope_apply(q_ref[0, :, sl], cs, sa, sb, half)
        kr = _rope_apply(k_ref[0, :, sl], cs, sa, sb, half)
        qf_scr[:, sl] = qr
        qo_ref[0, sl, :] = (qr * scale).T.astype(BF16)
        ko_ref[0, :, sl] = kr.astype(BF16)
        vo_ref[0, sl, :] = v_ref[0, :, sl].T.astype(BF16)
        kmeans.append(jnp.mean(kr, axis=0, keepdims=True))

    gate = _dot_nt(qf_scr[...], kmt_scr[...], precision=HIGHEST)
    lane = lax.broadcasted_iota(I32, (tb, LANES), 1)
    n = lane % nb
    valid = n < i
    gm = jnp.where(valid, gate, -jnp.inf)
    rank = jnp.zeros((tb, LANES), I32)
    for d in range(1, nb):
        wraps = (n + d) >= nb
        partner = jnp.where(wraps, pltpu.roll(gm, nb - d, 1), pltpu.roll(gm, LANES - d, 1))
        pn = jnp.where(wraps, n + d - nb, n + d)
        beats = (partner > gm) | ((partner == gm) & (pn < n))
        rank = rank + beats.astype(I32)
    sel_ref[0] = jnp.where(valid & (rank < MOBA_TOPK), 1.0, 0.0).astype(F32).T

    for h in range(MOBA_HEADS):
        _store_row(kmt_scr, h * nb + i, kmeans[h], slice(h * LANES, (h + 1) * LANES))


def _moba_prep(proj, pos3):
    bsz, s, _ = proj.shape
    tb = MOBA_BLOCK
    nb = s // tb
    assert s % tb == 0 and MOBA_HEADS * nb <= LANES and nb >= MOBA_TOPK
    inv, lo, hi = _rope_consts(MOBA_DH)
    cst = pl.BlockSpec((1, LANES), lambda b, i: (0, 0))
    osd = jax.ShapeDtypeStruct((bsz, s, C_W), BF16)
    osd_t = jax.ShapeDtypeStruct((bsz, C_W, s), BF16)
    spec_t = pl.BlockSpec((1, C_W, tb), lambda b, i: (b, 0, i))
    return pl.pallas_call(
        functools.partial(_moba_prep_kernel, nb=nb),
        grid=(bsz, nb),
        in_specs=[pl.BlockSpec((1, tb, 1), lambda b, i: (b, i, 0)), cst, cst, cst,
                  pl.BlockSpec((1, tb, C_W), lambda b, i: (b, i, 0)),
                  pl.BlockSpec((1, tb, C_W), lambda b, i: (b, i, 1)),
                  pl.BlockSpec((1, tb, C_W), lambda b, i: (b, i, 2))],
        out_specs=[spec_t, pl.BlockSpec((1, tb, C_W), lambda b, i: (b, i, 0)), spec_t,
                   pl.BlockSpec((1, LANES, tb), lambda b, i: (b, 0, i))],
        out_shape=[osd_t, osd, osd_t, jax.ShapeDtypeStruct((bsz, LANES, s), F32)],
        scratch_shapes=[pltpu.VMEM((LANES, C_W), F32), pltpu.VMEM((tb, C_W), F32)],
        compiler_params=_cparams("parallel", "arbitrary"),
        name="moba_prep",
    )(pos3, inv, lo, hi, proj, proj, proj)


def _moba_attn_kernel(it_ref, jt_ref, q_ref, k_ref, v_ref, sel_ref, o_ref, m_scr, l_scr, acc_scr, *, nb):
    h = pl.program_id(1)
    i = it_ref[pl.program_id(2)]
    j = jt_ref[pl.program_id(2)]
    tb = MOBA_BLOCK
    tq = 2 * tb

    def pick_bias(cols, blk):
        row = h * nb + blk
        base = pl.multiple_of((row // SUBLANES) * SUBLANES, SUBLANES)
        grp = sel_ref[0, pl.ds(base, SUBLANES), cols]
        sub = lax.broadcasted_iota(I32, grp.shape, 0)
        hit = jnp.max(jnp.where(sub == row - base, grp, 0.0), axis=0, keepdims=True)
        return jnp.where(hit > 0.5, 0.0, -jnp.inf)

    @pl.when(j == 0)
    def _():
        m_scr[...] = jnp.full_like(m_scr, -jnp.inf)
        l_scr[...] = jnp.zeros_like(l_scr)
        acc_scr[...] = jnp.zeros_like(acc_scr)

        def scores(cols):
            c0 = cols.start
            if c0 < tb:
                key = lax.broadcasted_iota(I32, (tb, ATTN_COLS), 0)
                qry = lax.broadcasted_iota(I32, (tb, ATTN_COLS), 1) + c0
                s = jnp.dot(k_ref[0, 0:tb, :], q_ref[0, :, cols], preferred_element_type=F32)
                return jnp.where(key <= qry, s, -jnp.inf), v_ref[0, :, 0:tb]
            key = lax.broadcasted_iota(I32, (tq, ATTN_COLS), 0)
            qry = lax.broadcasted_iota(I32, (tq, ATTN_COLS), 1) + c0
            s = jnp.dot(k_ref[0], q_ref[0, :, cols], preferred_element_type=F32)
            s = jnp.where(key < tb, s + pick_bias(cols, 2 * i), jnp.where(key <= qry, s, -jnp.inf))
            return s, v_ref[0]
        _flash_tiles_t(tq, scores, m_scr, l_scr, acc_scr)

    @pl.when((j > 0) & (j <= i))
    def _():
        kb = 2 * (i - j)
        key = lax.broadcasted_iota(I32, (tq, ATTN_COLS), 0)

        def scores(cols):
            bias = jnp.where(key < tb, pick_bias(cols, kb), pick_bias(cols, kb + 1))
            return jnp.dot(k_ref[0], q_ref[0, :, cols], preferred_element_type=F32) + bias, v_ref[0]
        _flash_tiles_t(tq, scores, m_scr, l_scr, acc_scr)

    @pl.when(j == i)
    def _():
        o_ref[0] = (acc_scr[...] / l_scr[...]).T.astype(o_ref.dtype)


def _moba_attn(qt, k, vt, sel_t):
    bsz, s, _ = k.shape
    tq = 2 * MOBA_BLOCK
    nb = s // MOBA_BLOCK
    assert s % tq == 0
    it, jt = _causal_pairs(s // tq)
    grid_spec = pltpu.PrefetchScalarGridSpec(
        num_scalar_prefetch=2,
        grid=(bsz, MOBA_HEADS, it.shape[0]),
        in_specs=[pl.BlockSpec((1, MOBA_DH, tq), lambda b, h, t, it, jt: (b, h, it[t])),
                  pl.BlockSpec((1, tq, MOBA_DH), lambda b, h, t, it, jt: (b, it[t] - jt[t], h)),
                  pl.BlockSpec((1, MOBA_DH, tq), lambda b, h, t, it, jt: (b, h, it[t] - jt[t])),
                  pl.BlockSpec((1, LANES, tq), lambda b, h, t, it, jt: (b, 0, it[t]))],
        out_specs=pl.BlockSpec((1, tq, MOBA_DH), lambda b, h, t, it, jt: (b, it[t], h)),
        scratch_shapes=[pltpu.VMEM((1, tq), F32), pltpu.VMEM((1, tq), F32), pltpu.VMEM((MOBA_DH, tq), F32)])
    return pl.pallas_call(
        functools.partial(_moba_attn_kernel, nb=nb),
        grid_spec=grid_spec,
        out_shape=jax.ShapeDtypeStruct((bsz, s, C_W), BF16),
        compiler_params=_cparams("parallel", "parallel", "arbitrary"),
        name="moba_attn",
    )(it, jt, qt, k, vt, sel_t)


def _sgu_kernel(u_ref, v_ref, lng_ref, lnb_ref, w_ref, bt_ref, o_ref, *, ts):
    L = SGU_CHUNK
    r_i = lax.broadcasted_iota(I32, (L, L), 0)
    c_i = lax.broadcasted_iota(I32, (L, L), 1)
    tril = c_i <= r_i
    for g in range(SGU_GROUPS):
        sl = slice(g * SGU_CH, (g + 1) * SGU_CH)
        w = jnp.where(tril, w_ref[g], 0.0).astype(BF16)
        bias = bt_ref[:, g:g + 1]
        for c in range(ts // L):
            rows = slice(c * L, (c + 1) * L)
            vv = _gelu_tanh(v_ref[0, rows, sl])
            mu = jnp.mean(vv, axis=-1, keepdims=True)
            dv = vv - mu
            var = jnp.mean(dv * dv, axis=-1, keepdims=True)
            vn = dv * lax.rsqrt(var + EPS) * lng_ref[:, sl] + lnb_ref[:, sl]
            mixed = jnp.dot(w, vn.astype(BF16), preferred_element_type=F32) + bias
            o_ref[0, rows, sl] = (_gelu_tanh(u_ref[0, rows, sl]) * mixed).astype(o_ref.dtype)


def _sgu(proj, ln_g, ln_b, w_s, b_s, ts=256):
    bsz, s, _ = proj.shape
    ts = min(ts, s)
    base = 3 * C_W // D_W
    return pl.pallas_call(
        functools.partial(_sgu_kernel, ts=ts),
        grid=(bsz, s // ts),
        in_specs=[pl.BlockSpec((1, ts, D_W), lambda b, i: (b, i, base)),
                  pl.BlockSpec((1, ts, D_W), lambda b, i: (b, i, base + 1)),
                  pl.BlockSpec((1, D_W), lambda b, i: (0, 0)),
                  pl.BlockSpec((1, D_W), lambda b, i: (0, 0)),
                  pl.BlockSpec((SGU_GROUPS, SGU_CHUNK, SGU_CHUNK), lambda b, i: (0, 0, 0)),
                  pl.BlockSpec((SGU_CHUNK, SGU_GROUPS), lambda b, i: (0, 0))],
        out_specs=pl.BlockSpec((1, ts, D_W), lambda b, i: (b, i, 0)),
        out_shape=jax.ShapeDtypeStruct((bsz, s, D_W), BF16),
        compiler_params=_cparams("parallel", "parallel"),
        name="sgu",
    )(proj, proj, ln_g.reshape(1, -1), ln_b.reshape(1, -1), w_s, b_s.T)


def _topk_rows(jobs, k):
    def body(it, carry):
        for s_scr, val_scr, idx_scr, payload_scr in jobs:
            n, t = s_scr.shape
            row = lax.broadcasted_iota(I32, (n, t), 0).astype(F32)
            s = s_scr[...]
            m = jnp.max(s, axis=0, keepdims=True)
            pos = jnp.min(jnp.where(s == m, row, float(n)), axis=0, keepdims=True)
            hit = row == pos
            _store_row(val_scr, it, m)
            if payload_scr is None:
                _store_row(idx_scr, it, pos)
            else:
                _store_row(idx_scr, it, jnp.max(jnp.where(hit, payload_scr[...], -1.0), axis=0, keepdims=True))
            s_scr[...] = jnp.where(hit, -jnp.inf, s)
        return carry

    for _, val_scr, idx_scr, _ in jobs:
        val_scr[...] = jnp.zeros_like(val_scr)
        idx_scr[...] = jnp.zeros_like(idx_scr)
    lax.fori_loop(0, k, body, 0)


def _peer_topk_kernel(q_ref, keys_ref, idx_ref, g_ref, s1_scr, s2_scr, v1, i1, v2, i2, cand, cidx, vt, it):
    K = PEER_TOPK
    half = PEER_DQ // 2
    for c, s_scr in enumerate((s1_scr, s2_scr)):
        s_scr[...] = _dot_nt(keys_ref[0, c], q_ref[:, c * half:(c + 1) * half], precision=HIGHEST)
    _topk_rows([(s1_scr, v1, i1, None), (s2_scr, v2, i2, None)], K)
    cand[...] = jnp.full(cand.shape, -jnp.inf, F32)
    cidx[...] = jnp.zeros(cidx.shape, F32)
    off = 0
    for a in range(K):
        nb = K // (a + 1)
        cand[off:off + nb, :] = v1[a:a + 1, :] + v2[0:nb, :]
        cidx[off:off + nb, :] = i1[a:a + 1, :] * float(PEER_NKEYS) + i2[0:nb, :]
        off += nb
    _topk_rows([(cand, vt, it, cidx)], K)
    ts = vt[...]
    e = jnp.exp(ts - jnp.max(ts, axis=0, keepdims=True))
    g_ref[0] = e / jnp.sum(e, axis=0, keepdims=True)
    idx_ref[0] = it[...].astype(I32)


def _peer_topk(q2, sub_keys, tt=512):
    n_tok = q2.shape[0]
    tt = min(tt, n_tok)
    K = PEER_TOPK
    half = PEER_DQ // 2
    vs = lambda r, dt: pltpu.VMEM((r, tt), dt)
    n_cand = sum(K // (a + 1) for a in range(K))
    n_cand = -(-n_cand // SUBLANES) * SUBLANES
    return pl.pallas_call(
        _peer_topk_kernel,
        grid=(n_tok // tt, PEER_HEADS),
        in_specs=[pl.BlockSpec((tt, PEER_DQ), lambda i, h: (i, h)),
                  pl.BlockSpec((1, 2, PEER_NKEYS, half), lambda i, h: (h, 0, 0, 0))],
        out_specs=[pl.BlockSpec((1, K, tt), lambda i, h: (h, 0, i)),
                   pl.BlockSpec((1, K, tt), lambda i, h: (h, 0, i))],
        out_shape=[jax.ShapeDtypeStruct((PEER_HEADS, K, n_tok), I32),
                   jax.ShapeDtypeStruct((PEER_HEADS, K, n_tok), F32)],
        scratch_shapes=[vs(PEER_NKEYS, F32), vs(PEER_NKEYS, F32), vs(K, F32), vs(K, F32), vs(K, F32), vs(K, F32),
                        vs(n_cand, F32), vs(n_cand, F32), vs(K, F32), vs(K, F32)],
        compiler_params=_cparams("parallel", "parallel"),
        name="peer_topk",
    )(q2, sub_keys)


def _pack_kernel(u_ref, v_ref, o_ref):
    ub = lax.bitcast_convert_type(u_ref[...].astype(BF16).astype(F32), I32)
    vb = lax.bitcast_convert_type(v_ref[...].astype(BF16).astype(F32), I32)
    o_ref[...] = (ub & jnp.int32(-65536)) | lax.shift_right_logical(vb, jnp.int32(16))


def _pack_tables(u_all, v_all, layer, tm=512):
    _, n, d = u_all.shape
    spec = pl.BlockSpec((None, tm, d), lambda i: (layer, i, 0))
    packed = pl.pallas_call(
        _pack_kernel,
        grid=(n // tm,),
        in_specs=[spec, spec],
        out_specs=pl.BlockSpec((tm, d), lambda i: (i, 0)),
        out_shape=jax.ShapeDtypeStruct((n, d), I32),
        compiler_params=_cparams("parallel"),
        name="peer_pack",
    )(u_all, v_all)
    return packed.reshape(n, d // LANES, LANES)


PEER_SLOTS = SUBLANES
PEER_AHEAD = 6
PEER_DMA_QUEUES = 2


def _fold_order():
    tiles = [[e] * SUBLANES for e in range(SUBLANES)]
    f = SUBLANES
    while len(tiles) > 1:
        tiles = [[x[s] if (s % f) < f // 2 else y[s] for s in range(SUBLANES)]
                 for x, y in zip(tiles[0::2], tiles[1::2])]
        f //= 2
    return tiles[0]


def _peer_gather_kernel(idx_ref, gate_ref, x_ref, h_ref, g2_ref, fg_ref, tab_hbm, o_ref, *rest, tt, final):
    R = PEER_SEL
    bufs, (stage, wscr, sem) = rest[:PEER_SLOTS], rest[PEER_SLOTS:]
    step = pl.program_id(0)
    d = x_ref.shape[1]
    nch = d // LANES
    nv = nch // SUBLANES
    ngrp = R // SUBLANES
    per_grp = R // ngrp
    order = _fold_order()

    def row_copy(t, j, slot):
        return pltpu.make_async_copy(tab_hbm.at[idx_ref[t, j]], bufs[slot].at[pl.ds(j * nch, nch)], sem.at[slot])

    def wait(slot):
        pltpu.make_async_copy(bufs[(slot + 1) % PEER_SLOTS], bufs[slot], sem.at[slot]).wait()

    def tile(slot, j, v):
        return bufs[slot][(j * nv + v) * SUBLANES:(j * nv + v + 1) * SUBLANES, :]

    lane = lax.broadcasted_iota(I32, (SUBLANES, LANES), 1)
    sub = lax.broadcasted_iota(I32, (SUBLANES, LANES), 0)
    hi_mask = jnp.int32(-65536)
    fold_masks = {f: (sub % f) < f // 2 for f in (8, 4, 2)}
    order_vec = jnp.zeros((SUBLANES, LANES), I32)
    for s_, e_ in enumerate(order):
        order_vec = jnp.where(sub == s_, e_, order_vec)

    def fold(x, y, f):
        m = fold_masks[f]
        if f == SUBLANES:
            return jnp.where(m, x, y) + pltpu.roll(jnp.where(m, y, x), f // 2, 0)
        return (jnp.where(m, x, pltpu.roll(y, f // 2, 0))
                + jnp.where(m, pltpu.roll(x, SUBLANES - f // 2, 0), y))

    @pl.when(step == 0)
    def _():
        for t0 in range(PEER_AHEAD):
            def prime(j, carry, t0=t0):
                row_copy(t0, j, t0).start()
                return carry
            lax.fori_loop(0, R, prime, 0)

    def group(gi, carry):
        base = pl.multiple_of(gi * SUBLANES, SUBLANES)
        xg = x_ref[pl.ds(base, SUBLANES), :]
        gg = gate_ref[pl.ds(base, SUBLANES), :]
        for r in range(SUBLANES):
            slot = r
            nxt = base + r + PEER_AHEAD
            nslot = (r + PEER_AHEAD) % PEER_SLOTS
            wait(slot)
            xs = []
            for v in range(nv):
                xv = jnp.zeros((SUBLANES, LANES), F32)
                for s_ in range(SUBLANES):
                    c = v * SUBLANES + s_
                    xv = jnp.where(sub == s_, jnp.broadcast_to(xg[r:r + 1, c * LANES:(c + 1) * LANES],
                                                                (SUBLANES, LANES)), xv)
                xs.append(xv)
            grow = jnp.broadcast_to(gg[r:r + 1, :], (SUBLANES, LANES))
            for g in range(ngrp):
                for j in range(g * per_grp, (g + 1) * per_grp):
                    row_copy(nxt, j, nslot).start(priority=j % PEER_DMA_QUEUES)
                parts = []
                for e_ in range(SUBLANES):
                    j = g * SUBLANES + e_
                    p = lax.bitcast_convert_type(tile(slot, j, 0) & hi_mask, F32) * xs[0]
                    for v in range(1, nv):
                        p = p + lax.bitcast_convert_type(tile(slot, j, v) & hi_mask, F32) * xs[v]
                    parts.append(p)
                f = SUBLANES
                while len(parts) > 1:
                    parts = [fold(x, y, f) for x, y in zip(parts[0::2], parts[1::2])]
                    f //= 2
                act = jnp.sum(parts[0], axis=1, keepdims=True)
                gcol = jnp.sum(jnp.where(lane == g * SUBLANES + order_vec, grow, 0.0), axis=1, keepdims=True)
                wscr[g * SUBLANES:(g + 1) * SUBLANES, :] = jnp.broadcast_to(gcol * _gelu_tanh(act),
                                                                             (SUBLANES, LANES))
            n_acc = 4
            accs = [[None] * nv for _ in range(n_acc)]
            for j in range(R):
                g, e_ = divmod(j, SUBLANES)
                row = g * SUBLANES + order.index(e_)
                wj = wscr[row:row + 1, :]
                for v in range(nv):
                    term = lax.bitcast_convert_type(tile(slot, j, v) << 16, F32) * wj
                    k = j % n_acc
                    accs[k][v] = term if accs[k][v] is None else accs[k][v] + term
            for v in range(nv):
                out_v = (accs[0][v] + accs[1][v]) + (accs[2][v] + accs[3][v])
                for s_ in range(SUBLANES):
                    c = v * SUBLANES + s_
                    stage[r:r + 1, c * LANES:(c + 1) * LANES] = out_v[s_:s_ + 1, :]
        rows = pl.ds(base, SUBLANES)
        y = h_ref[rows, :] + g2_ref[0] * stage[...]
        if final:
            y = y * lax.rsqrt(jnp.mean(y * y, axis=-1, keepdims=True) + EPS) * fg_ref[...]
        o_ref[rows, :] = y
        return carry

    lax.fori_loop(0, tt // SUBLANES, group, 0)

    @pl.when(step == pl.num_programs(0) - 1)
    def _():
        for k in range(PEER_AHEAD):
            wait((tt + k) % PEER_SLOTS)


def _peer_gather(idx, gates, hn2, h2, g2, table, s, final_g, final, tt=128):
    n_tok, d = hn2.shape
    tt = min(tt, s)
    assert s % tt == 0 and tt % PEER_SLOTS == 0 and d % (SUBLANES * LANES) == 0 and PEER_SEL % SUBLANES == 0
    assert PEER_SEL <= LANES and PEER_AHEAD < PEER_SLOTS - 1
    per_b = s // tt
    R = PEER_SEL
    nsteps = n_tok // tt
    idx3 = idx.reshape(nsteps, tt, R)
    head = jnp.concatenate([idx3[1:, :SUBLANES], jnp.zeros((1, SUBLANES, R), I32)], axis=0)
    idx = jnp.concatenate([idx3, head], axis=1).reshape(nsteps * (tt + SUBLANES), R)
    return pl.pallas_call(
        functools.partial(_peer_gather_kernel, tt=tt, final=final),
        grid=(nsteps,),
        in_specs=[pl.BlockSpec((tt + SUBLANES, R), lambda i: (i, 0), memory_space=pltpu.SMEM),
                  pl.BlockSpec((tt, R), lambda i: (i, 0)),
                  pl.BlockSpec((tt, d), lambda i: (i, 0)),
                  pl.BlockSpec((tt, d), lambda i: (i, 0)),
                  pl.BlockSpec((1, 1, d), lambda i: (i // per_b, 0, 0)),
                  pl.BlockSpec((1, d), lambda i: (0, 0)),
                  pl.BlockSpec(memory_space=pl.ANY)],
        out_specs=pl.BlockSpec((tt, d), lambda i: (i, 0)),
        out_shape=jax.ShapeDtypeStruct((n_tok, d), F32),
        scratch_shapes=[pltpu.VMEM((R * d // LANES, LANES), I32)] * PEER_SLOTS
        + [pltpu.VMEM((SUBLANES, d), F32), pltpu.VMEM((R, LANES), F32), pltpu.SemaphoreType.DMA((PEER_SLOTS,))],
        compiler_params=_cparams("arbitrary"),
        name="peer_gather",
    )(idx, gates, hn2, h2, g2.reshape(-1, 1, d), final_g.reshape(1, d), table)


def _peer_layer(h, g, sc, sh, g2, w_q, sub_keys, u_all, v_all, layer, final_g, final):
    bsz, s, d = h.shape
    q, hn = _nm_matmul(h, g, sc, sh, w_q.astype(BF16), tm=512, tn=2048, emit_hn=True, name="peer_query")
    n_tok = bsz * s
    idx, gates = _peer_topk(q.reshape(n_tok, -1), sub_keys)
    idx = idx.transpose(2, 0, 1).reshape(n_tok, PEER_SEL)
    gates = gates.transpose(2, 0, 1).reshape(n_tok, PEER_SEL)
    table = _pack_tables(u_all, v_all, layer)
    out = _peer_gather(idx, gates, hn.reshape(n_tok, d), h.reshape(n_tok, d), g2, table, s, final_g, final)
    return out.reshape(bsz, s, d)


def _rmsnorm_kernel(x_ref, g_ref, o_ref):
    x = x_ref[...]
    o_ref[...] = x * lax.rsqrt(jnp.mean(x * x, axis=-1, keepdims=True) + EPS) * g_ref[...]


def _final_norm(h, g, tm=512):
    bsz, s, d = h.shape
    n = bsz * s
    tm = min(tm, n)
    out = pl.pallas_call(
        _rmsnorm_kernel,
        grid=(n // tm,),
        in_specs=[pl.BlockSpec((tm, d), lambda i: (i, 0)), pl.BlockSpec((1, d), lambda i: (0, 0))],
        out_specs=pl.BlockSpec((tm, d), lambda i: (i, 0)),
        out_shape=jax.ShapeDtypeStruct((n, d), F32),
        compiler_params=_cparams("parallel"),
        name="final_norm",
    )(h.reshape(n, d), g.reshape(1, d))
    return out.reshape(bsz, s, d)


def _even_mixer(h, pos3, g, sc, sh, g1, layer, w_in, w_out, conv_w, conv_b, gate_b, mlstm_g, lam_vec, diff_g):
    lambda_init = 0.8 - 0.6 * math.exp(-0.3 * layer)
    d = h.shape[-1]
    cuts = np.cumsum([2 * A_QK, A_V, A_V, 2 * MLSTM_HEADS, B_QK, B_QK, B_V])
    n_gate = 2 * MLSTM_HEADS
    w_cat = jnp.concatenate([w_in[:, :cuts[2]], w_in[:, cuts[3]:], w_in[:, cuts[2]:cuts[3]],
                             jnp.zeros((d, LANES - n_gate), w_in.dtype)], axis=1).astype(BF16)
    proj = _nm_matmul(h, g, sc, sh, w_cat, tm=1024, tn=896, name="even_in_proj")
    h_a = _mlstm(proj, gate_b, conv_w, conv_b, mlstm_g)
    bq, bk, bv = _diff_prep(proj, pos3)
    h_b = _diff_attn(bq, bk, bv, lam_vec, diff_g, lambda_init)
    return _out_proj(h_a, h_b, w_out, h, g1)


def _odd_mixer(h, pos3, g, sc, sh, g1, w_in, w_out, ln_g, ln_b, sgu_w, sgu_b):
    proj = _nm_matmul(h, g, sc, sh, w_in.astype(BF16), tm=1024, tn=1024, name="odd_in_proj")
    cq, ck, cv, sel = _moba_prep(proj, pos3)
    h_c = _moba_attn(cq, ck, cv, sel)
    h_d = _sgu(proj, ln_g, ln_b, sgu_w, sgu_b)
    return _out_proj(h_c, h_d, w_out, h, g1)


def kernel(x, c, positions, mod_w, mod_b, norm_g, final_g, even_w_in, even_w_out, mlstm_conv_w, mlstm_conv_b, mlstm_gate_b, mlstm_head_g, diff_lambda, diff_head_g, odd_w_in, odd_w_out, sgu_ln_g, sgu_ln_b, sgu_w, sgu_b, peer_w_q, peer_sub_keys, peer_u, peer_v):
    depth = mod_w.shape[0]
    bsz, s, d = x.shape
    mod = _modulation(c, mod_w, mod_b)
    pos3 = positions.reshape(bsz, s, 1)
    h = x
    for layer in range(depth):
        sh1, sc1, g1, sh2, sc2, g2 = [mod[layer, :, i * d:(i + 1) * d] for i in range(6)]
        if layer % 2 == 0:
            e = layer // 2
            h = _even_mixer(h, pos3, norm_g[layer, 0], sc1, sh1, g1, layer, even_w_in[e], even_w_out[e],
                            mlstm_conv_w[e], mlstm_conv_b[e], mlstm_gate_b[e], mlstm_head_g[e],
                            diff_lambda[e], diff_head_g[e])
        else:
            o = layer // 2
            h = _odd_mixer(h, pos3, norm_g[layer, 0], sc1, sh1, g1, odd_w_in[o], odd_w_out[o],
                           sgu_ln_g[o], sgu_ln_b[o], sgu_w[o], sgu_b[o])
        h = _peer_layer(h, norm_g[layer, 1], sc2, sh2, g2, peer_w_q[layer], peer_sub_keys[layer],
                        peer_u, peer_v, layer, final_g, layer == depth - 1)
    return h if depth > 0 else _final_norm(h, final_g)
```
